```python
import math
import jax, jax.numpy as jnp
from jax import lax
import numpy as np


D_MODEL = 1024
BATCH = 4
SEQ = 8192
DEPTH = 2

HEAD_DIM = 64
SB_HEADS = 8
SB_WIDTH = SB_HEADS * HEAD_DIM
SB_BLOCK = 128
MOBA_HEADS = 8
MOBA_WIDTH = MOBA_HEADS * HEAD_DIM
MOBA_BLOCK = 256
MOBA_TOPK = 3
MOBA_Q_CHUNK = 32
SWA_HEADS = 8
SWA_KV_HEADS = 2
SWA_GROUP = SWA_HEADS // SWA_KV_HEADS
SWA_WIDTH = SWA_HEADS * HEAD_DIM
SWA_KV_WIDTH = SWA_KV_HEADS * HEAD_DIM
SWA_WINDOW = 128
REL_BUCKETS = 32
REL_MAX_DIST = 128
REL_HEADS = MOBA_HEADS + SWA_HEADS
RMS_EPS = 1e-6

SPLIT_SIZES = (SB_WIDTH, SB_WIDTH, SB_WIDTH, SB_WIDTH,
               MOBA_WIDTH, MOBA_WIDTH, MOBA_WIDTH, MOBA_WIDTH,
               SWA_WIDTH, SWA_KV_WIDTH, SWA_KV_WIDTH, SWA_WIDTH,
               D_MODEL, D_MODEL, D_MODEL)
D_IN = sum(SPLIT_SIZES)

kernel_name = "hybrid_sb_moba_swa_gated_block"


def rms_norm(x, w):
    xf = x.astype(jnp.float32)
    y = xf * lax.rsqrt(jnp.mean(xf * xf, axis=-1, keepdims=True) + RMS_EPS)
    return (y * w.astype(jnp.float32)).astype(x.dtype)


def _split_columns(u):
    parts, off = [], 0
    for n in SPLIT_SIZES:
        parts.append(u[..., off:off + n])
        off += n
    return parts


def _rel_bucket(dist):
    max_exact = REL_BUCKETS // 2
    n = jnp.maximum(dist, 0)
    nf = jnp.maximum(n, 1).astype(jnp.float32)
    large = max_exact + (jnp.log(nf / max_exact) / math.log(REL_MAX_DIST / max_exact)
                         * (REL_BUCKETS - max_exact)).astype(jnp.int32)
    large = jnp.minimum(large, REL_BUCKETS - 1)
    return jnp.where(n < max_exact, n, large)


def stick_breaking_attention(q, k, v):
    B, H, S, dh = q.shape
    nq = S // SB_BLOCK
    scale = dh ** -0.5
    qb = q.reshape(B, H, nq, SB_BLOCK, dh).transpose(2, 0, 1, 3, 4)
    kpos = jnp.arange(S)

    def block(args):
        qi, i = args
        qpos = i * SB_BLOCK + jnp.arange(SB_BLOCK)
        z = jnp.einsum('bhqd,bhkd->bhqk', qi, k).astype(jnp.float32) * scale
        past = kpos[None, :] < qpos[:, None]
        log_1m = jnp.where(past, jax.nn.log_sigmoid(-z), 0.0)
        between = lax.cumsum(log_1m, axis=3, reverse=True) - log_1m
        w = jnp.where(past, jnp.exp(jax.nn.log_sigmoid(z) + between), 0.0)
        return jnp.einsum('bhqk,bhkd->bhqd', w.astype(v.dtype), v)

    out = lax.map(block, (qb, jnp.arange(nq)))
    return out.transpose(1, 2, 0, 3, 4).reshape(B, H, S, dh)


def moba_attention(q, k, v, rel_table):
    B, H, S, dh = q.shape
    scale = dh ** -0.5
    nb = -(-S // MOBA_BLOCK)
    pad = nb * MOBA_BLOCK - S
    padding = ((0, 0), (0, 0), (0, pad), (0, 0))
    kb = jnp.pad(k, padding).reshape(B, H, nb, MOBA_BLOCK, dh)
    vb = jnp.pad(v, padding).reshape(B, H, nb, MOBA_BLOCK, dh)
    kmean = kb.astype(jnp.float32).mean(axis=3).astype(k.dtype)

    pos = jnp.arange(S)
    own = pos // MOBA_BLOCK
    gate = jnp.einsum('bhsd,bhnd->bhsn', q, kmean).astype(jnp.float32)
    past_blk = jnp.arange(nb)[None, :] < own[:, None]
    gate = jnp.where(past_blk, gate, -jnp.inf)
    k_sel = min(MOBA_TOPK, nb)
    _, idx = lax.top_k(gate, k_sel)
    valid = idx < own[:, None]

    nc = S // MOBA_Q_CHUNK
    C = MOBA_Q_CHUNK
    qc = q.reshape(B, H, nc, C, dh).transpose(2, 0, 1, 3, 4)
    idxc = idx.reshape(B, H, nc, C, k_sel).transpose(2, 0, 1, 3, 4)
    validc = valid.reshape(B, H, nc, C, k_sel).transpose(2, 0, 1, 3, 4)
    table = rel_table.T
    bi = jnp.arange(B)[:, None, None, None]
    hi = jnp.arange(H)[None, :, None, None]
    offs = jnp.arange(MOBA_BLOCK)
    n_sel = k_sel * MOBA_BLOCK

    def chunk(args):
        qi, ii, vi, c = args
        start = c * C
        qpos = start + jnp.arange(C)
        ob = start // MOBA_BLOCK
        ks = kb[bi, hi, ii].reshape(B, H, C, n_sel, dh)
        vs = vb[bi, hi, ii].reshape(B, H, C, n_sel, dh)
        kpos_sel = (ii[..., None] * MOBA_BLOCK + offs).reshape(B, H, C, n_sel)
        s_sel = jnp.einsum('bhqd,bhqkd->bhqk', qi, ks).astype(jnp.float32) * scale
        s_sel = s_sel + table[hi, _rel_bucket(qpos[:, None] - kpos_sel)]
        s_sel = jnp.where(jnp.repeat(vi, MOBA_BLOCK, axis=-1), s_sel, -jnp.inf)
        ko = lax.dynamic_index_in_dim(kb, ob, axis=2, keepdims=False)
        vo = lax.dynamic_index_in_dim(vb, ob, axis=2, keepdims=False)
        d_own = qpos[:, None] - (ob * MOBA_BLOCK + offs)[None, :]
        s_own = jnp.einsum('bhqd,bhkd->bhqk', qi, ko).astype(jnp.float32) * scale
        s_own = jnp.where(d_own >= 0, s_own + table[:, _rel_bucket(d_own)], -jnp.inf)
        p = jax.nn.softmax(jnp.concatenate([s_sel, s_own], axis=-1), axis=-1).astype(v.dtype)
        return (jnp.einsum('bhqk,bhqkd->bhqd', p[..., :n_sel], vs)
                + jnp.einsum('bhqk,bhkd->bhqd', p[..., n_sel:], vo))

    out = lax.map(chunk, (qc, idxc, validc, jnp.arange(nc)))
    return out.transpose(1, 2, 0, 3, 4).reshape(B, H, S, dh)


def swa_attention(q, k, v, sinks, rel_table):
    B, S, Hq, dh = q.shape
    W = SWA_WINDOW
    nb = S // W
    scale = dh ** -0.5
    qb = q.reshape(B, nb, W, SWA_KV_HEADS, SWA_GROUP, dh)
    kb = k.reshape(B, nb, W, SWA_KV_HEADS, dh)
    vb = v.reshape(B, nb, W, SWA_KV_HEADS, dh)
    prev = ((0, 0), (1, 0), (0, 0), (0, 0), (0, 0))
    kw = jnp.concatenate([jnp.pad(kb, prev)[:, :-1], kb], axis=2)
    vw = jnp.concatenate([jnp.pad(vb, prev)[:, :-1], vb], axis=2)
    s = jnp.einsum('bnqhgd,bnchd->bnhgqc', qb, kw).astype(jnp.float32) * scale
    ci = jnp.arange(2 * W)[None, :]
    dist = jnp.arange(W)[:, None] + W - ci
    bias = rel_table[_rel_bucket(dist)].astype(jnp.float32)
    bias = bias.transpose(2, 0, 1).reshape(SWA_KV_HEADS, SWA_GROUP, W, 2 * W)
    in_band = (dist >= 0) & (dist < W)
    key_exists = (jnp.arange(nb)[:, None] * W - W + ci) >= 0
    mask = in_band[None] & key_exists[:, None, :]
    s = jnp.where(mask[None, :, None, None], s + bias, -jnp.inf)
    sink = sinks.astype(jnp.float32).reshape(SWA_KV_HEADS, SWA_GROUP)[None, None, :, :, None, None]
    m = jnp.maximum(s.max(axis=-1, keepdims=True), sink)
    e = jnp.exp(s - m)
    p = (e / (e.sum(axis=-1, keepdims=True) + jnp.exp(sink - m))).astype(v.dtype)
    out = jnp.einsum('bnhgqc,bnchd->bnqhgd', p, vw)
    return out.reshape(B, S, Hq, dh)


def hybrid_layer(x, norm_w, w_in, w_proj_a, w_proj_b, w_proj_c, w_out, sinks, rel_bias):
    B, S, _ = x.shape
    h = rms_norm(x, norm_w)
    u = jnp.einsum('bsd,de->bse', h, w_in)
    (qa, ka, va, ga, qb, kb, vb, gb, qc, kc, vc, gc, ma, mb, mc) = _split_columns(u)

    def bhsd(t, n):
        return t.reshape(B, S, n, HEAD_DIM).transpose(0, 2, 1, 3)

    ya = stick_breaking_attention(bhsd(qa, SB_HEADS), bhsd(ka, SB_HEADS), bhsd(va, SB_HEADS))
    ya = ya.transpose(0, 2, 1, 3).reshape(B, S, SB_WIDTH)
    yb = moba_attention(bhsd(qb, MOBA_HEADS), bhsd(kb, MOBA_HEADS), bhsd(vb, MOBA_HEADS),
                        rel_bias[:, :MOBA_HEADS])
    yb = yb.transpose(0, 2, 1, 3).reshape(B, S, MOBA_WIDTH)
    yc = swa_attention(qc.reshape(B, S, SWA_HEADS, HEAD_DIM),
                       kc.reshape(B, S, SWA_KV_HEADS, HEAD_DIM),
                       vc.reshape(B, S, SWA_KV_HEADS, HEAD_DIM),
                       sinks, rel_bias[:, MOBA_HEADS:])
    yc = yc.reshape(B, S, SWA_WIDTH)

    ya = jnp.einsum('bse,ed->bsd', ya * jax.nn.silu(ga), w_proj_a)
    yb = jnp.einsum('bse,ed->bsd', yb * jax.nn.silu(gb), w_proj_b)
    yc = jnp.einsum('bse,ed->bsd', yc * jax.nn.silu(gc), w_proj_c)
    merged = jax.nn.sigmoid(ma) * ya + jax.nn.sigmoid(mb) * yb + jax.nn.sigmoid(mc) * yc
    return x + jnp.einsum('bsd,de->bse', merged, w_out)


def setup_inputs(seed: int = 0) -> dict:
    key = jax.random.key(seed)
    ks = jax.random.split(key, 10)
    f32 = jnp.float32
    x = jax.random.normal(ks[0], (BATCH, SEQ, D_MODEL), f32)
    norm_w = 1.0 + 0.02 * jax.random.normal(ks[1], (DEPTH, D_MODEL), f32)
    w_in = jax.random.normal(ks[2], (DEPTH, D_MODEL, D_IN), f32) * D_MODEL ** -0.5
    w_proj_a = jax.random.normal(ks[3], (DEPTH, SB_WIDTH, D_MODEL), f32) * SB_WIDTH ** -0.5
    w_proj_b = jax.random.normal(ks[4], (DEPTH, MOBA_WIDTH, D_MODEL), f32) * MOBA_WIDTH ** -0.5
    w_proj_c = jax.random.normal(ks[5], (DEPTH, SWA_WIDTH, D_MODEL), f32) * SWA_WIDTH ** -0.5
    w_out = jax.random.normal(ks[6], (DEPTH, D_MODEL, D_MODEL), f32) * D_MODEL ** -0.5
    sinks = 0.5 * jax.random.normal(ks[7], (DEPTH, SWA_HEADS), f32)
    rel_bias = 0.5 * jax.random.normal(ks[8], (REL_BUCKETS, REL_HEADS), f32)
    final_norm_w = 1.0 + 0.02 * jax.random.normal(ks[9], (D_MODEL,), f32)
    return {"x": x, "norm_w": norm_w, "w_in": w_in, "w_proj_a": w_proj_a,
            "w_proj_b": w_proj_b, "w_proj_c": w_proj_c, "w_out": w_out,
            "sinks": sinks, "rel_bias": rel_bias, "final_norm_w": final_norm_w}


def reference(x, norm_w, w_in, w_proj_a, w_proj_b, w_proj_c, w_out, sinks, rel_bias, final_norm_w):
    for layer in range(DEPTH):
        x = hybrid_layer(x, norm_w[layer], w_in[layer], w_proj_a[layer], w_proj_b[layer],
                         w_proj_c[layer], w_out[layer], sinks[layer], rel_bias)
    return rms_norm(x, final_norm_w)
```

```python
import functools
import math

import numpy as np
import jax
import jax.numpy as jnp
from jax import lax
from jax.experimental import pallas as pl
from jax.experimental.pallas import tpu as pltpu

D_MODEL = 1024
HEAD_DIM = 64
N_HEADS = 8
WIDTH = N_HEADS * HEAD_DIM
MOBA_BLOCK = 256
MOBA_TOPK = 3
SWA_KV_HEADS = 2
SWA_GROUP = N_HEADS // SWA_KV_HEADS
SWA_WINDOW = 128
REL_BUCKETS = 32
REL_MAX_DIST = 128
RMS_EPS = 1e-6

LANES = 128
HEADS_PER_TILE = LANES // HEAD_DIM
N_PAIRS = N_HEADS // HEADS_PER_TILE

COL_QA, COL_KA, COL_VA = 0, 512, 1024
COL_QB, COL_KB, COL_VB = 1536, 2048, 2560
COL_QC = 3072
COL_GA, COL_GB, COL_GC = 3584, 4096, 4608
COL_MA, COL_MB, COL_MC = 5120, 6144, 7168
COL_KC, COL_VC = 8192, 8320
D_IN = 8448

VMEM_LIMIT = 48 * 1024 * 1024

SB_DEAD_LOG_WEIGHT = -120.0

_NEG_INF = float("-inf")


def _rel_bucket(dist):
    max_exact = REL_BUCKETS // 2
    n = jnp.maximum(dist, 0)
    nf = jnp.maximum(n, 1).astype(jnp.float32)
    large = max_exact + (jnp.log(nf / max_exact) / math.log(REL_MAX_DIST / max_exact)
                         * (REL_BUCKETS - max_exact)).astype(jnp.int32)
    large = jnp.minimum(large, REL_BUCKETS - 1)
    return jnp.where(n < max_exact, n, large)


def _swa_head_perm():
    return np.array([half * SWA_GROUP + j for j in range(SWA_GROUP) for half in range(SWA_KV_HEADS)])


def _permute_input_columns(w):
    old = {"qa": 0, "ka": 512, "va": 1024, "ga": 1536, "qb": 2048, "kb": 2560, "vb": 3072, "gb": 3584,
           "qc": 4096, "kc": 4608, "vc": 4736, "gc": 4864, "ma": 5376, "mb": 6400, "mc": 7424}
    q_scale = HEAD_DIM ** -0.5

    def cols(name, width, scale=None):
        part = w[:, old[name]:old[name] + width]
        return part if scale is None else part * scale

    def swa_heads(name, scale=None):
        part = cols(name, WIDTH, scale)
        return jnp.concatenate([part[:, h * HEAD_DIM:(h + 1) * HEAD_DIM] for h in _swa_head_perm()], axis=1)

    parts = [cols("qa", 512, q_scale), cols("ka", 512), cols("va", 512),
             cols("qb", 512, q_scale), cols("kb", 512), cols("vb", 512),
             swa_heads("qc", q_scale),
             cols("ga", 512), cols("gb", 512), swa_heads("gc"),
             cols("ma", 1024), cols("mb", 1024), cols("mc", 1024),
             cols("kc", 128), cols("vc", 128)]
    return jnp.concatenate(parts, axis=1).astype(jnp.bfloat16)


def _inproj_kernel(x_ref, nw_ref, w_ref, u_ref, h_ref):
    @pl.when(pl.program_id(1) == 0)
    def _():
        x = x_ref[...]
        y = x * lax.rsqrt(jnp.mean(x * x, axis=-1, keepdims=True) + RMS_EPS)
        h_ref[...] = (y * nw_ref[...]).astype(h_ref.dtype)

    u_ref[...] = jnp.dot(h_ref[...], w_ref[...], preferred_element_type=jnp.float32).astype(u_ref.dtype)


def _inproj(x2, norm_w, w, *, tm=1024, tn=1408):
    n = x2.shape[0]
    return pl.pallas_call(
        _inproj_kernel,
        grid=(n // tm, D_IN // tn),
        in_specs=[pl.BlockSpec((tm, D_MODEL), lambda i, j: (i, 0)),
                  pl.BlockSpec((1, D_MODEL), lambda i, j: (0, 0)),
                  pl.BlockSpec((D_MODEL, tn), lambda i, j: (0, j))],
        out_specs=pl.BlockSpec((tm, tn), lambda i, j: (i, j)),
        out_shape=jax.ShapeDtypeStruct((n, D_IN), jnp.bfloat16),
        scratch_shapes=[pltpu.VMEM((tm, D_MODEL), jnp.bfloat16)],
        compiler_params=pltpu.CompilerParams(dimension_semantics=("parallel", "arbitrary"),
                                             vmem_limit_bytes=VMEM_LIMIT),
        name="inproj",
    )(x2, norm_w.reshape(1, D_MODEL), w)


def _head_lane_mask(hh, dtype):
    lane = lax.broadcasted_iota(jnp.int32, (1, LANES), 1)
    return ((lane // HEAD_DIM) == hh).astype(dtype)


def _qk(qm, k):
    return lax.dot_general(qm, k, (((1,), (1,)), ((), ())), preferred_element_type=jnp.float32)


def _merge_heads(parts):
    lane = lax.broadcasted_iota(jnp.int32, parts[0].shape, 1)
    return jnp.where(lane < HEAD_DIM, parts[0], parts[1])


SB_TILE = 256


def _sb_kernel(q_ref, k_ref, v_ref, tri_ref, o_ref, acc_ref, carry_ref):
    t = SB_TILE
    qi = pl.program_id(2)
    row = lax.broadcasted_iota(jnp.int32, (t, t), 0)
    col = lax.broadcasted_iota(jnp.int32, (t, t), 1)
    past_diag = col < row
    tri = tri_ref[...]

    def tile(qm, blk, hh, mask):
        start = pl.multiple_of(blk * t, t)
        z = _qk(qm, k_ref[pl.ds(start, t), :])
        soft = jnp.log(1.0 + jnp.exp(-jnp.abs(z)))
        log_beta = jnp.minimum(z, 0.0) - soft
        log_1m = log_beta - z
        if mask is not None:
            log_1m = jnp.where(mask, log_1m, 0.0)
        hi = log_1m.astype(jnp.bfloat16)
        lo = (log_1m - hi.astype(jnp.float32)).astype(jnp.bfloat16)
        cum = (jnp.dot(hi, tri, preferred_element_type=jnp.float32)
               + jnp.dot(lo, tri, preferred_element_type=jnp.float32))
        carry = carry_ref[hh]
        between = cum[:, :t] + jnp.concatenate([carry] * (t // LANES), axis=1)
        w = jnp.exp(log_beta + between)
        if mask is not None:
            w = jnp.where(mask, w, 0.0)
        pv = jnp.dot(w.astype(jnp.bfloat16), v_ref[pl.ds(start, t), :], preferred_element_type=jnp.float32)
        acc_ref[hh] = acc_ref[hh] + pv
        new_carry = carry + cum[:, t:]
        carry_ref[hh] = new_carry
        return (jnp.max(new_carry) < SB_DEAD_LOG_WEIGHT).astype(jnp.int32)

    outs = []
    for hh in range(HEADS_PER_TILE):
        qm = q_ref[...] * _head_lane_mask(hh, q_ref.dtype)
        acc_ref[hh] = jnp.zeros((t, LANES), jnp.float32)
        carry_ref[hh] = jnp.zeros((t, LANES), jnp.float32)
        dead = tile(qm, qi, hh, past_diag)

        def cond(c):
            kj, dead = c
            return jnp.logical_and(kj <= qi, dead == 0)

        def body(c, qm=qm, hh=hh):
            kj, _ = c
            return kj + 1, tile(qm, qi - kj, hh, None)

        lax.while_loop(cond, body, (jnp.int32(1), dead))
        outs.append(acc_ref[hh])
    o_ref[...] = _merge_heads(outs)


def _sb_attention(u, tri, batch, seq):
    t = SB_TILE
    nq = seq // t
    return pl.pallas_call(
        _sb_kernel,
        grid=(batch, N_PAIRS, nq),
        in_specs=[pl.BlockSpec((t, LANES), lambda b, p, i: (b * nq + i, COL_QA // LANES + p)),
                  pl.BlockSpec((seq, LANES), lambda b, p, i: (b, COL_KA // LANES + p)),
                  pl.BlockSpec((seq, LANES), lambda b, p, i: (b, COL_VA // LANES + p)),
                  pl.BlockSpec((t, t + LANES), lambda b, p, i: (0, 0))],
        out_specs=pl.BlockSpec((t, LANES), lambda b, p, i: (b * nq + i, p)),
        out_shape=jax.ShapeDtypeStruct((batch * seq, WIDTH), jnp.float32),
        scratch_shapes=[pltpu.VMEM((HEADS_PER_TILE, t, LANES), jnp.float32),
                        pltpu.VMEM((HEADS_PER_TILE, t, LANES), jnp.float32)],
        compiler_params=pltpu.CompilerParams(dimension_semantics=("parallel", "parallel", "arbitrary"),
                                             vmem_limit_bytes=VMEM_LIMIT),
        name="sb_attention",
    )(u, u, u, tri)


def _moba_kernel(far_ref, q_ref, k_ref, v_ref, btab_ref, o_ref, kmean_ref, sel_ref, m_ref, l_ref, acc_ref, *, nb):
    t = MOBA_BLOCK
    pair = pl.program_id(1)
    own = pl.program_id(2)
    lane = lax.broadcasted_iota(jnp.int32, (t, LANES), 1)
    lane_f = lane.astype(jnp.float32)

    @pl.when(own == 0)
    def _():
        kmean_ref[...] = jnp.zeros_like(kmean_ref)
        kf = k_ref[...].astype(jnp.float32).reshape(nb, t, LANES)
        kmean_ref[0:nb, :] = jnp.mean(kf, axis=1)

    km = kmean_ref[...]
    km0 = km.astype(jnp.bfloat16)
    r1 = km - km0.astype(jnp.float32)
    km1 = r1.astype(jnp.bfloat16)
    km2 = (r1 - km1.astype(jnp.float32)).astype(jnp.bfloat16)

    def softmax_step(s, start, hh, first):
        v = v_ref[pl.ds(start, t), :]
        row_max = jnp.max(s, axis=1, keepdims=True)
        if first:
            m_new = jnp.broadcast_to(row_max, (t, LANES))
            p = jnp.exp(s - row_max)
            l_ref[...] = jnp.broadcast_to(jnp.sum(p, axis=1, keepdims=True), (t, LANES))
            acc_ref[...] = jnp.dot(p.astype(jnp.bfloat16), v, preferred_element_type=jnp.float32)
        else:
            m_old = m_ref[...]
            m_new = jnp.maximum(m_old, row_max)
            alpha = jnp.exp(m_old - m_new)
            p = jnp.exp(s - m_new[:, 0:1])
            l_ref[...] = alpha * l_ref[...] + jnp.sum(p, axis=1, keepdims=True)
            acc_ref[...] = alpha * acc_ref[...] + jnp.dot(p.astype(jnp.bfloat16), v,
                                                          preferred_element_type=jnp.float32)
        m_ref[...] = m_new

    def past_block(qm, blk, hh, bias):
        start = pl.multiple_of(blk * t, t)
        onehot = (lax.broadcasted_iota(jnp.int32, (LANES, LANES), 0) == blk).astype(jnp.bfloat16)
        picked = jnp.dot(sel_ref[...], onehot, preferred_element_type=jnp.float32) > 0.5
        picked = jnp.concatenate([picked] * (t // LANES), axis=1)
        s = _qk(qm, k_ref[pl.ds(start, t), :]) + bias
        softmax_step(jnp.where(picked, s, _NEG_INF), start, hh, first=False)

    outs = []
    for hh in range(HEADS_PER_TILE):
        qm = q_ref[...] * _head_lane_mask(hh, q_ref.dtype)

        gate = _qk(qm, km0) + _qk(qm, km1) + _qk(qm, km2)
        gate = jnp.where(lane < own, gate, _NEG_INF)
        sel = jnp.zeros((t, LANES), jnp.float32)
        for _ in range(MOBA_TOPK):
            best = jnp.max(gate, axis=1, keepdims=True)
            idx = jnp.min(jnp.where(gate == best, lane_f, float(LANES)), axis=1, keepdims=True)
            hit = lane_f == idx
            sel = jnp.where(jnp.logical_and(hit, best > _NEG_INF), 1.0, sel)
            gate = jnp.where(hit, _NEG_INF, gate)
        sel_ref[...] = sel.astype(jnp.bfloat16)

        own_start = pl.multiple_of(own * t, t)
        s_own = _qk(qm, k_ref[pl.ds(own_start, t), :]) + btab_ref[hh, 0]
        softmax_step(s_own, own_start, hh, first=True)

        far_bias = far_ref[pair * HEADS_PER_TILE + hh]

        def far_body(blk, c, qm=qm, hh=hh, far_bias=far_bias):
            past_block(qm, blk, hh, far_bias)
            return c

        lax.fori_loop(0, own - 1, far_body, 0)

        @pl.when(own >= 1)
        def _(qm=qm, hh=hh):
            past_block(qm, own - 1, hh, btab_ref[hh, 1])

        outs.append(acc_ref[...] / l_ref[...])
    o_ref[...] = _merge_heads(outs)


def _moba_attention(u, far_bias, btab, batch, seq):
    t = MOBA_BLOCK
    nb = seq // t
    kernel = functools.partial(_moba_kernel, nb=nb)
    grid_spec = pltpu.PrefetchScalarGridSpec(
        num_scalar_prefetch=1,
        grid=(batch, N_PAIRS, nb),
        in_specs=[pl.BlockSpec((t, LANES), lambda b, p, i, far: (b * nb + i, COL_QB // LANES + p)),
                  pl.BlockSpec((seq, LANES), lambda b, p, i, far: (b, COL_KB // LANES + p)),
                  pl.BlockSpec((seq, LANES), lambda b, p, i, far: (b, COL_VB // LANES + p)),
                  pl.BlockSpec((HEADS_PER_TILE, 2, t, t), lambda b, p, i, far: (p, 0, 0, 0))],
        out_specs=pl.BlockSpec((t, LANES), lambda b, p, i, far: (b * nb + i, p)),
        scratch_shapes=[pltpu.VMEM((LANES, LANES), jnp.float32),
                        pltpu.VMEM((t, LANES), jnp.bfloat16),
                        pltpu.VMEM((t, LANES), jnp.float32),
                        pltpu.VMEM((t, LANES), jnp.float32),
                        pltpu.VMEM((t, LANES), jnp.float32)])
    return pl.pallas_call(
        kernel,
        grid_spec=grid_spec,
        out_shape=jax.ShapeDtypeStruct((batch * seq, WIDTH), jnp.float32),
        compiler_params=pltpu.CompilerParams(dimension_semantics=("parallel", "parallel", "arbitrary"),
                                             vmem_limit_bytes=VMEM_LIMIT),
        name="moba_attention",
    )(far_bias, u, u, u, btab)


def _swa_kernel(sink_ref, q_ref, kp_ref, kc_ref, vp_ref, vc_ref, bias_ref, o_ref):
    w = SWA_WINDOW
    n = pl.program_id(1)
    kw = jnp.concatenate([kp_ref[...], kc_ref[...]], axis=0)
    vw = jnp.concatenate([vp_ref[...], vc_ref[...]], axis=0)
    col = lax.broadcasted_iota(jnp.int32, (w, 2 * w), 1)
    key_exists = jnp.logical_or(col >= w, n > 0)
    for j in range(SWA_GROUP):
        q = q_ref[:, j * LANES:(j + 1) * LANES]
        outs = []
        for half in range(SWA_KV_HEADS):
            slot = j * SWA_KV_HEADS + half
            s = _qk(q * _head_lane_mask(half, q.dtype), kw) + bias_ref[slot]
            s = jnp.where(key_exists, s, _NEG_INF)
            sink = sink_ref[slot]
            m = jnp.maximum(jnp.max(s, axis=1, keepdims=True), sink)
            e = jnp.exp(s - m)
            p = e / (jnp.sum(e, axis=1, keepdims=True) + jnp.exp(sink - m))
            outs.append(jnp.dot(p.astype(jnp.bfloat16), vw, preferred_element_type=jnp.float32))
        o_ref[:, j * LANES:(j + 1) * LANES] = _merge_heads(outs)


def _swa_attention(u, sinks_perm, bias, batch, seq):
    w = SWA_WINDOW
    nblk = seq // w
    kcol, vcol = COL_KC // LANES, COL_VC // LANES
    grid_spec = pltpu.PrefetchScalarGridSpec(
        num_scalar_prefetch=1,
        grid=(batch, nblk),
        in_specs=[pl.BlockSpec((w, WIDTH), lambda b, n, s: (b * nblk + n, COL_QC // WIDTH)),
                  pl.BlockSpec((w, LANES), lambda b, n, s: (b * nblk + jnp.maximum(n - 1, 0), kcol)),
                  pl.BlockSpec((w, LANES), lambda b, n, s: (b * nblk + n, kcol)),
                  pl.BlockSpec((w, LANES), lambda b, n, s: (b * nblk + jnp.maximum(n - 1, 0), vcol)),
                  pl.BlockSpec((w, LANES), lambda b, n, s: (b * nblk + n, vcol)),
                  pl.BlockSpec((N_HEADS, w, 2 * w), lambda b, n, s: (0, 0, 0))],
        out_specs=pl.BlockSpec((w, WIDTH), lambda b, n, s: (b * nblk + n, 0)))
    return pl.pallas_call(
        _swa_kernel,
        grid_spec=grid_spec,
        out_shape=jax.ShapeDtypeStruct((batch * seq, WIDTH), jnp.float32),
        compiler_params=pltpu.CompilerParams(dimension_semantics=("parallel", "arbitrary"),
                                             vmem_limit_bytes=VMEM_LIMIT),
        name="swa_attention",
    )(sinks_perm, u, u, u, u, u, bias)


def _post_kernel(x_ref, ya_ref, yb_ref, yc_ref, ga_ref, gb_ref, gc_ref, ma_ref, mb_ref, mc_ref,
                 wa_ref, wb_ref, wc_ref, wo_ref, fw_ref, o_ref, *, final):
    def branch(y_ref, g_ref, w_ref, m_ref):
        g = g_ref[...].astype(jnp.float32)
        y = (y_ref[...] * (g * jax.nn.sigmoid(g))).astype(jnp.bfloat16)
        proj = jnp.dot(y, w_ref[...], preferred_element_type=jnp.float32)
        return jax.nn.sigmoid(m_ref[...].astype(jnp.float32)) * proj

    merged = (branch(ya_ref, ga_ref, wa_ref, ma_ref) + branch(yb_ref, gb_ref, wb_ref, mb_ref)
              + branch(yc_ref, gc_ref, wc_ref, mc_ref))
    out = x_ref[...] + jnp.dot(merged.astype(jnp.bfloat16), wo_ref[...], preferred_element_type=jnp.float32)
    if final:
        out = out * lax.rsqrt(jnp.mean(out * out, axis=-1, keepdims=True) + RMS_EPS) * fw_ref[...]
    o_ref[...] = out


def _post(x2, ya, yb, yc, u, wa, wb, wc, wo, fw, *, final, tm=256):
    n = x2.shape[0]
    row = lambda i: (i, 0)
    full = lambda i: (0, 0)
    y_spec = pl.BlockSpec((tm, WIDTH), row)
    in_specs = [pl.BlockSpec((tm, D_MODEL), row), y_spec, y_spec, y_spec,
                pl.BlockSpec((tm, WIDTH), lambda i: (i, COL_GA // WIDTH)),
                pl.BlockSpec((tm, WIDTH), lambda i: (i, COL_GB // WIDTH)),
                pl.BlockSpec((tm, WIDTH), lambda i: (i, COL_GC // WIDTH)),
                pl.BlockSpec((tm, D_MODEL), lambda i: (i, COL_MA // D_MODEL)),
                pl.BlockSpec((tm, D_MODEL), lambda i: (i, COL_MB // D_MODEL)),
                pl.BlockSpec((tm, D_MODEL), lambda i: (i, COL_MC // D_MODEL)),
                pl.BlockSpec((WIDTH, D_MODEL), full), pl.BlockSpec((WIDTH, D_MODEL), full),
                pl.BlockSpec((WIDTH, D_MODEL), full), pl.BlockSpec((D_MODEL, D_MODEL), full),
                pl.BlockSpec((1, D_MODEL), full)]
    return pl.pallas_call(
        functools.partial(_post_kernel, final=final),
        grid=(n // tm,),
        in_specs=in_specs,
        out_specs=pl.BlockSpec((tm, D_MODEL), row),
        out_shape=jax.ShapeDtypeStruct((n, D_MODEL), jnp.float32),
        compiler_params=pltpu.CompilerParams(dimension_semantics=("parallel",),
                                             vmem_limit_bytes=VMEM_LIMIT),
        name="post_final" if final else "post",
    )(x2, ya, yb, yc, u, u, u, u, u, u, wa, wb, wc, wo, fw.reshape(1, D_MODEL))


def _bias_tables(rel_bias):
    moba_tab = rel_bias[:, :N_HEADS]
    swa_tab = jnp.stack([rel_bias[:, N_HEADS + h] for h in _swa_head_perm()], axis=1)
    t = MOBA_BLOCK
    r = jnp.arange(t)[:, None]
    c = jnp.arange(t)[None, :]
    d_own = r - c
    own = jnp.where(d_own[None] >= 0, moba_tab[_rel_bucket(d_own)].transpose(2, 0, 1), _NEG_INF)
    prev = moba_tab[_rel_bucket(d_own + t)].transpose(2, 0, 1)
    btab = jnp.stack([own, prev], axis=1).astype(jnp.float32)
    far = moba_tab[_rel_bucket(jnp.int32(t + 1))].astype(jnp.float32)

    w = SWA_WINDOW
    dist = jnp.arange(w)[:, None] + w - jnp.arange(2 * w)[None, :]
    in_band = (dist >= 0) & (dist < w)
    swa = jnp.where(in_band[None], swa_tab[_rel_bucket(dist)].transpose(2, 0, 1), _NEG_INF).astype(jnp.float32)
    return btab, far, swa


def kernel(x, norm_w, w_in, w_proj_a, w_proj_b, w_proj_c, w_out, sinks, rel_bias, final_norm_w):
    batch, seq, _ = x.shape
    depth = w_in.shape[0]
    assert seq % MOBA_BLOCK == 0 and seq // MOBA_BLOCK <= LANES
    btab, far, swa_bias = _bias_tables(rel_bias)
    t = SB_TILE
    tri = jnp.concatenate([(jnp.arange(t)[:, None] > jnp.arange(t)[None, :]),
                           jnp.ones((t, LANES), bool)], axis=1).astype(jnp.bfloat16)
    head_perm = _swa_head_perm()
    sinks_perm = jnp.stack([sinks[:, h] for h in head_perm], axis=1).astype(jnp.float32)

    x2 = x.reshape(batch * seq, D_MODEL)
    for layer in range(depth):
        u = _inproj(x2, norm_w[layer], _permute_input_columns(w_in[layer]))
        ya = _sb_attention(u, tri, batch, seq)
        yb = _moba_attention(u, far, btab, batch, seq)
        yc = _swa_attention(u, sinks_perm[layer], swa_bias, batch, seq)
        wc = jnp.concatenate([w_proj_c[layer][h * HEAD_DIM:(h + 1) * HEAD_DIM] for h in head_perm], axis=0)
        x2 = _post(x2, ya, yb, yc, u,
                   w_proj_a[layer].astype(jnp.bfloat16), w_proj_b[layer].astype(jnp.bfloat16),
                   wc.astype(jnp.bfloat16), w_out[layer].astype(jnp.bfloat16),
                   final_norm_w, final=(layer == depth - 1))
    return x2.reshape(batch, seq, D_MODEL)
```

```python
import functools
import math

import numpy as np
import jax
import jax.numpy as jnp
from jax import lax
from jax.experimental import pallas as pl
from jax.experimental.pallas import tpu as pltpu

D_MODEL = 1024
HEAD_DIM = 64
N_HEADS = 8
WIDTH = N_HEADS * HEAD_DIM
MOBA_BLOCK = 256
MOBA_TOPK = 3
SWA_KV_HEADS = 2
SWA_GROUP = N_HEADS // SWA_KV_HEADS
SWA_WINDOW = 128
REL_BUCKETS = 32
REL_MAX_DIST = 128
RMS_EPS = 1e-6

LANES = 128
BF16_SUBLANES = 16
HEADS_PER_TILE = LANES // HEAD_DIM
N_PAIRS = N_HEADS // HEADS_PER_TILE

COL_QA, COL_KA, COL_VA = 0, 512, 1024
COL_QB, COL_KB, COL_VB = 1536, 2048, 2560
COL_QC = 3072
COL_GA, COL_GB, COL_GC = 3584, 4096, 4608
COL_MA, COL_MB, COL_MC = 5120, 6144, 7168
COL_KC, COL_VC = 8192, 8320
D_IN = 8448

VMEM_LIMIT = 48 * 1024 * 1024

SB_DEAD_LOG_WEIGHT = -120.0

_NEG_INF = float("-inf")


def _rel_bucket(dist):
    max_exact = REL_BUCKETS // 2
    n = jnp.maximum(dist, 0)
    nf = jnp.maximum(n, 1).astype(jnp.float32)
    large = max_exact + (jnp.log(nf / max_exact) / math.log(REL_MAX_DIST / max_exact)
                         * (REL_BUCKETS - max_exact)).astype(jnp.int32)
    large = jnp.minimum(large, REL_BUCKETS - 1)
    return jnp.where(n < max_exact, n, large)


def _swa_head_perm():
    return [half * SWA_GROUP + j for j in range(SWA_GROUP) for half in range(SWA_KV_HEADS)]


def _permute_input_columns(w):
    old = {"qa": 0, "ka": 512, "va": 1024, "ga": 1536, "qb": 2048, "kb": 2560, "vb": 3072, "gb": 3584,
           "qc": 4096, "kc": 4608, "vc": 4736, "gc": 4864, "ma": 5376, "mb": 6400, "mc": 7424}
    q_scale = HEAD_DIM ** -0.5

    def cols(name, width, scale=None):
        part = w[:, old[name]:old[name] + width]
        return part if scale is None else part * scale

    def swa_heads(name, scale=None):
        part = cols(name, WIDTH, scale)
        return jnp.concatenate([part[:, h * HEAD_DIM:(h + 1) * HEAD_DIM] for h in _swa_head_perm()], axis=1)

    parts = [cols("qa", 512, q_scale), cols("ka", 512), cols("va", 512),
             cols("qb", 512, q_scale), cols("kb", 512), cols("vb", 512),
             swa_heads("qc", q_scale),
             cols("ga", 512), cols("gb", 512), swa_heads("gc"),
             cols("ma", 1024), cols("mb", 1024), cols("mc", 1024),
             cols("kc", 128), cols("vc", 128)]
    return jnp.concatenate(parts, axis=1).astype(jnp.bfloat16)


def _inproj_kernel(x_ref, nw_ref, w_ref, u_ref, h_ref):
    @pl.when(pl.program_id(1) == 0)
    def _():
        x = x_ref[...]
        y = x * lax.rsqrt(jnp.mean(x * x, axis=-1, keepdims=True) + RMS_EPS)
        h_ref[...] = (y * nw_ref[...]).astype(h_ref.dtype)

    u_ref[...] = jnp.dot(h_ref[...], w_ref[...], preferred_element_type=jnp.float32).astype(u_ref.dtype)


def _inproj(x2, norm_w, w, *, tm=1024, tn=1408):
    n = x2.shape[0]
    return pl.pallas_call(
        _inproj_kernel,
        grid=(n // tm, D_IN // tn),
        in_specs=[pl.BlockSpec((tm, D_MODEL), lambda i, j: (i, 0)),
                  pl.BlockSpec((1, D_MODEL), lambda i, j: (0, 0)),
                  pl.BlockSpec((D_MODEL, tn), lambda i, j: (0, j))],
        out_specs=pl.BlockSpec((tm, tn), lambda i, j: (i, j)),
        out_shape=jax.ShapeDtypeStruct((n, D_IN), jnp.bfloat16),
        scratch_shapes=[pltpu.VMEM((tm, D_MODEL), jnp.bfloat16)],
        compiler_params=pltpu.CompilerParams(dimension_semantics=("parallel", "arbitrary"),
                                             vmem_limit_bytes=VMEM_LIMIT),
        name="inproj",
    )(x2, norm_w.reshape(1, D_MODEL), w)


def _head_lane_mask(hh, dtype):
    lane = lax.broadcasted_iota(jnp.int32, (1, LANES), 1)
    return ((lane // HEAD_DIM) == hh).astype(dtype)


def _qk(qm, k):
    return lax.dot_general(qm, k, (((1,), (1,)), ((), ())), preferred_element_type=jnp.float32)


def _merge_heads(parts):
    lane = lax.broadcasted_iota(jnp.int32, parts[0].shape, 1)
    return jnp.where(lane < HEAD_DIM, parts[0], parts[1])


SB_TILE = 256


def _sb_kernel(q_ref, k_ref, v_ref, tri_ref, o_ref, acc_ref, carry_ref):
    t = SB_TILE
    qi = pl.program_id(2)
    row = lax.broadcasted_iota(jnp.int32, (t, t), 0)
    col = lax.broadcasted_iota(jnp.int32, (t, t), 1)
    past_diag = col < row

    def tile(blk, mask):
        start = pl.multiple_of(blk * t, t)
        k = k_ref[pl.ds(start, t), :]
        v = v_ref[pl.ds(start, t), :]
        tri = tri_ref[...]
        alive = []
        for hh in range(HEADS_PER_TILE):
            z = _qk(q_ref[...] * _head_lane_mask(hh, q_ref.dtype), k)
            soft = jnp.log(1.0 + jnp.exp(-jnp.abs(z)))
            log_beta = jnp.minimum(z, 0.0) - soft
            log_1m = log_beta - z
            if mask is not None:
                log_1m = jnp.where(mask, log_1m, 0.0)
            hi = log_1m.astype(jnp.bfloat16)
            lo = (log_1m - hi.astype(jnp.float32)).astype(jnp.bfloat16)
            cum = (jnp.dot(hi, tri, preferred_element_type=jnp.float32)
                   + jnp.dot(lo, tri, preferred_element_type=jnp.float32))
            carry = carry_ref[hh]
            between = cum[:, :t] + jnp.concatenate([carry] * (t // LANES), axis=1)
            w = jnp.exp(log_beta + between)
            if mask is not None:
                w = jnp.where(mask, w, 0.0)
            acc_ref[hh] = acc_ref[hh] + jnp.dot(w.astype(jnp.bfloat16), v, preferred_element_type=jnp.float32)
            new_carry = carry + cum[:, t:]
            carry_ref[hh] = new_carry
            alive.append(jnp.max(new_carry))
        return (jnp.maximum(alive[0], alive[1]) < SB_DEAD_LOG_WEIGHT).astype(jnp.int32)

    acc_ref[...] = jnp.zeros_like(acc_ref)
    carry_ref[...] = jnp.zeros_like(carry_ref)
    dead = tile(qi, past_diag)

    def cond(c):
        kj, dead = c
        return jnp.logical_and(kj <= qi, dead == 0)

    def body(c):
        kj, _ = c
        return kj + 1, tile(qi - kj, None)

    lax.while_loop(cond, body, (jnp.int32(1), dead))
    o_ref[...] = _merge_heads([acc_ref[0], acc_ref[1]])


def _sb_attention(u, tri, batch, seq):
    t = SB_TILE
    nq = seq // t
    return pl.pallas_call(
        _sb_kernel,
        grid=(batch, N_PAIRS, nq),
        in_specs=[pl.BlockSpec((t, LANES), lambda b, p, i: (b * nq + i, COL_QA // LANES + p)),
                  pl.BlockSpec((seq, LANES), lambda b, p, i: (b, COL_KA // LANES + p)),
                  pl.BlockSpec((seq, LANES), lambda b, p, i: (b, COL_VA // LANES + p)),
                  pl.BlockSpec((t, t + LANES), lambda b, p, i: (0, 0))],
        out_specs=pl.BlockSpec((t, LANES), lambda b, p, i: (b * nq + i, p)),
        out_shape=jax.ShapeDtypeStruct((batch * seq, WIDTH), jnp.float32),
        scratch_shapes=[pltpu.VMEM((HEADS_PER_TILE, t, LANES), jnp.float32),
                        pltpu.VMEM((HEADS_PER_TILE, t, LANES), jnp.float32)],
        compiler_params=pltpu.CompilerParams(dimension_semantics=("parallel", "parallel", "arbitrary"),
                                             vmem_limit_bytes=VMEM_LIMIT),
        name="sb_attention",
    )(u, u, u, tri)


MOBA_LO_LANE = 64
MOBA_NEVER = 63
MOBA_MASKED = -1e30


def _moba_kernel(far_ref, q_ref, k_ref, v_ref, btab_ref, oh_ref, o_ref,
                 kmean_ref, qa_ref, m_ref, l_ref, acc_ref, *, nb):
    t = MOBA_BLOCK
    pair = pl.program_id(1)
    own = pl.program_id(2)
    lane = lax.broadcasted_iota(jnp.int32, (t, LANES), 1)
    lane_f = lane.astype(jnp.float32)

    @pl.when(own == 0)
    def _():
        kmean_ref[...] = jnp.zeros_like(kmean_ref)
        kf = k_ref[...].astype(jnp.float32).reshape(nb, t, LANES)
        kmean_ref[0:nb, :] = jnp.mean(kf, axis=1)

    km = kmean_ref[...]
    km0 = km.astype(jnp.bfloat16)
    r1 = km - km0.astype(jnp.float32)
    km1 = r1.astype(jnp.bfloat16)
    km2 = (r1 - km1.astype(jnp.float32)).astype(jnp.bfloat16)

    def update(hh, s, v, first):
        row_max = jnp.broadcast_to(jnp.max(s, axis=1, keepdims=True), (t, LANES))
        if first:
            m_new = row_max
        else:
            m_old = m_ref[hh]
            m_new = jnp.maximum(m_old, row_max)
            alpha = jnp.exp(m_old - m_new)
        p = jnp.exp(s - jnp.concatenate([m_new] * (s.shape[1] // LANES), axis=1))
        row_sum = jnp.broadcast_to(jnp.sum(p, axis=1, keepdims=True), (t, LANES))
        pv = jnp.dot(p.astype(jnp.bfloat16), v, preferred_element_type=jnp.float32)
        if first:
            l_ref[hh] = row_sum
            acc_ref[hh] = pv
        else:
            l_ref[hh] = alpha * l_ref[hh] + row_sum
            acc_ref[hh] = alpha * acc_ref[hh] + pv
        m_ref[hh] = m_new

    def block_lanes(blk):
        return jnp.concatenate([oh_ref[blk]] * (t // BF16_SUBLANES), axis=0)

    own_start = pl.multiple_of(own * t, t)
    k_own = k_ref[pl.ds(own_start, t), :]
    v_own = v_ref[pl.ds(own_start, t), :]
    for hh in range(HEADS_PER_TILE):
        qm = q_ref[...] * _head_lane_mask(hh, q_ref.dtype)

        gate = _qk(qm, km0) + _qk(qm, km1) + _qk(qm, km2)
        gate = jnp.where(lane < own, gate, _NEG_INF)
        sel = jnp.zeros((t, LANES), jnp.float32)
        for _ in range(MOBA_TOPK):
            best = jnp.max(gate, axis=1, keepdims=True)
            idx = jnp.min(jnp.where(gate == best, lane_f, float(LANES)), axis=1, keepdims=True)
            hit = lane_f == idx
            sel = jnp.where(jnp.logical_and(hit, best > _NEG_INF), 1.0, sel)
            gate = jnp.where(hit, _NEG_INF, gate)

        far_bias = far_ref[pair * HEADS_PER_TILE + hh]
        term = jnp.where(sel > 0.0, jnp.where(lane < own - 1, far_bias, 0.0), MOBA_MASKED)
        hi = term.astype(jnp.bfloat16).astype(jnp.float32)
        inj = jnp.where(lane < MOBA_LO_LANE, hi, pltpu.roll(term - hi, MOBA_LO_LANE, axis=1))
        qa_ref[hh] = jnp.concatenate([qm, inj.astype(jnp.bfloat16)], axis=1)

        update(hh, _qk(qm, k_own) + btab_ref[hh, 0], v_own, first=True)

    n_far = jnp.maximum(own - 1, 0)

    def far_body(i, c):
        start = pl.multiple_of(i * (2 * t), 2 * t)
        second = jnp.where(2 * i + 1 < n_far, 2 * i + 1, MOBA_NEVER)
        k_aug = jnp.concatenate(
            [k_ref[pl.ds(start, 2 * t), :],
             jnp.concatenate([block_lanes(2 * i), block_lanes(second)], axis=0)], axis=1)
        v2 = v_ref[pl.ds(start, 2 * t), :]
        for hh in range(HEADS_PER_TILE):
            update(hh, _qk(qa_ref[hh], k_aug), v2, first=False)
        return c

    lax.fori_loop(0, (n_far + 1) // 2, far_body, 0)

    @pl.when(own >= 1)
    def _():
        start = pl.multiple_of((own - 1) * t, t)
        k_aug = jnp.concatenate([k_ref[pl.ds(start, t), :], block_lanes(own - 1)], axis=1)
        v1 = v_ref[pl.ds(start, t), :]
        for hh in range(HEADS_PER_TILE):
            update(hh, _qk(qa_ref[hh], k_aug) + btab_ref[hh, 1], v1, first=False)

    o_ref[...] = _merge_heads([acc_ref[hh] / l_ref[hh] for hh in range(HEADS_PER_TILE)])


def _moba_attention(u, far_bias, btab, block_lanes, batch, seq):
    t = MOBA_BLOCK
    nb = seq // t
    kernel = functools.partial(_moba_kernel, nb=nb)
    grid_spec = pltpu.PrefetchScalarGridSpec(
        num_scalar_prefetch=1,
        grid=(batch, N_PAIRS, nb),
        in_specs=[pl.BlockSpec((t, LANES), lambda b, p, i, far: (b * nb + i, COL_QB // LANES + p)),
                  pl.BlockSpec((seq, LANES), lambda b, p, i, far: (b, COL_KB // LANES + p)),
                  pl.BlockSpec((seq, LANES), lambda b, p, i, far: (b, COL_VB // LANES + p)),
                  pl.BlockSpec((HEADS_PER_TILE, 2, t, t), lambda b, p, i, far: (p, 0, 0, 0)),
                  pl.BlockSpec(block_lanes.shape, lambda b, p, i, far: (0, 0, 0))],
        out_specs=pl.BlockSpec((t, LANES), lambda b, p, i, far: (b * nb + i, p)),
        scratch_shapes=[pltpu.VMEM((LANES, LANES), jnp.float32),
                        pltpu.VMEM((HEADS_PER_TILE, t, 2 * LANES), jnp.bfloat16),
                        pltpu.VMEM((HEADS_PER_TILE, t, LANES), jnp.float32),
                        pltpu.VMEM((HEADS_PER_TILE, t, LANES), jnp.float32),
                        pltpu.VMEM((HEADS_PER_TILE, t, LANES), jnp.float32)])
    return pl.pallas_call(
        kernel,
        grid_spec=grid_spec,
        out_shape=jax.ShapeDtypeStruct((batch * seq, WIDTH), jnp.float32),
        compiler_params=pltpu.CompilerParams(dimension_semantics=("parallel", "parallel", "arbitrary"),
                                             vmem_limit_bytes=VMEM_LIMIT),
        name="moba_attention",
    )(far_bias, u, u, u, btab, block_lanes)


def _swa_kernel(sink_ref, q_ref, kp_ref, kc_ref, vp_ref, vc_ref, bias_ref, o_ref):
    w = SWA_WINDOW
    n = pl.program_id(1)
    kw = jnp.concatenate([kp_ref[...], kc_ref[...]], axis=0)
    vw = jnp.concatenate([vp_ref[...], vc_ref[...]], axis=0)
    col = lax.broadcasted_iota(jnp.int32, (w, 2 * w), 1)
    key_exists = jnp.logical_or(col >= w, n > 0)
    for j in range(SWA_GROUP):
        q = q_ref[:, j * LANES:(j + 1) * LANES]
        outs = []
        for half in range(SWA_KV_HEADS):
            slot = j * SWA_KV_HEADS + half
            s = _qk(q * _head_lane_mask(half, q.dtype), kw) + bias_ref[slot]
            s = jnp.where(key_exists, s, _NEG_INF)
            sink = sink_ref[slot]
            m = jnp.maximum(jnp.max(s, axis=1, keepdims=True), sink)
            e = jnp.exp(s - m)
            p = e / (jnp.sum(e, axis=1, keepdims=True) + jnp.exp(sink - m))
            outs.append(jnp.dot(p.astype(jnp.bfloat16), vw, preferred_element_type=jnp.float32))
        o_ref[:, j * LANES:(j + 1) * LANES] = _merge_heads(outs)


def _swa_attention(u, sinks_perm, bias, batch, seq):
    w = SWA_WINDOW
    nblk = seq // w
    kcol, vcol = COL_KC // LANES, COL_VC // LANES
    grid_spec = pltpu.PrefetchScalarGridSpec(
        num_scalar_prefetch=1,
        grid=(batch, nblk),
        in_specs=[pl.BlockSpec((w, WIDTH), lambda b, n, s: (b * nblk + n, COL_QC // WIDTH)),
                  pl.BlockSpec((w, LANES), lambda b, n, s: (b * nblk + jnp.maximum(n - 1, 0), kcol)),
                  pl.BlockSpec((w, LANES), lambda b, n, s: (b * nblk + n, kcol)),
                  pl.BlockSpec((w, LANES), lambda b, n, s: (b * nblk + jnp.maximum(n - 1, 0), vcol)),
                  pl.BlockSpec((w, LANES), lambda b, n, s: (b * nblk + n, vcol)),
                  pl.BlockSpec((N_HEADS, w, 2 * w), lambda b, n, s: (0, 0, 0))],
        out_specs=pl.BlockSpec((w, WIDTH), lambda b, n, s: (b * nblk + n, 0)))
    return pl.pallas_call(
        _swa_kernel,
        grid_spec=grid_spec,
        out_shape=jax.ShapeDtypeStruct((batch * seq, WIDTH), jnp.float32),
        compiler_params=pltpu.CompilerParams(dimension_semantics=("parallel", "arbitrary"),
                                             vmem_limit_bytes=VMEM_LIMIT),
        name="swa_attention",
    )(sinks_perm, u, u, u, u, u, bias)


def _post_kernel(x_ref, ya_ref, yb_ref, yc_ref, ga_ref, gb_ref, gc_ref, ma_ref, mb_ref, mc_ref,
                 wa_ref, wb_ref, wc_ref, wo_ref, fw_ref, o_ref, *, final):
    def branch(y_ref, g_ref, w_ref, m_ref):
        g = g_ref[...].astype(jnp.float32)
        y = (y_ref[...] * (g * jax.nn.sigmoid(g))).astype(jnp.bfloat16)
        proj = jnp.dot(y, w_ref[...], preferred_element_type=jnp.float32)
        return jax.nn.sigmoid(m_ref[...].astype(jnp.float32)) * proj

    merged = (branch(ya_ref, ga_ref, wa_ref, ma_ref) + branch(yb_ref, gb_ref, wb_ref, mb_ref)
              + branch(yc_ref, gc_ref, wc_ref, mc_ref))
    out = x_ref[...] + jnp.dot(merged.astype(jnp.bfloat16), wo_ref[...], preferred_element_type=jnp.float32)
    if final:
        out = out * lax.rsqrt(jnp.mean(out * out, axis=-1, keepdims=True) + RMS_EPS) * fw_ref[...]
    o_ref[...] = out


def _post(x2, ya, yb, yc, u, wa, wb, wc, wo, fw, *, final, tm=256):
    n = x2.shape[0]
    row = lambda i: (i, 0)
    full = lambda i: (0, 0)
    y_spec = pl.BlockSpec((tm, WIDTH), row)
    in_specs = [pl.BlockSpec((tm, D_MODEL), row), y_spec, y_spec, y_spec,
                pl.BlockSpec((tm, WIDTH), lambda i: (i, COL_GA // WIDTH)),
                pl.BlockSpec((tm, WIDTH), lambda i: (i, COL_GB // WIDTH)),
                pl.BlockSpec((tm, WIDTH), lambda i: (i, COL_GC // WIDTH)),
                pl.BlockSpec((tm, D_MODEL), lambda i: (i, COL_MA // D_MODEL)),
                pl.BlockSpec((tm, D_MODEL), lambda i: (i, COL_MB // D_MODEL)),
                pl.BlockSpec((tm, D_MODEL), lambda i: (i, COL_MC // D_MODEL)),
                pl.BlockSpec((WIDTH, D_MODEL), full), pl.BlockSpec((WIDTH, D_MODEL), full),
                pl.BlockSpec((WIDTH, D_MODEL), full), pl.BlockSpec((D_MODEL, D_MODEL), full),
                pl.BlockSpec((1, D_MODEL), full)]
    return pl.pallas_call(
        functools.partial(_post_kernel, final=final),
        grid=(n // tm,),
        in_specs=in_specs,
        out_specs=pl.BlockSpec((tm, D_MODEL), row),
        out_shape=jax.ShapeDtypeStruct((n, D_MODEL), jnp.float32),
        compiler_params=pltpu.CompilerParams(dimension_semantics=("parallel",),
                                             vmem_limit_bytes=VMEM_LIMIT),
        name="post_final" if final else "post",
    )(x2, ya, yb, yc, u, u, u, u, u, u, wa, wb, wc, wo, fw.reshape(1, D_MODEL))


def _bucket_lookup(table, dist):
    onehot = (_rel_bucket(dist)[..., None] == jnp.arange(REL_BUCKETS)).astype(jnp.float32)
    return jnp.einsum("...b,bh->h...", onehot, table.astype(jnp.float32), precision=lax.Precision.HIGHEST)


def _bias_tables(rel_bias):
    moba_tab = rel_bias[:, :N_HEADS]
    swa_tab = jnp.stack([rel_bias[:, N_HEADS + h] for h in _swa_head_perm()], axis=1)
    t = MOBA_BLOCK
    d_own = jnp.arange(t)[:, None] - jnp.arange(t)[None, :]
    own = jnp.where(d_own[None] >= 0, _bucket_lookup(moba_tab, d_own), _NEG_INF)
    prev = _bucket_lookup(moba_tab, d_own + t)
    btab = jnp.stack([own, prev], axis=1)
    far = _bucket_lookup(moba_tab, jnp.full((1,), t + 1, jnp.int32))[:, 0]

    w = SWA_WINDOW
    dist = jnp.arange(w)[:, None] + w - jnp.arange(2 * w)[None, :]
    in_band = (dist >= 0) & (dist < w)
    swa = jnp.where(in_band[None], _bucket_lookup(swa_tab, dist), _NEG_INF)
    return btab, far, swa


def kernel(x, norm_w, w_in, w_proj_a, w_proj_b, w_proj_c, w_out, sinks, rel_bias, final_norm_w):
    batch, seq, _ = x.shape
    depth = w_in.shape[0]
    assert seq % MOBA_BLOCK == 0 and seq // MOBA_BLOCK <= MOBA_NEVER
    btab, far, swa_bias = _bias_tables(rel_bias)
    t = SB_TILE
    tri = jnp.concatenate([(jnp.arange(t)[:, None] > jnp.arange(t)[None, :]),
                           jnp.ones((t, LANES), bool)], axis=1).astype(jnp.bfloat16)
    lane = jnp.arange(LANES)[None, :]
    blk = jnp.arange(MOBA_LO_LANE)[:, None]
    block_lanes = jnp.broadcast_to(((lane == blk) | (lane == blk + MOBA_LO_LANE))[:, None, :],
                                   (MOBA_LO_LANE, BF16_SUBLANES, LANES)).astype(jnp.bfloat16)
    head_perm = _swa_head_perm()
    sinks_perm = jnp.stack([sinks[:, h] for h in head_perm], axis=1).astype(jnp.float32)

    x2 = x.reshape(batch * seq, D_MODEL)
    for layer in range(depth):
        u = _inproj(x2, norm_w[layer], _permute_input_columns(w_in[layer]))
        ya = _sb_attention(u, tri, batch, seq)
        yb = _moba_attention(u, far, btab, block_lanes, batch, seq)
        yc = _swa_attention(u, sinks_perm[layer], swa_bias, batch, seq)
        wc = jnp.concatenate([w_proj_c[layer][h * HEAD_DIM:(h + 1) * HEAD_DIM] for h in head_perm], axis=0)
        x2 = _post(x2, ya, yb, yc, u,
                   w_proj_a[layer].astype(jnp.bfloat16), w_proj_b[layer].astype(jnp.bfloat16),
                   wc.astype(jnp.bfloat16), w_out[layer].astype(jnp.bfloat16),
                   final_norm_w, final=(layer == depth - 1))
    return x2.reshape(batch, seq, D_MODEL)
```

```python
import functools
import math

import numpy as np
import jax
import jax.numpy as jnp
from jax import lax
from jax.experimental import pallas as pl
from jax.experimental.pallas import tpu as pltpu

D_MODEL = 1024
HEAD_DIM = 64
N_HEADS = 8
WIDTH = N_HEADS * HEAD_DIM
MOBA_BLOCK = 256
MOBA_TOPK = 3
SWA_KV_HEADS = 2
SWA_GROUP = N_HEADS // SWA_KV_HEADS
SWA_WINDOW = 128
REL_BUCKETS = 32
REL_MAX_DIST = 128
RMS_EPS = 1e-6

LANES = 128
BF16_SUBLANES = 16
HEADS_PER_TILE = LANES // HEAD_DIM
N_PAIRS = N_HEADS // HEADS_PER_TILE

COL_QA, COL_KA, COL_VA = 0, 512, 1024
COL_QB, COL_KB, COL_VB = 1536, 2048, 2560
COL_QC = 3072
COL_GA, COL_GB, COL_GC = 3584, 4096, 4608
COL_MA, COL_MB, COL_MC = 5120, 6144, 7168
COL_KC, COL_VC = 8192, 8320
D_IN = 8448

VMEM_LIMIT = 48 * 1024 * 1024

SB_DEAD_LOG_WEIGHT = -120.0

_NEG_INF = float("-inf")


def _rel_bucket(dist):
    max_exact = REL_BUCKETS // 2
    n = jnp.maximum(dist, 0)
    nf = jnp.maximum(n, 1).astype(jnp.float32)
    large = max_exact + (jnp.log(nf / max_exact) / math.log(REL_MAX_DIST / max_exact)
                         * (REL_BUCKETS - max_exact)).astype(jnp.int32)
    large = jnp.minimum(large, REL_BUCKETS - 1)
    return jnp.where(n < max_exact, n, large)


def _swa_head_perm():
    return [half * SWA_GROUP + j for j in range(SWA_GROUP) for half in range(SWA_KV_HEADS)]


def _permute_input_columns(w):
    old = {"qa": 0, "ka": 512, "va": 1024, "ga": 1536, "qb": 2048, "kb": 2560, "vb": 3072, "gb": 3584,
           "qc": 4096, "kc": 4608, "vc": 4736, "gc": 4864, "ma": 5376, "mb": 6400, "mc": 7424}
    q_scale = HEAD_DIM ** -0.5

    def cols(name, width, scale=None):
        part = w[:, old[name]:old[name] + width]
        return part if scale is None else part * scale

    def swa_heads(name, scale=None):
        part = cols(name, WIDTH, scale)
        return jnp.concatenate([part[:, h * HEAD_DIM:(h + 1) * HEAD_DIM] for h in _swa_head_perm()], axis=1)

    parts = [cols("qa", 512, q_scale), cols("ka", 512), cols("va", 512),
             cols("qb", 512, q_scale), cols("kb", 512), cols("vb", 512),
             swa_heads("qc", q_scale),
             cols("ga", 512), cols("gb", 512), swa_heads("gc"),
             cols("ma", 1024), cols("mb", 1024), cols("mc", 1024),
             cols("kc", 128), cols("vc", 128)]
    return jnp.concatenate(parts, axis=1).astype(jnp.bfloat16)


def _inproj_kernel(x_ref, nw_ref, w_ref, u_ref, h_ref):
    @pl.when(pl.program_id(1) == 0)
    def _():
        x = x_ref[...]
        y = x * lax.rsqrt(jnp.mean(x * x, axis=-1, keepdims=True) + RMS_EPS)
        h_ref[...] = (y * nw_ref[...]).astype(h_ref.dtype)

    u_ref[...] = jnp.dot(h_ref[...], w_ref[...], preferred_element_type=jnp.float32).astype(u_ref.dtype)


def _inproj(x2, norm_w, w, *, tm=1024, tn=1408):
    n = x2.shape[0]
    return pl.pallas_call(
        _inproj_kernel,
        grid=(n // tm, D_IN // tn),
        in_specs=[pl.BlockSpec((tm, D_MODEL), lambda i, j: (i, 0)),
                  pl.BlockSpec((1, D_MODEL), lambda i, j: (0, 0)),
                  pl.BlockSpec((D_MODEL, tn), lambda i, j: (0, j))],
        out_specs=pl.BlockSpec((tm, tn), lambda i, j: (i, j)),
        out_shape=jax.ShapeDtypeStruct((n, D_IN), jnp.bfloat16),
        scratch_shapes=[pltpu.VMEM((tm, D_MODEL), jnp.bfloat16)],
        compiler_params=pltpu.CompilerParams(dimension_semantics=("parallel", "arbitrary"),
                                             vmem_limit_bytes=VMEM_LIMIT),
        name="inproj",
    )(x2, norm_w.reshape(1, D_MODEL), w)


def _head_lane_mask(hh, dtype):
    lane = lax.broadcasted_iota(jnp.int32, (1, LANES), 1)
    return ((lane // HEAD_DIM) == hh).astype(dtype)


def _qk(qm, k):
    return lax.dot_general(qm, k, (((1,), (1,)), ((), ())), preferred_element_type=jnp.float32)


def _merge_heads(parts):
    lane = lax.broadcasted_iota(jnp.int32, parts[0].shape, 1)
    return jnp.where(lane < HEAD_DIM, parts[0], parts[1])


SB_TILE = 256


def _sb_kernel(q_ref, k_ref, v_ref, tri_ref, o_ref, acc_ref, carry_ref):
    t = SB_TILE
    qi = pl.program_id(2)
    row = lax.broadcasted_iota(jnp.int32, (t, t), 0)
    col = lax.broadcasted_iota(jnp.int32, (t, t), 1)
    past_diag = col < row

    def tile(blk, mask):
        start = pl.multiple_of(blk * t, t)
        k = k_ref[pl.ds(start, t), :]
        v = v_ref[pl.ds(start, t), :]
        tri = tri_ref[...]
        alive = []
        for hh in range(HEADS_PER_TILE):
            z = _qk(q_ref[...] * _head_lane_mask(hh, q_ref.dtype), k)
            soft = jnp.log(1.0 + jnp.exp(-jnp.abs(z)))
            log_beta = jnp.minimum(z, 0.0) - soft
            log_1m = log_beta - z
            if mask is not None:
                log_1m = jnp.where(mask, log_1m, 0.0)
            hi = log_1m.astype(jnp.bfloat16)
            lo = (log_1m - hi.astype(jnp.float32)).astype(jnp.bfloat16)
            cum = (jnp.dot(hi, tri, preferred_element_type=jnp.float32)
                   + jnp.dot(lo, tri, preferred_element_type=jnp.float32))
            carry = carry_ref[hh]
            between = cum[:, :t] + jnp.concatenate([carry] * (t // LANES), axis=1)
            w = jnp.exp(log_beta + between)
            if mask is not None:
                w = jnp.where(mask, w, 0.0)
            acc_ref[hh] = acc_ref[hh] + jnp.dot(w.astype(jnp.bfloat16), v, preferred_element_type=jnp.float32)
            new_carry = carry + cum[:, t:]
            carry_ref[hh] = new_carry
            alive.append(jnp.max(new_carry))
        return (jnp.maximum(alive[0], alive[1]) < SB_DEAD_LOG_WEIGHT).astype(jnp.int32)

    acc_ref[...] = jnp.zeros_like(acc_ref)
    carry_ref[...] = jnp.zeros_like(carry_ref)
    dead = tile(qi, past_diag)

    def cond(c):
        kj, dead = c
        return jnp.logical_and(kj <= qi, dead == 0)

    def body(c):
        kj, _ = c
        return kj + 1, tile(qi - kj, None)

    lax.while_loop(cond, body, (jnp.int32(1), dead))
    o_ref[...] = _merge_heads([acc_ref[0], acc_ref[1]])


def _sb_attention(u, tri, batch, seq):
    t = SB_TILE
    nq = seq // t
    return pl.pallas_call(
        _sb_kernel,
        grid=(batch, N_PAIRS, nq),
        in_specs=[pl.BlockSpec((t, LANES), lambda b, p, i: (b * nq + i, COL_QA // LANES + p)),
                  pl.BlockSpec((seq, LANES), lambda b, p, i: (b, COL_KA // LANES + p)),
                  pl.BlockSpec((seq, LANES), lambda b, p, i: (b, COL_VA // LANES + p)),
                  pl.BlockSpec((t, t + LANES), lambda b, p, i: (0, 0))],
        out_specs=pl.BlockSpec((t, LANES), lambda b, p, i: (b * nq + i, p)),
        out_shape=jax.ShapeDtypeStruct((batch * seq, WIDTH), jnp.float32),
        scratch_shapes=[pltpu.VMEM((HEADS_PER_TILE, t, LANES), jnp.float32),
                        pltpu.VMEM((HEADS_PER_TILE, t, LANES), jnp.float32)],
        compiler_params=pltpu.CompilerParams(dimension_semantics=("parallel", "parallel", "arbitrary"),
                                             vmem_limit_bytes=VMEM_LIMIT),
        name="sb_attention",
    )(u, u, u, tri)


MOBA_LO_LANE = 64
MOBA_NEVER = 63
MOBA_ZERO_ROW = 64
MOBA_MASKED = -1e30
MOBA_GROUP = 4


def _moba_kernel(far_ref, q_ref, k_ref, v_ref, btab_ref, oh_ref, o_ref,
                 kmean_ref, qa_ref, m_ref, acc_ref, *, nb):
    t = MOBA_BLOCK
    g = MOBA_GROUP
    pair = pl.program_id(1)
    own = pl.program_id(2)

    @pl.when(own == 0)
    def _():
        kmean_ref[...] = jnp.zeros_like(kmean_ref)
        kf = k_ref[...].astype(jnp.float32).reshape(nb, t, LANES)
        kmean_ref[0:nb, :] = jnp.mean(kf, axis=1)

    km = kmean_ref[...]
    km0 = km.astype(jnp.bfloat16)
    r1 = km - km0.astype(jnp.float32)
    km1 = r1.astype(jnp.bfloat16)
    km2 = (r1 - km1.astype(jnp.float32)).astype(jnp.bfloat16)

    def update(hh, s, v, first):
        row_max = jnp.broadcast_to(jnp.max(s, axis=1, keepdims=True), (t, LANES))
        if first:
            m_new = row_max
        else:
            m_old = m_ref[hh]
            m_new = jnp.maximum(m_old, row_max)
            alpha = jnp.exp(m_old - m_new)
        p = jnp.exp(s - jnp.concatenate([m_new] * (s.shape[1] // LANES), axis=1))
        v_ones = jnp.where(_head_lane_mask(hh, jnp.int32) > 0, v, jnp.ones_like(v))
        pv = jnp.dot(p.astype(jnp.bfloat16), v_ones, preferred_element_type=jnp.float32)
        acc_ref[hh] = pv if first else alpha * acc_ref[hh] + pv
        m_ref[hh] = m_new

    def block_lanes(row):
        return jnp.concatenate([oh_ref[row]] * (t // BF16_SUBLANES), axis=0)

    blk = lax.broadcasted_iota(jnp.int32, (MOBA_LO_LANE, t), 0)
    blk_f = blk.astype(jnp.float32)
    for hh in range(HEADS_PER_TILE):
        qm = q_ref[...] * _head_lane_mask(hh, q_ref.dtype)
        gate = _qk(km0, qm) + _qk(km1, qm) + _qk(km2, qm)
        gate = jnp.where(blk < own, gate, _NEG_INF)
        sel = jnp.zeros((MOBA_LO_LANE, t), jnp.float32)
        for _ in range(MOBA_TOPK):
            best = jnp.max(gate, axis=0, keepdims=True)
            idx = jnp.min(jnp.where(gate == best, blk_f, float(LANES)), axis=0, keepdims=True)
            hit = blk_f == idx
            sel = jnp.where(jnp.logical_and(hit, best > _NEG_INF), 1.0, sel)
            gate = jnp.where(hit, _NEG_INF, gate)

        far_bias = far_ref[pair * HEADS_PER_TILE + hh]
        term = jnp.where(sel > 0.0, jnp.where(blk < own - 1, far_bias, 0.0), MOBA_MASKED)
        hi = term.astype(jnp.bfloat16).astype(jnp.float32)
        inj = jnp.concatenate([hi, term - hi], axis=0).T
        qa_ref[hh] = jnp.concatenate([qm, inj.astype(jnp.bfloat16)], axis=1)

    prev = jnp.maximum(own - 1, 0)
    prev_start = pl.multiple_of(prev * t, t)
    own_start = pl.multiple_of(own * t, t)
    k_aug = jnp.concatenate(
        [jnp.concatenate([k_ref[pl.ds(prev_start, t), :], k_ref[pl.ds(own_start, t), :]], axis=0),
         jnp.concatenate([block_lanes(jnp.where(own >= 1, prev, MOBA_NEVER)), block_lanes(MOBA_ZERO_ROW)],
                         axis=0)], axis=1)
    v01 = jnp.concatenate([v_ref[pl.ds(prev_start, t), :], v_ref[pl.ds(own_start, t), :]], axis=0)
    for hh in range(HEADS_PER_TILE):
        bias = jnp.concatenate([btab_ref[hh, 1], btab_ref[hh, 0]], axis=1)
        update(hh, _qk(qa_ref[hh], k_aug) + bias, v01, first=True)

    n_far = jnp.maximum(own - 1, 0)

    def far_body(i, c):
        first_blk = i * g
        g0 = jnp.minimum(first_blk, nb - g)
        start = pl.multiple_of(g0 * t, t)
        rows = []
        for j in range(g):
            b = g0 + j
            active = jnp.logical_and(b >= first_blk, b < n_far)
            rows.append(block_lanes(jnp.where(active, b, MOBA_NEVER)))
        k_aug = jnp.concatenate([k_ref[pl.ds(start, g * t), :], jnp.concatenate(rows, axis=0)], axis=1)
        vg = v_ref[pl.ds(start, g * t), :]
        for hh in range(HEADS_PER_TILE):
            update(hh, _qk(qa_ref[hh], k_aug), vg, first=False)
        return c

    lax.fori_loop(0, (n_far + g - 1) // g, far_body, 0)

    o_ref[...] = _merge_heads([acc_ref[hh] / pltpu.roll(acc_ref[hh], HEAD_DIM, axis=1)
                               for hh in range(HEADS_PER_TILE)])


def _moba_attention(u, far_bias, btab, block_lanes, batch, seq):
    t = MOBA_BLOCK
    nb = seq // t
    kernel = functools.partial(_moba_kernel, nb=nb)
    grid_spec = pltpu.PrefetchScalarGridSpec(
        num_scalar_prefetch=1,
        grid=(batch, N_PAIRS, nb),
        in_specs=[pl.BlockSpec((t, LANES), lambda b, p, i, far: (b * nb + i, COL_QB // LANES + p)),
                  pl.BlockSpec((seq, LANES), lambda b, p, i, far: (b, COL_KB // LANES + p)),
                  pl.BlockSpec((seq, LANES), lambda b, p, i, far: (b, COL_VB // LANES + p)),
                  pl.BlockSpec((HEADS_PER_TILE, 2, t, t), lambda b, p, i, far: (p, 0, 0, 0)),
                  pl.BlockSpec(block_lanes.shape, lambda b, p, i, far: (0, 0, 0))],
        out_specs=pl.BlockSpec((t, LANES), lambda b, p, i, far: (b * nb + i, p)),
        scratch_shapes=[pltpu.VMEM((MOBA_LO_LANE, LANES), jnp.float32),
                        pltpu.VMEM((HEADS_PER_TILE, t, 2 * LANES), jnp.bfloat16),
                        pltpu.VMEM((HEADS_PER_TILE, t, LANES), jnp.float32),
                        pltpu.VMEM((HEADS_PER_TILE, t, LANES), jnp.float32)])
    return pl.pallas_call(
        kernel,
        grid_spec=grid_spec,
        out_shape=jax.ShapeDtypeStruct((batch * seq, WIDTH), jnp.float32),
        compiler_params=pltpu.CompilerParams(dimension_semantics=("parallel", "parallel", "arbitrary"),
                                             vmem_limit_bytes=VMEM_LIMIT),
        name="moba_attention",
    )(far_bias, u, u, u, btab, block_lanes)


def _swa_kernel(sink_ref, q_ref, kp_ref, kc_ref, vp_ref, vc_ref, bias_ref, o_ref):
    w = SWA_WINDOW
    n = pl.program_id(1)
    kw = jnp.concatenate([kp_ref[...], kc_ref[...]], axis=0)
    vw = jnp.concatenate([vp_ref[...], vc_ref[...]], axis=0)
    col = lax.broadcasted_iota(jnp.int32, (w, 2 * w), 1)
    key_exists = jnp.logical_or(col >= w, n > 0)
    for j in range(SWA_GROUP):
        q = q_ref[:, j * LANES:(j + 1) * LANES]
        outs = []
        for half in range(SWA_KV_HEADS):
            slot = j * SWA_KV_HEADS + half
            s = _qk(q * _head_lane_mask(half, q.dtype), kw) + bias_ref[slot]
            s = jnp.where(key_exists, s, _NEG_INF)
            sink = sink_ref[slot]
            m = jnp.maximum(jnp.max(s, axis=1, keepdims=True), sink)
            e = jnp.exp(s - m)
            p = e / (jnp.sum(e, axis=1, keepdims=True) + jnp.exp(sink - m))
            outs.append(jnp.dot(p.astype(jnp.bfloat16), vw, preferred_element_type=jnp.float32))
        o_ref[:, j * LANES:(j + 1) * LANES] = _merge_heads(outs)


def _swa_attention(u, sinks_perm, bias, batch, seq):
    w = SWA_WINDOW
    nblk = seq // w
    kcol, vcol = COL_KC // LANES, COL_VC // LANES
    grid_spec = pltpu.PrefetchScalarGridSpec(
        num_scalar_prefetch=1,
        grid=(batch, nblk),
        in_specs=[pl.BlockSpec((w, WIDTH), lambda b, n, s: (b * nblk + n, COL_QC // WIDTH)),
                  pl.BlockSpec((w, LANES), lambda b, n, s: (b * nblk + jnp.maximum(n - 1, 0), kcol)),
                  pl.BlockSpec((w, LANES), lambda b, n, s: (b * nblk + n, kcol)),
                  pl.BlockSpec((w, LANES), lambda b, n, s: (b * nblk + jnp.maximum(n - 1, 0), vcol)),
                  pl.BlockSpec((w, LANES), lambda b, n, s: (b * nblk + n, vcol)),
                  pl.BlockSpec((N_HEADS, w, 2 * w), lambda b, n, s: (0, 0, 0))],
        out_specs=pl.BlockSpec((w, WIDTH), lambda b, n, s: (b * nblk + n, 0)))
    return pl.pallas_call(
        _swa_kernel,
        grid_spec=grid_spec,
        out_shape=jax.ShapeDtypeStruct((batch * seq, WIDTH), jnp.float32),
        compiler_params=pltpu.CompilerParams(dimension_semantics=("parallel", "arbitrary"),
                                             vmem_limit_bytes=VMEM_LIMIT),
        name="swa_attention",
    )(sinks_perm, u, u, u, u, u, bias)


def _post_kernel(x_ref, ya_ref, yb_ref, yc_ref, ga_ref, gb_ref, gc_ref, ma_ref, mb_ref, mc_ref,
                 wa_ref, wb_ref, wc_ref, wo_ref, fw_ref, o_ref, *, final):
    def branch(y_ref, g_ref, w_ref, m_ref):
        g = g_ref[...].astype(jnp.float32)
        y = (y_ref[...] * (g * jax.nn.sigmoid(g))).astype(jnp.bfloat16)
        proj = jnp.dot(y, w_ref[...], preferred_element_type=jnp.float32)
        return jax.nn.sigmoid(m_ref[...].astype(jnp.float32)) * proj

    merged = (branch(ya_ref, ga_ref, wa_ref, ma_ref) + branch(yb_ref, gb_ref, wb_ref, mb_ref)
              + branch(yc_ref, gc_ref, wc_ref, mc_ref))
    out = x_ref[...] + jnp.dot(merged.astype(jnp.bfloat16), wo_ref[...], preferred_element_type=jnp.float32)
    if final:
        out = out * lax.rsqrt(jnp.mean(out * out, axis=-1, keepdims=True) + RMS_EPS) * fw_ref[...]
    o_ref[...] = out


def _post(x2, ya, yb, yc, u, wa, wb, wc, wo, fw, *, final, tm=256):
    n = x2.shape[0]
    row = lambda i: (i, 0)
    full = lambda i: (0, 0)
    y_spec = pl.BlockSpec((tm, WIDTH), row)
    in_specs = [pl.BlockSpec((tm, D_MODEL), row), y_spec, y_spec, y_spec,
                pl.BlockSpec((tm, WIDTH), lambda i: (i, COL_GA // WIDTH)),
                pl.BlockSpec((tm, WIDTH), lambda i: (i, COL_GB // WIDTH)),
                pl.BlockSpec((tm, WIDTH), lambda i: (i, COL_GC // WIDTH)),
                pl.BlockSpec((tm, D_MODEL), lambda i: (i, COL_MA // D_MODEL)),
                pl.BlockSpec((tm, D_MODEL), lambda i: (i, COL_MB // D_MODEL)),
                pl.BlockSpec((tm, D_MODEL), lambda i: (i, COL_MC // D_MODEL)),
                pl.BlockSpec((WIDTH, D_MODEL), full), pl.BlockSpec((WIDTH, D_MODEL), full),
                pl.BlockSpec((WIDTH, D_MODEL), full), pl.BlockSpec((D_MODEL, D_MODEL), full),
                pl.BlockSpec((1, D_MODEL), full)]
    return pl.pallas_call(
        functools.partial(_post_kernel, final=final),
        grid=(n // tm,),
        in_specs=in_specs,
        out_specs=pl.BlockSpec((tm, D_MODEL), row),
        out_shape=jax.ShapeDtypeStruct((n, D_MODEL), jnp.float32),
        compiler_params=pltpu.CompilerParams(dimension_semantics=("parallel",),
                                             vmem_limit_bytes=VMEM_LIMIT),
        name="post_final" if final else "post",
    )(x2, ya, yb, yc, u, u, u, u, u, u, wa, wb, wc, wo, fw.reshape(1, D_MODEL))


def _bucket_lookup(table, dist):
    onehot = (_rel_bucket(dist)[..., None] == jnp.arange(REL_BUCKETS)).astype(jnp.float32)
    return jnp.einsum("...b,bh->h...", onehot, table.astype(jnp.float32), precision=lax.Precision.HIGHEST)


def _bias_tables(rel_bias):
    moba_tab = rel_bias[:, :N_HEADS]
    swa_tab = jnp.stack([rel_bias[:, N_HEADS + h] for h in _swa_head_perm()], axis=1)
    t = MOBA_BLOCK
    d_own = jnp.arange(t)[:, None] - jnp.arange(t)[None, :]
    own = jnp.where(d_own[None] >= 0, _bucket_lookup(moba_tab, d_own), _NEG_INF)
    prev = _bucket_lookup(moba_tab, d_own + t)
    btab = jnp.stack([own, prev], axis=1)
    far = _bucket_lookup(moba_tab, jnp.full((1,), t + 1, jnp.int32))[:, 0]

    w = SWA_WINDOW
    dist = jnp.arange(w)[:, None] + w - jnp.arange(2 * w)[None, :]
    in_band = (dist >= 0) & (dist < w)
    swa = jnp.where(in_band[None], _bucket_lookup(swa_tab, dist), _NEG_INF)
    return btab, far, swa


def kernel(x, norm_w, w_in, w_proj_a, w_proj_b, w_proj_c, w_out, sinks, rel_bias, final_norm_w):
    batch, seq, _ = x.shape
    depth = w_in.shape[0]
    assert seq % MOBA_BLOCK == 0 and MOBA_GROUP <= seq // MOBA_BLOCK <= MOBA_NEVER
    btab, far, swa_bias = _bias_tables(rel_bias)
    t = SB_TILE
    tri = jnp.concatenate([(jnp.arange(t)[:, None] > jnp.arange(t)[None, :]),
                           jnp.ones((t, LANES), bool)], axis=1).astype(jnp.bfloat16)
    lane = jnp.arange(LANES)[None, :]
    blk = jnp.arange(MOBA_ZERO_ROW + 1)[:, None]
    block_lanes = ((lane == blk) | (lane == blk + MOBA_LO_LANE)) & (blk < MOBA_ZERO_ROW)
    block_lanes = jnp.broadcast_to(block_lanes[:, None, :],
                                   (MOBA_ZERO_ROW + 1, BF16_SUBLANES, LANES)).astype(jnp.bfloat16)
    head_perm = _swa_head_perm()
    sinks_perm = jnp.stack([sinks[:, h] for h in head_perm], axis=1).astype(jnp.float32)

    x2 = x.reshape(batch * seq, D_MODEL)
    for layer in range(depth):
        u = _inproj(x2, norm_w[layer], _permute_input_columns(w_in[layer]))
        ya = _sb_attention(u, tri, batch, seq)
        yb = _moba_attention(u, far, btab, block_lanes, batch, seq)
        yc = _swa_attention(u, sinks_perm[layer], swa_bias, batch, seq)
        wc = jnp.concatenate([w_proj_c[layer][h * HEAD_DIM:(h + 1) * HEAD_DIM] for h in head_perm], axis=0)
        x2 = _post(x2, ya, yb, yc, u,
                   w_proj_a[layer].astype(jnp.bfloat16), w_proj_b[layer].astype(jnp.bfloat16),
                   wc.astype(jnp.bfloat16), w_out[layer].astype(jnp.bfloat16),
                   final_norm_w, final=(layer == depth - 1))
    return x2.reshape(batch, seq, D_MODEL)
```

```python
import functools
import math

import numpy as np
import jax
import jax.numpy as jnp
from jax import lax
from jax.experimental import pallas as pl
from jax.experimental.pallas import tpu as pltpu

D_MODEL = 1024
HEAD_DIM = 64
N_HEADS = 8
WIDTH = N_HEADS * HEAD_DIM
MOBA_BLOCK = 256
MOBA_TOPK = 3
SWA_KV_HEADS = 2
SWA_GROUP = N_HEADS // SWA_KV_HEADS
SWA_WINDOW = 128
REL_BUCKETS = 32
REL_MAX_DIST = 128
RMS_EPS = 1e-6

LANES = 128
BF16_SUBLANES = 16
HEADS_PER_TILE = LANES // HEAD_DIM
N_PAIRS = N_HEADS // HEADS_PER_TILE

COL_QA, COL_KA, COL_VA = 0, 512, 1024
COL_QB, COL_KB, COL_VB = 1536, 2048, 2560
COL_QC = 3072
COL_GA, COL_GB, COL_GC = 3584, 4096, 4608
COL_MA, COL_MB, COL_MC = 5120, 6144, 7168
COL_KC, COL_VC = 8192, 8320
D_IN = 8448

VMEM_LIMIT = 48 * 1024 * 1024

LOG2_E = math.log2(math.e)

SB_DEAD_LOG_WEIGHT = -120.0
SB_DEAD_LOG2_WEIGHT = SB_DEAD_LOG_WEIGHT * LOG2_E

_NEG_INF = float("-inf")


def _rel_bucket(dist):
    max_exact = REL_BUCKETS // 2
    n = jnp.maximum(dist, 0)
    nf = jnp.maximum(n, 1).astype(jnp.float32)
    large = max_exact + (jnp.log(nf / max_exact) / math.log(REL_MAX_DIST / max_exact)
                         * (REL_BUCKETS - max_exact)).astype(jnp.int32)
    large = jnp.minimum(large, REL_BUCKETS - 1)
    return jnp.where(n < max_exact, n, large)


def _swa_head_perm():
    return [half * SWA_GROUP + j for j in range(SWA_GROUP) for half in range(SWA_KV_HEADS)]


def _permute_input_columns(w):
    old = {"qa": 0, "ka": 512, "va": 1024, "ga": 1536, "qb": 2048, "kb": 2560, "vb": 3072, "gb": 3584,
           "qc": 4096, "kc": 4608, "vc": 4736, "gc": 4864, "ma": 5376, "mb": 6400, "mc": 7424}
    q_scale = HEAD_DIM ** -0.5

    def cols(name, width, scale=None):
        part = w[:, old[name]:old[name] + width]
        return part if scale is None else part * scale

    def swa_heads(name, scale=None):
        part = cols(name, WIDTH, scale)
        return jnp.concatenate([part[:, h * HEAD_DIM:(h + 1) * HEAD_DIM] for h in _swa_head_perm()], axis=1)

    parts = [cols("qa", 512, q_scale * LOG2_E), cols("ka", 512), cols("va", 512),
             cols("qb", 512, q_scale * LOG2_E), cols("kb", 512), cols("vb", 512),
             swa_heads("qc", q_scale),
             cols("ga", 512), cols("gb", 512), swa_heads("gc"),
             cols("ma", 1024), cols("mb", 1024), cols("mc", 1024),
             cols("kc", 128), cols("vc", 128)]
    return jnp.concatenate(parts, axis=1).astype(jnp.bfloat16)


def _inproj_kernel(x_ref, nw_ref, w_ref, u_ref, h_ref):
    @pl.when(pl.program_id(1) == 0)
    def _():
        x = x_ref[...]
        y = x * lax.rsqrt(jnp.mean(x * x, axis=-1, keepdims=True) + RMS_EPS)
        h_ref[...] = (y * nw_ref[...]).astype(h_ref.dtype)

    u_ref[...] = jnp.dot(h_ref[...], w_ref[...], preferred_element_type=jnp.float32).astype(u_ref.dtype)


def _inproj(x2, norm_w, w, *, tm=1024, tn=1408):
    n = x2.shape[0]
    return pl.pallas_call(
        _inproj_kernel,
        grid=(n // tm, D_IN // tn),
        in_specs=[pl.BlockSpec((tm, D_MODEL), lambda i, j: (i, 0)),
                  pl.BlockSpec((1, D_MODEL), lambda i, j: (0, 0)),
                  pl.BlockSpec((D_MODEL, tn), lambda i, j: (0, j))],
        out_specs=pl.BlockSpec((tm, tn), lambda i, j: (i, j)),
        out_shape=jax.ShapeDtypeStruct((n, D_IN), jnp.bfloat16),
        scratch_shapes=[pltpu.VMEM((tm, D_MODEL), jnp.bfloat16)],
        compiler_params=pltpu.CompilerParams(dimension_semantics=("parallel", "arbitrary"),
                                             vmem_limit_bytes=VMEM_LIMIT),
        name="inproj",
    )(x2, norm_w.reshape(1, D_MODEL), w)


def _head_lane_mask(hh, dtype):
    lane = lax.broadcasted_iota(jnp.int32, (1, LANES), 1)
    return ((lane // HEAD_DIM) == hh).astype(dtype)


def _qk(qm, k):
    return lax.dot_general(qm, k, (((1,), (1,)), ((), ())), preferred_element_type=jnp.float32)


def _merge_heads(parts):
    lane = lax.broadcasted_iota(jnp.int32, parts[0].shape, 1)
    return jnp.where(lane < HEAD_DIM, parts[0], parts[1])


SB_TILE = 256
SB_ROW_CHUNK = 256


def _sb_kernel(q_ref, k_ref, v_ref, tri_ref, o_ref, acc_ref, carry_ref):
    t = SB_TILE
    qi = pl.program_id(2)

    def step(newest, n_blocks, diagonal):
        tri = tri_ref[...]
        rc = SB_ROW_CHUNK
        chains = [(hh, r) for hh in range(HEADS_PER_TILE) for r in range(t // rc)]
        starts = [pl.multiple_of((newest - j) * t, t) for j in range(n_blocks)]

        terms = {}
        for hh, r in chains:
            qm = q_ref[r * rc:(r + 1) * rc, :] * _head_lane_mask(hh, q_ref.dtype)
            for j in range(n_blocks):
                z = _qk(qm, k_ref[pl.ds(starts[j], t), :])
                neg_abs = lax.bitcast_convert_type(
                    lax.bitcast_convert_type(z, jnp.uint32) | jnp.uint32(0x80000000), jnp.float32)
                soft = jnp.log(1.0 + jnp.exp2(neg_abs)) * LOG2_E
                log_beta = jnp.minimum(z, 0.0) - soft
                log_1m = log_beta - z
                mask = None
                if diagonal and j == 0:
                    row = lax.broadcasted_iota(jnp.int32, (rc, t), 0) + r * rc
                    mask = lax.broadcasted_iota(jnp.int32, (rc, t), 1) < row
                    log_1m = jnp.where(mask, log_1m, 0.0)
                hi = lax.bitcast_convert_type(
                    lax.bitcast_convert_type(log_1m, jnp.uint32) & jnp.uint32(0xFFFF0000), jnp.float32)
                split = jnp.concatenate([hi.astype(jnp.bfloat16), (log_1m - hi).astype(jnp.bfloat16)], axis=1)
                terms[hh, r, j] = (log_beta, log_1m[:, 0:1], split, mask)

        cums = {key: jnp.dot(split, tri, preferred_element_type=jnp.float32)
                for key, (_, _, split, _) in terms.items()}

        alive = None
        for hh, r in chains:
            rows = slice(r * rc, (r + 1) * rc)
            carry = carry_ref[hh, rows, :]
            acc = acc_ref[hh, rows, :]
            for j in range(n_blocks):
                log_beta, first_term, _, mask = terms[hh, r, j]
                cum = cums[hh, r, j]
                w = jnp.exp2(log_beta + cum + jnp.concatenate([carry] * (t // LANES), axis=1))
                if mask is not None:
                    w = jnp.where(mask, w, 0.0)
                acc = acc + jnp.dot(w.astype(jnp.bfloat16), v_ref[pl.ds(starts[j], t), :],
                                    preferred_element_type=jnp.float32)
                carry = carry + jnp.broadcast_to(cum[:, 0:1] + first_term, (rc, LANES))
            acc_ref[hh, rows, :] = acc
            carry_ref[hh, rows, :] = carry
            alive = carry if alive is None else jnp.maximum(alive, carry)
        return (jnp.max(alive) < SB_DEAD_LOG2_WEIGHT).astype(jnp.int32)

    acc_ref[...] = jnp.zeros_like(acc_ref)
    carry_ref[...] = jnp.zeros_like(carry_ref)
    dead = lax.cond(qi >= 1, lambda: step(qi, 2, True), lambda: step(qi, 1, True))

    def cond(c):
        newest, dead = c
        return jnp.logical_and(newest >= 1, dead == 0)

    def body(c):
        newest, _ = c
        return newest - 2, step(newest, 2, False)

    newest, dead = lax.while_loop(cond, body, (qi - 2, dead))

    @pl.when(jnp.logical_and(newest == 0, dead == 0))
    def _():
        step(0, 1, False)

    o_ref[...] = _merge_heads([acc_ref[0], acc_ref[1]])


def _sb_attention(u, tri, batch, seq):
    t = SB_TILE
    nq = seq // t
    return pl.pallas_call(
        _sb_kernel,
        grid=(batch, N_PAIRS, nq),
        in_specs=[pl.BlockSpec((t, LANES), lambda b, p, i: (b * nq + i, COL_QA // LANES + p)),
                  pl.BlockSpec((seq, LANES), lambda b, p, i: (b, COL_KA // LANES + p)),
                  pl.BlockSpec((seq, LANES), lambda b, p, i: (b, COL_VA // LANES + p)),
                  pl.BlockSpec((2 * t, t), lambda b, p, i: (0, 0))],
        out_specs=pl.BlockSpec((t, LANES), lambda b, p, i: (b * nq + i, p)),
        out_shape=jax.ShapeDtypeStruct((batch * seq, WIDTH), jnp.float32),
        scratch_shapes=[pltpu.VMEM((HEADS_PER_TILE, t, LANES), jnp.float32),
                        pltpu.VMEM((HEADS_PER_TILE, t, LANES), jnp.float32)],
        compiler_params=pltpu.CompilerParams(dimension_semantics=("parallel", "parallel", "arbitrary"),
                                             vmem_limit_bytes=VMEM_LIMIT),
        name="sb_attention",
    )(u, u, u, tri)


MOBA_LO_LANE = 64
MOBA_NEVER = 63
MOBA_ZERO_ROW = 64
MOBA_MASKED = -1e30
MOBA_GROUP = 4


def _moba_kernel(far_ref, q_ref, k_ref, v_ref, btab_ref, oh_ref, o_ref,
                 kmean_ref, qa_ref, m_ref, acc_ref, *, nb):
    t = MOBA_BLOCK
    g = MOBA_GROUP
    pair = pl.program_id(1)
    own = pl.program_id(2)

    @pl.when(own == 0)
    def _():
        kmean_ref[...] = jnp.zeros_like(kmean_ref)
        kf = k_ref[...].astype(jnp.float32).reshape(nb, t, LANES)
        kmean_ref[0:nb, :] = jnp.mean(kf, axis=1)

    km = kmean_ref[...]
    km0 = km.astype(jnp.bfloat16)
    r1 = km - km0.astype(jnp.float32)
    km1 = r1.astype(jnp.bfloat16)
    km2 = (r1 - km1.astype(jnp.float32)).astype(jnp.bfloat16)

    heads = range(HEADS_PER_TILE)

    def update(scores, v, first):
        width = scores[0].shape[1] // LANES
        row_max = [jnp.broadcast_to(jnp.max(s, axis=1, keepdims=True), (t, LANES)) for s in scores]
        if first:
            m_new = row_max
        else:
            m_old = [m_ref[hh] for hh in heads]
            m_new = [jnp.maximum(m_old[hh], row_max[hh]) for hh in heads]
            alpha = [jnp.exp2(m_old[hh] - m_new[hh]) for hh in heads]
        p = [jnp.exp2(scores[hh] - jnp.concatenate([m_new[hh]] * width, axis=1)).astype(jnp.bfloat16)
             for hh in heads]
        pv = [jnp.dot(p[hh], jnp.where(_head_lane_mask(hh, jnp.int32) > 0, v, jnp.ones_like(v)),
                      preferred_element_type=jnp.float32) for hh in heads]
        for hh in heads:
            acc_ref[hh] = pv[hh] if first else alpha[hh] * acc_ref[hh] + pv[hh]
            m_ref[hh] = m_new[hh]

    def block_lanes(row):
        return jnp.concatenate([oh_ref[row]] * (t // BF16_SUBLANES), axis=0)

    blk = lax.broadcasted_iota(jnp.int32, (MOBA_LO_LANE, t), 0)
    blk_f = blk.astype(jnp.float32)
    for hh in range(HEADS_PER_TILE):
        qm = q_ref[...] * _head_lane_mask(hh, q_ref.dtype)
        gate = _qk(km0, qm) + _qk(km1, qm) + _qk(km2, qm)
        gate = jnp.where(blk < own, gate, _NEG_INF)
        sel = jnp.zeros((MOBA_LO_LANE, t), jnp.float32)
        for _ in range(MOBA_TOPK):
            best = jnp.max(gate, axis=0, keepdims=True)
            idx = jnp.min(jnp.where(gate == best, blk_f, float(LANES)), axis=0, keepdims=True)
            hit = blk_f == idx
            sel = jnp.where(jnp.logical_and(hit, best > _NEG_INF), 1.0, sel)
            gate = jnp.where(hit, _NEG_INF, gate)

        far_bias = far_ref[pair * HEADS_PER_TILE + hh]
        term = jnp.where(sel > 0.0, jnp.where(blk < own - 1, far_bias, 0.0), MOBA_MASKED)
        hi = term.astype(jnp.bfloat16).astype(jnp.float32)
        inj = jnp.concatenate([hi, term - hi], axis=0).T
        qa_ref[hh] = jnp.concatenate([qm, inj.astype(jnp.bfloat16)], axis=1)

    prev = jnp.maximum(own - 1, 0)
    prev_start = pl.multiple_of(prev * t, t)
    own_start = pl.multiple_of(own * t, t)
    k_aug = jnp.concatenate(
        [jnp.concatenate([k_ref[pl.ds(prev_start, t), :], k_ref[pl.ds(own_start, t), :]], axis=0),
         jnp.concatenate([block_lanes(jnp.where(own >= 1, prev, MOBA_NEVER)), block_lanes(MOBA_ZERO_ROW)],
                         axis=0)], axis=1)
    v01 = jnp.concatenate([v_ref[pl.ds(prev_start, t), :], v_ref[pl.ds(own_start, t), :]], axis=0)
    update([_qk(qa_ref[hh], k_aug) + jnp.concatenate([btab_ref[hh, 1], btab_ref[hh, 0]], axis=1)
            for hh in heads], v01, first=True)

    n_far = jnp.maximum(own - 1, 0)

    def far_body(i, c):
        first_blk = i * g
        g0 = jnp.minimum(first_blk, nb - g)
        start = pl.multiple_of(g0 * t, t)
        rows = []
        for j in range(g):
            b = g0 + j
            active = jnp.logical_and(b >= first_blk, b < n_far)
            rows.append(block_lanes(jnp.where(active, b, MOBA_NEVER)))
        k_aug = jnp.concatenate([k_ref[pl.ds(start, g * t), :], jnp.concatenate(rows, axis=0)], axis=1)
        vg = v_ref[pl.ds(start, g * t), :]
        update([_qk(qa_ref[hh], k_aug) for hh in heads], vg, first=False)
        return c

    lax.fori_loop(0, (n_far + g - 1) // g, far_body, 0)

    o_ref[...] = _merge_heads([acc_ref[hh] / pltpu.roll(acc_ref[hh], HEAD_DIM, axis=1)
                               for hh in range(HEADS_PER_TILE)])


def _moba_attention(u, far_bias, btab, block_lanes, batch, seq):
    t = MOBA_BLOCK
    nb = seq // t
    kernel = functools.partial(_moba_kernel, nb=nb)
    grid_spec = pltpu.PrefetchScalarGridSpec(
        num_scalar_prefetch=1,
        grid=(batch, N_PAIRS, nb),
        in_specs=[pl.BlockSpec((t, LANES), lambda b, p, i, far: (b * nb + i, COL_QB // LANES + p)),
                  pl.BlockSpec((seq, LANES), lambda b, p, i, far: (b, COL_KB // LANES + p)),
                  pl.BlockSpec((seq, LANES), lambda b, p, i, far: (b, COL_VB // LANES + p)),
                  pl.BlockSpec((HEADS_PER_TILE, 2, t, t), lambda b, p, i, far: (p, 0, 0, 0)),
                  pl.BlockSpec(block_lanes.shape, lambda b, p, i, far: (0, 0, 0))],
        out_specs=pl.BlockSpec((t, LANES), lambda b, p, i, far: (b * nb + i, p)),
        scratch_shapes=[pltpu.VMEM((MOBA_LO_LANE, LANES), jnp.float32),
                        pltpu.VMEM((HEADS_PER_TILE, t, 2 * LANES), jnp.bfloat16),
                        pltpu.VMEM((HEADS_PER_TILE, t, LANES), jnp.float32),
                        pltpu.VMEM((HEADS_PER_TILE, t, LANES), jnp.float32)])
    return pl.pallas_call(
        kernel,
        grid_spec=grid_spec,
        out_shape=jax.ShapeDtypeStruct((batch * seq, WIDTH), jnp.float32),
        compiler_params=pltpu.CompilerParams(dimension_semantics=("parallel", "parallel", "arbitrary"),
                                             vmem_limit_bytes=VMEM_LIMIT),
        name="moba_attention",
    )(far_bias, u, u, u, btab, block_lanes)


def _swa_kernel(sink_ref, q_ref, kp_ref, kc_ref, vp_ref, vc_ref, bias_ref, o_ref):
    w = SWA_WINDOW
    n = pl.program_id(1)
    kw = jnp.concatenate([kp_ref[...], kc_ref[...]], axis=0)
    vw = jnp.concatenate([vp_ref[...], vc_ref[...]], axis=0)
    col = lax.broadcasted_iota(jnp.int32, (w, 2 * w), 1)
    key_exists = jnp.logical_or(col >= w, n > 0)
    for j in range(SWA_GROUP):
        q = q_ref[:, j * LANES:(j + 1) * LANES]
        outs = []
        for half in range(SWA_KV_HEADS):
            slot = j * SWA_KV_HEADS + half
            s = _qk(q * _head_lane_mask(half, q.dtype), kw) + bias_ref[slot]
            s = jnp.where(key_exists, s, _NEG_INF)
            sink = sink_ref[slot]
            m = jnp.maximum(jnp.max(s, axis=1, keepdims=True), sink)
            e = jnp.exp(s - m)
            p = e / (jnp.sum(e, axis=1, keepdims=True) + jnp.exp(sink - m))
            outs.append(jnp.dot(p.astype(jnp.bfloat16), vw, preferred_element_type=jnp.float32))
        o_ref[:, j * LANES:(j + 1) * LANES] = _merge_heads(outs)


def _swa_attention(u, sinks_perm, bias, batch, seq):
    w = SWA_WINDOW
    nblk = seq // w
    kcol, vcol = COL_KC // LANES, COL_VC // LANES
    grid_spec = pltpu.PrefetchScalarGridSpec(
        num_scalar_prefetch=1,
        grid=(batch, nblk),
        in_specs=[pl.BlockSpec((w, WIDTH), lambda b, n, s: (b * nblk + n, COL_QC // WIDTH)),
                  pl.BlockSpec((w, LANES), lambda b, n, s: (b * nblk + jnp.maximum(n - 1, 0), kcol)),
                  pl.BlockSpec((w, LANES), lambda b, n, s: (b * nblk + n, kcol)),
                  pl.BlockSpec((w, LANES), lambda b, n, s: (b * nblk + jnp.maximum(n - 1, 0), vcol)),
                  pl.BlockSpec((w, LANES), lambda b, n, s: (b * nblk + n, vcol)),
                  pl.BlockSpec((N_HEADS, w, 2 * w), lambda b, n, s: (0, 0, 0))],
        out_specs=pl.BlockSpec((w, WIDTH), lambda b, n, s: (b * nblk + n, 0)))
    return pl.pallas_call(
        _swa_kernel,
        grid_spec=grid_spec,
        out_shape=jax.ShapeDtypeStruct((batch * seq, WIDTH), jnp.float32),
        compiler_params=pltpu.CompilerParams(dimension_semantics=("parallel", "arbitrary"),
                                             vmem_limit_bytes=VMEM_LIMIT),
        name="swa_attention",
    )(sinks_perm, u, u, u, u, u, bias)


def _post_kernel(x_ref, ya_ref, yb_ref, yc_ref, ga_ref, gb_ref, gc_ref, ma_ref, mb_ref, mc_ref,
                 wa_ref, wb_ref, wc_ref, wo_ref, fw_ref, o_ref, *, final):
    def branch(y_ref, g_ref, w_ref, m_ref):
        g = g_ref[...].astype(jnp.float32)
        y = (y_ref[...] * (g * jax.nn.sigmoid(g))).astype(jnp.bfloat16)
        proj = jnp.dot(y, w_ref[...], preferred_element_type=jnp.float32)
        return jax.nn.sigmoid(m_ref[...].astype(jnp.float32)) * proj

    merged = (branch(ya_ref, ga_ref, wa_ref, ma_ref) + branch(yb_ref, gb_ref, wb_ref, mb_ref)
              + branch(yc_ref, gc_ref, wc_ref, mc_ref))
    out = x_ref[...] + jnp.dot(merged.astype(jnp.bfloat16), wo_ref[...], preferred_element_type=jnp.float32)
    if final:
        out = out * lax.rsqrt(jnp.mean(out * out, axis=-1, keepdims=True) + RMS_EPS) * fw_ref[...]
    o_ref[...] = out


def _post(x2, ya, yb, yc, u, wa, wb, wc, wo, fw, *, final, tm=256):
    n = x2.shape[0]
    row = lambda i: (i, 0)
    full = lambda i: (0, 0)
    y_spec = pl.BlockSpec((tm, WIDTH), row)
    in_specs = [pl.BlockSpec((tm, D_MODEL), row), y_spec, y_spec, y_spec,
                pl.BlockSpec((tm, WIDTH), lambda i: (i, COL_GA // WIDTH)),
                pl.BlockSpec((tm, WIDTH), lambda i: (i, COL_GB // WIDTH)),
                pl.BlockSpec((tm, WIDTH), lambda i: (i, COL_GC // WIDTH)),
                pl.BlockSpec((tm, D_MODEL), lambda i: (i, COL_MA // D_MODEL)),
                pl.BlockSpec((tm, D_MODEL), lambda i: (i, COL_MB // D_MODEL)),
                pl.BlockSpec((tm, D_MODEL), lambda i: (i, COL_MC // D_MODEL)),
                pl.BlockSpec((WIDTH, D_MODEL), full), pl.BlockSpec((WIDTH, D_MODEL), full),
                pl.BlockSpec((WIDTH, D_MODEL), full), pl.BlockSpec((D_MODEL, D_MODEL), full),
                pl.BlockSpec((1, D_MODEL), full)]
    return pl.pallas_call(
        functools.partial(_post_kernel, final=final),
        grid=(n // tm,),
        in_specs=in_specs,
        out_specs=pl.BlockSpec((tm, D_MODEL), row),
        out_shape=jax.ShapeDtypeStruct((n, D_MODEL), jnp.float32),
        compiler_params=pltpu.CompilerParams(dimension_semantics=("parallel",),
                                             vmem_limit_bytes=VMEM_LIMIT),
        name="post_final" if final else "post",
    )(x2, ya, yb, yc, u, u, u, u, u, u, wa, wb, wc, wo, fw.reshape(1, D_MODEL))


def _bucket_lookup(table, dist):
    onehot = (_rel_bucket(dist)[..., None] == jnp.arange(REL_BUCKETS)).astype(jnp.float32)
    return jnp.einsum("...b,bh->h...", onehot, table.astype(jnp.float32), precision=lax.Precision.HIGHEST)


def _bias_tables(rel_bias):
    moba_tab = rel_bias[:, :N_HEADS]
    swa_tab = jnp.stack([rel_bias[:, N_HEADS + h] for h in _swa_head_perm()], axis=1)
    t = MOBA_BLOCK
    d_own = jnp.arange(t)[:, None] - jnp.arange(t)[None, :]
    own = jnp.where(d_own[None] >= 0, _bucket_lookup(moba_tab, d_own), _NEG_INF)
    prev = _bucket_lookup(moba_tab, d_own + t)
    btab = jnp.stack([own, prev], axis=1) * LOG2_E
    far = _bucket_lookup(moba_tab, jnp.full((1,), t + 1, jnp.int32))[:, 0] * LOG2_E

    w = SWA_WINDOW
    dist = jnp.arange(w)[:, None] + w - jnp.arange(2 * w)[None, :]
    in_band = (dist >= 0) & (dist < w)
    swa = jnp.where(in_band[None], _bucket_lookup(swa_tab, dist), _NEG_INF)
    return btab, far, swa


def kernel(x, norm_w, w_in, w_proj_a, w_proj_b, w_proj_c, w_out, sinks, rel_bias, final_norm_w):
    batch, seq, _ = x.shape
    depth = w_in.shape[0]
    assert seq % MOBA_BLOCK == 0 and MOBA_GROUP <= seq // MOBA_BLOCK <= MOBA_NEVER
    btab, far, swa_bias = _bias_tables(rel_bias)
    t = SB_TILE
    tri = (jnp.arange(t)[:, None] > jnp.arange(t)[None, :]).astype(jnp.bfloat16)
    tri = jnp.concatenate([tri, tri], axis=0)
    lane = jnp.arange(LANES)[None, :]
    blk = jnp.arange(MOBA_ZERO_ROW + 1)[:, None]
    block_lanes = ((lane == blk) | (lane == blk + MOBA_LO_LANE)) & (blk < MOBA_ZERO_ROW)
    block_lanes = jnp.broadcast_to(block_lanes[:, None, :],
                                   (MOBA_ZERO_ROW + 1, BF16_SUBLANES, LANES)).astype(jnp.bfloat16)
    head_perm = _swa_head_perm()
    sinks_perm = jnp.stack([sinks[:, h] for h in head_perm], axis=1).astype(jnp.float32)

    x2 = x.reshape(batch * seq, D_MODEL)
    for layer in range(depth):
        u = _inproj(x2, norm_w[layer], _permute_input_columns(w_in[layer]))
        ya = _sb_attention(u, tri, batch, seq)
        yb = _moba_attention(u, far, btab, block_lanes, batch, seq)
        yc = _swa_attention(u, sinks_perm[layer], swa_bias, batch, seq)
        wc = jnp.concatenate([w_proj_c[layer][h * HEAD_DIM:(h + 1) * HEAD_DIM] for h in head_perm], axis=0)
        x2 = _post(x2, ya, yb, yc, u,
                   w_proj_a[layer].astype(jnp.bfloat16), w_proj_b[layer].astype(jnp.bfloat16),
                   wc.astype(jnp.bfloat16), w_out[layer].astype(jnp.bfloat16),
                   final_norm_w, final=(layer == depth - 1))
    return x2.reshape(batch, seq, D_MODEL)
```

```python
import functools
import math

import numpy as np
import jax
import jax.numpy as jnp
from jax import lax
from jax.experimental import pallas as pl
from jax.experimental.pallas import tpu as pltpu

D_MODEL = 1024
HEAD_DIM = 64
N_HEADS = 8
WIDTH = N_HEADS * HEAD_DIM
MOBA_BLOCK = 256
MOBA_TOPK = 3
SWA_KV_HEADS = 2
SWA_GROUP = N_HEADS // SWA_KV_HEADS
SWA_WINDOW = 128
REL_BUCKETS = 32
REL_MAX_DIST = 128
RMS_EPS = 1e-6

LANES = 128
BF16_SUBLANES = 16
HEADS_PER_TILE = LANES // HEAD_DIM
N_PAIRS = N_HEADS // HEADS_PER_TILE

COL_QA, COL_KA, COL_VA = 0, 512, 1024
COL_QB, COL_KB, COL_VB = 1536, 2048, 2560
COL_QC = 3072
COL_GA, COL_GB, COL_GC = 3584, 4096, 4608
COL_MA, COL_MB, COL_MC = 5120, 6144, 7168
COL_KC, COL_VC = 8192, 8320
D_IN = 8448

VMEM_LIMIT = 48 * 1024 * 1024

LOG2_E = math.log2(math.e)

SB_DEAD_LOG_WEIGHT = -120.0
SB_DEAD_LOG2_WEIGHT = SB_DEAD_LOG_WEIGHT * LOG2_E

_NEG_INF = float("-inf")


def _rel_bucket(dist):
    max_exact = REL_BUCKETS // 2
    n = jnp.maximum(dist, 0)
    nf = jnp.maximum(n, 1).astype(jnp.float32)
    large = max_exact + (jnp.log(nf / max_exact) / math.log(REL_MAX_DIST / max_exact)
                         * (REL_BUCKETS - max_exact)).astype(jnp.int32)
    large = jnp.minimum(large, REL_BUCKETS - 1)
    return jnp.where(n < max_exact, n, large)


def _swa_head_perm():
    return [half * SWA_GROUP + j for j in range(SWA_GROUP) for half in range(SWA_KV_HEADS)]


def _permute_input_columns(w):
    old = {"qa": 0, "ka": 512, "va": 1024, "ga": 1536, "qb": 2048, "kb": 2560, "vb": 3072, "gb": 3584,
           "qc": 4096, "kc": 4608, "vc": 4736, "gc": 4864, "ma": 5376, "mb": 6400, "mc": 7424}
    q_scale = HEAD_DIM ** -0.5 * LOG2_E

    def cols(name, width, scale=None):
        part = w[:, old[name]:old[name] + width]
        return part if scale is None else part * scale

    def swa_heads(name, scale=None):
        part = cols(name, WIDTH, scale)
        return jnp.concatenate([part[:, h * HEAD_DIM:(h + 1) * HEAD_DIM] for h in _swa_head_perm()], axis=1)

    parts = [cols("qa", 512, q_scale), cols("ka", 512), cols("va", 512),
             cols("qb", 512, q_scale), cols("kb", 512), cols("vb", 512),
             swa_heads("qc", q_scale),
             cols("ga", 512), cols("gb", 512), swa_heads("gc"),
             cols("ma", 1024), cols("mb", 1024), cols("mc", 1024),
             cols("kc", 128), cols("vc", 128)]
    return jnp.concatenate(parts, axis=1).astype(jnp.bfloat16)


def _inproj_kernel(x_ref, nw_ref, w_ref, u_ref, h_ref):
    @pl.when(pl.program_id(1) == 0)
    def _():
        x = x_ref[...]
        y = x * lax.rsqrt(jnp.mean(x * x, axis=-1, keepdims=True) + RMS_EPS)
        h_ref[...] = (y * nw_ref[...]).astype(h_ref.dtype)

    u_ref[...] = jnp.dot(h_ref[...], w_ref[...], preferred_element_type=jnp.float32).astype(u_ref.dtype)


def _inproj(x2, norm_w, w, *, tm=1024, tn=2816):
    n = x2.shape[0]
    return pl.pallas_call(
        _inproj_kernel,
        grid=(n // tm, D_IN // tn),
        in_specs=[pl.BlockSpec((tm, D_MODEL), lambda i, j: (i, 0)),
                  pl.BlockSpec((1, D_MODEL), lambda i, j: (0, 0)),
                  pl.BlockSpec((D_MODEL, tn), lambda i, j: (0, j))],
        out_specs=pl.BlockSpec((tm, tn), lambda i, j: (i, j)),
        out_shape=jax.ShapeDtypeStruct((n, D_IN), jnp.bfloat16),
        scratch_shapes=[pltpu.VMEM((tm, D_MODEL), jnp.bfloat16)],
        compiler_params=pltpu.CompilerParams(dimension_semantics=("parallel", "arbitrary"),
                                             vmem_limit_bytes=VMEM_LIMIT),
        name="inproj",
    )(x2, norm_w.reshape(1, D_MODEL), w)


def _head_lane_mask(hh, dtype):
    lane = lax.broadcasted_iota(jnp.int32, (1, LANES), 1)
    return ((lane // HEAD_DIM) == hh).astype(dtype)


def _qk(qm, k):
    return lax.dot_general(qm, k, (((1,), (1,)), ((), ())), preferred_element_type=jnp.float32)


def _merge_heads(parts):
    lane = lax.broadcasted_iota(jnp.int32, parts[0].shape, 1)
    return jnp.where(lane < HEAD_DIM, parts[0], parts[1])


SB_TILE = 256
SB_ROW_CHUNK = 256


def _sb_kernel(q_ref, k_ref, v_ref, tri_ref, o_ref, acc_ref, carry_ref):
    t = SB_TILE
    qi = pl.program_id(2)

    def step(newest, n_blocks, diagonal):
        tri = tri_ref[...]
        rc = SB_ROW_CHUNK
        chains = [(hh, r) for hh in range(HEADS_PER_TILE) for r in range(t // rc)]
        starts = [pl.multiple_of((newest - j) * t, t) for j in range(n_blocks)]

        terms = {}
        for hh, r in chains:
            qm = q_ref[r * rc:(r + 1) * rc, :] * _head_lane_mask(hh, q_ref.dtype)
            for j in range(n_blocks):
                z = _qk(qm, k_ref[pl.ds(starts[j], t), :])
                neg_abs = lax.bitcast_convert_type(
                    lax.bitcast_convert_type(z, jnp.uint32) | jnp.uint32(0x80000000), jnp.float32)
                soft = jnp.log(1.0 + jnp.exp2(neg_abs)) * LOG2_E
                log_beta = jnp.minimum(z, 0.0) - soft
                log_1m = log_beta - z
                mask = None
                if diagonal and j == 0:
                    row = lax.broadcasted_iota(jnp.int32, (rc, t), 0) + r * rc
                    mask = lax.broadcasted_iota(jnp.int32, (rc, t), 1) < row
                    log_1m = jnp.where(mask, log_1m, 0.0)
                hi = lax.bitcast_convert_type(
                    lax.bitcast_convert_type(log_1m, jnp.uint32) & jnp.uint32(0xFFFF0000), jnp.float32)
                split = jnp.concatenate([hi.astype(jnp.bfloat16), (log_1m - hi).astype(jnp.bfloat16)], axis=1)
                terms[hh, r, j] = (log_beta, log_1m[:, 0:1], split, mask)

        cums = {key: jnp.dot(split, tri, preferred_element_type=jnp.float32)
                for key, (_, _, split, _) in terms.items()}

        alive = None
        for hh, r in chains:
            rows = slice(r * rc, (r + 1) * rc)
            carry = carry_ref[hh, rows, :]
            acc = acc_ref[hh, rows, :]
            for j in range(n_blocks):
                log_beta, first_term, _, mask = terms[hh, r, j]
                cum = cums[hh, r, j]
                w = jnp.exp2(log_beta + cum + jnp.concatenate([carry] * (t // LANES), axis=1))
                if mask is not None:
                    w = jnp.where(mask, w, 0.0)
                acc = acc + jnp.dot(w.astype(jnp.bfloat16), v_ref[pl.ds(starts[j], t), :],
                                    preferred_element_type=jnp.float32)
                carry = carry + jnp.broadcast_to(cum[:, 0:1] + first_term, (rc, LANES))
            acc_ref[hh, rows, :] = acc
            carry_ref[hh, rows, :] = carry
            alive = carry if alive is None else jnp.maximum(alive, carry)
        return (jnp.max(alive) < SB_DEAD_LOG2_WEIGHT).astype(jnp.int32)

    acc_ref[...] = jnp.zeros_like(acc_ref)
    carry_ref[...] = jnp.zeros_like(carry_ref)
    dead = lax.cond(qi >= 1, lambda: step(qi, 2, True), lambda: step(qi, 1, True))

    def cond(c):
        newest, dead = c
        return jnp.logical_and(newest >= 1, dead == 0)

    def body(c):
        newest, _ = c
        return newest - 2, step(newest, 2, False)

    newest, dead = lax.while_loop(cond, body, (qi - 2, dead))

    @pl.when(jnp.logical_and(newest == 0, dead == 0))
    def _():
        step(0, 1, False)

    o_ref[...] = _merge_heads([acc_ref[0], acc_ref[1]])


def _sb_attention(u, tri, batch, seq):
    t = SB_TILE
    nq = seq // t
    return pl.pallas_call(
        _sb_kernel,
        grid=(batch, N_PAIRS, nq),
        in_specs=[pl.BlockSpec((t, LANES), lambda b, p, i: (b * nq + i, COL_QA // LANES + p)),
                  pl.BlockSpec((seq, LANES), lambda b, p, i: (b, COL_KA // LANES + p)),
                  pl.BlockSpec((seq, LANES), lambda b, p, i: (b, COL_VA // LANES + p)),
                  pl.BlockSpec((2 * t, t), lambda b, p, i: (0, 0))],
        out_specs=pl.BlockSpec((t, LANES), lambda b, p, i: (b * nq + i, p)),
        out_shape=jax.ShapeDtypeStruct((batch * seq, WIDTH), jnp.float32),
        scratch_shapes=[pltpu.VMEM((HEADS_PER_TILE, t, LANES), jnp.float32),
                        pltpu.VMEM((HEADS_PER_TILE, t, LANES), jnp.float32)],
        compiler_params=pltpu.CompilerParams(dimension_semantics=("parallel", "parallel", "arbitrary"),
                                             vmem_limit_bytes=VMEM_LIMIT),
        name="sb_attention",
    )(u, u, u, tri)


MOBA_LO_LANE = 64
MOBA_NEVER = 63
MOBA_ZERO_ROW = 64
MOBA_MASKED = -1e30
MOBA_GROUP = 4


def _moba_kernel(far_ref, q_ref, k_ref, v_ref, btab_ref, oh_ref, o_ref,
                 kmean_ref, qa_ref, m_ref, acc_ref, *, nb):
    t = MOBA_BLOCK
    g = MOBA_GROUP
    pair = pl.program_id(1)
    own = pl.program_id(2)

    @pl.when(own == 0)
    def _():
        kmean_ref[...] = jnp.zeros_like(kmean_ref)
        kf = k_ref[...].astype(jnp.float32).reshape(nb, t, LANES)
        kmean_ref[0:nb, :] = jnp.mean(kf, axis=1)

    km = kmean_ref[...]
    km0 = km.astype(jnp.bfloat16)
    r1 = km - km0.astype(jnp.float32)
    km1 = r1.astype(jnp.bfloat16)
    km2 = (r1 - km1.astype(jnp.float32)).astype(jnp.bfloat16)

    heads = range(HEADS_PER_TILE)

    def update(scores, v, first):
        width = scores[0].shape[1] // LANES
        row_max = [jnp.broadcast_to(jnp.max(s, axis=1, keepdims=True), (t, LANES)) for s in scores]
        if first:
            m_new = row_max
        else:
            m_old = [m_ref[hh] for hh in heads]
            m_new = [jnp.maximum(m_old[hh], row_max[hh]) for hh in heads]
            alpha = [jnp.exp2(m_old[hh] - m_new[hh]) for hh in heads]
        p = [jnp.exp2(scores[hh] - jnp.concatenate([m_new[hh]] * width, axis=1)).astype(jnp.bfloat16)
             for hh in heads]
        pv = [jnp.dot(p[hh], jnp.where(_head_lane_mask(hh, jnp.int32) > 0, v, jnp.ones_like(v)),
                      preferred_element_type=jnp.float32) for hh in heads]
        for hh in heads:
            acc_ref[hh] = pv[hh] if first else alpha[hh] * acc_ref[hh] + pv[hh]
            m_ref[hh] = m_new[hh]

    def block_lanes(row):
        return jnp.concatenate([oh_ref[row]] * (t // BF16_SUBLANES), axis=0)

    blk = lax.broadcasted_iota(jnp.int32, (MOBA_LO_LANE, t), 0)
    blk_f = blk.astype(jnp.float32)
    for hh in range(HEADS_PER_TILE):
        qm = q_ref[...] * _head_lane_mask(hh, q_ref.dtype)
        gate = _qk(km0, qm) + _qk(km1, qm) + _qk(km2, qm)
        gate = jnp.where(blk < own, gate, _NEG_INF)
        sel = jnp.zeros((MOBA_LO_LANE, t), jnp.float32)
        for _ in range(MOBA_TOPK):
            best = jnp.max(gate, axis=0, keepdims=True)
            idx = jnp.min(jnp.where(gate == best, blk_f, float(LANES)), axis=0, keepdims=True)
            hit = blk_f == idx
            sel = jnp.where(jnp.logical_and(hit, best > _NEG_INF), 1.0, sel)
            gate = jnp.where(hit, _NEG_INF, gate)

        far_bias = far_ref[pair * HEADS_PER_TILE + hh]
        term = jnp.where(sel > 0.0, jnp.where(blk < own - 1, far_bias, 0.0), MOBA_MASKED)
        hi = term.astype(jnp.bfloat16).astype(jnp.float32)
        inj = jnp.concatenate([hi, term - hi], axis=0).T
        qa_ref[hh] = jnp.concatenate([qm, inj.astype(jnp.bfloat16)], axis=1)

    prev = jnp.maximum(own - 1, 0)
    prev_start = pl.multiple_of(prev * t, t)
    own_start = pl.multiple_of(own * t, t)
    k_aug = jnp.concatenate(
        [jnp.concatenate([k_ref[pl.ds(prev_start, t), :], k_ref[pl.ds(own_start, t), :]], axis=0),
         jnp.concatenate([block_lanes(jnp.where(own >= 1, prev, MOBA_NEVER)), block_lanes(MOBA_ZERO_ROW)],
                         axis=0)], axis=1)
    v01 = jnp.concatenate([v_ref[pl.ds(prev_start, t), :], v_ref[pl.ds(own_start, t), :]], axis=0)
    update([_qk(qa_ref[hh], k_aug) + jnp.concatenate([btab_ref[hh, 1], btab_ref[hh, 0]], axis=1)
            for hh in heads], v01, first=True)

    n_far = jnp.maximum(own - 1, 0)

    def far_body(i, c):
        first_blk = i * g
        g0 = jnp.minimum(first_blk, nb - g)
        start = pl.multiple_of(g0 * t, t)
        rows = []
        for j in range(g):
            b = g0 + j
            active = jnp.logical_and(b >= first_blk, b < n_far)
            rows.append(block_lanes(jnp.where(active, b, MOBA_NEVER)))
        k_aug = jnp.concatenate([k_ref[pl.ds(start, g * t), :], jnp.concatenate(rows, axis=0)], axis=1)
        vg = v_ref[pl.ds(start, g * t), :]
        update([_qk(qa_ref[hh], k_aug) for hh in heads], vg, first=False)
        return c

    lax.fori_loop(0, (n_far + g - 1) // g, far_body, 0)

    o_ref[...] = _merge_heads([acc_ref[hh] / pltpu.roll(acc_ref[hh], HEAD_DIM, axis=1)
                               for hh in range(HEADS_PER_TILE)])


def _moba_attention(u, far_bias, btab, block_lanes, batch, seq):
    t = MOBA_BLOCK
    nb = seq // t
    kernel = functools.partial(_moba_kernel, nb=nb)
    grid_spec = pltpu.PrefetchScalarGridSpec(
        num_scalar_prefetch=1,
        grid=(batch, N_PAIRS, nb),
        in_specs=[pl.BlockSpec((t, LANES), lambda b, p, i, far: (b * nb + i, COL_QB // LANES + p)),
                  pl.BlockSpec((seq, LANES), lambda b, p, i, far: (b, COL_KB // LANES + p)),
                  pl.BlockSpec((seq, LANES), lambda b, p, i, far: (b, COL_VB // LANES + p)),
                  pl.BlockSpec((HEADS_PER_TILE, 2, t, t), lambda b, p, i, far: (p, 0, 0, 0)),
                  pl.BlockSpec(block_lanes.shape, lambda b, p, i, far: (0, 0, 0))],
        out_specs=pl.BlockSpec((t, LANES), lambda b, p, i, far: (b * nb + i, p)),
        scratch_shapes=[pltpu.VMEM((MOBA_LO_LANE, LANES), jnp.float32),
                        pltpu.VMEM((HEADS_PER_TILE, t, 2 * LANES), jnp.bfloat16),
                        pltpu.VMEM((HEADS_PER_TILE, t, LANES), jnp.float32),
                        pltpu.VMEM((HEADS_PER_TILE, t, LANES), jnp.float32)])
    return pl.pallas_call(
        kernel,
        grid_spec=grid_spec,
        out_shape=jax.ShapeDtypeStruct((batch * seq, WIDTH), jnp.float32),
        compiler_params=pltpu.CompilerParams(dimension_semantics=("parallel", "parallel", "arbitrary"),
                                             vmem_limit_bytes=VMEM_LIMIT),
        name="moba_attention",
    )(far_bias, u, u, u, btab, block_lanes)


def _swa_kernel(sink_ref, q_ref, kp_ref, kc_ref, vp_ref, vc_ref, bias_ref, o_ref):
    w = SWA_WINDOW
    kw = jnp.concatenate([kp_ref[...], kc_ref[...]], axis=0)
    vw = jnp.concatenate([vp_ref[...], vc_ref[...]], axis=0)
    slots = [(j, half) for j in range(SWA_GROUP) for half in range(SWA_KV_HEADS)]

    def attend(first_block):
        scores = []
        for slot, (j, half) in enumerate(slots):
            q = q_ref[:, j * LANES:(j + 1) * LANES]
            s = _qk(q * _head_lane_mask(half, q.dtype), kw) + bias_ref[slot]
            if first_block:
                s = jnp.where(lax.broadcasted_iota(jnp.int32, (w, 2 * w), 1) >= w, s, _NEG_INF)
            scores.append(s)
        m = [jnp.maximum(jnp.broadcast_to(jnp.max(s, axis=1, keepdims=True), (w, LANES)), sink_ref[slot])
             for slot, s in enumerate(scores)]
        p = [jnp.exp2(s - jnp.concatenate([m[slot]] * (2 * w // LANES), axis=1)).astype(jnp.bfloat16)
             for slot, s in enumerate(scores)]
        pv = [jnp.dot(p[slot], jnp.where(_head_lane_mask(half, jnp.int32) > 0, vw, jnp.ones_like(vw)),
                      preferred_element_type=jnp.float32) for slot, (j, half) in enumerate(slots)]
        out = [pv[slot] / (pltpu.roll(pv[slot], HEAD_DIM, axis=1) + jnp.exp2(sink_ref[slot] - m[slot]))
               for slot in range(len(slots))]
        for j in range(SWA_GROUP):
            o_ref[:, j * LANES:(j + 1) * LANES] = _merge_heads(out[j * SWA_KV_HEADS:(j + 1) * SWA_KV_HEADS])

    @pl.when(pl.program_id(1) == 0)
    def _():
        attend(True)

    @pl.when(pl.program_id(1) > 0)
    def _():
        attend(False)


def _swa_attention(u, sinks_perm, bias, batch, seq):
    w = SWA_WINDOW
    nblk = seq // w
    kcol, vcol = COL_KC // LANES, COL_VC // LANES
    grid_spec = pltpu.PrefetchScalarGridSpec(
        num_scalar_prefetch=1,
        grid=(batch, nblk),
        in_specs=[pl.BlockSpec((w, WIDTH), lambda b, n, s: (b * nblk + n, COL_QC // WIDTH)),
                  pl.BlockSpec((w, LANES), lambda b, n, s: (b * nblk + jnp.maximum(n - 1, 0), kcol)),
                  pl.BlockSpec((w, LANES), lambda b, n, s: (b * nblk + n, kcol)),
                  pl.BlockSpec((w, LANES), lambda b, n, s: (b * nblk + jnp.maximum(n - 1, 0), vcol)),
                  pl.BlockSpec((w, LANES), lambda b, n, s: (b * nblk + n, vcol)),
                  pl.BlockSpec((N_HEADS, w, 2 * w), lambda b, n, s: (0, 0, 0))],
        out_specs=pl.BlockSpec((w, WIDTH), lambda b, n, s: (b * nblk + n, 0)))
    return pl.pallas_call(
        _swa_kernel,
        grid_spec=grid_spec,
        out_shape=jax.ShapeDtypeStruct((batch * seq, WIDTH), jnp.float32),
        compiler_params=pltpu.CompilerParams(dimension_semantics=("parallel", "arbitrary"),
                                             vmem_limit_bytes=VMEM_LIMIT),
        name="swa_attention",
    )(sinks_perm, u, u, u, u, u, bias)


def _post_kernel(x_ref, ya_ref, yb_ref, yc_ref, ga_ref, gb_ref, gc_ref, ma_ref, mb_ref, mc_ref,
                 wa_ref, wb_ref, wc_ref, wo_ref, fw_ref, o_ref, *, final):
    def branch(y_ref, g_ref, w_ref, m_ref):
        g = g_ref[...].astype(jnp.float32)
        y = (y_ref[...] * (g * jax.nn.sigmoid(g))).astype(jnp.bfloat16)
        proj = jnp.dot(y, w_ref[...], preferred_element_type=jnp.float32)
        return jax.nn.sigmoid(m_ref[...].astype(jnp.float32)) * proj

    merged = (branch(ya_ref, ga_ref, wa_ref, ma_ref) + branch(yb_ref, gb_ref, wb_ref, mb_ref)
              + branch(yc_ref, gc_ref, wc_ref, mc_ref))
    out = x_ref[...] + jnp.dot(merged.astype(jnp.bfloat16), wo_ref[...], preferred_element_type=jnp.float32)
    if final:
        out = out * lax.rsqrt(jnp.mean(out * out, axis=-1, keepdims=True) + RMS_EPS) * fw_ref[...]
    o_ref[...] = out


def _post(x2, ya, yb, yc, u, wa, wb, wc, wo, fw, *, final, tm=256):
    n = x2.shape[0]
    row = lambda i: (i, 0)
    full = lambda i: (0, 0)
    y_spec = pl.BlockSpec((tm, WIDTH), row)
    in_specs = [pl.BlockSpec((tm, D_MODEL), row), y_spec, y_spec, y_spec,
                pl.BlockSpec((tm, WIDTH), lambda i: (i, COL_GA // WIDTH)),
                pl.BlockSpec((tm, WIDTH), lambda i: (i, COL_GB // WIDTH)),
                pl.BlockSpec((tm, WIDTH), lambda i: (i, COL_GC // WIDTH)),
                pl.BlockSpec((tm, D_MODEL), lambda i: (i, COL_MA // D_MODEL)),
                pl.BlockSpec((tm, D_MODEL), lambda i: (i, COL_MB // D_MODEL)),
                pl.BlockSpec((tm, D_MODEL), lambda i: (i, COL_MC // D_MODEL)),
                pl.BlockSpec((WIDTH, D_MODEL), full), pl.BlockSpec((WIDTH, D_MODEL), full),
                pl.BlockSpec((WIDTH, D_MODEL), full), pl.BlockSpec((D_MODEL, D_MODEL), full),
                pl.BlockSpec((1, D_MODEL), full)]
    return pl.pallas_call(
        functools.partial(_post_kernel, final=final),
        grid=(n // tm,),
        in_specs=in_specs,
        out_specs=pl.BlockSpec((tm, D_MODEL), row),
        out_shape=jax.ShapeDtypeStruct((n, D_MODEL), jnp.float32),
        compiler_params=pltpu.CompilerParams(dimension_semantics=("parallel",),
                                             vmem_limit_bytes=VMEM_LIMIT),
        name="post_final" if final else "post",
    )(x2, ya, yb, yc, u, u, u, u, u, u, wa, wb, wc, wo, fw.reshape(1, D_MODEL))


def _bucket_lookup(table, dist):
    onehot = (_rel_bucket(dist)[..., None] == jnp.arange(REL_BUCKETS)).astype(jnp.float32)
    return jnp.einsum("...b,bh->h...", onehot, table.astype(jnp.float32), precision=lax.Precision.HIGHEST)


def _bias_tables(rel_bias):
    moba_tab = rel_bias[:, :N_HEADS]
    swa_tab = jnp.stack([rel_bias[:, N_HEADS + h] for h in _swa_head_perm()], axis=1)
    t = MOBA_BLOCK
    d_own = jnp.arange(t)[:, None] - jnp.arange(t)[None, :]
    own = jnp.where(d_own[None] >= 0, _bucket_lookup(moba_tab, d_own), _NEG_INF)
    prev = _bucket_lookup(moba_tab, d_own + t)
    btab = jnp.stack([own, prev], axis=1) * LOG2_E
    far = _bucket_lookup(moba_tab, jnp.full((1,), t + 1, jnp.int32))[:, 0] * LOG2_E

    w = SWA_WINDOW
    dist = jnp.arange(w)[:, None] + w - jnp.arange(2 * w)[None, :]
    in_band = (dist >= 0) & (dist < w)
    swa = jnp.where(in_band[None], _bucket_lookup(swa_tab, dist) * LOG2_E, _NEG_INF)
    return btab, far, swa


def kernel(x, norm_w, w_in, w_proj_a, w_proj_b, w_proj_c, w_out, sinks, rel_bias, final_norm_w):
    batch, seq, _ = x.shape
    depth = w_in.shape[0]
    assert seq % MOBA_BLOCK == 0 and MOBA_GROUP <= seq // MOBA_BLOCK <= MOBA_NEVER
    btab, far, swa_bias = _bias_tables(rel_bias)
    t = SB_TILE
    tri = (jnp.arange(t)[:, None] > jnp.arange(t)[None, :]).astype(jnp.bfloat16)
    tri = jnp.concatenate([tri, tri], axis=0)
    lane = jnp.arange(LANES)[None, :]
    blk = jnp.arange(MOBA_ZERO_ROW + 1)[:, None]
    block_lanes = ((lane == blk) | (lane == blk + MOBA_LO_LANE)) & (blk < MOBA_ZERO_ROW)
    block_lanes = jnp.broadcast_to(block_lanes[:, None, :],
                                   (MOBA_ZERO_ROW + 1, BF16_SUBLANES, LANES)).astype(jnp.bfloat16)
    head_perm = _swa_head_perm()
    sinks_perm = jnp.stack([sinks[:, h] for h in head_perm], axis=1).astype(jnp.float32) * LOG2_E

    x2 = x.reshape(batch * seq, D_MODEL)
    for layer in range(depth):
        u = _inproj(x2, norm_w[layer], _permute_input_columns(w_in[layer]))
        ya = _sb_attention(u, tri, batch, seq)
        yb = _moba_attention(u, far, btab, block_lanes, batch, seq)
        yc = _swa_attention(u, sinks_perm[layer], swa_bias, batch, seq)
        wc = jnp.concatenate([w_proj_c[layer][h * HEAD_DIM:(h + 1) * HEAD_DIM] for h in head_perm], axis=0)
        x2 = _post(x2, ya, yb, yc, u,
                   w_proj_a[layer].astype(jnp.bfloat16), w_proj_b[layer].astype(jnp.bfloat16),
                   wc.astype(jnp.bfloat16), w_out[layer].astype(jnp.bfloat16),
                   final_norm_w, final=(layer == depth - 1))
    return x2.reshape(batch, seq, D_MODEL)
```

```python
import functools
import math

import numpy as np
import jax
import jax.numpy as jnp
from jax import lax
from jax.experimental import pallas as pl
from jax.experimental.pallas import tpu as pltpu

D_MODEL = 1024
HEAD_DIM = 64
N_HEADS = 8
WIDTH = N_HEADS * HEAD_DIM
MOBA_BLOCK = 256
MOBA_TOPK = 3
SWA_KV_HEADS = 2
SWA_GROUP = N_HEADS // SWA_KV_HEADS
SWA_WINDOW = 128
REL_BUCKETS = 32
REL_MAX_DIST = 128
RMS_EPS = 1e-6

LANES = 128
BF16_SUBLANES = 16
F32_SUBLANES = 8
HEADS_PER_TILE = LANES // HEAD_DIM
N_PAIRS = N_HEADS // HEADS_PER_TILE

COL_QA, COL_KA, COL_VA = 0, 512, 1024
COL_QB, COL_KB, COL_VB = 1536, 2048, 2560
COL_QC = 3072
COL_GA, COL_GB, COL_GC = 3584, 4096, 4608
COL_MA, COL_MB, COL_MC = 5120, 6144, 7168
COL_KC, COL_VC = 8192, 8320
D_IN = 8448

VMEM_LIMIT = 48 * 1024 * 1024

LOG2_E = math.log2(math.e)

SB_DEAD_LOG_WEIGHT = -120.0
SB_DEAD_LOG2_WEIGHT = SB_DEAD_LOG_WEIGHT * LOG2_E

_NEG_INF = float("-inf")


def _rel_bucket(dist):
    max_exact = REL_BUCKETS // 2
    n = jnp.maximum(dist, 0)
    nf = jnp.maximum(n, 1).astype(jnp.float32)
    large = max_exact + (jnp.log(nf / max_exact) / math.log(REL_MAX_DIST / max_exact)
                         * (REL_BUCKETS - max_exact)).astype(jnp.int32)
    large = jnp.minimum(large, REL_BUCKETS - 1)
    return jnp.where(n < max_exact, n, large)


def _swa_head_perm():
    return [half * SWA_GROUP + j for j in range(SWA_GROUP) for half in range(SWA_KV_HEADS)]


def _permute_input_columns(w):
    old = {"qa": 0, "ka": 512, "va": 1024, "ga": 1536, "qb": 2048, "kb": 2560, "vb": 3072, "gb": 3584,
           "qc": 4096, "kc": 4608, "vc": 4736, "gc": 4864, "ma": 5376, "mb": 6400, "mc": 7424}
    q_scale = HEAD_DIM ** -0.5 * LOG2_E

    def cols(name, width, scale=None):
        part = w[:, old[name]:old[name] + width]
        return part if scale is None else part * scale

    def swa_heads(name, scale=None):
        part = cols(name, WIDTH, scale)
        return jnp.concatenate([part[:, h * HEAD_DIM:(h + 1) * HEAD_DIM] for h in _swa_head_perm()], axis=1)

    parts = [cols("qa", 512, q_scale), cols("ka", 512), cols("va", 512),
             cols("qb", 512, q_scale), cols("kb", 512), cols("vb", 512),
             swa_heads("qc", q_scale),
             cols("ga", 512), cols("gb", 512), swa_heads("gc"),
             cols("ma", 1024), cols("mb", 1024), cols("mc", 1024),
             cols("kc", 128), cols("vc", 128)]
    return jnp.concatenate(parts, axis=1).astype(jnp.bfloat16)


def _inproj_kernel(x_ref, nw_ref, w_ref, u_ref, h_ref):
    @pl.when(pl.program_id(1) == 0)
    def _():
        x = x_ref[...]
        y = x * lax.rsqrt(jnp.mean(x * x, axis=-1, keepdims=True) + RMS_EPS)
        h_ref[...] = (y * nw_ref[...]).astype(h_ref.dtype)

    u_ref[...] = jnp.dot(h_ref[...], w_ref[...], preferred_element_type=jnp.float32).astype(u_ref.dtype)


def _inproj(x2, norm_w, w, *, tm=1024, tn=2816):
    n = x2.shape[0]
    return pl.pallas_call(
        _inproj_kernel,
        grid=(n // tm, D_IN // tn),
        in_specs=[pl.BlockSpec((tm, D_MODEL), lambda i, j: (i, 0)),
                  pl.BlockSpec((1, D_MODEL), lambda i, j: (0, 0)),
                  pl.BlockSpec((D_MODEL, tn), lambda i, j: (0, j))],
        out_specs=pl.BlockSpec((tm, tn), lambda i, j: (i, j)),
        out_shape=jax.ShapeDtypeStruct((n, D_IN), jnp.bfloat16),
        scratch_shapes=[pltpu.VMEM((tm, D_MODEL), jnp.bfloat16)],
        compiler_params=pltpu.CompilerParams(dimension_semantics=("parallel", "arbitrary"),
                                             vmem_limit_bytes=VMEM_LIMIT),
        name="inproj",
    )(x2, norm_w.reshape(1, D_MODEL), w)


def _head_lane_mask(hh, dtype):
    lane = lax.broadcasted_iota(jnp.int32, (1, LANES), 1)
    return ((lane // HEAD_DIM) == hh).astype(dtype)


def _qk(qm, k):
    return lax.dot_general(qm, k, (((1,), (1,)), ((), ())), preferred_element_type=jnp.float32)


def _merge_heads(parts):
    lane = lax.broadcasted_iota(jnp.int32, parts[0].shape, 1)
    return jnp.where(lane < HEAD_DIM, parts[0], parts[1])


SB_TILE = 256
SB_ROW_CHUNK = 256


def _sb_kernel(q_ref, k_ref, v_ref, tri_ref, o_ref, acc_ref, carry_ref):
    t = SB_TILE
    qi = pl.program_id(2)

    def step(newest, n_blocks, diagonal):
        tri = tri_ref[...]
        rc = SB_ROW_CHUNK
        chains = [(hh, r) for hh in range(HEADS_PER_TILE) for r in range(t // rc)]
        starts = [pl.multiple_of((newest - j) * t, t) for j in range(n_blocks)]

        terms = {}
        for hh, r in chains:
            qm = q_ref[r * rc:(r + 1) * rc, :] * _head_lane_mask(hh, q_ref.dtype)
            for j in range(n_blocks):
                z = _qk(qm, k_ref[pl.ds(starts[j], t), :])
                neg_abs = lax.bitcast_convert_type(
                    lax.bitcast_convert_type(z, jnp.uint32) | jnp.uint32(0x80000000), jnp.float32)
                soft = jnp.log(1.0 + jnp.exp2(neg_abs)) * LOG2_E
                log_beta = jnp.minimum(z, 0.0) - soft
                log_1m = log_beta - z
                mask = None
                if diagonal and j == 0:
                    row = lax.broadcasted_iota(jnp.int32, (rc, t), 0) + r * rc
                    mask = lax.broadcasted_iota(jnp.int32, (rc, t), 1) < row
                    log_1m = jnp.where(mask, log_1m, 0.0)
                hi = lax.bitcast_convert_type(
                    lax.bitcast_convert_type(log_1m, jnp.uint32) & jnp.uint32(0xFFFF0000), jnp.float32)
                split = jnp.concatenate([hi.astype(jnp.bfloat16), (log_1m - hi).astype(jnp.bfloat16)], axis=1)
                terms[hh, r, j] = (log_beta, log_1m[:, 0:1], split, mask)

        cums = {key: jnp.dot(split, tri, preferred_element_type=jnp.float32)
                for key, (_, _, split, _) in terms.items()}

        alive = None
        for hh, r in chains:
            rows = slice(r * rc, (r + 1) * rc)
            carry = carry_ref[hh, rows, :]
            acc = acc_ref[hh, rows, :]
            for j in range(n_blocks):
                log_beta, first_term, _, mask = terms[hh, r, j]
                cum = cums[hh, r, j]
                w = jnp.exp2(log_beta + cum + jnp.concatenate([carry] * (t // LANES), axis=1))
                if mask is not None:
                    w = jnp.where(mask, w, 0.0)
                acc = acc + jnp.dot(w.astype(jnp.bfloat16), v_ref[pl.ds(starts[j], t), :],
                                    preferred_element_type=jnp.float32)
                carry = carry + jnp.broadcast_to(cum[:, 0:1] + first_term, (rc, LANES))
            acc_ref[hh, rows, :] = acc
            carry_ref[hh, rows, :] = carry
            alive = carry if alive is None else jnp.maximum(alive, carry)
        return (jnp.max(alive) < SB_DEAD_LOG2_WEIGHT).astype(jnp.int32)

    acc_ref[...] = jnp.zeros_like(acc_ref)
    carry_ref[...] = jnp.zeros_like(carry_ref)
    dead = lax.cond(qi >= 1, lambda: step(qi, 2, True), lambda: step(qi, 1, True))

    def cond(c):
        newest, dead = c
        return jnp.logical_and(newest >= 1, dead == 0)

    def body(c):
        newest, _ = c
        return newest - 2, step(newest, 2, False)

    newest, dead = lax.while_loop(cond, body, (qi - 2, dead))

    @pl.when(jnp.logical_and(newest == 0, dead == 0))
    def _():
        step(0, 1, False)

    o_ref[...] = _merge_heads([acc_ref[0], acc_ref[1]])


def _sb_attention(u, tri, batch, seq):
    t = SB_TILE
    nq = seq // t
    return pl.pallas_call(
        _sb_kernel,
        grid=(batch, N_PAIRS, nq),
        in_specs=[pl.BlockSpec((t, LANES), lambda b, p, i: (b * nq + i, COL_QA // LANES + p)),
                  pl.BlockSpec((seq, LANES), lambda b, p, i: (b, COL_KA // LANES + p)),
                  pl.BlockSpec((seq, LANES), lambda b, p, i: (b, COL_VA // LANES + p)),
                  pl.BlockSpec((2 * t, t), lambda b, p, i: (0, 0))],
        out_specs=pl.BlockSpec((t, LANES), lambda b, p, i: (b * nq + i, p)),
        out_shape=jax.ShapeDtypeStruct((batch * seq, WIDTH), jnp.float32),
        scratch_shapes=[pltpu.VMEM((HEADS_PER_TILE, t, LANES), jnp.float32),
                        pltpu.VMEM((HEADS_PER_TILE, t, LANES), jnp.float32)],
        compiler_params=pltpu.CompilerParams(dimension_semantics=("parallel", "parallel", "arbitrary"),
                                             vmem_limit_bytes=VMEM_LIMIT),
        name="sb_attention",
    )(u, u, u, tri)


MOBA_LO_LANE = 64
MOBA_NEVER = 63
MOBA_ZERO_ROW = 64
MOBA_MASKED = -1e30
MOBA_GROUP = 8
MOBA_MIN_GROUP = 2


def _moba_kernel(far_ref, q_ref, k_ref, v_ref, btab_ref, oh_ref, o_ref,
                 kmean_ref, vt_ref, qa_ref, m_ref, acc_ref, *, nb):
    t = MOBA_BLOCK
    g = MOBA_GROUP
    pair = pl.program_id(1)
    own = pl.program_id(2)
    heads = range(HEADS_PER_TILE)

    @pl.when(own == 0)
    def _():
        kmean_ref[...] = jnp.zeros_like(kmean_ref)
        kf = k_ref[...].astype(jnp.float32).reshape(nb, t, LANES)
        kmean_ref[0:nb, :] = jnp.mean(kf, axis=1)

        head_of_row = lax.broadcasted_iota(jnp.int32, (LANES, t), 0) // HEAD_DIM

        def transpose_block(b, c):
            vb = v_ref[pl.ds(pl.multiple_of(b * t, t), t), :].astype(jnp.float32).T
            for hh in heads:
                vt_ref[hh, b] = jnp.where(head_of_row == hh, vb, 1.0).astype(jnp.bfloat16)
            return c

        lax.fori_loop(0, nb, transpose_block, 0)

    km = kmean_ref[...]
    km0 = km.astype(jnp.bfloat16)
    r1 = km - km0.astype(jnp.float32)
    km1 = r1.astype(jnp.bfloat16)
    km2 = (r1 - km1.astype(jnp.float32)).astype(jnp.bfloat16)

    def update(scores, blocks, first):
        col_max = [jnp.max(s, axis=0, keepdims=True) for s in scores]
        if first:
            m_new = col_max
        else:
            m_old = [m_ref[hh][0:1, :] for hh in heads]
            m_new = [jnp.maximum(m_old[hh], col_max[hh]) for hh in heads]
            alpha = [jnp.exp2(m_old[hh] - m_new[hh]) for hh in heads]
        p = [jnp.exp2(scores[hh] - m_new[hh]).astype(jnp.bfloat16) for hh in heads]
        pv = [jnp.dot(jnp.concatenate([vt_ref[hh, b] for b in blocks], axis=1), p[hh],
                      preferred_element_type=jnp.float32) for hh in heads]
        for hh in heads:
            acc_ref[hh] = pv[hh] if first else alpha[hh] * acc_ref[hh] + pv[hh]
            m_ref[hh] = jnp.broadcast_to(m_new[hh], m_ref.shape[1:])

    def block_lanes(row):
        return jnp.concatenate([oh_ref[row]] * (t // BF16_SUBLANES), axis=0)

    blk = lax.broadcasted_iota(jnp.int32, (MOBA_LO_LANE, t), 0)
    blk_f = blk.astype(jnp.float32)
    for hh in range(HEADS_PER_TILE):
        qm = q_ref[...] * _head_lane_mask(hh, q_ref.dtype)
        gate = _qk(km0, qm) + _qk(km1, qm) + _qk(km2, qm)
        gate = jnp.where(blk < own, gate, _NEG_INF)
        sel = jnp.zeros((MOBA_LO_LANE, t), jnp.float32)
        for _ in range(MOBA_TOPK):
            best = jnp.max(gate, axis=0, keepdims=True)
            idx = jnp.min(jnp.where(gate == best, blk_f, float(LANES)), axis=0, keepdims=True)
            hit = blk_f == idx
            sel = jnp.where(jnp.logical_and(hit, best > _NEG_INF), 1.0, sel)
            gate = jnp.where(hit, _NEG_INF, gate)

        far_bias = far_ref[pair * HEADS_PER_TILE + hh]
        term = jnp.where(sel > 0.0, jnp.where(blk < own - 1, far_bias, 0.0), MOBA_MASKED)
        hi = term.astype(jnp.bfloat16).astype(jnp.float32)
        inj = jnp.concatenate([hi, term - hi], axis=0).T
        qa_ref[hh] = jnp.concatenate([qm, inj.astype(jnp.bfloat16)], axis=1)

    prev = jnp.maximum(own - 1, 0)
    prev_start = pl.multiple_of(prev * t, t)
    own_start = pl.multiple_of(own * t, t)
    k_aug = jnp.concatenate(
        [jnp.concatenate([k_ref[pl.ds(prev_start, t), :], k_ref[pl.ds(own_start, t), :]], axis=0),
         jnp.concatenate([block_lanes(jnp.where(own >= 1, prev, MOBA_NEVER)), block_lanes(MOBA_ZERO_ROW)],
                         axis=0)], axis=1)
    update([_qk(k_aug, qa_ref[hh]) + jnp.concatenate([btab_ref[hh, 1], btab_ref[hh, 0]], axis=0)
            for hh in heads], [prev, own], first=True)

    n_far = jnp.maximum(own - 1, 0)

    def far_step(first_blk, group):
        g0 = jnp.minimum(first_blk, nb - group)
        start = pl.multiple_of(g0 * t, t)
        rows = []
        for j in range(group):
            b = g0 + j
            active = jnp.logical_and(b >= first_blk, b < n_far)
            rows.append(block_lanes(jnp.where(active, b, MOBA_NEVER)))
        k_aug = jnp.concatenate([k_ref[pl.ds(start, group * t), :], jnp.concatenate(rows, axis=0)], axis=1)
        scores = [_qk(k_aug, qa_ref[hh]) for hh in heads]
        for j in range(group):
            update([s[j * t:(j + 1) * t] for s in scores], [g0 + j], first=False)

    def far_body(i, c):
        far_step(i * g, g)
        return c

    n_full = n_far // g
    lax.fori_loop(0, n_full, far_body, 0)

    left = n_far - n_full * g
    group = g
    while group >= MOBA_MIN_GROUP:
        lower = group // 2 if group > MOBA_MIN_GROUP else 0

        @pl.when(jnp.logical_and(left > lower, left <= group))
        def _(group=group):
            far_step(n_full * g, group)

        group //= 2

    out_t = jnp.concatenate([acc_ref[0][:HEAD_DIM] / acc_ref[0][HEAD_DIM:],
                             acc_ref[1][HEAD_DIM:] / acc_ref[1][:HEAD_DIM]], axis=0)
    o_ref[...] = out_t.T


def _moba_attention(u, far_bias, btab, block_lanes, batch, seq):
    t = MOBA_BLOCK
    nb = seq // t
    kernel = functools.partial(_moba_kernel, nb=nb)
    grid_spec = pltpu.PrefetchScalarGridSpec(
        num_scalar_prefetch=1,
        grid=(batch, N_PAIRS, nb),
        in_specs=[pl.BlockSpec((t, LANES), lambda b, p, i, far: (b * nb + i, COL_QB // LANES + p)),
                  pl.BlockSpec((seq, LANES), lambda b, p, i, far: (b, COL_KB // LANES + p)),
                  pl.BlockSpec((seq, LANES), lambda b, p, i, far: (b, COL_VB // LANES + p)),
                  pl.BlockSpec((HEADS_PER_TILE, 2, t, t), lambda b, p, i, far: (p, 0, 0, 0)),
                  pl.BlockSpec(block_lanes.shape, lambda b, p, i, far: (0, 0, 0))],
        out_specs=pl.BlockSpec((t, LANES), lambda b, p, i, far: (b * nb + i, p)),
        scratch_shapes=[pltpu.VMEM((MOBA_LO_LANE, LANES), jnp.float32),
                        pltpu.VMEM((HEADS_PER_TILE, nb, LANES, t), jnp.bfloat16),
                        pltpu.VMEM((HEADS_PER_TILE, t, 2 * LANES), jnp.bfloat16),
                        pltpu.VMEM((HEADS_PER_TILE, F32_SUBLANES, t), jnp.float32),
                        pltpu.VMEM((HEADS_PER_TILE, LANES, t), jnp.float32)])
    return pl.pallas_call(
        kernel,
        grid_spec=grid_spec,
        out_shape=jax.ShapeDtypeStruct((batch * seq, WIDTH), jnp.float32),
        compiler_params=pltpu.CompilerParams(dimension_semantics=("parallel", "parallel", "arbitrary"),
                                             vmem_limit_bytes=VMEM_LIMIT),
        name="moba_attention",
    )(far_bias, u, u, u, btab, block_lanes)


def _swa_kernel(sink_ref, q_ref, kp_ref, kc_ref, vp_ref, vc_ref, bias_ref, o_ref):
    w = SWA_WINDOW
    kw = jnp.concatenate([kp_ref[...], kc_ref[...]], axis=0)
    vw = jnp.concatenate([vp_ref[...], vc_ref[...]], axis=0)
    slots = [(j, half) for j in range(SWA_GROUP) for half in range(SWA_KV_HEADS)]

    def attend(first_block):
        scores = []
        for slot, (j, half) in enumerate(slots):
            q = q_ref[:, j * LANES:(j + 1) * LANES]
            s = _qk(q * _head_lane_mask(half, q.dtype), kw) + bias_ref[slot]
            if first_block:
                s = jnp.where(lax.broadcasted_iota(jnp.int32, (w, 2 * w), 1) >= w, s, _NEG_INF)
            scores.append(s)
        m = [jnp.maximum(jnp.broadcast_to(jnp.max(s, axis=1, keepdims=True), (w, LANES)), sink_ref[slot])
             for slot, s in enumerate(scores)]
        p = [jnp.exp2(s - jnp.concatenate([m[slot]] * (2 * w // LANES), axis=1)).astype(jnp.bfloat16)
             for slot, s in enumerate(scores)]
        pv = [jnp.dot(p[slot], jnp.where(_head_lane_mask(half, jnp.int32) > 0, vw, jnp.ones_like(vw)),
                      preferred_element_type=jnp.float32) for slot, (j, half) in enumerate(slots)]
        out = [pv[slot] / (pltpu.roll(pv[slot], HEAD_DIM, axis=1) + jnp.exp2(sink_ref[slot] - m[slot]))
               for slot in range(len(slots))]
        for j in range(SWA_GROUP):
            o_ref[:, j * LANES:(j + 1) * LANES] = _merge_heads(out[j * SWA_KV_HEADS:(j + 1) * SWA_KV_HEADS])

    @pl.when(pl.program_id(1) == 0)
    def _():
        attend(True)

    @pl.when(pl.program_id(1) > 0)
    def _():
        attend(False)


def _swa_attention(u, sinks_perm, bias, batch, seq):
    w = SWA_WINDOW
    nblk = seq // w
    kcol, vcol = COL_KC // LANES, COL_VC // LANES
    grid_spec = pltpu.PrefetchScalarGridSpec(
        num_scalar_prefetch=1,
        grid=(batch, nblk),
        in_specs=[pl.BlockSpec((w, WIDTH), lambda b, n, s: (b * nblk + n, COL_QC // WIDTH)),
                  pl.BlockSpec((w, LANES), lambda b, n, s: (b * nblk + jnp.maximum(n - 1, 0), kcol)),
                  pl.BlockSpec((w, LANES), lambda b, n, s: (b * nblk + n, kcol)),
                  pl.BlockSpec((w, LANES), lambda b, n, s: (b * nblk + jnp.maximum(n - 1, 0), vcol)),
                  pl.BlockSpec((w, LANES), lambda b, n, s: (b * nblk + n, vcol)),
                  pl.BlockSpec((N_HEADS, w, 2 * w), lambda b, n, s: (0, 0, 0))],
        out_specs=pl.BlockSpec((w, WIDTH), lambda b, n, s: (b * nblk + n, 0)))
    return pl.pallas_call(
        _swa_kernel,
        grid_spec=grid_spec,
        out_shape=jax.ShapeDtypeStruct((batch * seq, WIDTH), jnp.float32),
        compiler_params=pltpu.CompilerParams(dimension_semantics=("parallel", "arbitrary"),
                                             vmem_limit_bytes=VMEM_LIMIT),
        name="swa_attention",
    )(sinks_perm, u, u, u, u, u, bias)


def _post_kernel(x_ref, ya_ref, yb_ref, yc_ref, ga_ref, gb_ref, gc_ref, ma_ref, mb_ref, mc_ref,
                 wa_ref, wb_ref, wc_ref, wo_ref, fw_ref, o_ref, *, final):
    def branch(y_ref, g_ref, w_ref, m_ref):
        g = g_ref[...].astype(jnp.float32)
        y = (y_ref[...] * (g * jax.nn.sigmoid(g))).astype(jnp.bfloat16)
        proj = jnp.dot(y, w_ref[...], preferred_element_type=jnp.float32)
        return jax.nn.sigmoid(m_ref[...].astype(jnp.float32)) * proj

    merged = (branch(ya_ref, ga_ref, wa_ref, ma_ref) + branch(yb_ref, gb_ref, wb_ref, mb_ref)
              + branch(yc_ref, gc_ref, wc_ref, mc_ref))
    out = x_ref[...] + jnp.dot(merged.astype(jnp.bfloat16), wo_ref[...], preferred_element_type=jnp.float32)
    if final:
        out = out * lax.rsqrt(jnp.mean(out * out, axis=-1, keepdims=True) + RMS_EPS) * fw_ref[...]
    o_ref[...] = out


def _post(x2, ya, yb, yc, u, wa, wb, wc, wo, fw, *, final, tm=256):
    n = x2.shape[0]
    row = lambda i: (i, 0)
    full = lambda i: (0, 0)
    y_spec = pl.BlockSpec((tm, WIDTH), row)
    in_specs = [pl.BlockSpec((tm, D_MODEL), row), y_spec, y_spec, y_spec,
                pl.BlockSpec((tm, WIDTH), lambda i: (i, COL_GA // WIDTH)),
                pl.BlockSpec((tm, WIDTH), lambda i: (i, COL_GB // WIDTH)),
                pl.BlockSpec((tm, WIDTH), lambda i: (i, COL_GC // WIDTH)),
                pl.BlockSpec((tm, D_MODEL), lambda i: (i, COL_MA // D_MODEL)),
                pl.BlockSpec((tm, D_MODEL), lambda i: (i, COL_MB // D_MODEL)),
                pl.BlockSpec((tm, D_MODEL), lambda i: (i, COL_MC // D_MODEL)),
                pl.BlockSpec((WIDTH, D_MODEL), full), pl.BlockSpec((WIDTH, D_MODEL), full),
                pl.BlockSpec((WIDTH, D_MODEL), full), pl.BlockSpec((D_MODEL, D_MODEL), full),
                pl.BlockSpec((1, D_MODEL), full)]
    return pl.pallas_call(
        functools.partial(_post_kernel, final=final),
        grid=(n // tm,),
        in_specs=in_specs,
        out_specs=pl.BlockSpec((tm, D_MODEL), row),
        out_shape=jax.ShapeDtypeStruct((n, D_MODEL), jnp.float32),
        compiler_params=pltpu.CompilerParams(dimension_semantics=("parallel",),
                                             vmem_limit_bytes=VMEM_LIMIT),
        name="post_final" if final else "post",
    )(x2, ya, yb, yc, u, u, u, u, u, u, wa, wb, wc, wo, fw.reshape(1, D_MODEL))


def _bucket_lookup(table, dist):
    onehot = (_rel_bucket(dist)[..., None] == jnp.arange(REL_BUCKETS)).astype(jnp.float32)
    return jnp.einsum("...b,bh->h...", onehot, table.astype(jnp.float32), precision=lax.Precision.HIGHEST)


def _bias_tables(rel_bias):
    moba_tab = rel_bias[:, :N_HEADS]
    swa_tab = jnp.stack([rel_bias[:, N_HEADS + h] for h in _swa_head_perm()], axis=1)
    t = MOBA_BLOCK
    d_own = jnp.arange(t)[:, None] - jnp.arange(t)[None, :]
    own = jnp.where(d_own[None] >= 0, _bucket_lookup(moba_tab, d_own), _NEG_INF)
    prev = _bucket_lookup(moba_tab, d_own + t)
    btab = jnp.swapaxes(jnp.stack([own, prev], axis=1), 2, 3) * LOG2_E
    far = _bucket_lookup(moba_tab, jnp.full((1,), t + 1, jnp.int32))[:, 0] * LOG2_E

    w = SWA_WINDOW
    dist = jnp.arange(w)[:, None] + w - jnp.arange(2 * w)[None, :]
    in_band = (dist >= 0) & (dist < w)
    swa = jnp.where(in_band[None], _bucket_lookup(swa_tab, dist) * LOG2_E, _NEG_INF)
    return btab, far, swa


def kernel(x, norm_w, w_in, w_proj_a, w_proj_b, w_proj_c, w_out, sinks, rel_bias, final_norm_w):
    batch, seq, _ = x.shape
    depth = w_in.shape[0]
    assert seq % MOBA_BLOCK == 0 and MOBA_GROUP <= seq // MOBA_BLOCK <= MOBA_NEVER
    btab, far, swa_bias = _bias_tables(rel_bias)
    t = SB_TILE
    tri = (jnp.arange(t)[:, None] > jnp.arange(t)[None, :]).astype(jnp.bfloat16)
    tri = jnp.concatenate([tri, tri], axis=0)
    lane = jnp.arange(LANES)[None, :]
    blk = jnp.arange(MOBA_ZERO_ROW + 1)[:, None]
    block_lanes = ((lane == blk) | (lane == blk + MOBA_LO_LANE)) & (blk < MOBA_ZERO_ROW)
    block_lanes = jnp.broadcast_to(block_lanes[:, None, :],
                                   (MOBA_ZERO_ROW + 1, BF16_SUBLANES, LANES)).astype(jnp.bfloat16)
    head_perm = _swa_head_perm()
    sinks_perm = jnp.stack([sinks[:, h] for h in head_perm], axis=1).astype(jnp.float32) * LOG2_E

    x2 = x.reshape(batch * seq, D_MODEL)
    for layer in range(depth):
        u = _inproj(x2, norm_w[layer], _permute_input_columns(w_in[layer]))
        ya = _sb_attention(u, tri, batch, seq)
        yb = _moba_attention(u, far, btab, block_lanes, batch, seq)
        yc = _swa_attention(u, sinks_perm[layer], swa_bias, batch, seq)
        wc = jnp.concatenate([w_proj_c[layer][h * HEAD_DIM:(h + 1) * HEAD_DIM] for h in head_perm], axis=0)
        x2 = _post(x2, ya, yb, yc, u,
                   w_proj_a[layer].astype(jnp.bfloat16), w_proj_b[layer].astype(jnp.bfloat16),
                   wc.astype(jnp.bfloat16), w_out[layer].astype(jnp.bfloat16),
                   final_norm_w, final=(layer == depth - 1))
    return x2.reshape(batch, seq, D_MODEL)
```

```python
import functools
import math

import numpy as np
import jax
import jax.numpy as jnp
from jax import lax
from jax.experimental import pallas as pl
from jax.experimental.pallas import tpu as pltpu

D_MODEL = 1024
HEAD_DIM = 64
N_HEADS = 8
WIDTH = N_HEADS * HEAD_DIM
MOBA_BLOCK = 256
MOBA_TOPK = 3
SWA_KV_HEADS = 2
SWA_GROUP = N_HEADS // SWA_KV_HEADS
SWA_WINDOW = 128
REL_BUCKETS = 32
REL_MAX_DIST = 128
RMS_EPS = 1e-6

LANES = 128
BF16_SUBLANES = 16
F32_SUBLANES = 8
HEADS_PER_TILE = LANES // HEAD_DIM
N_PAIRS = N_HEADS // HEADS_PER_TILE

COL_QA, COL_KA, COL_VA = 0, 512, 1024
COL_QB, COL_KB, COL_VB = 1536, 2048, 2560
COL_QC = 3072
COL_GA, COL_GB, COL_GC = 3584, 4096, 4608
COL_MA, COL_MB, COL_MC = 5120, 6144, 7168
COL_KC, COL_VC = 8192, 8320
D_IN = 8448

VMEM_LIMIT = 48 * 1024 * 1024

LOG2_E = math.log2(math.e)

SB_DEAD_LOG_WEIGHT = -120.0
SB_DEAD_LOG2_WEIGHT = SB_DEAD_LOG_WEIGHT * LOG2_E

_NEG_INF = float("-inf")


def _rel_bucket(dist):
    max_exact = REL_BUCKETS // 2
    n = jnp.maximum(dist, 0)
    nf = jnp.maximum(n, 1).astype(jnp.float32)
    large = max_exact + (jnp.log(nf / max_exact) / math.log(REL_MAX_DIST / max_exact)
                         * (REL_BUCKETS - max_exact)).astype(jnp.int32)
    large = jnp.minimum(large, REL_BUCKETS - 1)
    return jnp.where(n < max_exact, n, large)


def _swa_head_perm():
    return [half * SWA_GROUP + j for j in range(SWA_GROUP) for half in range(SWA_KV_HEADS)]


def _permute_input_columns(w):
    old = {"qa": 0, "ka": 512, "va": 1024, "ga": 1536, "qb": 2048, "kb": 2560, "vb": 3072, "gb": 3584,
           "qc": 4096, "kc": 4608, "vc": 4736, "gc": 4864, "ma": 5376, "mb": 6400, "mc": 7424}
    q_scale = HEAD_DIM ** -0.5 * LOG2_E

    def cols(name, width, scale=None):
        part = w[:, old[name]:old[name] + width]
        return part if scale is None else part * scale

    def swa_heads(name, scale=None):
        part = cols(name, WIDTH, scale)
        return jnp.concatenate([part[:, h * HEAD_DIM:(h + 1) * HEAD_DIM] for h in _swa_head_perm()], axis=1)

    parts = [cols("qa", 512, q_scale), cols("ka", 512), cols("va", 512),
             cols("qb", 512, q_scale), cols("kb", 512), cols("vb", 512),
             swa_heads("qc", q_scale),
             cols("ga", 512), cols("gb", 512), swa_heads("gc"),
             cols("ma", 1024), cols("mb", 1024), cols("mc", 1024),
             cols("kc", 128), cols("vc", 128)]
    return jnp.concatenate(parts, axis=1).astype(jnp.bfloat16)


def _inproj_kernel(x_ref, nw_ref, w_ref, u_ref, h_ref):
    @pl.when(pl.program_id(1) == 0)
    def _():
        x = x_ref[...]
        y = x * lax.rsqrt(jnp.mean(x * x, axis=-1, keepdims=True) + RMS_EPS)
        h_ref[...] = (y * nw_ref[...]).astype(h_ref.dtype)

    u_ref[...] = jnp.dot(h_ref[...], w_ref[...], preferred_element_type=jnp.float32).astype(u_ref.dtype)


def _inproj(x2, norm_w, w, *, tm=1024, tn=2816):
    n = x2.shape[0]
    return pl.pallas_call(
        _inproj_kernel,
        grid=(n // tm, D_IN // tn),
        in_specs=[pl.BlockSpec((tm, D_MODEL), lambda i, j: (i, 0)),
                  pl.BlockSpec((1, D_MODEL), lambda i, j: (0, 0)),
                  pl.BlockSpec((D_MODEL, tn), lambda i, j: (0, j))],
        out_specs=pl.BlockSpec((tm, tn), lambda i, j: (i, j)),
        out_shape=jax.ShapeDtypeStruct((n, D_IN), jnp.bfloat16),
        scratch_shapes=[pltpu.VMEM((tm, D_MODEL), jnp.bfloat16)],
        compiler_params=pltpu.CompilerParams(dimension_semantics=("parallel", "arbitrary"),
                                             vmem_limit_bytes=VMEM_LIMIT),
        name="inproj",
    )(x2, norm_w.reshape(1, D_MODEL), w)


def _head_lane_mask(hh, dtype):
    lane = lax.broadcasted_iota(jnp.int32, (1, LANES), 1)
    return ((lane // HEAD_DIM) == hh).astype(dtype)


def _qk(qm, k):
    return lax.dot_general(qm, k, (((1,), (1,)), ((), ())), preferred_element_type=jnp.float32)


def _merge_heads(parts):
    lane = lax.broadcasted_iota(jnp.int32, parts[0].shape, 1)
    return jnp.where(lane < HEAD_DIM, parts[0], parts[1])


SB_TILE = 256
SB_ROW_CHUNK = 256


def _sb_kernel(q_ref, k_ref, v_ref, tri_ref, o_ref, acc_ref, carry_ref):
    t = SB_TILE
    qi = pl.program_id(2)

    def step(newest, n_blocks, diagonal):
        tri = tri_ref[...]
        rc = SB_ROW_CHUNK
        chains = [(hh, r) for hh in range(HEADS_PER_TILE) for r in range(t // rc)]
        starts = [pl.multiple_of((newest - j) * t, t) for j in range(n_blocks)]

        terms = {}
        for hh, r in chains:
            qm = q_ref[r * rc:(r + 1) * rc, :] * _head_lane_mask(hh, q_ref.dtype)
            for j in range(n_blocks):
                z = _qk(qm, k_ref[pl.ds(starts[j], t), :])
                neg_abs = lax.bitcast_convert_type(
                    lax.bitcast_convert_type(z, jnp.uint32) | jnp.uint32(0x80000000), jnp.float32)
                soft = jnp.log(1.0 + jnp.exp2(neg_abs)) * LOG2_E
                log_beta = jnp.minimum(z, 0.0) - soft
                log_1m = log_beta - z
                mask = None
                if diagonal and j == 0:
                    row = lax.broadcasted_iota(jnp.int32, (rc, t), 0) + r * rc
                    mask = lax.broadcasted_iota(jnp.int32, (rc, t), 1) < row
                    log_1m = jnp.where(mask, log_1m, 0.0)
                hi = lax.bitcast_convert_type(
                    lax.bitcast_convert_type(log_1m, jnp.uint32) & jnp.uint32(0xFFFF0000), jnp.float32)
                split = jnp.concatenate([hi.astype(jnp.bfloat16), (log_1m - hi).astype(jnp.bfloat16)], axis=1)
                terms[hh, r, j] = (log_beta, log_1m[:, 0:1], split, mask)

        cums = {key: jnp.dot(split, tri, preferred_element_type=jnp.float32)
                for key, (_, _, split, _) in terms.items()}

        alive = None
        for hh, r in chains:
            rows = slice(r * rc, (r + 1) * rc)
            carry = carry_ref[hh, rows, :]
            acc = acc_ref[hh, rows, :]
            for j in range(n_blocks):
                log_beta, first_term, _, mask = terms[hh, r, j]
                cum = cums[hh, r, j]
                w = jnp.exp2(log_beta + cum + jnp.concatenate([carry] * (t // LANES), axis=1))
                if mask is not None:
                    w = jnp.where(mask, w, 0.0)
                acc = acc + jnp.dot(w.astype(jnp.bfloat16), v_ref[pl.ds(starts[j], t), :],
                                    preferred_element_type=jnp.float32)
                carry = carry + jnp.broadcast_to(cum[:, 0:1] + first_term, (rc, LANES))
            acc_ref[hh, rows, :] = acc
            carry_ref[hh, rows, :] = carry
            alive = carry if alive is None else jnp.maximum(alive, carry)
        return (jnp.max(alive) < SB_DEAD_LOG2_WEIGHT).astype(jnp.int32)

    acc_ref[...] = jnp.zeros_like(acc_ref)
    carry_ref[...] = jnp.zeros_like(carry_ref)
    dead = lax.cond(qi >= 1, lambda: step(qi, 2, True), lambda: step(qi, 1, True))

    def cond(c):
        newest, dead = c
        return jnp.logical_and(newest >= 1, dead == 0)

    def body(c):
        newest, _ = c
        return newest - 2, step(newest, 2, False)

    newest, dead = lax.while_loop(cond, body, (qi - 2, dead))

    @pl.when(jnp.logical_and(newest == 0, dead == 0))
    def _():
        step(0, 1, False)

    o_ref[...] = _merge_heads([acc_ref[0], acc_ref[1]])


def _sb_attention(u, tri, batch, seq):
    t = SB_TILE
    nq = seq // t
    return pl.pallas_call(
        _sb_kernel,
        grid=(batch, N_PAIRS, nq),
        in_specs=[pl.BlockSpec((t, LANES), lambda b, p, i: (b * nq + i, COL_QA // LANES + p)),
                  pl.BlockSpec((seq, LANES), lambda b, p, i: (b, COL_KA // LANES + p)),
                  pl.BlockSpec((seq, LANES), lambda b, p, i: (b, COL_VA // LANES + p)),
                  pl.BlockSpec((2 * t, t), lambda b, p, i: (0, 0))],
        out_specs=pl.BlockSpec((t, LANES), lambda b, p, i: (b * nq + i, p)),
        out_shape=jax.ShapeDtypeStruct((batch * seq, WIDTH), jnp.float32),
        scratch_shapes=[pltpu.VMEM((HEADS_PER_TILE, t, LANES), jnp.float32),
                        pltpu.VMEM((HEADS_PER_TILE, t, LANES), jnp.float32)],
        compiler_params=pltpu.CompilerParams(dimension_semantics=("parallel", "parallel", "arbitrary"),
                                             vmem_limit_bytes=VMEM_LIMIT),
        name="sb_attention",
    )(u, u, u, tri)


MOBA_LO_LANE = 64
MOBA_NEVER = 63
MOBA_ZERO_ROW = 64
MOBA_MASKED = -1e30
MOBA_GROUP = 8
MOBA_MIN_GROUP = 2


def _moba_kernel(far_ref, q_ref, k_ref, v_ref, btab_ref, oh_ref, o_ref,
                 kmean_ref, vt_ref, qa_ref, m_ref, acc_ref, *, nb):
    t = MOBA_BLOCK
    g = MOBA_GROUP
    pair = pl.program_id(1)
    own = pl.program_id(2)
    heads = range(HEADS_PER_TILE)

    @pl.when(own == 0)
    def _():
        kmean_ref[...] = jnp.zeros_like(kmean_ref)
        kf = k_ref[...].astype(jnp.float32).reshape(nb, t, LANES)
        kmean_ref[0:nb, :] = jnp.mean(kf, axis=1)

        head_of_row = lax.broadcasted_iota(jnp.int32, (LANES, t), 0) // HEAD_DIM

        def transpose_block(b, c):
            vb = v_ref[pl.ds(pl.multiple_of(b * t, t), t), :].astype(jnp.float32).T
            for hh in heads:
                vt_ref[hh, b] = jnp.where(head_of_row == hh, vb, 1.0).astype(jnp.bfloat16)
            return c

        lax.fori_loop(0, nb, transpose_block, 0)

    km = kmean_ref[...]
    km0 = km.astype(jnp.bfloat16)
    r1 = km - km0.astype(jnp.float32)
    km1 = r1.astype(jnp.bfloat16)
    km2 = (r1 - km1.astype(jnp.float32)).astype(jnp.bfloat16)

    def update(scores, blocks, first):
        col_max = [jnp.max(s, axis=0, keepdims=True) for s in scores]
        if first:
            m_new = col_max
        else:
            m_old = [m_ref[hh][0:1, :] for hh in heads]
            m_new = [jnp.maximum(m_old[hh], col_max[hh]) for hh in heads]
            alpha = [jnp.exp2(m_old[hh] - m_new[hh]) for hh in heads]
        p = [jnp.exp2(scores[hh] - m_new[hh]).astype(jnp.bfloat16) for hh in heads]
        pv = [jnp.dot(jnp.concatenate([vt_ref[hh, b] for b in blocks], axis=1), p[hh],
                      preferred_element_type=jnp.float32) for hh in heads]
        for hh in heads:
            acc_ref[hh] = pv[hh] if first else alpha[hh] * acc_ref[hh] + pv[hh]
            m_ref[hh] = jnp.broadcast_to(m_new[hh], m_ref.shape[1:])

    def block_lanes(row):
        return jnp.concatenate([oh_ref[row]] * (t // BF16_SUBLANES), axis=0)

    blk = lax.broadcasted_iota(jnp.int32, (MOBA_LO_LANE, t), 0)
    blk_f = blk.astype(jnp.float32)
    for hh in range(HEADS_PER_TILE):
        qm = q_ref[...] * _head_lane_mask(hh, q_ref.dtype)
        gate = _qk(km0, qm) + _qk(km1, qm) + _qk(km2, qm)
        gate = jnp.where(blk < own, gate, _NEG_INF)
        sel = jnp.zeros((MOBA_LO_LANE, t), jnp.float32)
        for _ in range(MOBA_TOPK):
            best = jnp.max(gate, axis=0, keepdims=True)
            idx = jnp.min(jnp.where(gate == best, blk_f, float(LANES)), axis=0, keepdims=True)
            hit = blk_f == idx
            sel = jnp.where(jnp.logical_and(hit, best > _NEG_INF), 1.0, sel)
            gate = jnp.where(hit, _NEG_INF, gate)

        far_bias = far_ref[pair * HEADS_PER_TILE + hh]
        term = jnp.where(sel > 0.0, jnp.where(blk < own - 1, far_bias, 0.0), MOBA_MASKED)
        hi = term.astype(jnp.bfloat16).astype(jnp.float32)
        inj = jnp.concatenate([hi, term - hi], axis=0).T
        qa_ref[hh] = jnp.concatenate([qm, inj.astype(jnp.bfloat16)], axis=1)

    prev = jnp.maximum(own - 1, 0)
    n_far = jnp.maximum(own - 1, 0)
    n_full = n_far // g
    left = n_far - n_full * g

    def far_keys(first_blk, group):
        g0 = jnp.minimum(first_blk, nb - group)
        rows = []
        for j in range(group):
            b = g0 + j
            active = jnp.logical_and(b >= first_blk, b < n_far)
            rows.append(block_lanes(jnp.where(active, b, MOBA_NEVER)))
        keys = k_ref[pl.ds(pl.multiple_of(g0 * t, t), group * t), :]
        return g0, keys, rows

    def first_step(group):
        keys = [k_ref[pl.ds(pl.multiple_of(own * t, t), t), :], k_ref[pl.ds(pl.multiple_of(prev * t, t), t), :]]
        rows = [block_lanes(MOBA_ZERO_ROW), block_lanes(jnp.where(own >= 1, prev, MOBA_NEVER))]
        if group:
            g0, far, far_rows = far_keys(n_full * g, group)
            keys.append(far)
            rows.extend(far_rows)
        k_aug = jnp.concatenate([jnp.concatenate(keys, axis=0), jnp.concatenate(rows, axis=0)], axis=1)
        scores = [_qk(k_aug, qa_ref[hh]) for hh in heads]
        update([scores[hh][:t] + btab_ref[hh, 0] for hh in heads], [own], first=True)
        update([scores[hh][t:2 * t] + btab_ref[hh, 1] for hh in heads], [prev], first=False)
        for j in range(group):
            update([s[(2 + j) * t:(3 + j) * t] for s in scores], [g0 + j], first=False)

    group = g
    while group >= MOBA_MIN_GROUP:
        lower = group // 2 if group > MOBA_MIN_GROUP else 0

        @pl.when(jnp.logical_and(left > lower, left <= group))
        def _(group=group):
            first_step(group)

        group //= 2

    @pl.when(left == 0)
    def _():
        first_step(0)

    def far_body(i, c):
        g0, keys, rows = far_keys(i * g, g)
        k_aug = jnp.concatenate([keys, jnp.concatenate(rows, axis=0)], axis=1)
        scores = [_qk(k_aug, qa_ref[hh]) for hh in heads]
        for j in range(g):
            update([s[j * t:(j + 1) * t] for s in scores], [g0 + j], first=False)
        return c

    lax.fori_loop(0, n_full, far_body, 0)

    out_t = jnp.concatenate([acc_ref[0][:HEAD_DIM] / acc_ref[0][HEAD_DIM:],
                             acc_ref[1][HEAD_DIM:] / acc_ref[1][:HEAD_DIM]], axis=0)
    o_ref[...] = out_t.T


def _moba_attention(u, far_bias, btab, block_lanes, batch, seq):
    t = MOBA_BLOCK
    nb = seq // t
    kernel = functools.partial(_moba_kernel, nb=nb)
    grid_spec = pltpu.PrefetchScalarGridSpec(
        num_scalar_prefetch=1,
        grid=(batch, N_PAIRS, nb),
        in_specs=[pl.BlockSpec((t, LANES), lambda b, p, i, far: (b * nb + i, COL_QB // LANES + p)),
                  pl.BlockSpec((seq, LANES), lambda b, p, i, far: (b, COL_KB // LANES + p)),
                  pl.BlockSpec((seq, LANES), lambda b, p, i, far: (b, COL_VB // LANES + p)),
                  pl.BlockSpec((HEADS_PER_TILE, 2, t, t), lambda b, p, i, far: (p, 0, 0, 0)),
                  pl.BlockSpec(block_lanes.shape, lambda b, p, i, far: (0, 0, 0))],
        out_specs=pl.BlockSpec((t, LANES), lambda b, p, i, far: (b * nb + i, p)),
        scratch_shapes=[pltpu.VMEM((MOBA_LO_LANE, LANES), jnp.float32),
                        pltpu.VMEM((HEADS_PER_TILE, nb, LANES, t), jnp.bfloat16),
                        pltpu.VMEM((HEADS_PER_TILE, t, 2 * LANES), jnp.bfloat16),
                        pltpu.VMEM((HEADS_PER_TILE, F32_SUBLANES, t), jnp.float32),
                        pltpu.VMEM((HEADS_PER_TILE, LANES, t), jnp.float32)])
    return pl.pallas_call(
        kernel,
        grid_spec=grid_spec,
        out_shape=jax.ShapeDtypeStruct((batch * seq, WIDTH), jnp.float32),
        compiler_params=pltpu.CompilerParams(dimension_semantics=("parallel", "parallel", "arbitrary"),
                                             vmem_limit_bytes=VMEM_LIMIT),
        name="moba_attention",
    )(far_bias, u, u, u, btab, block_lanes)


def _swa_kernel(sink_ref, q_ref, kp_ref, kc_ref, vp_ref, vc_ref, bias_ref, o_ref):
    w = SWA_WINDOW
    kw = jnp.concatenate([kp_ref[...], kc_ref[...]], axis=0)
    vw = jnp.concatenate([vp_ref[...], vc_ref[...]], axis=0)
    slots = [(j, half) for j in range(SWA_GROUP) for half in range(SWA_KV_HEADS)]

    def attend(first_block):
        scores = []
        for slot, (j, half) in enumerate(slots):
            q = q_ref[:, j * LANES:(j + 1) * LANES]
            s = _qk(q * _head_lane_mask(half, q.dtype), kw) + bias_ref[slot]
            if first_block:
                s = jnp.where(lax.broadcasted_iota(jnp.int32, (w, 2 * w), 1) >= w, s, _NEG_INF)
            scores.append(s)
        m = [jnp.maximum(jnp.broadcast_to(jnp.max(s, axis=1, keepdims=True), (w, LANES)), sink_ref[slot])
             for slot, s in enumerate(scores)]
        p = [jnp.exp2(s - jnp.concatenate([m[slot]] * (2 * w // LANES), axis=1)).astype(jnp.bfloat16)
             for slot, s in enumerate(scores)]
        pv = [jnp.dot(p[slot], jnp.where(_head_lane_mask(half, jnp.int32) > 0, vw, jnp.ones_like(vw)),
                      preferred_element_type=jnp.float32) for slot, (j, half) in enumerate(slots)]
        out = [pv[slot] / (pltpu.roll(pv[slot], HEAD_DIM, axis=1) + jnp.exp2(sink_ref[slot] - m[slot]))
               for slot in range(len(slots))]
        for j in range(SWA_GROUP):
            o_ref[:, j * LANES:(j + 1) * LANES] = _merge_heads(out[j * SWA_KV_HEADS:(j + 1) * SWA_KV_HEADS])

    @pl.when(pl.program_id(1) == 0)
    def _():
        attend(True)

    @pl.when(pl.program_id(1) > 0)
    def _():
        attend(False)


def _swa_attention(u, sinks_perm, bias, batch, seq):
    w = SWA_WINDOW
    nblk = seq // w
    kcol, vcol = COL_KC // LANES, COL_VC // LANES
    grid_spec = pltpu.PrefetchScalarGridSpec(
        num_scalar_prefetch=1,
        grid=(batch, nblk),
        in_specs=[pl.BlockSpec((w, WIDTH), lambda b, n, s: (b * nblk + n, COL_QC // WIDTH)),
                  pl.BlockSpec((w, LANES), lambda b, n, s: (b * nblk + jnp.maximum(n - 1, 0), kcol)),
                  pl.BlockSpec((w, LANES), lambda b, n, s: (b * nblk + n, kcol)),
                  pl.BlockSpec((w, LANES), lambda b, n, s: (b * nblk + jnp.maximum(n - 1, 0), vcol)),
                  pl.BlockSpec((w, LANES), lambda b, n, s: (b * nblk + n, vcol)),
                  pl.BlockSpec((N_HEADS, w, 2 * w), lambda b, n, s: (0, 0, 0))],
        out_specs=pl.BlockSpec((w, WIDTH), lambda b, n, s: (b * nblk + n, 0)))
    return pl.pallas_call(
        _swa_kernel,
        grid_spec=grid_spec,
        out_shape=jax.ShapeDtypeStruct((batch * seq, WIDTH), jnp.float32),
        compiler_params=pltpu.CompilerParams(dimension_semantics=("parallel", "arbitrary"),
                                             vmem_limit_bytes=VMEM_LIMIT),
        name="swa_attention",
    )(sinks_perm, u, u, u, u, u, bias)


def _post_kernel(x_ref, ya_ref, yb_ref, yc_ref, ga_ref, gb_ref, gc_ref, ma_ref, mb_ref, mc_ref,
                 wa_ref, wb_ref, wc_ref, wo_ref, fw_ref, o_ref, *, final):
    def branch(y_ref, g_ref, w_ref, m_ref):
        g = g_ref[...].astype(jnp.float32)
        y = (y_ref[...] * (g * jax.nn.sigmoid(g))).astype(jnp.bfloat16)
        proj = jnp.dot(y, w_ref[...], preferred_element_type=jnp.float32)
        return jax.nn.sigmoid(m_ref[...].astype(jnp.float32)) * proj

    merged = (branch(ya_ref, ga_ref, wa_ref, ma_ref) + branch(yb_ref, gb_ref, wb_ref, mb_ref)
              + branch(yc_ref, gc_ref, wc_ref, mc_ref))
    out = x_ref[...] + jnp.dot(merged.astype(jnp.bfloat16), wo_ref[...], preferred_element_type=jnp.float32)
    if final:
        out = out * lax.rsqrt(jnp.mean(out * out, axis=-1, keepdims=True) + RMS_EPS) * fw_ref[...]
    o_ref[...] = out


def _post(x2, ya, yb, yc, u, wa, wb, wc, wo, fw, *, final, tm=256):
    n = x2.shape[0]
    row = lambda i: (i, 0)
    full = lambda i: (0, 0)
    y_spec = pl.BlockSpec((tm, WIDTH), row)
    in_specs = [pl.BlockSpec((tm, D_MODEL), row), y_spec, y_spec, y_spec,
                pl.BlockSpec((tm, WIDTH), lambda i: (i, COL_GA // WIDTH)),
                pl.BlockSpec((tm, WIDTH), lambda i: (i, COL_GB // WIDTH)),
                pl.BlockSpec((tm, WIDTH), lambda i: (i, COL_GC // WIDTH)),
                pl.BlockSpec((tm, D_MODEL), lambda i: (i, COL_MA // D_MODEL)),
                pl.BlockSpec((tm, D_MODEL), lambda i: (i, COL_MB // D_MODEL)),
                pl.BlockSpec((tm, D_MODEL), lambda i: (i, COL_MC // D_MODEL)),
                pl.BlockSpec((WIDTH, D_MODEL), full), pl.BlockSpec((WIDTH, D_MODEL), full),
                pl.BlockSpec((WIDTH, D_MODEL), full), pl.BlockSpec((D_MODEL, D_MODEL), full),
                pl.BlockSpec((1, D_MODEL), full)]
    return pl.pallas_call(
        functools.partial(_post_kernel, final=final),
        grid=(n // tm,),
        in_specs=in_specs,
        out_specs=pl.BlockSpec((tm, D_MODEL), row),
        out_shape=jax.ShapeDtypeStruct((n, D_MODEL), jnp.float32),
        compiler_params=pltpu.CompilerParams(dimension_semantics=("parallel",),
                                             vmem_limit_bytes=VMEM_LIMIT),
        name="post_final" if final else "post",
    )(x2, ya, yb, yc, u, u, u, u, u, u, wa, wb, wc, wo, fw.reshape(1, D_MODEL))


def _bucket_lookup(table, dist):
    onehot = (_rel_bucket(dist)[..., None] == jnp.arange(REL_BUCKETS)).astype(jnp.float32)
    return jnp.einsum("...b,bh->h...", onehot, table.astype(jnp.float32), precision=lax.Precision.HIGHEST)


def _bias_tables(rel_bias):
    moba_tab = rel_bias[:, :N_HEADS]
    swa_tab = jnp.stack([rel_bias[:, N_HEADS + h] for h in _swa_head_perm()], axis=1)
    t = MOBA_BLOCK
    d_own = jnp.arange(t)[:, None] - jnp.arange(t)[None, :]
    own = jnp.where(d_own[None] >= 0, _bucket_lookup(moba_tab, d_own), _NEG_INF)
    prev = _bucket_lookup(moba_tab, d_own + t)
    btab = jnp.swapaxes(jnp.stack([own, prev], axis=1), 2, 3) * LOG2_E
    far = _bucket_lookup(moba_tab, jnp.full((1,), t + 1, jnp.int32))[:, 0] * LOG2_E

    w = SWA_WINDOW
    dist = jnp.arange(w)[:, None] + w - jnp.arange(2 * w)[None, :]
    in_band = (dist >= 0) & (dist < w)
    swa = jnp.where(in_band[None], _bucket_lookup(swa_tab, dist) * LOG2_E, _NEG_INF)
    return btab, far, swa


def kernel(x, norm_w, w_in, w_proj_a, w_proj_b, w_proj_c, w_out, sinks, rel_bias, final_norm_w):
    batch, seq, _ = x.shape
    depth = w_in.shape[0]
    assert seq % MOBA_BLOCK == 0 and MOBA_GROUP <= seq // MOBA_BLOCK <= MOBA_NEVER
    btab, far, swa_bias = _bias_tables(rel_bias)
    t = SB_TILE
    tri = (jnp.arange(t)[:, None] > jnp.arange(t)[None, :]).astype(jnp.bfloat16)
    tri = jnp.concatenate([tri, tri], axis=0)
    lane = jnp.arange(LANES)[None, :]
    blk = jnp.arange(MOBA_ZERO_ROW + 1)[:, None]
    block_lanes = ((lane == blk) | (lane == blk + MOBA_LO_LANE)) & (blk < MOBA_ZERO_ROW)
    block_lanes = jnp.broadcast_to(block_lanes[:, None, :],
                                   (MOBA_ZERO_ROW + 1, BF16_SUBLANES, LANES)).astype(jnp.bfloat16)
    head_perm = _swa_head_perm()
    sinks_perm = jnp.stack([sinks[:, h] for h in head_perm], axis=1).astype(jnp.float32) * LOG2_E

    x2 = x.reshape(batch * seq, D_MODEL)
    for layer in range(depth):
        u = _inproj(x2, norm_w[layer], _permute_input_columns(w_in[layer]))
        ya = _sb_attention(u, tri, batch, seq)
        yb = _moba_attention(u, far, btab, block_lanes, batch, seq)
        yc = _swa_attention(u, sinks_perm[layer], swa_bias, batch, seq)
        wc = jnp.concatenate([w_proj_c[layer][h * HEAD_DIM:(h + 1) * HEAD_DIM] for h in head_perm], axis=0)
        x2 = _post(x2, ya, yb, yc, u,
                   w_proj_a[layer].astype(jnp.bfloat16), w_proj_b[layer].astype(jnp.bfloat16),
                   wc.astype(jnp.bfloat16), w_out[layer].astype(jnp.bfloat16),
                   final_norm_w, final=(layer == depth - 1))
    return x2.reshape(batch, seq, D_MODEL)
```

```python
import functools
import math

import numpy as np
import jax
import jax.numpy as jnp
from jax import lax
from jax.experimental import pallas as pl
from jax.experimental.pallas import tpu as pltpu

D_MODEL = 1024
HEAD_DIM = 64
N_HEADS = 8
WIDTH = N_HEADS * HEAD_DIM
MOBA_BLOCK = 256
MOBA_TOPK = 3
SWA_KV_HEADS = 2
SWA_GROUP = N_HEADS // SWA_KV_HEADS
SWA_WINDOW = 128
REL_BUCKETS = 32
REL_MAX_DIST = 128
RMS_EPS = 1e-6

LANES = 128
BF16_SUBLANES = 16
F32_SUBLANES = 8
HEADS_PER_TILE = LANES // HEAD_DIM
N_PAIRS = N_HEADS // HEADS_PER_TILE

COL_QA, COL_KA, COL_VA = 0, 512, 1024
COL_QB, COL_KB, COL_VB = 1536, 2048, 2560
COL_QC = 3072
COL_GA, COL_GB, COL_GC = 3584, 4096, 4608
COL_MA, COL_MB, COL_MC = 5120, 6144, 7168
COL_KC, COL_VC = 8192, 8320
D_IN = 8448

VMEM_LIMIT = 48 * 1024 * 1024

LOG2_E = math.log2(math.e)

SB_DEAD_LOG_WEIGHT = -120.0
SB_DEAD_LOG2_WEIGHT = SB_DEAD_LOG_WEIGHT * LOG2_E

_NEG_INF = float("-inf")


def _rel_bucket(dist):
    max_exact = REL_BUCKETS // 2
    n = jnp.maximum(dist, 0)
    nf = jnp.maximum(n, 1).astype(jnp.float32)
    large = max_exact + (jnp.log(nf / max_exact) / math.log(REL_MAX_DIST / max_exact)
                         * (REL_BUCKETS - max_exact)).astype(jnp.int32)
    large = jnp.minimum(large, REL_BUCKETS - 1)
    return jnp.where(n < max_exact, n, large)


def _swa_head_order(t, axis, per_head=HEAD_DIM):
    axis = axis % t.ndim
    shape = t.shape
    t = t.reshape(shape[:axis] + (SWA_KV_HEADS, SWA_GROUP, per_head) + shape[axis + 1:])
    return jnp.swapaxes(t, axis, axis + 1).reshape(shape)


def _permute_input_columns(w):
    old = {"qa": 0, "ka": 512, "va": 1024, "ga": 1536, "qb": 2048, "kb": 2560, "vb": 3072, "gb": 3584,
           "qc": 4096, "kc": 4608, "vc": 4736, "gc": 4864, "ma": 5376, "mb": 6400, "mc": 7424}
    q_scale = HEAD_DIM ** -0.5 * LOG2_E

    def cols(name, width, scale=None):
        part = w[..., old[name]:old[name] + width]
        return part if scale is None else part * scale

    parts = [cols("qa", 512, q_scale), cols("ka", 512), cols("va", 512),
             cols("qb", 512, q_scale), cols("kb", 512), cols("vb", 512),
             _swa_head_order(cols("qc", WIDTH, q_scale), -1),
             cols("ga", 512), cols("gb", 512), _swa_head_order(cols("gc", WIDTH), -1),
             cols("ma", 1024), cols("mb", 1024), cols("mc", 1024),
             cols("kc", 128), cols("vc", 128)]
    return jnp.concatenate(parts, axis=-1).astype(jnp.bfloat16)


def _inproj_kernel(x_ref, nw_ref, w_ref, u_ref, h_ref):
    @pl.when(pl.program_id(1) == 0)
    def _():
        x = x_ref[...]
        y = x * lax.rsqrt(jnp.mean(x * x, axis=-1, keepdims=True) + RMS_EPS)
        h_ref[...] = (y * nw_ref[...]).astype(h_ref.dtype)

    u_ref[...] = jnp.dot(h_ref[...], w_ref[...], preferred_element_type=jnp.float32).astype(u_ref.dtype)


def _inproj(x2, norm_w, w, layer, *, tm=1024, tn=2816):
    n = x2.shape[0]
    return pl.pallas_call(
        _inproj_kernel,
        grid=(n // tm, D_IN // tn),
        in_specs=[pl.BlockSpec((tm, D_MODEL), lambda i, j: (i, 0)),
                  pl.BlockSpec((None, 1, D_MODEL), lambda i, j: (layer, 0, 0)),
                  pl.BlockSpec((None, D_MODEL, tn), lambda i, j: (layer, 0, j))],
        out_specs=pl.BlockSpec((tm, tn), lambda i, j: (i, j)),
        out_shape=jax.ShapeDtypeStruct((n, D_IN), jnp.bfloat16),
        scratch_shapes=[pltpu.VMEM((tm, D_MODEL), jnp.bfloat16)],
        compiler_params=pltpu.CompilerParams(dimension_semantics=("parallel", "arbitrary"),
                                             vmem_limit_bytes=VMEM_LIMIT),
        name="inproj",
    )(x2, norm_w, w)


def _head_lane_mask(hh, dtype):
    lane = lax.broadcasted_iota(jnp.int32, (1, LANES), 1)
    return ((lane // HEAD_DIM) == hh).astype(dtype)


def _qk(qm, k):
    return lax.dot_general(qm, k, (((1,), (1,)), ((), ())), preferred_element_type=jnp.float32)


def _merge_heads(parts):
    lane = lax.broadcasted_iota(jnp.int32, parts[0].shape, 1)
    return jnp.where(lane < HEAD_DIM, parts[0], parts[1])


SB_TILE = 256
SB_ROW_CHUNK = 256


def _sb_kernel(q_ref, k_ref, v_ref, tri_ref, o_ref, acc_ref, carry_ref):
    t = SB_TILE
    qi = pl.program_id(2)

    def step(newest, n_blocks, diagonal):
        tri = tri_ref[...]
        rc = SB_ROW_CHUNK
        chains = [(hh, r) for hh in range(HEADS_PER_TILE) for r in range(t // rc)]
        starts = [pl.multiple_of((newest - j) * t, t) for j in range(n_blocks)]

        terms = {}
        for hh, r in chains:
            qm = q_ref[r * rc:(r + 1) * rc, :] * _head_lane_mask(hh, q_ref.dtype)
            for j in range(n_blocks):
                z = _qk(qm, k_ref[pl.ds(starts[j], t), :])
                neg_abs = lax.bitcast_convert_type(
                    lax.bitcast_convert_type(z, jnp.uint32) | jnp.uint32(0x80000000), jnp.float32)
                soft = jnp.log(1.0 + jnp.exp2(neg_abs)) * LOG2_E
                log_beta = jnp.minimum(z, 0.0) - soft
                log_1m = log_beta - z
                mask = None
                if diagonal and j == 0:
                    row = lax.broadcasted_iota(jnp.int32, (rc, t), 0) + r * rc
                    mask = lax.broadcasted_iota(jnp.int32, (rc, t), 1) < row
                    log_1m = jnp.where(mask, log_1m, 0.0)
                hi = lax.bitcast_convert_type(
                    lax.bitcast_convert_type(log_1m, jnp.uint32) & jnp.uint32(0xFFFF0000), jnp.float32)
                split = jnp.concatenate([hi.astype(jnp.bfloat16), (log_1m - hi).astype(jnp.bfloat16)], axis=1)
                terms[hh, r, j] = (log_beta, log_1m[:, 0:1], split, mask)

        cums = {key: jnp.dot(split, tri, preferred_element_type=jnp.float32)
                for key, (_, _, split, _) in terms.items()}

        alive = None
        for hh, r in chains:
            rows = slice(r * rc, (r + 1) * rc)
            carry = carry_ref[hh, rows, :]
            acc = acc_ref[hh, rows, :]
            for j in range(n_blocks):
                log_beta, first_term, _, mask = terms[hh, r, j]
                cum = cums[hh, r, j]
                w = jnp.exp2(log_beta + cum + jnp.concatenate([carry] * (t // LANES), axis=1))
                if mask is not None:
                    w = jnp.where(mask, w, 0.0)
                acc = acc + jnp.dot(w.astype(jnp.bfloat16), v_ref[pl.ds(starts[j], t), :],
                                    preferred_element_type=jnp.float32)
                carry = carry + jnp.broadcast_to(cum[:, 0:1] + first_term, (rc, LANES))
            acc_ref[hh, rows, :] = acc
            carry_ref[hh, rows, :] = carry
            alive = carry if alive is None else jnp.maximum(alive, carry)
        return (jnp.max(alive) < SB_DEAD_LOG2_WEIGHT).astype(jnp.int32)

    acc_ref[...] = jnp.zeros_like(acc_ref)
    carry_ref[...] = jnp.zeros_like(carry_ref)
    dead = lax.cond(qi >= 1, lambda: step(qi, 2, True), lambda: step(qi, 1, True))

    def cond(c):
        newest, dead = c
        return jnp.logical_and(newest >= 1, dead == 0)

    def body(c):
        newest, _ = c
        return newest - 2, step(newest, 2, False)

    newest, dead = lax.while_loop(cond, body, (qi - 2, dead))

    @pl.when(jnp.logical_and(newest == 0, dead == 0))
    def _():
        step(0, 1, False)

    o_ref[...] = _merge_heads([acc_ref[0], acc_ref[1]])


def _sb_attention(u, tri, batch, seq):
    t = SB_TILE
    nq = seq // t
    return pl.pallas_call(
        _sb_kernel,
        grid=(batch, N_PAIRS, nq),
        in_specs=[pl.BlockSpec((t, LANES), lambda b, p, i: (b * nq + i, COL_QA // LANES + p)),
                  pl.BlockSpec((seq, LANES), lambda b, p, i: (b, COL_KA // LANES + p)),
                  pl.BlockSpec((seq, LANES), lambda b, p, i: (b, COL_VA // LANES + p)),
                  pl.BlockSpec((2 * t, t), lambda b, p, i: (0, 0))],
        out_specs=pl.BlockSpec((t, LANES), lambda b, p, i: (b * nq + i, p)),
        out_shape=jax.ShapeDtypeStruct((batch * seq, WIDTH), jnp.float32),
        scratch_shapes=[pltpu.VMEM((HEADS_PER_TILE, t, LANES), jnp.float32),
                        pltpu.VMEM((HEADS_PER_TILE, t, LANES), jnp.float32)],
        compiler_params=pltpu.CompilerParams(dimension_semantics=("parallel", "parallel", "arbitrary"),
                                             vmem_limit_bytes=VMEM_LIMIT),
        name="sb_attention",
    )(u, u, u, tri)


MOBA_LO_LANE = 64
MOBA_NEVER = 63
MOBA_ZERO_ROW = 64
MOBA_MASKED = -1e30
MOBA_GROUP = 8
MOBA_MIN_GROUP = 2


def _moba_kernel(far_ref, q_ref, k_ref, v_ref, btab_ref, oh_ref, o_ref,
                 kmean_ref, vt_ref, qa_ref, m_ref, acc_ref, *, nb):
    t = MOBA_BLOCK
    g = MOBA_GROUP
    pair = pl.program_id(1)
    own = pl.program_id(2)
    heads = range(HEADS_PER_TILE)

    @pl.when(own == 0)
    def _():
        kmean_ref[...] = jnp.zeros_like(kmean_ref)
        kf = k_ref[...].astype(jnp.float32).reshape(nb, t, LANES)
        kmean_ref[0:nb, :] = jnp.mean(kf, axis=1)

        head_of_row = lax.broadcasted_iota(jnp.int32, (LANES, t), 0) // HEAD_DIM

        def transpose_block(b, c):
            vb = v_ref[pl.ds(pl.multiple_of(b * t, t), t), :].astype(jnp.float32).T
            for hh in heads:
                vt_ref[hh, b] = jnp.where(head_of_row == hh, vb, 1.0).astype(jnp.bfloat16)
            return c

        lax.fori_loop(0, nb, transpose_block, 0)

    km = kmean_ref[...]
    km0 = km.astype(jnp.bfloat16)
    r1 = km - km0.astype(jnp.float32)
    km1 = r1.astype(jnp.bfloat16)
    km2 = (r1 - km1.astype(jnp.float32)).astype(jnp.bfloat16)

    def update(scores, blocks, first):
        col_max = [jnp.max(s, axis=0, keepdims=True) for s in scores]
        if first:
            m_new = col_max
        else:
            m_old = [m_ref[hh][0:1, :] for hh in heads]
            m_new = [jnp.maximum(m_old[hh], col_max[hh]) for hh in heads]
            alpha = [jnp.exp2(m_old[hh] - m_new[hh]) for hh in heads]
        p = [jnp.exp2(scores[hh] - m_new[hh]).astype(jnp.bfloat16) for hh in heads]
        pv = [jnp.dot(jnp.concatenate([vt_ref[hh, b] for b in blocks], axis=1), p[hh],
                      preferred_element_type=jnp.float32) for hh in heads]
        for hh in heads:
            acc_ref[hh] = pv[hh] if first else alpha[hh] * acc_ref[hh] + pv[hh]
            m_ref[hh] = jnp.broadcast_to(m_new[hh], m_ref.shape[1:])

    def block_lanes(row):
        return jnp.concatenate([oh_ref[row]] * (t // BF16_SUBLANES), axis=0)

    nh = HEADS_PER_TILE
    blk = lax.broadcasted_iota(jnp.int32, (MOBA_LO_LANE, nh * t), 0)
    blk_f = blk.astype(jnp.float32)
    gate_rows = sum(_qk(jnp.concatenate([part * _head_lane_mask(hh, part.dtype) for hh in heads], axis=0),
                        q_ref[...]) for part in (km0, km1, km2))
    gate = jnp.concatenate([gate_rows[hh * MOBA_LO_LANE:(hh + 1) * MOBA_LO_LANE] for hh in heads], axis=1)
    gate = jnp.where(blk < own, gate, _NEG_INF)
    sel = jnp.zeros_like(gate)
    for _ in range(MOBA_TOPK):
        best = jnp.max(gate, axis=0, keepdims=True)
        idx = jnp.min(jnp.where(gate == best, blk_f, float(LANES)), axis=0, keepdims=True)
        hit = blk_f == idx
        sel = jnp.where(jnp.logical_and(hit, best > _NEG_INF), 1.0, sel)
        gate = jnp.where(hit, _NEG_INF, gate)

    col_head = lax.broadcasted_iota(jnp.int32, (1, nh * t), 1) // t
    far_bias = jnp.where(col_head == 0, far_ref[pair * nh], far_ref[pair * nh + 1])
    term = jnp.where(sel > 0.0, jnp.where(blk < own - 1, far_bias, 0.0), MOBA_MASKED)
    hi = term.astype(jnp.bfloat16).astype(jnp.float32)
    lo = term - hi
    for hh in heads:
        cols = slice(hh * t, (hh + 1) * t)
        inj = jnp.concatenate([hi[:, cols], lo[:, cols]], axis=0).T
        qa_ref[hh] = jnp.concatenate([q_ref[...] * _head_lane_mask(hh, q_ref.dtype), inj.astype(jnp.bfloat16)],
                                     axis=1)

    prev = jnp.maximum(own - 1, 0)
    n_far = jnp.maximum(own - 1, 0)
    n_full = n_far // g
    left = n_far - n_full * g

    def far_keys(first_blk, group):
        g0 = jnp.minimum(first_blk, nb - group)
        rows = []
        for j in range(group):
            b = g0 + j
            active = jnp.logical_and(b >= first_blk, b < n_far)
            rows.append(block_lanes(jnp.where(active, b, MOBA_NEVER)))
        keys = k_ref[pl.ds(pl.multiple_of(g0 * t, t), group * t), :]
        return g0, keys, rows

    def first_step(group):
        keys = [k_ref[pl.ds(pl.multiple_of(own * t, t), t), :], k_ref[pl.ds(pl.multiple_of(prev * t, t), t), :]]
        rows = [block_lanes(MOBA_ZERO_ROW), block_lanes(jnp.where(own >= 1, prev, MOBA_NEVER))]
        if group:
            g0, far, far_rows = far_keys(n_full * g, group)
            keys.append(far)
            rows.extend(far_rows)
        k_aug = jnp.concatenate([jnp.concatenate(keys, axis=0), jnp.concatenate(rows, axis=0)], axis=1)
        scores = [_qk(k_aug, qa_ref[hh]) for hh in heads]
        update([scores[hh][:t] + btab_ref[hh, 0] for hh in heads], [own], first=True)
        update([scores[hh][t:2 * t] + btab_ref[hh, 1] for hh in heads], [prev], first=False)
        for j in range(group):
            update([s[(2 + j) * t:(3 + j) * t] for s in scores], [g0 + j], first=False)

    group = g
    while group >= MOBA_MIN_GROUP:
        lower = group // 2 if group > MOBA_MIN_GROUP else 0

        @pl.when(jnp.logical_and(left > lower, left <= group))
        def _(group=group):
            first_step(group)

        group //= 2

    @pl.when(left == 0)
    def _():
        first_step(0)

    def far_body(i, c):
        g0, keys, rows = far_keys(i * g, g)
        k_aug = jnp.concatenate([keys, jnp.concatenate(rows, axis=0)], axis=1)
        scores = [_qk(k_aug, qa_ref[hh]) for hh in heads]
        for j in range(g):
            update([s[j * t:(j + 1) * t] for s in scores], [g0 + j], first=False)
        return c

    lax.fori_loop(0, n_full, far_body, 0)

    out_t = jnp.concatenate([acc_ref[0][:HEAD_DIM] / acc_ref[0][HEAD_DIM:],
                             acc_ref[1][HEAD_DIM:] / acc_ref[1][:HEAD_DIM]], axis=0)
    o_ref[...] = out_t.T


def _moba_attention(u, far_bias, btab, block_lanes, batch, seq):
    t = MOBA_BLOCK
    nb = seq // t
    kernel = functools.partial(_moba_kernel, nb=nb)
    grid_spec = pltpu.PrefetchScalarGridSpec(
        num_scalar_prefetch=1,
        grid=(batch, N_PAIRS, nb),
        in_specs=[pl.BlockSpec((t, LANES), lambda b, p, i, far: (b * nb + i, COL_QB // LANES + p)),
                  pl.BlockSpec((seq, LANES), lambda b, p, i, far: (b, COL_KB // LANES + p)),
                  pl.BlockSpec((seq, LANES), lambda b, p, i, far: (b, COL_VB // LANES + p)),
                  pl.BlockSpec((HEADS_PER_TILE, 2, t, t), lambda b, p, i, far: (p, 0, 0, 0)),
                  pl.BlockSpec(block_lanes.shape, lambda b, p, i, far: (0, 0, 0))],
        out_specs=pl.BlockSpec((t, LANES), lambda b, p, i, far: (b * nb + i, p)),
        scratch_shapes=[pltpu.VMEM((MOBA_LO_LANE, LANES), jnp.float32),
                        pltpu.VMEM((HEADS_PER_TILE, nb, LANES, t), jnp.bfloat16),
                        pltpu.VMEM((HEADS_PER_TILE, t, 2 * LANES), jnp.bfloat16),
                        pltpu.VMEM((HEADS_PER_TILE, F32_SUBLANES, t), jnp.float32),
                        pltpu.VMEM((HEADS_PER_TILE, LANES, t), jnp.float32)])
    return pl.pallas_call(
        kernel,
        grid_spec=grid_spec,
        out_shape=jax.ShapeDtypeStruct((batch * seq, WIDTH), jnp.float32),
        compiler_params=pltpu.CompilerParams(dimension_semantics=("parallel", "parallel", "arbitrary"),
                                             vmem_limit_bytes=VMEM_LIMIT),
        name="moba_attention",
    )(far_bias, u, u, u, btab, block_lanes)


SWA_STEP_WINDOWS = 2


def _swa_kernel(sink_ref, q_ref, kp_ref, kc_ref, vp_ref, vc_ref, bias_ref, o_ref):
    w = SWA_WINDOW
    keys = jnp.concatenate([kp_ref[...], kc_ref[...]], axis=0)
    values = jnp.concatenate([vp_ref[...], vc_ref[...]], axis=0)
    slots = [(j, half) for j in range(SWA_GROUP) for half in range(SWA_KV_HEADS)]
    chains = [(win, slot) for win in range(SWA_STEP_WINDOWS) for slot in range(len(slots))]

    def attend(first_step):
        scores = []
        for win, slot in chains:
            j, half = slots[slot]
            q = q_ref[win * w:(win + 1) * w, j * LANES:(j + 1) * LANES]
            s = _qk(q * _head_lane_mask(half, q.dtype), keys[win * w:(win + 2) * w]) + bias_ref[slot]
            if first_step and win == 0:
                s = jnp.where(lax.broadcasted_iota(jnp.int32, (w, 2 * w), 1) >= w, s, _NEG_INF)
            scores.append(s)
        m = [jnp.maximum(jnp.broadcast_to(jnp.max(s, axis=1, keepdims=True), (w, LANES)), sink_ref[slot])
             for (win, slot), s in zip(chains, scores)]
        p = [jnp.exp2(s - jnp.concatenate([mi] * (2 * w // LANES), axis=1)).astype(jnp.bfloat16)
             for mi, s in zip(m, scores)]
        pv = []
        for (win, slot), pi in zip(chains, p):
            vw = values[win * w:(win + 2) * w]
            vw = jnp.where(_head_lane_mask(slots[slot][1], jnp.int32) > 0, vw, jnp.ones_like(vw))
            pv.append(jnp.dot(pi, vw, preferred_element_type=jnp.float32))
        out = [pvi / (pltpu.roll(pvi, HEAD_DIM, axis=1) + jnp.exp2(sink_ref[slot] - mi))
               for (win, slot), pvi, mi in zip(chains, pv, m)]
        for win in range(SWA_STEP_WINDOWS):
            for j in range(SWA_GROUP):
                first = win * len(slots) + j * SWA_KV_HEADS
                o_ref[win * w:(win + 1) * w, j * LANES:(j + 1) * LANES] = _merge_heads(
                    out[first:first + SWA_KV_HEADS])

    @pl.when(pl.program_id(1) == 0)
    def _():
        attend(True)

    @pl.when(pl.program_id(1) > 0)
    def _():
        attend(False)


def _swa_attention(u, sinks_perm, bias, batch, seq):
    w = SWA_WINDOW
    sw = SWA_STEP_WINDOWS
    nstep = seq // (sw * w)
    kcol, vcol = COL_KC // LANES, COL_VC // LANES
    prev_window = lambda b, n, s: b * (seq // w) + jnp.maximum(sw * n - 1, 0)
    grid_spec = pltpu.PrefetchScalarGridSpec(
        num_scalar_prefetch=1,
        grid=(batch, nstep),
        in_specs=[pl.BlockSpec((sw * w, WIDTH), lambda b, n, s: (b * nstep + n, COL_QC // WIDTH)),
                  pl.BlockSpec((w, LANES), lambda b, n, s: (prev_window(b, n, s), kcol)),
                  pl.BlockSpec((sw * w, LANES), lambda b, n, s: (b * nstep + n, kcol)),
                  pl.BlockSpec((w, LANES), lambda b, n, s: (prev_window(b, n, s), vcol)),
                  pl.BlockSpec((sw * w, LANES), lambda b, n, s: (b * nstep + n, vcol)),
                  pl.BlockSpec((N_HEADS, w, 2 * w), lambda b, n, s: (0, 0, 0))],
        out_specs=pl.BlockSpec((sw * w, WIDTH), lambda b, n, s: (b * nstep + n, 0)))
    return pl.pallas_call(
        _swa_kernel,
        grid_spec=grid_spec,
        out_shape=jax.ShapeDtypeStruct((batch * seq, WIDTH), jnp.float32),
        compiler_params=pltpu.CompilerParams(dimension_semantics=("parallel", "arbitrary"),
                                             vmem_limit_bytes=VMEM_LIMIT),
        name="swa_attention",
    )(sinks_perm, u, u, u, u, u, bias)


def _post_kernel(x_ref, ya_ref, yb_ref, yc_ref, ga_ref, gb_ref, gc_ref, ma_ref, mb_ref, mc_ref,
                 wa_ref, wb_ref, wc_ref, wo_ref, fw_ref, o_ref, *, final):
    def branch(y_ref, g_ref, w_ref, m_ref):
        g = g_ref[...].astype(jnp.float32)
        y = (y_ref[...] * (g * jax.nn.sigmoid(g))).astype(jnp.bfloat16)
        proj = jnp.dot(y, w_ref[...], preferred_element_type=jnp.float32)
        return jax.nn.sigmoid(m_ref[...].astype(jnp.float32)) * proj

    merged = (branch(ya_ref, ga_ref, wa_ref, ma_ref) + branch(yb_ref, gb_ref, wb_ref, mb_ref)
              + branch(yc_ref, gc_ref, wc_ref, mc_ref))
    out = x_ref[...] + jnp.dot(merged.astype(jnp.bfloat16), wo_ref[...], preferred_element_type=jnp.float32)
    if final:
        out = out * lax.rsqrt(jnp.mean(out * out, axis=-1, keepdims=True) + RMS_EPS) * fw_ref[...]
    o_ref[...] = out


def _post(x2, ya, yb, yc, u, wa, wb, wc, wo, fw, layer, *, final, tm=256):
    n = x2.shape[0]
    row = lambda i: (i, 0)
    full = lambda i: (0, 0)
    of_layer = lambda i: (layer, 0, 0)
    y_spec = pl.BlockSpec((tm, WIDTH), row)
    in_specs = [pl.BlockSpec((tm, D_MODEL), row), y_spec, y_spec, y_spec,
                pl.BlockSpec((tm, WIDTH), lambda i: (i, COL_GA // WIDTH)),
                pl.BlockSpec((tm, WIDTH), lambda i: (i, COL_GB // WIDTH)),
                pl.BlockSpec((tm, WIDTH), lambda i: (i, COL_GC // WIDTH)),
                pl.BlockSpec((tm, D_MODEL), lambda i: (i, COL_MA // D_MODEL)),
                pl.BlockSpec((tm, D_MODEL), lambda i: (i, COL_MB // D_MODEL)),
                pl.BlockSpec((tm, D_MODEL), lambda i: (i, COL_MC // D_MODEL)),
                pl.BlockSpec((None, WIDTH, D_MODEL), of_layer), pl.BlockSpec((None, WIDTH, D_MODEL), of_layer),
                pl.BlockSpec((None, WIDTH, D_MODEL), of_layer), pl.BlockSpec((None, D_MODEL, D_MODEL), of_layer),
                pl.BlockSpec((1, D_MODEL), full)]
    return pl.pallas_call(
        functools.partial(_post_kernel, final=final),
        grid=(n // tm,),
        in_specs=in_specs,
        out_specs=pl.BlockSpec((tm, D_MODEL), row),
        out_shape=jax.ShapeDtypeStruct((n, D_MODEL), jnp.float32),
        compiler_params=pltpu.CompilerParams(dimension_semantics=("parallel",),
                                             vmem_limit_bytes=VMEM_LIMIT),
        name="post_final" if final else "post",
    )(x2, ya, yb, yc, u, u, u, u, u, u, wa, wb, wc, wo, fw)


def _bucket_lookup(table, dist):
    onehot = (_rel_bucket(dist)[..., None] == jnp.arange(REL_BUCKETS)).astype(jnp.float32)
    return jnp.einsum("...b,bh->h...", onehot, table.astype(jnp.float32), precision=lax.Precision.HIGHEST)


def _bias_tables(rel_bias):
    moba_tab = rel_bias[:, :N_HEADS]
    swa_tab = _swa_head_order(rel_bias[:, N_HEADS:], 1, per_head=1)
    t = MOBA_BLOCK
    d_own = jnp.arange(t)[:, None] - jnp.arange(t)[None, :]
    own = jnp.where(d_own[None] >= 0, _bucket_lookup(moba_tab, d_own), _NEG_INF)
    prev = _bucket_lookup(moba_tab, d_own + t)
    btab = jnp.swapaxes(jnp.stack([own, prev], axis=1), 2, 3) * LOG2_E
    far = _bucket_lookup(moba_tab, jnp.full((1,), t + 1, jnp.int32))[:, 0] * LOG2_E

    w = SWA_WINDOW
    dist = jnp.arange(w)[:, None] + w - jnp.arange(2 * w)[None, :]
    in_band = (dist >= 0) & (dist < w)
    swa = jnp.where(in_band[None], _bucket_lookup(swa_tab, dist) * LOG2_E, _NEG_INF)
    return btab, far, swa


def kernel(x, norm_w, w_in, w_proj_a, w_proj_b, w_proj_c, w_out, sinks, rel_bias, final_norm_w):
    batch, seq, _ = x.shape
    depth = w_in.shape[0]
    assert seq % MOBA_BLOCK == 0 and MOBA_GROUP <= seq // MOBA_BLOCK <= MOBA_NEVER
    btab, far, swa_bias = _bias_tables(rel_bias)
    t = SB_TILE
    tri = (jnp.arange(t)[:, None] > jnp.arange(t)[None, :]).astype(jnp.bfloat16)
    tri = jnp.concatenate([tri, tri], axis=0)
    lane = jnp.arange(LANES)[None, :]
    blk = jnp.arange(MOBA_ZERO_ROW + 1)[:, None]
    block_lanes = ((lane == blk) | (lane == blk + MOBA_LO_LANE)) & (blk < MOBA_ZERO_ROW)
    block_lanes = jnp.broadcast_to(block_lanes[:, None, :],
                                   (MOBA_ZERO_ROW + 1, BF16_SUBLANES, LANES)).astype(jnp.bfloat16)
    sinks_perm = _swa_head_order(sinks.astype(jnp.float32), 1, per_head=1) * LOG2_E

    w_u = _permute_input_columns(w_in)
    norm_w3 = norm_w.reshape(depth, 1, D_MODEL)
    wa = w_proj_a.astype(jnp.bfloat16)
    wb = w_proj_b.astype(jnp.bfloat16)
    wc = _swa_head_order(w_proj_c, 1).astype(jnp.bfloat16)
    wo = w_out.astype(jnp.bfloat16)
    fw = final_norm_w.reshape(1, D_MODEL)

    x2 = x.reshape(batch * seq, D_MODEL)
    for layer in range(depth):
        u = _inproj(x2, norm_w3, w_u, layer)
        ya = _sb_attention(u, tri, batch, seq)
        yb = _moba_attention(u, far, btab, block_lanes, batch, seq)
        yc = _swa_attention(u, sinks_perm[layer], swa_bias, batch, seq)
        x2 = _post(x2, ya, yb, yc, u, wa, wb, wc, wo, fw, layer, final=(layer == depth - 1))
    return x2.reshape(batch, seq, D_MODEL)
```

```python
import functools
import math

import jax
import jax.numpy as jnp
from jax import lax
from jax.experimental import pallas as pl
from jax.experimental.pallas import tpu as pltpu

D_MODEL = 1024
HEAD_DIM = 64
N_HEADS = 8
WIDTH = N_HEADS * HEAD_DIM
MOBA_BLOCK = 256
MOBA_TOPK = 3
SWA_KV_HEADS = 2
SWA_GROUP = N_HEADS // SWA_KV_HEADS
SWA_WINDOW = 128
REL_BUCKETS = 32
REL_MAX_DIST = 128
RMS_EPS = 1e-6

LANES = 128
BF16_SUBLANES = 16
F32_SUBLANES = 8
HEADS_PER_TILE = LANES // HEAD_DIM
N_PAIRS = N_HEADS // HEADS_PER_TILE

COL_QA, COL_KA, COL_VA = 0, 512, 1024
COL_QB, COL_KB, COL_VB = 1536, 2048, 2560
COL_QC = 3072
COL_GA, COL_GB, COL_GC = 3584, 4096, 4608
COL_MA, COL_MB, COL_MC = 5120, 6144, 7168
COL_KC, COL_VC = 8192, 8320
D_IN = 8448

VMEM_LIMIT = 48 * 1024 * 1024

LOG2_E = math.log2(math.e)

SB_DEAD_LOG_WEIGHT = -120.0
SB_DEAD_LOG2_WEIGHT = SB_DEAD_LOG_WEIGHT * LOG2_E

_NEG_INF = float("-inf")


def _rel_bucket(dist):
    max_exact = REL_BUCKETS // 2
    n = jnp.maximum(dist, 0)
    nf = jnp.maximum(n, 1).astype(jnp.float32)
    large = max_exact + (jnp.log(nf / max_exact) / math.log(REL_MAX_DIST / max_exact)
                         * (REL_BUCKETS - max_exact)).astype(jnp.int32)
    large = jnp.minimum(large, REL_BUCKETS - 1)
    return jnp.where(n < max_exact, n, large)


def _swa_head_order(t, axis, per_head=HEAD_DIM):
    axis = axis % t.ndim
    shape = t.shape
    t = t.reshape(shape[:axis] + (SWA_KV_HEADS, SWA_GROUP, per_head) + shape[axis + 1:])
    return jnp.swapaxes(t, axis, axis + 1).reshape(shape)


def _permute_input_columns(w):
    old = {"qa": 0, "ka": 512, "va": 1024, "ga": 1536, "qb": 2048, "kb": 2560, "vb": 3072, "gb": 3584,
           "qc": 4096, "kc": 4608, "vc": 4736, "gc": 4864, "ma": 5376, "mb": 6400, "mc": 7424}
    q_scale = HEAD_DIM ** -0.5 * LOG2_E

    def cols(name, width, scale=None):
        part = w[..., old[name]:old[name] + width]
        return part if scale is None else part * scale

    parts = [cols("qa", 512, q_scale), cols("ka", 512), cols("va", 512),
             cols("qb", 512, q_scale), cols("kb", 512), cols("vb", 512),
             _swa_head_order(cols("qc", WIDTH, q_scale), -1),
             cols("ga", 512), cols("gb", 512), _swa_head_order(cols("gc", WIDTH), -1),
             cols("ma", 1024), cols("mb", 1024), cols("mc", 1024),
             cols("kc", 128), cols("vc", 128)]
    return jnp.concatenate(parts, axis=-1).astype(jnp.bfloat16)


def _inproj_kernel(x_ref, nw_ref, w_ref, u_ref, h_ref):
    @pl.when(pl.program_id(1) == 0)
    def _():
        x = x_ref[...]
        y = x * lax.rsqrt(jnp.mean(x * x, axis=-1, keepdims=True) + RMS_EPS)
        h_ref[...] = (y * nw_ref[...]).astype(h_ref.dtype)

    u_ref[...] = jnp.dot(h_ref[...], w_ref[...], preferred_element_type=jnp.float32).astype(u_ref.dtype)


def _inproj(x2, norm_w, w, layer, *, tm=1024, tn=2816):
    n = x2.shape[0]
    return pl.pallas_call(
        _inproj_kernel,
        grid=(n // tm, D_IN // tn),
        in_specs=[pl.BlockSpec((tm, D_MODEL), lambda i, j: (i, 0)),
                  pl.BlockSpec((None, 1, D_MODEL), lambda i, j: (layer, 0, 0)),
                  pl.BlockSpec((None, D_MODEL, tn), lambda i, j: (layer, 0, j))],
        out_specs=pl.BlockSpec((tm, tn), lambda i, j: (i, j)),
        out_shape=jax.ShapeDtypeStruct((n, D_IN), jnp.bfloat16),
        scratch_shapes=[pltpu.VMEM((tm, D_MODEL), jnp.bfloat16)],
        compiler_params=pltpu.CompilerParams(dimension_semantics=("parallel", "arbitrary"),
                                             vmem_limit_bytes=VMEM_LIMIT),
        name="inproj",
    )(x2, norm_w, w)


def _head_lane_mask(hh, dtype):
    lane = lax.broadcasted_iota(jnp.int32, (1, LANES), 1)
    return ((lane // HEAD_DIM) == hh).astype(dtype)


def _qk(qm, k):
    return lax.dot_general(qm, k, (((1,), (1,)), ((), ())), preferred_element_type=jnp.float32)


def _merge_heads(parts):
    lane = lax.broadcasted_iota(jnp.int32, parts[0].shape, 1)
    return jnp.where(lane < HEAD_DIM, parts[0], parts[1])


SB_TILE = 256
SB_TILES_PER_STEP = 4
SB_ROW_CHUNK = 256


def _sb_kernel(q_ref, k_ref, v_ref, tri_ref, o_ref, acc_ref, carry_ref):
    t = SB_TILE

    def step(row0, newest, n_blocks, diagonal):
        tri = tri_ref[...]
        rc = SB_ROW_CHUNK
        chains = [(hh, r) for hh in range(HEADS_PER_TILE) for r in range(t // rc)]
        starts = [pl.multiple_of((newest - j) * t, t) for j in range(n_blocks)]

        terms = {}
        for hh, r in chains:
            qm = q_ref[pl.ds(row0 + r * rc, rc), :] * _head_lane_mask(hh, q_ref.dtype)
            for j in range(n_blocks):
                z = _qk(qm, k_ref[pl.ds(starts[j], t), :])
                neg_abs = lax.bitcast_convert_type(
                    lax.bitcast_convert_type(z, jnp.uint32) | jnp.uint32(0x80000000), jnp.float32)
                soft = jnp.log(1.0 + jnp.exp2(neg_abs)) * LOG2_E
                log_beta = jnp.minimum(z, 0.0) - soft
                log_1m = log_beta - z
                mask = None
                if diagonal and j == 0:
                    row = lax.broadcasted_iota(jnp.int32, (rc, t), 0) + r * rc
                    mask = lax.broadcasted_iota(jnp.int32, (rc, t), 1) < row
                    log_1m = jnp.where(mask, log_1m, 0.0)
                hi = lax.bitcast_convert_type(
                    lax.bitcast_convert_type(log_1m, jnp.uint32) & jnp.uint32(0xFFFF0000), jnp.float32)
                split = jnp.concatenate([hi.astype(jnp.bfloat16), (log_1m - hi).astype(jnp.bfloat16)], axis=1)
                terms[hh, r, j] = (log_beta, log_1m[:, 0:1], split, mask)

        cums = {key: jnp.dot(split, tri, preferred_element_type=jnp.float32)
                for key, (_, _, split, _) in terms.items()}

        alive = None
        for hh, r in chains:
            rows = slice(r * rc, (r + 1) * rc)
            carry = carry_ref[hh, rows, :]
            acc = acc_ref[hh, rows, :]
            for j in range(n_blocks):
                log_beta, first_term, _, mask = terms[hh, r, j]
                cum = cums[hh, r, j]
                w = jnp.exp2(log_beta + cum + jnp.concatenate([carry] * (t // LANES), axis=1))
                if mask is not None:
                    w = jnp.where(mask, w, 0.0)
                acc = acc + jnp.dot(w.astype(jnp.bfloat16), v_ref[pl.ds(starts[j], t), :],
                                    preferred_element_type=jnp.float32)
                carry = carry + jnp.broadcast_to(cum[:, 0:1] + first_term, (rc, LANES))
            acc_ref[hh, rows, :] = acc
            carry_ref[hh, rows, :] = carry
            alive = carry if alive is None else jnp.maximum(alive, carry)
        return (jnp.max(alive) < SB_DEAD_LOG2_WEIGHT).astype(jnp.int32)

    def query_tile(sub, c):
        qi = pl.program_id(2) * SB_TILES_PER_STEP + sub
        row0 = pl.multiple_of(sub * t, t)
        acc_ref[...] = jnp.zeros_like(acc_ref)
        carry_ref[...] = jnp.zeros_like(carry_ref)
        dead = lax.cond(qi >= 1, lambda: step(row0, qi, 2, True), lambda: step(row0, qi, 1, True))

        def cond(c):
            newest, dead = c
            return jnp.logical_and(newest >= 1, dead == 0)

        def body(c):
            newest, _ = c
            return newest - 2, step(row0, newest, 2, False)

        newest, dead = lax.while_loop(cond, body, (qi - 2, dead))

        @pl.when(jnp.logical_and(newest == 0, dead == 0))
        def _():
            step(row0, 0, 1, False)

        o_ref[pl.ds(row0, t), :] = _merge_heads([acc_ref[0], acc_ref[1]])
        return c

    lax.fori_loop(0, SB_TILES_PER_STEP, query_tile, 0)


def _sb_attention(u, tri, batch, seq):
    t = SB_TILE
    rows = SB_TILES_PER_STEP * t
    nq = seq // rows
    return pl.pallas_call(
        _sb_kernel,
        grid=(batch, N_PAIRS, nq),
        in_specs=[pl.BlockSpec((rows, LANES), lambda b, p, i: (b * nq + i, COL_QA // LANES + p)),
                  pl.BlockSpec((seq, LANES), lambda b, p, i: (b, COL_KA // LANES + p)),
                  pl.BlockSpec((seq, LANES), lambda b, p, i: (b, COL_VA // LANES + p)),
                  pl.BlockSpec((2 * t, t), lambda b, p, i: (0, 0))],
        out_specs=pl.BlockSpec((rows, LANES), lambda b, p, i: (b * nq + i, p)),
        out_shape=jax.ShapeDtypeStruct((batch * seq, WIDTH), jnp.float32),
        scratch_shapes=[pltpu.VMEM((HEADS_PER_TILE, t, LANES), jnp.float32),
                        pltpu.VMEM((HEADS_PER_TILE, t, LANES), jnp.float32)],
        compiler_params=pltpu.CompilerParams(dimension_semantics=("parallel", "parallel", "arbitrary"),
                                             vmem_limit_bytes=VMEM_LIMIT),
        name="sb_attention",
    )(u, u, u, tri)


MOBA_LO_LANE = 64
MOBA_NEVER = 63
MOBA_ZERO_ROW = 64
MOBA_MASKED = -1e30
MOBA_GROUP = 8
MOBA_MIN_GROUP = 2
MOBA_BLOCKS_PER_STEP = 4


def _moba_kernel(*refs, nb):
    def query_block(sub, c):
        _moba_query_block(sub, *refs, nb=nb)
        return c

    lax.fori_loop(0, MOBA_BLOCKS_PER_STEP, query_block, 0)


def _moba_query_block(sub, far_ref, q_ref, k_ref, v_ref, btab_ref, oh_ref, o_ref,
                      kmean_ref, vt_ref, qa_ref, m_ref, acc_ref, *, nb):
    t = MOBA_BLOCK
    g = MOBA_GROUP
    pair = pl.program_id(1)
    own = pl.program_id(2) * MOBA_BLOCKS_PER_STEP + sub
    q_rows = pl.ds(pl.multiple_of(sub * t, t), t)
    heads = range(HEADS_PER_TILE)

    @pl.when(own == 0)
    def _():
        kmean_ref[...] = jnp.zeros_like(kmean_ref)
        kf = k_ref[...].astype(jnp.float32).reshape(nb, t, LANES)
        kmean_ref[0:nb, :] = jnp.mean(kf, axis=1)

        head_of_row = lax.broadcasted_iota(jnp.int32, (LANES, t), 0) // HEAD_DIM

        def transpose_block(b, c):
            vb = v_ref[pl.ds(pl.multiple_of(b * t, t), t), :].astype(jnp.float32).T
            for hh in heads:
                vt_ref[hh, b] = jnp.where(head_of_row == hh, vb, 1.0).astype(jnp.bfloat16)
            return c

        lax.fori_loop(0, nb, transpose_block, 0)

    km = kmean_ref[...]
    km0 = km.astype(jnp.bfloat16)
    r1 = km - km0.astype(jnp.float32)
    km1 = r1.astype(jnp.bfloat16)
    km2 = (r1 - km1.astype(jnp.float32)).astype(jnp.bfloat16)

    def update(scores, blocks, first):
        col_max = [jnp.max(s, axis=0, keepdims=True) for s in scores]
        if first:
            m_new = col_max
        else:
            m_old = [m_ref[hh][0:1, :] for hh in heads]
            m_new = [jnp.maximum(m_old[hh], col_max[hh]) for hh in heads]
            alpha = [jnp.exp2(m_old[hh] - m_new[hh]) for hh in heads]
        p = [jnp.exp2(scores[hh] - m_new[hh]).astype(jnp.bfloat16) for hh in heads]
        pv = [jnp.dot(jnp.concatenate([vt_ref[hh, b] for b in blocks], axis=1), p[hh],
                      preferred_element_type=jnp.float32) for hh in heads]
        for hh in heads:
            acc_ref[hh] = pv[hh] if first else alpha[hh] * acc_ref[hh] + pv[hh]
            m_ref[hh] = jnp.broadcast_to(m_new[hh], m_ref.shape[1:])

    def block_lanes(row):
        return jnp.concatenate([oh_ref[row]] * (t // BF16_SUBLANES), axis=0)

    nh = HEADS_PER_TILE
    blk = lax.broadcasted_iota(jnp.int32, (MOBA_LO_LANE, nh * t), 0)
    blk_f = blk.astype(jnp.float32)
    gate_rows = sum(_qk(jnp.concatenate([part * _head_lane_mask(hh, part.dtype) for hh in heads], axis=0),
                        q_ref[q_rows, :]) for part in (km0, km1, km2))
    gate = jnp.concatenate([gate_rows[hh * MOBA_LO_LANE:(hh + 1) * MOBA_LO_LANE] for hh in heads], axis=1)
    gate = jnp.where(blk < own, gate, _NEG_INF)
    sel = jnp.zeros_like(gate)
    for _ in range(MOBA_TOPK):
        best = jnp.max(gate, axis=0, keepdims=True)
        idx = jnp.min(jnp.where(gate == best, blk_f, float(LANES)), axis=0, keepdims=True)
        hit = blk_f == idx
        sel = jnp.where(jnp.logical_and(hit, best > _NEG_INF), 1.0, sel)
        gate = jnp.where(hit, _NEG_INF, gate)

    col_head = lax.broadcasted_iota(jnp.int32, (1, nh * t), 1) // t
    far_bias = jnp.where(col_head == 0, far_ref[pair * nh], far_ref[pair * nh + 1])
    term = jnp.where(sel > 0.0, jnp.where(blk < own - 1, far_bias, 0.0), MOBA_MASKED)
    hi = term.astype(jnp.bfloat16).astype(jnp.float32)
    lo = term - hi
    for hh in heads:
        cols = slice(hh * t, (hh + 1) * t)
        inj = jnp.concatenate([hi[:, cols], lo[:, cols]], axis=0).T
        qa_ref[hh] = jnp.concatenate([q_ref[q_rows, :] * _head_lane_mask(hh, q_ref.dtype), inj.astype(jnp.bfloat16)],
                                     axis=1)

    prev = jnp.maximum(own - 1, 0)
    n_far = jnp.maximum(own - 1, 0)
    n_full = n_far // g
    left = n_far - n_full * g

    def far_keys(first_blk, group):
        g0 = jnp.minimum(first_blk, nb - group)
        rows = []
        for j in range(group):
            b = g0 + j
            active = jnp.logical_and(b >= first_blk, b < n_far)
            rows.append(block_lanes(jnp.where(active, b, MOBA_NEVER)))
        keys = k_ref[pl.ds(pl.multiple_of(g0 * t, t), group * t), :]
        return g0, keys, rows

    def first_step(group):
        keys = [k_ref[pl.ds(pl.multiple_of(own * t, t), t), :], k_ref[pl.ds(pl.multiple_of(prev * t, t), t), :]]
        rows = [block_lanes(MOBA_ZERO_ROW), block_lanes(jnp.where(own >= 1, prev, MOBA_NEVER))]
        if group:
            g0, far, far_rows = far_keys(n_full * g, group)
            keys.append(far)
            rows.extend(far_rows)
        k_aug = jnp.concatenate([jnp.concatenate(keys, axis=0), jnp.concatenate(rows, axis=0)], axis=1)
        scores = [_qk(k_aug, qa_ref[hh]) for hh in heads]
        update([scores[hh][:t] + btab_ref[hh, 0] for hh in heads], [own], first=True)
        update([scores[hh][t:2 * t] + btab_ref[hh, 1] for hh in heads], [prev], first=False)
        for j in range(group):
            update([s[(2 + j) * t:(3 + j) * t] for s in scores], [g0 + j], first=False)

    group = g
    while group >= MOBA_MIN_GROUP:
        lower = group // 2 if group > MOBA_MIN_GROUP else 0

        @pl.when(jnp.logical_and(left > lower, left <= group))
        def _(group=group):
            first_step(group)

        group //= 2

    @pl.when(left == 0)
    def _():
        first_step(0)

    def far_body(i, c):
        g0, keys, rows = far_keys(i * g, g)
        k_aug = jnp.concatenate([keys, jnp.concatenate(rows, axis=0)], axis=1)
        scores = [_qk(k_aug, qa_ref[hh]) for hh in heads]
        for j in range(g):
            update([s[j * t:(j + 1) * t] for s in scores], [g0 + j], first=False)
        return c

    lax.fori_loop(0, n_full, far_body, 0)

    out_t = jnp.concatenate([acc_ref[0][:HEAD_DIM] / acc_ref[0][HEAD_DIM:],
                             acc_ref[1][HEAD_DIM:] / acc_ref[1][:HEAD_DIM]], axis=0)
    o_ref[q_rows, :] = out_t.T


def _moba_attention(u, far_bias, btab, block_lanes, batch, seq):
    t = MOBA_BLOCK
    nb = seq // t
    rows = MOBA_BLOCKS_PER_STEP * t
    nstep = seq // rows
    kernel = functools.partial(_moba_kernel, nb=nb)
    grid_spec = pltpu.PrefetchScalarGridSpec(
        num_scalar_prefetch=1,
        grid=(batch, N_PAIRS, nstep),
        in_specs=[pl.BlockSpec((rows, LANES), lambda b, p, i, far: (b * nstep + i, COL_QB // LANES + p)),
                  pl.BlockSpec((seq, LANES), lambda b, p, i, far: (b, COL_KB // LANES + p)),
                  pl.BlockSpec((seq, LANES), lambda b, p, i, far: (b, COL_VB // LANES + p)),
                  pl.BlockSpec((HEADS_PER_TILE, 2, t, t), lambda b, p, i, far: (p, 0, 0, 0)),
                  pl.BlockSpec(block_lanes.shape, lambda b, p, i, far: (0, 0, 0))],
        out_specs=pl.BlockSpec((rows, LANES), lambda b, p, i, far: (b * nstep + i, p)),
        scratch_shapes=[pltpu.VMEM((MOBA_LO_LANE, LANES), jnp.float32),
                        pltpu.VMEM((HEADS_PER_TILE, nb, LANES, t), jnp.bfloat16),
                        pltpu.VMEM((HEADS_PER_TILE, t, 2 * LANES), jnp.bfloat16),
                        pltpu.VMEM((HEADS_PER_TILE, F32_SUBLANES, t), jnp.float32),
                        pltpu.VMEM((HEADS_PER_TILE, LANES, t), jnp.float32)])
    return pl.pallas_call(
        kernel,
        grid_spec=grid_spec,
        out_shape=jax.ShapeDtypeStruct((batch * seq, WIDTH), jnp.float32),
        compiler_params=pltpu.CompilerParams(dimension_semantics=("parallel", "parallel", "arbitrary"),
                                             vmem_limit_bytes=VMEM_LIMIT),
        name="moba_attention",
    )(far_bias, u, u, u, btab, block_lanes)


SWA_STEP_WINDOWS = 2


def _swa_kernel(sink_ref, q_ref, kp_ref, kc_ref, vp_ref, vc_ref, bias_ref, o_ref):
    w = SWA_WINDOW
    keys = jnp.concatenate([kp_ref[...], kc_ref[...]], axis=0)
    values = jnp.concatenate([vp_ref[...], vc_ref[...]], axis=0)
    slots = [(j, half) for j in range(SWA_GROUP) for half in range(SWA_KV_HEADS)]
    chains = [(win, slot) for win in range(SWA_STEP_WINDOWS) for slot in range(len(slots))]

    def attend(first_step):
        scores = []
        for win, slot in chains:
            j, half = slots[slot]
            q = q_ref[win * w:(win + 1) * w, j * LANES:(j + 1) * LANES]
            s = _qk(q * _head_lane_mask(half, q.dtype), keys[win * w:(win + 2) * w]) + bias_ref[slot]
            if first_step and win == 0:
                s = jnp.where(lax.broadcasted_iota(jnp.int32, (w, 2 * w), 1) >= w, s, _NEG_INF)
            scores.append(s)
        m = [jnp.maximum(jnp.broadcast_to(jnp.max(s, axis=1, keepdims=True), (w, LANES)), sink_ref[slot])
             for (win, slot), s in zip(chains, scores)]
        p = [jnp.exp2(s - jnp.concatenate([mi] * (2 * w // LANES), axis=1)).astype(jnp.bfloat16)
             for mi, s in zip(m, scores)]
        pv = []
        for (win, slot), pi in zip(chains, p):
            vw = values[win * w:(win + 2) * w]
            vw = jnp.where(_head_lane_mask(slots[slot][1], jnp.int32) > 0, vw, jnp.ones_like(vw))
            pv.append(jnp.dot(pi, vw, preferred_element_type=jnp.float32))
        out = [pvi / (pltpu.roll(pvi, HEAD_DIM, axis=1) + jnp.exp2(sink_ref[slot] - mi))
               for (win, slot), pvi, mi in zip(chains, pv, m)]
        for win in range(SWA_STEP_WINDOWS):
            for j in range(SWA_GROUP):
                first = win * len(slots) + j * SWA_KV_HEADS
                o_ref[win * w:(win + 1) * w, j * LANES:(j + 1) * LANES] = _merge_heads(
                    out[first:first + SWA_KV_HEADS])

    @pl.when(pl.program_id(1) == 0)
    def _():
        attend(True)

    @pl.when(pl.program_id(1) > 0)
    def _():
        attend(False)


def _swa_attention(u, sinks_perm, bias, batch, seq):
    w = SWA_WINDOW
    sw = SWA_STEP_WINDOWS
    nstep = seq // (sw * w)
    kcol, vcol = COL_KC // LANES, COL_VC // LANES
    prev_window = lambda b, n, s: b * (seq // w) + jnp.maximum(sw * n - 1, 0)
    grid_spec = pltpu.PrefetchScalarGridSpec(
        num_scalar_prefetch=1,
        grid=(batch, nstep),
        in_specs=[pl.BlockSpec((sw * w, WIDTH), lambda b, n, s: (b * nstep + n, COL_QC // WIDTH)),
                  pl.BlockSpec((w, LANES), lambda b, n, s: (prev_window(b, n, s), kcol)),
                  pl.BlockSpec((sw * w, LANES), lambda b, n, s: (b * nstep + n, kcol)),
                  pl.BlockSpec((w, LANES), lambda b, n, s: (prev_window(b, n, s), vcol)),
                  pl.BlockSpec((sw * w, LANES), lambda b, n, s: (b * nstep + n, vcol)),
                  pl.BlockSpec((N_HEADS, w, 2 * w), lambda b, n, s: (0, 0, 0))],
        out_specs=pl.BlockSpec((sw * w, WIDTH), lambda b, n, s: (b * nstep + n, 0)))
    return pl.pallas_call(
        _swa_kernel,
        grid_spec=grid_spec,
        out_shape=jax.ShapeDtypeStruct((batch * seq, WIDTH), jnp.float32),
        compiler_params=pltpu.CompilerParams(dimension_semantics=("parallel", "arbitrary"),
                                             vmem_limit_bytes=VMEM_LIMIT),
        name="swa_attention",
    )(sinks_perm, u, u, u, u, u, bias)


def _post_kernel(x_ref, ya_ref, yb_ref, yc_ref, ga_ref, gb_ref, gc_ref, ma_ref, mb_ref, mc_ref,
                 wa_ref, wb_ref, wc_ref, wo_ref, fw_ref, o_ref, *, final):
    def branch(y_ref, g_ref, w_ref, m_ref):
        g = g_ref[...].astype(jnp.float32)
        y = (y_ref[...] * (g * jax.nn.sigmoid(g))).astype(jnp.bfloat16)
        proj = jnp.dot(y, w_ref[...], preferred_element_type=jnp.float32)
        return jax.nn.sigmoid(m_ref[...].astype(jnp.float32)) * proj

    merged = (branch(ya_ref, ga_ref, wa_ref, ma_ref) + branch(yb_ref, gb_ref, wb_ref, mb_ref)
              + branch(yc_ref, gc_ref, wc_ref, mc_ref))
    out = x_ref[...] + jnp.dot(merged.astype(jnp.bfloat16), wo_ref[...], preferred_element_type=jnp.float32)
    if final:
        out = out * lax.rsqrt(jnp.mean(out * out, axis=-1, keepdims=True) + RMS_EPS) * fw_ref[...]
    o_ref[...] = out


def _post(x2, ya, yb, yc, u, wa, wb, wc, wo, fw, layer, *, final, tm=256):
    n = x2.shape[0]
    row = lambda i: (i, 0)
    full = lambda i: (0, 0)
    of_layer = lambda i: (layer, 0, 0)
    y_spec = pl.BlockSpec((tm, WIDTH), row)
    in_specs = [pl.BlockSpec((tm, D_MODEL), row), y_spec, y_spec, y_spec,
                pl.BlockSpec((tm, WIDTH), lambda i: (i, COL_GA // WIDTH)),
                pl.BlockSpec((tm, WIDTH), lambda i: (i, COL_GB // WIDTH)),
                pl.BlockSpec((tm, WIDTH), lambda i: (i, COL_GC // WIDTH)),
                pl.BlockSpec((tm, D_MODEL), lambda i: (i, COL_MA // D_MODEL)),
                pl.BlockSpec((tm, D_MODEL), lambda i: (i, COL_MB // D_MODEL)),
                pl.BlockSpec((tm, D_MODEL), lambda i: (i, COL_MC // D_MODEL)),
                pl.BlockSpec((None, WIDTH, D_MODEL), of_layer), pl.BlockSpec((None, WIDTH, D_MODEL), of_layer),
                pl.BlockSpec((None, WIDTH, D_MODEL), of_layer), pl.BlockSpec((None, D_MODEL, D_MODEL), of_layer),
                pl.BlockSpec((1, D_MODEL), full)]
    return pl.pallas_call(
        functools.partial(_post_kernel, final=final),
        grid=(n // tm,),
        in_specs=in_specs,
        out_specs=pl.BlockSpec((tm, D_MODEL), row),
        out_shape=jax.ShapeDtypeStruct((n, D_MODEL), jnp.float32),
        compiler_params=pltpu.CompilerParams(dimension_semantics=("parallel",),
                                             vmem_limit_bytes=VMEM_LIMIT),
        name="post_final" if final else "post",
    )(x2, ya, yb, yc, u, u, u, u, u, u, wa, wb, wc, wo, fw)


def _bucket_lookup(table, dist):
    onehot = (_rel_bucket(dist)[..., None] == jnp.arange(REL_BUCKETS)).astype(jnp.float32)
    return jnp.einsum("...b,bh->h...", onehot, table.astype(jnp.float32), precision=lax.Precision.HIGHEST)


def _bias_tables(rel_bias):
    moba_tab = rel_bias[:, :N_HEADS]
    swa_tab = _swa_head_order(rel_bias[:, N_HEADS:], 1, per_head=1)
    t = MOBA_BLOCK
    d_own = jnp.arange(t)[:, None] - jnp.arange(t)[None, :]
    own = jnp.where(d_own[None] >= 0, _bucket_lookup(moba_tab, d_own), _NEG_INF)
    prev = _bucket_lookup(moba_tab, d_own + t)
    btab = jnp.swapaxes(jnp.stack([own, prev], axis=1), 2, 3) * LOG2_E
    far = _bucket_lookup(moba_tab, jnp.full((1,), t + 1, jnp.int32))[:, 0] * LOG2_E

    w = SWA_WINDOW
    dist = jnp.arange(w)[:, None] + w - jnp.arange(2 * w)[None, :]
    in_band = (dist >= 0) & (dist < w)
    swa = jnp.where(in_band[None], _bucket_lookup(swa_tab, dist) * LOG2_E, _NEG_INF)
    return btab, far, swa


def kernel(x, norm_w, w_in, w_proj_a, w_proj_b, w_proj_c, w_out, sinks, rel_bias, final_norm_w):
    batch, seq, _ = x.shape
    depth = w_in.shape[0]
    assert seq % MOBA_BLOCK == 0 and MOBA_GROUP <= seq // MOBA_BLOCK <= MOBA_NEVER
    btab, far, swa_bias = _bias_tables(rel_bias)
    t = SB_TILE
    tri = (jnp.arange(t)[:, None] > jnp.arange(t)[None, :]).astype(jnp.bfloat16)
    tri = jnp.concatenate([tri, tri], axis=0)
    lane = jnp.arange(LANES)[None, :]
    blk = jnp.arange(MOBA_ZERO_ROW + 1)[:, None]
    block_lanes = ((lane == blk) | (lane == blk + MOBA_LO_LANE)) & (blk < MOBA_ZERO_ROW)
    block_lanes = jnp.broadcast_to(block_lanes[:, None, :],
                                   (MOBA_ZERO_ROW + 1, BF16_SUBLANES, LANES)).astype(jnp.bfloat16)
    sinks_perm = _swa_head_order(sinks.astype(jnp.float32), 1, per_head=1) * LOG2_E

    w_u = _permute_input_columns(w_in)
    norm_w3 = norm_w.reshape(depth, 1, D_MODEL)
    wa = w_proj_a.astype(jnp.bfloat16)
    wb = w_proj_b.astype(jnp.bfloat16)
    wc = _swa_head_order(w_proj_c, 1).astype(jnp.bfloat16)
    wo = w_out.astype(jnp.bfloat16)
    fw = final_norm_w.reshape(1, D_MODEL)

    x2 = x.reshape(batch * seq, D_MODEL)
    for layer in range(depth):
        u = _inproj(x2, norm_w3, w_u, layer)
        ya = _sb_attention(u, tri, batch, seq)
        yb = _moba_attention(u, far, btab, block_lanes, batch, seq)
        yc = _swa_attention(u, sinks_perm[layer], swa_bias, batch, seq)
        x2 = _post(x2, ya, yb, yc, u, wa, wb, wc, wo, fw, layer, final=(layer == depth - 1))
    return x2.reshape(batch, seq, D_MODEL)
```

```python
import functools
import math

import jax
import jax.numpy as jnp
from jax import lax
from jax.experimental import pallas as pl
from jax.experimental.pallas import tpu as pltpu

D_MODEL = 1024
HEAD_DIM = 64
N_HEADS = 8
WIDTH = N_HEADS * HEAD_DIM
MOBA_BLOCK = 256
MOBA_TOPK = 3
SWA_KV_HEADS = 2
SWA_GROUP = N_HEADS // SWA_KV_HEADS
SWA_WINDOW = 128
REL_BUCKETS = 32
REL_MAX_DIST = 128
RMS_EPS = 1e-6

LANES = 128
BF16_SUBLANES = 16
F32_SUBLANES = 8
HEADS_PER_TILE = LANES // HEAD_DIM
N_PAIRS = N_HEADS // HEADS_PER_TILE

COL_QA, COL_KA, COL_VA = 0, 512, 1024
COL_QB, COL_KB, COL_VB = 1536, 2048, 2560
COL_QC = 3072
COL_GA, COL_GB, COL_GC = 3584, 4096, 4608
COL_MA, COL_MB, COL_MC = 5120, 6144, 7168
COL_KC, COL_VC = 8192, 8320
D_IN = 8448

VMEM_LIMIT = 48 * 1024 * 1024

LOG2_E = math.log2(math.e)

SB_DEAD_LOG_WEIGHT = -120.0
SB_DEAD_LOG2_WEIGHT = SB_DEAD_LOG_WEIGHT * LOG2_E

_NEG_INF = float("-inf")


def _rel_bucket(dist):
    max_exact = REL_BUCKETS // 2
    n = jnp.maximum(dist, 0)
    nf = jnp.maximum(n, 1).astype(jnp.float32)
    large = max_exact + (jnp.log(nf / max_exact) / math.log(REL_MAX_DIST / max_exact)
                         * (REL_BUCKETS - max_exact)).astype(jnp.int32)
    large = jnp.minimum(large, REL_BUCKETS - 1)
    return jnp.where(n < max_exact, n, large)


def _swa_head_order(t, axis, per_head=HEAD_DIM):
    axis = axis % t.ndim
    shape = t.shape
    t = t.reshape(shape[:axis] + (SWA_KV_HEADS, SWA_GROUP, per_head) + shape[axis + 1:])
    return jnp.swapaxes(t, axis, axis + 1).reshape(shape)


def _permute_input_columns(w):
    old = {"qa": 0, "ka": 512, "va": 1024, "ga": 1536, "qb": 2048, "kb": 2560, "vb": 3072, "gb": 3584,
           "qc": 4096, "kc": 4608, "vc": 4736, "gc": 4864, "ma": 5376, "mb": 6400, "mc": 7424}
    q_scale = HEAD_DIM ** -0.5 * LOG2_E

    def cols(name, width, scale=None):
        part = w[..., old[name]:old[name] + width]
        return part if scale is None else part * scale

    parts = [cols("qa", 512, q_scale), cols("ka", 512), cols("va", 512),
             cols("qb", 512, q_scale), cols("kb", 512), cols("vb", 512),
             _swa_head_order(cols("qc", WIDTH, q_scale), -1),
             cols("ga", 512), cols("gb", 512), _swa_head_order(cols("gc", WIDTH), -1),
             cols("ma", 1024), cols("mb", 1024), cols("mc", 1024),
             cols("kc", 128), cols("vc", 128)]
    return jnp.concatenate(parts, axis=-1).astype(jnp.bfloat16)


def _inproj_kernel(x_ref, nw_ref, w_ref, u_ref, h_ref):
    @pl.when(pl.program_id(1) == 0)
    def _():
        x = x_ref[...]
        y = x * lax.rsqrt(jnp.mean(x * x, axis=-1, keepdims=True) + RMS_EPS)
        h_ref[...] = (y * nw_ref[...]).astype(h_ref.dtype)

    u_ref[...] = jnp.dot(h_ref[...], w_ref[...], preferred_element_type=jnp.float32).astype(u_ref.dtype)


def _inproj(x2, norm_w, w, layer, *, tm=1024, tn=2816):
    n = x2.shape[0]
    return pl.pallas_call(
        _inproj_kernel,
        grid=(n // tm, D_IN // tn),
        in_specs=[pl.BlockSpec((tm, D_MODEL), lambda i, j: (i, 0)),
                  pl.BlockSpec((None, 1, D_MODEL), lambda i, j: (layer, 0, 0)),
                  pl.BlockSpec((None, D_MODEL, tn), lambda i, j: (layer, 0, j))],
        out_specs=pl.BlockSpec((tm, tn), lambda i, j: (i, j)),
        out_shape=jax.ShapeDtypeStruct((n, D_IN), jnp.bfloat16),
        scratch_shapes=[pltpu.VMEM((tm, D_MODEL), jnp.bfloat16)],
        compiler_params=pltpu.CompilerParams(dimension_semantics=("parallel", "arbitrary"),
                                             vmem_limit_bytes=VMEM_LIMIT),
        name="inproj",
    )(x2, norm_w, w)


def _head_lane_mask(hh, dtype):
    lane = lax.broadcasted_iota(jnp.int32, (1, LANES), 1)
    return ((lane // HEAD_DIM) == hh).astype(dtype)


def _qk(qm, k):
    return lax.dot_general(qm, k, (((1,), (1,)), ((), ())), preferred_element_type=jnp.float32)


def _merge_heads(parts):
    lane = lax.broadcasted_iota(jnp.int32, parts[0].shape, 1)
    return jnp.where(lane < HEAD_DIM, parts[0], parts[1])


SB_TILE = 256
SB_TILES_PER_STEP = 4
SB_ROW_CHUNK = 256


def _sb_kernel(q_ref, k_ref, v_ref, tri_ref, o_ref, acc_ref, carry_ref):
    t = SB_TILE

    def step(row0, newest, n_blocks, diagonal):
        tri = tri_ref[...]
        rc = SB_ROW_CHUNK
        chains = [(hh, r) for hh in range(HEADS_PER_TILE) for r in range(t // rc)]
        starts = [pl.multiple_of((newest - j) * t, t) for j in range(n_blocks)]

        terms = {}
        for hh, r in chains:
            qm = q_ref[pl.ds(row0 + r * rc, rc), :] * _head_lane_mask(hh, q_ref.dtype)
            for j in range(n_blocks):
                z = _qk(qm, k_ref[pl.ds(starts[j], t), :])
                neg_abs = lax.bitcast_convert_type(
                    lax.bitcast_convert_type(z, jnp.uint32) | jnp.uint32(0x80000000), jnp.float32)
                soft = jnp.log(1.0 + jnp.exp2(neg_abs)) * LOG2_E
                log_beta = jnp.minimum(z, 0.0) - soft
                log_1m = log_beta - z
                mask = None
                if diagonal and j == 0:
                    row = lax.broadcasted_iota(jnp.int32, (rc, t), 0) + r * rc
                    mask = lax.broadcasted_iota(jnp.int32, (rc, t), 1) < row
                    log_1m = jnp.where(mask, log_1m, 0.0)
                hi = lax.bitcast_convert_type(
                    lax.bitcast_convert_type(log_1m, jnp.uint32) & jnp.uint32(0xFFFF0000), jnp.float32)
                split = jnp.concatenate([hi.astype(jnp.bfloat16), (log_1m - hi).astype(jnp.bfloat16)], axis=1)
                terms[hh, r, j] = (log_beta, log_1m[:, 0:1], split, mask)

        cums = {key: jnp.dot(split, tri, preferred_element_type=jnp.float32)
                for key, (_, _, split, _) in terms.items()}

        alive = None
        for hh, r in chains:
            rows = slice(r * rc, (r + 1) * rc)
            carry = carry_ref[hh, rows, :]
            acc = acc_ref[hh, rows, :]
            for j in range(n_blocks):
                log_beta, first_term, _, mask = terms[hh, r, j]
                cum = cums[hh, r, j]
                w = jnp.exp2(log_beta + cum + jnp.concatenate([carry] * (t // LANES), axis=1))
                if mask is not None:
                    w = jnp.where(mask, w, 0.0)
                acc = acc + jnp.dot(w.astype(jnp.bfloat16), v_ref[pl.ds(starts[j], t), :],
                                    preferred_element_type=jnp.float32)
                carry = carry + jnp.broadcast_to(cum[:, 0:1] + first_term, (rc, LANES))
            acc_ref[hh, rows, :] = acc
            carry_ref[hh, rows, :] = carry
            alive = carry if alive is None else jnp.maximum(alive, carry)
        return (jnp.max(alive) < SB_DEAD_LOG2_WEIGHT).astype(jnp.int32)

    def query_tile(sub, c):
        qi = pl.program_id(2) * SB_TILES_PER_STEP + sub
        row0 = pl.multiple_of(sub * t, t)
        acc_ref[...] = jnp.zeros_like(acc_ref)
        carry_ref[...] = jnp.zeros_like(carry_ref)
        dead = lax.cond(qi >= 1, lambda: step(row0, qi, 2, True), lambda: step(row0, qi, 1, True))

        def cond(c):
            newest, dead = c
            return jnp.logical_and(newest >= 1, dead == 0)

        def body(c):
            newest, _ = c
            return newest - 2, step(row0, newest, 2, False)

        newest, dead = lax.while_loop(cond, body, (qi - 2, dead))

        @pl.when(jnp.logical_and(newest == 0, dead == 0))
        def _():
            step(row0, 0, 1, False)

        o_ref[pl.ds(row0, t), :] = _merge_heads([acc_ref[0], acc_ref[1]])
        return c

    lax.fori_loop(0, SB_TILES_PER_STEP, query_tile, 0)


def _sb_attention(u, tri, batch, seq):
    t = SB_TILE
    rows = SB_TILES_PER_STEP * t
    nq = seq // rows
    return pl.pallas_call(
        _sb_kernel,
        grid=(batch, N_PAIRS, nq),
        in_specs=[pl.BlockSpec((rows, LANES), lambda b, p, i: (b * nq + i, COL_QA // LANES + p)),
                  pl.BlockSpec((seq, LANES), lambda b, p, i: (b, COL_KA // LANES + p)),
                  pl.BlockSpec((seq, LANES), lambda b, p, i: (b, COL_VA // LANES + p)),
                  pl.BlockSpec((2 * t, t), lambda b, p, i: (0, 0))],
        out_specs=pl.BlockSpec((rows, LANES), lambda b, p, i: (b * nq + i, p)),
        out_shape=jax.ShapeDtypeStruct((batch * seq, WIDTH), jnp.float32),
        scratch_shapes=[pltpu.VMEM((HEADS_PER_TILE, t, LANES), jnp.float32),
                        pltpu.VMEM((HEADS_PER_TILE, t, LANES), jnp.float32)],
        compiler_params=pltpu.CompilerParams(dimension_semantics=("parallel", "parallel", "arbitrary"),
                                             vmem_limit_bytes=VMEM_LIMIT),
        name="sb_attention",
    )(u, u, u, tri)


MOBA_LO_LANE = 64
MOBA_NEVER = 63
MOBA_ZERO_ROW = 64
MOBA_MASKED = -1e30
MOBA_GROUP = 8
MOBA_MIN_GROUP = 2
MOBA_BLOCKS_PER_STEP = 4


def _moba_kernel(*refs, nb):
    def query_block(sub, c):
        _moba_query_block(sub, *refs, nb=nb)
        return c

    lax.fori_loop(0, MOBA_BLOCKS_PER_STEP, query_block, 0)


def _moba_query_block(sub, far_ref, q_ref, k_ref, v_ref, btab_ref, oh_ref, o_ref,
                      kmean_ref, vt_ref, qa_ref, m_ref, acc_ref, *, nb):
    t = MOBA_BLOCK
    g = MOBA_GROUP
    pair = pl.program_id(1)
    own = pl.program_id(2) * MOBA_BLOCKS_PER_STEP + sub
    q_rows = pl.ds(pl.multiple_of(sub * t, t), t)
    heads = range(HEADS_PER_TILE)

    @pl.when(own == 0)
    def _():
        kmean_ref[...] = jnp.zeros_like(kmean_ref)
        kf = k_ref[...].astype(jnp.float32).reshape(nb, t, LANES)
        kmean_ref[0:nb, :] = jnp.mean(kf, axis=1)

        head_of_row = lax.broadcasted_iota(jnp.int32, (LANES, t), 0) // HEAD_DIM

        def transpose_block(b, c):
            vb = v_ref[pl.ds(pl.multiple_of(b * t, t), t), :].astype(jnp.float32).T
            for hh in heads:
                vt_ref[hh, b] = jnp.where(head_of_row == hh, vb, 1.0).astype(jnp.bfloat16)
            return c

        lax.fori_loop(0, nb, transpose_block, 0)

    km = kmean_ref[...]
    km0 = km.astype(jnp.bfloat16)
    r1 = km - km0.astype(jnp.float32)
    km1 = r1.astype(jnp.bfloat16)
    km2 = (r1 - km1.astype(jnp.float32)).astype(jnp.bfloat16)

    def update(scores, blocks, first):
        col_max = [jnp.max(s, axis=0, keepdims=True).astype(jnp.bfloat16).astype(jnp.float32)
                   for s in scores]
        if first:
            m_new = col_max
        else:
            m_old = [m_ref[hh][0:1, :] for hh in heads]
            m_new = [jnp.maximum(m_old[hh], col_max[hh]) for hh in heads]
            alpha = [jnp.exp2(m_old[hh] - m_new[hh]) for hh in heads]
        p = [jnp.exp2(scores[hh].astype(jnp.bfloat16) - m_new[hh].astype(jnp.bfloat16)) for hh in heads]
        pv = [jnp.dot(jnp.concatenate([vt_ref[hh, b] for b in blocks], axis=1), p[hh],
                      preferred_element_type=jnp.float32) for hh in heads]
        for hh in heads:
            acc_ref[hh] = pv[hh] if first else alpha[hh] * acc_ref[hh] + pv[hh]
            m_ref[hh] = jnp.broadcast_to(m_new[hh], m_ref.shape[1:])

    def block_lanes(row):
        return jnp.concatenate([oh_ref[row]] * (t // BF16_SUBLANES), axis=0)

    nh = HEADS_PER_TILE
    blk = lax.broadcasted_iota(jnp.int32, (MOBA_LO_LANE, nh * t), 0)
    blk_f = blk.astype(jnp.float32)
    gate_rows = sum(_qk(jnp.concatenate([part * _head_lane_mask(hh, part.dtype) for hh in heads], axis=0),
                        q_ref[q_rows, :]) for part in (km0, km1, km2))
    gate = jnp.concatenate([gate_rows[hh * MOBA_LO_LANE:(hh + 1) * MOBA_LO_LANE] for hh in heads], axis=1)
    gate = jnp.where(blk < own, gate, _NEG_INF)
    sel = jnp.zeros_like(gate)
    for _ in range(MOBA_TOPK):
        best = jnp.max(gate, axis=0, keepdims=True)
        idx = jnp.min(jnp.where(gate == best, blk_f, float(LANES)), axis=0, keepdims=True)
        hit = blk_f == idx
        sel = jnp.where(jnp.logical_and(hit, best > _NEG_INF), 1.0, sel)
        gate = jnp.where(hit, _NEG_INF, gate)

    col_head = lax.broadcasted_iota(jnp.int32, (1, nh * t), 1) // t
    far_bias = jnp.where(col_head == 0, far_ref[pair * nh], far_ref[pair * nh + 1])
    term = jnp.where(sel > 0.0, jnp.where(blk < own - 1, far_bias, 0.0), MOBA_MASKED)
    hi = term.astype(jnp.bfloat16).astype(jnp.float32)
    lo = term - hi
    for hh in heads:
        cols = slice(hh * t, (hh + 1) * t)
        inj = jnp.concatenate([hi[:, cols], lo[:, cols]], axis=0).T
        qa_ref[hh] = jnp.concatenate([q_ref[q_rows, :] * _head_lane_mask(hh, q_ref.dtype), inj.astype(jnp.bfloat16)],
                                     axis=1)

    prev = jnp.maximum(own - 1, 0)
    n_far = jnp.maximum(own - 1, 0)
    n_full = n_far // g
    left = n_far - n_full * g

    def far_keys(first_blk, group):
        g0 = jnp.minimum(first_blk, nb - group)
        rows = []
        for j in range(group):
            b = g0 + j
            active = jnp.logical_and(b >= first_blk, b < n_far)
            rows.append(block_lanes(jnp.where(active, b, MOBA_NEVER)))
        keys = k_ref[pl.ds(pl.multiple_of(g0 * t, t), group * t), :]
        return g0, keys, rows

    def first_step(group):
        keys = [k_ref[pl.ds(pl.multiple_of(own * t, t), t), :], k_ref[pl.ds(pl.multiple_of(prev * t, t), t), :]]
        rows = [block_lanes(MOBA_ZERO_ROW), block_lanes(jnp.where(own >= 1, prev, MOBA_NEVER))]
        if group:
            g0, far, far_rows = far_keys(n_full * g, group)
            keys.append(far)
            rows.extend(far_rows)
        k_aug = jnp.concatenate([jnp.concatenate(keys, axis=0), jnp.concatenate(rows, axis=0)], axis=1)
        scores = [_qk(k_aug, qa_ref[hh]) for hh in heads]
        update([scores[hh][:t] + btab_ref[hh, 0] for hh in heads], [own], first=True)
        update([scores[hh][t:2 * t] + btab_ref[hh, 1] for hh in heads], [prev], first=False)
        for j in range(group):
            update([s[(2 + j) * t:(3 + j) * t] for s in scores], [g0 + j], first=False)

    group = g
    while group >= MOBA_MIN_GROUP:
        lower = group // 2 if group > MOBA_MIN_GROUP else 0

        @pl.when(jnp.logical_and(left > lower, left <= group))
        def _(group=group):
            first_step(group)

        group //= 2

    @pl.when(left == 0)
    def _():
        first_step(0)

    def far_body(i, c):
        g0, keys, rows = far_keys(i * g, g)
        k_aug = jnp.concatenate([keys, jnp.concatenate(rows, axis=0)], axis=1)
        scores = [_qk(k_aug, qa_ref[hh]) for hh in heads]
        for j in range(g):
            update([s[j * t:(j + 1) * t] for s in scores], [g0 + j], first=False)
        return c

    lax.fori_loop(0, n_full, far_body, 0)

    out_t = jnp.concatenate([acc_ref[0][:HEAD_DIM] / acc_ref[0][HEAD_DIM:],
                             acc_ref[1][HEAD_DIM:] / acc_ref[1][:HEAD_DIM]], axis=0)
    o_ref[q_rows, :] = out_t.T


def _moba_attention(u, far_bias, btab, block_lanes, batch, seq):
    t = MOBA_BLOCK
    nb = seq // t
    rows = MOBA_BLOCKS_PER_STEP * t
    nstep = seq // rows
    kernel = functools.partial(_moba_kernel, nb=nb)
    grid_spec = pltpu.PrefetchScalarGridSpec(
        num_scalar_prefetch=1,
        grid=(batch, N_PAIRS, nstep),
        in_specs=[pl.BlockSpec((rows, LANES), lambda b, p, i, far: (b * nstep + i, COL_QB // LANES + p)),
                  pl.BlockSpec((seq, LANES), lambda b, p, i, far: (b, COL_KB // LANES + p)),
                  pl.BlockSpec((seq, LANES), lambda b, p, i, far: (b, COL_VB // LANES + p)),
                  pl.BlockSpec((HEADS_PER_TILE, 2, t, t), lambda b, p, i, far: (p, 0, 0, 0)),
                  pl.BlockSpec(block_lanes.shape, lambda b, p, i, far: (0, 0, 0))],
        out_specs=pl.BlockSpec((rows, LANES), lambda b, p, i, far: (b * nstep + i, p)),
        scratch_shapes=[pltpu.VMEM((MOBA_LO_LANE, LANES), jnp.float32),
                        pltpu.VMEM((HEADS_PER_TILE, nb, LANES, t), jnp.bfloat16),
                        pltpu.VMEM((HEADS_PER_TILE, t, 2 * LANES), jnp.bfloat16),
                        pltpu.VMEM((HEADS_PER_TILE, F32_SUBLANES, t), jnp.float32),
                        pltpu.VMEM((HEADS_PER_TILE, LANES, t), jnp.float32)])
    return pl.pallas_call(
        kernel,
        grid_spec=grid_spec,
        out_shape=jax.ShapeDtypeStruct((batch * seq, WIDTH), jnp.float32),
        compiler_params=pltpu.CompilerParams(dimension_semantics=("parallel", "parallel", "arbitrary"),
                                             vmem_limit_bytes=VMEM_LIMIT),
        name="moba_attention",
    )(far_bias, u, u, u, btab, block_lanes)


SWA_STEP_WINDOWS = 2


def _swa_kernel(sink_ref, q_ref, kp_ref, kc_ref, vp_ref, vc_ref, bias_ref, o_ref):
    w = SWA_WINDOW
    keys = jnp.concatenate([kp_ref[...], kc_ref[...]], axis=0)
    values = jnp.concatenate([vp_ref[...], vc_ref[...]], axis=0)
    slots = [(j, half) for j in range(SWA_GROUP) for half in range(SWA_KV_HEADS)]
    chains = [(win, slot) for win in range(SWA_STEP_WINDOWS) for slot in range(len(slots))]

    def attend(first_step):
        scores = []
        for win, slot in chains:
            j, half = slots[slot]
            q = q_ref[win * w:(win + 1) * w, j * LANES:(j + 1) * LANES]
            s = _qk(q * _head_lane_mask(half, q.dtype), keys[win * w:(win + 2) * w]) + bias_ref[slot]
            if first_step and win == 0:
                s = jnp.where(lax.broadcasted_iota(jnp.int32, (w, 2 * w), 1) >= w, s, _NEG_INF)
            scores.append(s)
        m = [jnp.maximum(jnp.broadcast_to(jnp.max(s, axis=1, keepdims=True), (w, LANES)), sink_ref[slot])
             for (win, slot), s in zip(chains, scores)]
        p = [jnp.exp2(s - jnp.concatenate([mi] * (2 * w // LANES), axis=1)).astype(jnp.bfloat16)
             for mi, s in zip(m, scores)]
        pv = []
        for (win, slot), pi in zip(chains, p):
            vw = values[win * w:(win + 2) * w]
            vw = jnp.where(_head_lane_mask(slots[slot][1], jnp.int32) > 0, vw, jnp.ones_like(vw))
            pv.append(jnp.dot(pi, vw, preferred_element_type=jnp.float32))
        out = [pvi / (pltpu.roll(pvi, HEAD_DIM, axis=1) + jnp.exp2(sink_ref[slot] - mi))
               for (win, slot), pvi, mi in zip(chains, pv, m)]
        for win in range(SWA_STEP_WINDOWS):
            for j in range(SWA_GROUP):
                first = win * len(slots) + j * SWA_KV_HEADS
                o_ref[win * w:(win + 1) * w, j * LANES:(j + 1) * LANES] = _merge_heads(
                    out[first:first + SWA_KV_HEADS])

    @pl.when(pl.program_id(1) == 0)
    def _():
        attend(True)

    @pl.when(pl.program_id(1) > 0)
    def _():
        attend(False)


def _swa_attention(u, sinks_perm, bias, batch, seq):
    w = SWA_WINDOW
    sw = SWA_STEP_WINDOWS
    nstep = seq // (sw * w)
    kcol, vcol = COL_KC // LANES, COL_VC // LANES
    prev_window = lambda b, n, s: b * (seq // w) + jnp.maximum(sw * n - 1, 0)
    grid_spec = pltpu.PrefetchScalarGridSpec(
        num_scalar_prefetch=1,
        grid=(batch, nstep),
        in_specs=[pl.BlockSpec((sw * w, WIDTH), lambda b, n, s: (b * nstep + n, COL_QC // WIDTH)),
                  pl.BlockSpec((w, LANES), lambda b, n, s: (prev_window(b, n, s), kcol)),
                  pl.BlockSpec((sw * w, LANES), lambda b, n, s: (b * nstep + n, kcol)),
                  pl.BlockSpec((w, LANES), lambda b, n, s: (prev_window(b, n, s), vcol)),
                  pl.BlockSpec((sw * w, LANES), lambda b, n, s: (b * nstep + n, vcol)),
                  pl.BlockSpec((N_HEADS, w, 2 * w), lambda b, n, s: (0, 0, 0))],
        out_specs=pl.BlockSpec((sw * w, WIDTH), lambda b, n, s: (b * nstep + n, 0)))
    return pl.pallas_call(
        _swa_kernel,
        grid_spec=grid_spec,
        out_shape=jax.ShapeDtypeStruct((batch * seq, WIDTH), jnp.float32),
        compiler_params=pltpu.CompilerParams(dimension_semantics=("parallel", "arbitrary"),
                                             vmem_limit_bytes=VMEM_LIMIT),
        name="swa_attention",
    )(sinks_perm, u, u, u, u, u, bias)


def _post_kernel(x_ref, ya_ref, yb_ref, yc_ref, ga_ref, gb_ref, gc_ref, ma_ref, mb_ref, mc_ref,
                 wa_ref, wb_ref, wc_ref, wo_ref, fw_ref, o_ref, *, final):
    def branch(y_ref, g_ref, w_ref, m_ref):
        g = g_ref[...].astype(jnp.float32)
        y = (y_ref[...] * (g * jax.nn.sigmoid(g))).astype(jnp.bfloat16)
        proj = jnp.dot(y, w_ref[...], preferred_element_type=jnp.float32)
        return jax.nn.sigmoid(m_ref[...].astype(jnp.float32)) * proj

    merged = (branch(ya_ref, ga_ref, wa_ref, ma_ref) + branch(yb_ref, gb_ref, wb_ref, mb_ref)
              + branch(yc_ref, gc_ref, wc_ref, mc_ref))
    out = x_ref[...] + jnp.dot(merged.astype(jnp.bfloat16), wo_ref[...], preferred_element_type=jnp.float32)
    if final:
        out = out * lax.rsqrt(jnp.mean(out * out, axis=-1, keepdims=True) + RMS_EPS) * fw_ref[...]
    o_ref[...] = out


def _post(x2, ya, yb, yc, u, wa, wb, wc, wo, fw, layer, *, final, tm=256):
    n = x2.shape[0]
    row = lambda i: (i, 0)
    full = lambda i: (0, 0)
    of_layer = lambda i: (layer, 0, 0)
    y_spec = pl.BlockSpec((tm, WIDTH), row)
    in_specs = [pl.BlockSpec((tm, D_MODEL), row), y_spec, y_spec, y_spec,
                pl.BlockSpec((tm, WIDTH), lambda i: (i, COL_GA // WIDTH)),
                pl.BlockSpec((tm, WIDTH), lambda i: (i, COL_GB // WIDTH)),
                pl.BlockSpec((tm, WIDTH), lambda i: (i, COL_GC // WIDTH)),
                pl.BlockSpec((tm, D_MODEL), lambda i: (i, COL_MA // D_MODEL)),
                pl.BlockSpec((tm, D_MODEL), lambda i: (i, COL_MB // D_MODEL)),
                pl.BlockSpec((tm, D_MODEL), lambda i: (i, COL_MC // D_MODEL)),
                pl.BlockSpec((None, WIDTH, D_MODEL), of_layer), pl.BlockSpec((None, WIDTH, D_MODEL), of_layer),
                pl.BlockSpec((None, WIDTH, D_MODEL), of_layer), pl.BlockSpec((None, D_MODEL, D_MODEL), of_layer),
                pl.BlockSpec((1, D_MODEL), full)]
    return pl.pallas_call(
        functools.partial(_post_kernel, final=final),
        grid=(n // tm,),
        in_specs=in_specs,
        out_specs=pl.BlockSpec((tm, D_MODEL), row),
        out_shape=jax.ShapeDtypeStruct((n, D_MODEL), jnp.float32),
        compiler_params=pltpu.CompilerParams(dimension_semantics=("parallel",),
                                             vmem_limit_bytes=VMEM_LIMIT),
        name="post_final" if final else "post",
    )(x2, ya, yb, yc, u, u, u, u, u, u, wa, wb, wc, wo, fw)


def _bucket_lookup(table, dist):
    onehot = (_rel_bucket(dist)[..., None] == jnp.arange(REL_BUCKETS)).astype(jnp.float32)
    return jnp.einsum("...b,bh->h...", onehot, table.astype(jnp.float32), precision=lax.Precision.HIGHEST)


def _bias_tables(rel_bias):
    moba_tab = rel_bias[:, :N_HEADS]
    swa_tab = _swa_head_order(rel_bias[:, N_HEADS:], 1, per_head=1)
    t = MOBA_BLOCK
    d_own = jnp.arange(t)[:, None] - jnp.arange(t)[None, :]
    own = jnp.where(d_own[None] >= 0, _bucket_lookup(moba_tab, d_own), _NEG_INF)
    prev = _bucket_lookup(moba_tab, d_own + t)
    btab = jnp.swapaxes(jnp.stack([own, prev], axis=1), 2, 3) * LOG2_E
    far = _bucket_lookup(moba_tab, jnp.full((1,), t + 1, jnp.int32))[:, 0] * LOG2_E

    w = SWA_WINDOW
    dist = jnp.arange(w)[:, None] + w - jnp.arange(2 * w)[None, :]
    in_band = (dist >= 0) & (dist < w)
    swa = jnp.where(in_band[None], _bucket_lookup(swa_tab, dist) * LOG2_E, _NEG_INF)
    return btab, far, swa


def kernel(x, norm_w, w_in, w_proj_a, w_proj_b, w_proj_c, w_out, sinks, rel_bias, final_norm_w):
    batch, seq, _ = x.shape
    depth = w_in.shape[0]
    assert seq % MOBA_BLOCK == 0 and MOBA_GROUP <= seq // MOBA_BLOCK <= MOBA_NEVER
    btab, far, swa_bias = _bias_tables(rel_bias)
    t = SB_TILE
    tri = (jnp.arange(t)[:, None] > jnp.arange(t)[None, :]).astype(jnp.bfloat16)
    tri = jnp.concatenate([tri, tri], axis=0)
    lane = jnp.arange(LANES)[None, :]
    blk = jnp.arange(MOBA_ZERO_ROW + 1)[:, None]
    block_lanes = ((lane == blk) | (lane == blk + MOBA_LO_LANE)) & (blk < MOBA_ZERO_ROW)
    block_lanes = jnp.broadcast_to(block_lanes[:, None, :],
                                   (MOBA_ZERO_ROW + 1, BF16_SUBLANES, LANES)).astype(jnp.bfloat16)
    sinks_perm = _swa_head_order(sinks.astype(jnp.float32), 1, per_head=1) * LOG2_E

    w_u = _permute_input_columns(w_in)
    norm_w3 = norm_w.reshape(depth, 1, D_MODEL)
    wa = w_proj_a.astype(jnp.bfloat16)
    wb = w_proj_b.astype(jnp.bfloat16)
    wc = _swa_head_order(w_proj_c, 1).astype(jnp.bfloat16)
    wo = w_out.astype(jnp.bfloat16)
    fw = final_norm_w.reshape(1, D_MODEL)

    x2 = x.reshape(batch * seq, D_MODEL)
    for layer in range(depth):
        u = _inproj(x2, norm_w3, w_u, layer)
        ya = _sb_attention(u, tri, batch, seq)
        yb = _moba_attention(u, far, btab, block_lanes, batch, seq)
        yc = _swa_attention(u, sinks_perm[layer], swa_bias, batch, seq)
        x2 = _post(x2, ya, yb, yc, u, wa, wb, wc, wo, fw, layer, final=(layer == depth - 1))
    return x2.reshape(batch, seq, D_MODEL)
```

```python
import functools
import math

import jax
import jax.numpy as jnp
from jax import lax
from jax.experimental import pallas as pl
from jax.experimental.pallas import tpu as pltpu

D_MODEL = 1024
HEAD_DIM = 64
N_HEADS = 8
WIDTH = N_HEADS * HEAD_DIM
MOBA_BLOCK = 256
MOBA_TOPK = 3
SWA_KV_HEADS = 2
SWA_GROUP = N_HEADS // SWA_KV_HEADS
SWA_WINDOW = 128
REL_BUCKETS = 32
REL_MAX_DIST = 128
RMS_EPS = 1e-6

LANES = 128
BF16_SUBLANES = 16
F32_SUBLANES = 8
HEADS_PER_TILE = LANES // HEAD_DIM
N_PAIRS = N_HEADS // HEADS_PER_TILE

COL_QA, COL_KA, COL_VA = 0, 512, 1024
COL_QB, COL_KB, COL_VB = 1536, 2048, 2560
COL_QC = 3072
COL_GA, COL_GB, COL_GC = 3584, 4096, 4608
COL_MA, COL_MB, COL_MC = 5120, 6144, 7168
COL_KC, COL_VC = 8192, 8320
D_IN = 8448

V7X_VMEM_BYTES = 64 * 1024 * 1024
VMEM_LIMIT = V7X_VMEM_BYTES * 3 // 4

LOG2_E = math.log2(math.e)

SB_DEAD_LOG_WEIGHT = -120.0
SB_DEAD_LOG2_WEIGHT = SB_DEAD_LOG_WEIGHT * LOG2_E

_NEG_INF = float("-inf")


def _rel_bucket(dist):
    max_exact = REL_BUCKETS // 2
    n = jnp.maximum(dist, 0)
    nf = jnp.maximum(n, 1).astype(jnp.float32)
    large = max_exact + (jnp.log(nf / max_exact) / math.log(REL_MAX_DIST / max_exact)
                         * (REL_BUCKETS - max_exact)).astype(jnp.int32)
    large = jnp.minimum(large, REL_BUCKETS - 1)
    return jnp.where(n < max_exact, n, large)


def _swa_head_order(t, axis, per_head=HEAD_DIM):
    axis = axis % t.ndim
    shape = t.shape
    t = t.reshape(shape[:axis] + (SWA_KV_HEADS, SWA_GROUP, per_head) + shape[axis + 1:])
    return jnp.swapaxes(t, axis, axis + 1).reshape(shape)


def _permute_input_columns(w):
    old = {"qa": 0, "ka": 512, "va": 1024, "ga": 1536, "qb": 2048, "kb": 2560, "vb": 3072, "gb": 3584,
           "qc": 4096, "kc": 4608, "vc": 4736, "gc": 4864, "ma": 5376, "mb": 6400, "mc": 7424}
    q_scale = HEAD_DIM ** -0.5 * LOG2_E

    def cols(name, width, scale=None):
        part = w[..., old[name]:old[name] + width]
        return part if scale is None else part * scale

    parts = [cols("qa", 512, q_scale), cols("ka", 512), cols("va", 512),
             cols("qb", 512, q_scale), cols("kb", 512), cols("vb", 512),
             _swa_head_order(cols("qc", WIDTH, q_scale), -1),
             cols("ga", 512), cols("gb", 512), _swa_head_order(cols("gc", WIDTH), -1),
             cols("ma", 1024), cols("mb", 1024), cols("mc", 1024),
             cols("kc", 128), cols("vc", 128)]
    return jnp.concatenate(parts, axis=-1).astype(jnp.bfloat16)


def _inproj_kernel(x_ref, nw_ref, w_ref, u_ref, h_ref):
    @pl.when(pl.program_id(1) == 0)
    def _():
        x = x_ref[...]
        y = x * lax.rsqrt(jnp.mean(x * x, axis=-1, keepdims=True) + RMS_EPS)
        h_ref[...] = (y * nw_ref[...]).astype(h_ref.dtype)

    u_ref[...] = jnp.dot(h_ref[...], w_ref[...], preferred_element_type=jnp.float32).astype(u_ref.dtype)


def _inproj(x2, norm_w, w, layer, *, tm=1024, tn=2816):
    n = x2.shape[0]
    return pl.pallas_call(
        _inproj_kernel,
        grid=(n // tm, D_IN // tn),
        in_specs=[pl.BlockSpec((tm, D_MODEL), lambda i, j: (i, 0)),
                  pl.BlockSpec((None, 1, D_MODEL), lambda i, j: (layer, 0, 0)),
                  pl.BlockSpec((None, D_MODEL, tn), lambda i, j: (layer, 0, j))],
        out_specs=pl.BlockSpec((tm, tn), lambda i, j: (i, j)),
        out_shape=jax.ShapeDtypeStruct((n, D_IN), jnp.bfloat16),
        scratch_shapes=[pltpu.VMEM((tm, D_MODEL), jnp.bfloat16)],
        compiler_params=pltpu.CompilerParams(dimension_semantics=("parallel", "arbitrary"),
                                             vmem_limit_bytes=VMEM_LIMIT),
        name="inproj",
    )(x2, norm_w, w)


def _head_lane_mask(hh, dtype):
    lane = lax.broadcasted_iota(jnp.int32, (1, LANES), 1)
    return ((lane // HEAD_DIM) == hh).astype(dtype)


def _qk(qm, k):
    return lax.dot_general(qm, k, (((1,), (1,)), ((), ())), preferred_element_type=jnp.float32)


def _merge_heads(parts):
    lane = lax.broadcasted_iota(jnp.int32, parts[0].shape, 1)
    return jnp.where(lane < HEAD_DIM, parts[0], parts[1])


SB_TILE = 256
SB_TILES_PER_STEP = 4
SB_ROW_CHUNK = 256


def _sb_kernel(q_ref, k_ref, v_ref, tri_ref, o_ref, acc_ref, carry_ref):
    t = SB_TILE

    def step(row0, newest, n_blocks, diagonal):
        tri = tri_ref[...]
        rc = SB_ROW_CHUNK
        chains = [(hh, r) for hh in range(HEADS_PER_TILE) for r in range(t // rc)]
        starts = [pl.multiple_of((newest - j) * t, t) for j in range(n_blocks)]

        terms = {}
        for hh, r in chains:
            qm = q_ref[pl.ds(row0 + r * rc, rc), :] * _head_lane_mask(hh, q_ref.dtype)
            for j in range(n_blocks):
                z = _qk(qm, k_ref[pl.ds(starts[j], t), :])
                neg_abs = lax.bitcast_convert_type(
                    lax.bitcast_convert_type(z, jnp.uint32) | jnp.uint32(0x80000000), jnp.float32)
                soft = jnp.log(1.0 + jnp.exp2(neg_abs)) * LOG2_E
                log_beta = jnp.minimum(z, 0.0) - soft
                log_1m = log_beta - z
                mask = None
                if diagonal and j == 0:
                    row = lax.broadcasted_iota(jnp.int32, (rc, t), 0) + r * rc
                    mask = lax.broadcasted_iota(jnp.int32, (rc, t), 1) < row
                    log_1m = jnp.where(mask, log_1m, 0.0)
                hi = lax.bitcast_convert_type(
                    lax.bitcast_convert_type(log_1m, jnp.uint32) & jnp.uint32(0xFFFF0000), jnp.float32)
                split = jnp.concatenate([hi.astype(jnp.bfloat16), (log_1m - hi).astype(jnp.bfloat16)], axis=1)
                terms[hh, r, j] = (log_beta, log_1m[:, 0:1], split, mask)

        cums = {key: jnp.dot(split, tri, preferred_element_type=jnp.float32)
                for key, (_, _, split, _) in terms.items()}

        alive = None
        for hh, r in chains:
            rows = slice(r * rc, (r + 1) * rc)
            carry = carry_ref[hh, rows, :]
            acc = acc_ref[hh, rows, :]
            for j in range(n_blocks):
                log_beta, first_term, _, mask = terms[hh, r, j]
                cum = cums[hh, r, j]
                w = jnp.exp2(log_beta + cum + jnp.concatenate([carry] * (t // LANES), axis=1))
                if mask is not None:
                    w = jnp.where(mask, w, 0.0)
                acc = acc + jnp.dot(w.astype(jnp.bfloat16), v_ref[pl.ds(starts[j], t), :],
                                    preferred_element_type=jnp.float32)
                carry = carry + jnp.broadcast_to(cum[:, 0:1] + first_term, (rc, LANES))
            acc_ref[hh, rows, :] = acc
            carry_ref[hh, rows, :] = carry
            alive = carry if alive is None else jnp.maximum(alive, carry)
        return (jnp.max(alive) < SB_DEAD_LOG2_WEIGHT).astype(jnp.int32)

    def query_tile(sub, c):
        qi = pl.program_id(2) * SB_TILES_PER_STEP + sub
        row0 = pl.multiple_of(sub * t, t)
        acc_ref[...] = jnp.zeros_like(acc_ref)
        carry_ref[...] = jnp.zeros_like(carry_ref)
        dead = lax.cond(qi >= 1, lambda: step(row0, qi, 2, True), lambda: step(row0, qi, 1, True))

        def cond(c):
            newest, dead = c
            return jnp.logical_and(newest >= 1, dead == 0)

        def body(c):
            newest, _ = c
            return newest - 2, step(row0, newest, 2, False)

        newest, dead = lax.while_loop(cond, body, (qi - 2, dead))

        @pl.when(jnp.logical_and(newest == 0, dead == 0))
        def _():
            step(row0, 0, 1, False)

        o_ref[pl.ds(row0, t), :] = _merge_heads([acc_ref[0], acc_ref[1]])
        return c

    lax.fori_loop(0, SB_TILES_PER_STEP, query_tile, 0)


def _sb_attention(u, tri, batch, seq):
    t = SB_TILE
    rows = SB_TILES_PER_STEP * t
    nq = seq // rows
    return pl.pallas_call(
        _sb_kernel,
        grid=(batch, N_PAIRS, nq),
        in_specs=[pl.BlockSpec((rows, LANES), lambda b, p, i: (b * nq + i, COL_QA // LANES + p)),
                  pl.BlockSpec((seq, LANES), lambda b, p, i: (b, COL_KA // LANES + p)),
                  pl.BlockSpec((seq, LANES), lambda b, p, i: (b, COL_VA // LANES + p)),
                  pl.BlockSpec((2 * t, t), lambda b, p, i: (0, 0))],
        out_specs=pl.BlockSpec((rows, LANES), lambda b, p, i: (b * nq + i, p)),
        out_shape=jax.ShapeDtypeStruct((batch * seq, WIDTH), jnp.float32),
        scratch_shapes=[pltpu.VMEM((HEADS_PER_TILE, t, LANES), jnp.float32),
                        pltpu.VMEM((HEADS_PER_TILE, t, LANES), jnp.float32)],
        compiler_params=pltpu.CompilerParams(dimension_semantics=("parallel", "parallel", "arbitrary"),
                                             vmem_limit_bytes=VMEM_LIMIT),
        name="sb_attention",
    )(u, u, u, tri)


MOBA_LO_LANE = 64
MOBA_NEVER = 63
MOBA_ZERO_ROW = 64
MOBA_MASKED = -1e30
MOBA_GROUP = 8
MOBA_MIN_GROUP = 2
MOBA_BLOCKS_PER_STEP = 4


def _moba_kernel(*refs, nb):
    def query_block(sub, c):
        _moba_query_block(sub, *refs, nb=nb)
        return c

    lax.fori_loop(0, MOBA_BLOCKS_PER_STEP, query_block, 0)


def _moba_query_block(sub, far_ref, q_ref, k_ref, v_ref, btab_ref, oh_ref, o_ref,
                      kmean_ref, vt_ref, qa_ref, m_ref, acc_ref, *, nb):
    t = MOBA_BLOCK
    g = MOBA_GROUP
    pair = pl.program_id(1)
    own = pl.program_id(2) * MOBA_BLOCKS_PER_STEP + sub
    q_rows = pl.ds(pl.multiple_of(sub * t, t), t)
    heads = range(HEADS_PER_TILE)

    @pl.when(own == 0)
    def _():
        kmean_ref[...] = jnp.zeros_like(kmean_ref)
        kf = k_ref[...].astype(jnp.float32).reshape(nb, t, LANES)
        kmean_ref[0:nb, :] = jnp.mean(kf, axis=1)

        head_of_row = lax.broadcasted_iota(jnp.int32, (LANES, t), 0) // HEAD_DIM

        def transpose_block(b, c):
            vb = v_ref[pl.ds(pl.multiple_of(b * t, t), t), :].astype(jnp.float32).T
            for hh in heads:
                vt_ref[hh, b] = jnp.where(head_of_row == hh, vb, 1.0).astype(jnp.bfloat16)
            return c

        lax.fori_loop(0, nb, transpose_block, 0)

    km = kmean_ref[...]
    km0 = km.astype(jnp.bfloat16)
    r1 = km - km0.astype(jnp.float32)
    km1 = r1.astype(jnp.bfloat16)
    km2 = (r1 - km1.astype(jnp.float32)).astype(jnp.bfloat16)

    def update(scores, blocks, first):
        col_max = [jnp.max(s, axis=0, keepdims=True) for s in scores]
        if first:
            m_new = col_max
        else:
            m_old = [m_ref[hh][0:1, :] for hh in heads]
            m_new = [jnp.maximum(m_old[hh], col_max[hh]) for hh in heads]
            alpha = [jnp.exp2(m_old[hh] - m_new[hh]) for hh in heads]
        p = [jnp.exp2(scores[hh] - m_new[hh]).astype(jnp.bfloat16) for hh in heads]
        pv = [jnp.dot(jnp.concatenate([vt_ref[hh, b] for b in blocks], axis=1), p[hh],
                      preferred_element_type=jnp.float32) for hh in heads]
        for hh in heads:
            acc_ref[hh] = pv[hh] if first else alpha[hh] * acc_ref[hh] + pv[hh]
            m_ref[hh] = jnp.broadcast_to(m_new[hh], m_ref.shape[1:])

    def block_lanes(row):
        return jnp.concatenate([oh_ref[row]] * (t // BF16_SUBLANES), axis=0)

    nh = HEADS_PER_TILE
    blk = lax.broadcasted_iota(jnp.int32, (MOBA_LO_LANE, nh * t), 0)
    blk_f = blk.astype(jnp.float32)
    gate_rows = sum(_qk(jnp.concatenate([part * _head_lane_mask(hh, part.dtype) for hh in heads], axis=0),
                        q_ref[q_rows, :]) for part in (km0, km1, km2))
    gate = jnp.concatenate([gate_rows[hh * MOBA_LO_LANE:(hh + 1) * MOBA_LO_LANE] for hh in heads], axis=1)
    gate = jnp.where(blk < own, gate, _NEG_INF)
    sel = jnp.zeros_like(gate)
    for _ in range(MOBA_TOPK):
        best = jnp.max(gate, axis=0, keepdims=True)
        idx = jnp.min(jnp.where(gate == best, blk_f, float(LANES)), axis=0, keepdims=True)
        hit = blk_f == idx
        sel = jnp.where(jnp.logical_and(hit, best > _NEG_INF), 1.0, sel)
        gate = jnp.where(hit, _NEG_INF, gate)

    col_head = lax.broadcasted_iota(jnp.int32, (1, nh * t), 1) // t
    far_bias = jnp.where(col_head == 0, far_ref[pair * nh], far_ref[pair * nh + 1])
    term = jnp.where(sel > 0.0, jnp.where(blk < own - 1, far_bias, 0.0), MOBA_MASKED)
    hi = term.astype(jnp.bfloat16).astype(jnp.float32)
    lo = term - hi
    for hh in heads:
        cols = slice(hh * t, (hh + 1) * t)
        inj = jnp.concatenate([hi[:, cols], lo[:, cols]], axis=0).T
        qa_ref[hh] = jnp.concatenate([q_ref[q_rows, :] * _head_lane_mask(hh, q_ref.dtype), inj.astype(jnp.bfloat16)],
                                     axis=1)

    prev = jnp.maximum(own - 1, 0)
    n_far = jnp.maximum(own - 1, 0)
    n_full = n_far // g
    left = n_far - n_full * g

    def far_keys(first_blk, group):
        g0 = jnp.minimum(first_blk, nb - group)
        rows = []
        for j in range(group):
            b = g0 + j
            active = jnp.logical_and(b >= first_blk, b < n_far)
            rows.append(block_lanes(jnp.where(active, b, MOBA_NEVER)))
        keys = k_ref[pl.ds(pl.multiple_of(g0 * t, t), group * t), :]
        return g0, keys, rows

    def first_step(group):
        keys = [k_ref[pl.ds(pl.multiple_of(own * t, t), t), :], k_ref[pl.ds(pl.multiple_of(prev * t, t), t), :]]
        rows = [block_lanes(MOBA_ZERO_ROW), block_lanes(jnp.where(own >= 1, prev, MOBA_NEVER))]
        if group:
            g0, far, far_rows = far_keys(n_full * g, group)
            keys.append(far)
            rows.extend(far_rows)
        k_aug = jnp.concatenate([jnp.concatenate(keys, axis=0), jnp.concatenate(rows, axis=0)], axis=1)
        scores = [_qk(k_aug, qa_ref[hh]) for hh in heads]
        update([scores[hh][:t] + btab_ref[hh, 0] for hh in heads], [own], first=True)
        update([scores[hh][t:2 * t] + btab_ref[hh, 1] for hh in heads], [prev], first=False)
        for j in range(group):
            update([s[(2 + j) * t:(3 + j) * t] for s in scores], [g0 + j], first=False)

    group = g
    while group >= MOBA_MIN_GROUP:
        lower = group // 2 if group > MOBA_MIN_GROUP else 0

        @pl.when(jnp.logical_and(left > lower, left <= group))
        def _(group=group):
            first_step(group)

        group //= 2

    @pl.when(left == 0)
    def _():
        first_step(0)

    def far_body(i, c):
        g0, keys, rows = far_keys(i * g, g)
        k_aug = jnp.concatenate([keys, jnp.concatenate(rows, axis=0)], axis=1)
        scores = [_qk(k_aug, qa_ref[hh]) for hh in heads]
        for j in range(g):
            update([s[j * t:(j + 1) * t] for s in scores], [g0 + j], first=False)
        return c

    lax.fori_loop(0, n_full, far_body, 0)

    out_t = jnp.concatenate([acc_ref[0][:HEAD_DIM] / acc_ref[0][HEAD_DIM:],
                             acc_ref[1][HEAD_DIM:] / acc_ref[1][:HEAD_DIM]], axis=0)
    o_ref[q_rows, :] = out_t.T


def _moba_attention(u, far_bias, btab, block_lanes, batch, seq):
    t = MOBA_BLOCK
    nb = seq // t
    rows = MOBA_BLOCKS_PER_STEP * t
    nstep = seq // rows
    kernel = functools.partial(_moba_kernel, nb=nb)
    grid_spec = pltpu.PrefetchScalarGridSpec(
        num_scalar_prefetch=1,
        grid=(batch, N_PAIRS, nstep),
        in_specs=[pl.BlockSpec((rows, LANES), lambda b, p, i, far: (b * nstep + i, COL_QB // LANES + p)),
                  pl.BlockSpec((seq, LANES), lambda b, p, i, far: (b, COL_KB // LANES + p)),
                  pl.BlockSpec((seq, LANES), lambda b, p, i, far: (b, COL_VB // LANES + p)),
                  pl.BlockSpec((HEADS_PER_TILE, 2, t, t), lambda b, p, i, far: (p, 0, 0, 0)),
                  pl.BlockSpec(block_lanes.shape, lambda b, p, i, far: (0, 0, 0))],
        out_specs=pl.BlockSpec((rows, LANES), lambda b, p, i, far: (b * nstep + i, p)),
        scratch_shapes=[pltpu.VMEM((MOBA_LO_LANE, LANES), jnp.float32),
                        pltpu.VMEM((HEADS_PER_TILE, nb, LANES, t), jnp.bfloat16),
                        pltpu.VMEM((HEADS_PER_TILE, t, 2 * LANES), jnp.bfloat16),
                        pltpu.VMEM((HEADS_PER_TILE, F32_SUBLANES, t), jnp.float32),
                        pltpu.VMEM((HEADS_PER_TILE, LANES, t), jnp.float32)])
    return pl.pallas_call(
        kernel,
        grid_spec=grid_spec,
        out_shape=jax.ShapeDtypeStruct((batch * seq, WIDTH), jnp.float32),
        compiler_params=pltpu.CompilerParams(dimension_semantics=("parallel", "parallel", "arbitrary"),
                                             vmem_limit_bytes=VMEM_LIMIT),
        name="moba_attention",
    )(far_bias, u, u, u, btab, block_lanes)


SWA_STEP_WINDOWS = 2


def _swa_kernel(sink_ref, q_ref, kp_ref, kc_ref, vp_ref, vc_ref, bias_ref, o_ref):
    w = SWA_WINDOW
    keys = jnp.concatenate([kp_ref[...], kc_ref[...]], axis=0)
    values = jnp.concatenate([vp_ref[...], vc_ref[...]], axis=0)
    slots = [(j, half) for j in range(SWA_GROUP) for half in range(SWA_KV_HEADS)]
    chains = [(win, slot) for win in range(SWA_STEP_WINDOWS) for slot in range(len(slots))]

    def attend(first_step):
        scores = []
        for win, slot in chains:
            j, half = slots[slot]
            q = q_ref[win * w:(win + 1) * w, j * LANES:(j + 1) * LANES]
            s = _qk(q * _head_lane_mask(half, q.dtype), keys[win * w:(win + 2) * w]) + bias_ref[slot]
            if first_step and win == 0:
                s = jnp.where(lax.broadcasted_iota(jnp.int32, (w, 2 * w), 1) >= w, s, _NEG_INF)
            scores.append(s)
        m = [jnp.maximum(jnp.broadcast_to(jnp.max(s, axis=1, keepdims=True), (w, LANES)), sink_ref[slot])
             for (win, slot), s in zip(chains, scores)]
        p = [jnp.exp2(s - jnp.concatenate([mi] * (2 * w // LANES), axis=1)).astype(jnp.bfloat16)
             for mi, s in zip(m, scores)]
        pv = []
        for (win, slot), pi in zip(chains, p):
            vw = values[win * w:(win + 2) * w]
            vw = jnp.where(_head_lane_mask(slots[slot][1], jnp.int32) > 0, vw, jnp.ones_like(vw))
            pv.append(jnp.dot(pi, vw, preferred_element_type=jnp.float32))
        out = [pvi / (pltpu.roll(pvi, HEAD_DIM, axis=1) + jnp.exp2(sink_ref[slot] - mi))
               for (win, slot), pvi, mi in zip(chains, pv, m)]
        for win in range(SWA_STEP_WINDOWS):
            for j in range(SWA_GROUP):
                first = win * len(slots) + j * SWA_KV_HEADS
                o_ref[win * w:(win + 1) * w, j * LANES:(j + 1) * LANES] = _merge_heads(
                    out[first:first + SWA_KV_HEADS])

    @pl.when(pl.program_id(1) == 0)
    def _():
        attend(True)

    @pl.when(pl.program_id(1) > 0)
    def _():
        attend(False)


def _swa_attention(u, sinks_perm, bias, batch, seq):
    w = SWA_WINDOW
    sw = SWA_STEP_WINDOWS
    nstep = seq // (sw * w)
    kcol, vcol = COL_KC // LANES, COL_VC // LANES
    prev_window = lambda b, n, s: b * (seq // w) + jnp.maximum(sw * n - 1, 0)
    grid_spec = pltpu.PrefetchScalarGridSpec(
        num_scalar_prefetch=1,
        grid=(batch, nstep),
        in_specs=[pl.BlockSpec((sw * w, WIDTH), lambda b, n, s: (b * nstep + n, COL_QC // WIDTH)),
                  pl.BlockSpec((w, LANES), lambda b, n, s: (prev_window(b, n, s), kcol)),
                  pl.BlockSpec((sw * w, LANES), lambda b, n, s: (b * nstep + n, kcol)),
                  pl.BlockSpec((w, LANES), lambda b, n, s: (prev_window(b, n, s), vcol)),
                  pl.BlockSpec((sw * w, LANES), lambda b, n, s: (b * nstep + n, vcol)),
                  pl.BlockSpec((N_HEADS, w, 2 * w), lambda b, n, s: (0, 0, 0))],
        out_specs=pl.BlockSpec((sw * w, WIDTH), lambda b, n, s: (b * nstep + n, 0)))
    return pl.pallas_call(
        _swa_kernel,
        grid_spec=grid_spec,
        out_shape=jax.ShapeDtypeStruct((batch * seq, WIDTH), jnp.float32),
        compiler_params=pltpu.CompilerParams(dimension_semantics=("parallel", "arbitrary"),
                                             vmem_limit_bytes=VMEM_LIMIT),
        name="swa_attention",
    )(sinks_perm, u, u, u, u, u, bias)


def _post_kernel(x_ref, ya_ref, yb_ref, yc_ref, ga_ref, gb_ref, gc_ref, ma_ref, mb_ref, mc_ref,
                 wa_ref, wb_ref, wc_ref, wo_ref, fw_ref, o_ref, *, final):
    def branch(y_ref, g_ref, w_ref, m_ref):
        g = g_ref[...].astype(jnp.float32)
        y = (y_ref[...] * (g * jax.nn.sigmoid(g))).astype(jnp.bfloat16)
        proj = jnp.dot(y, w_ref[...], preferred_element_type=jnp.float32)
        return jax.nn.sigmoid(m_ref[...].astype(jnp.float32)) * proj

    merged = (branch(ya_ref, ga_ref, wa_ref, ma_ref) + branch(yb_ref, gb_ref, wb_ref, mb_ref)
              + branch(yc_ref, gc_ref, wc_ref, mc_ref))
    out = x_ref[...] + jnp.dot(merged.astype(jnp.bfloat16), wo_ref[...], preferred_element_type=jnp.float32)
    if final:
        out = out * lax.rsqrt(jnp.mean(out * out, axis=-1, keepdims=True) + RMS_EPS) * fw_ref[...]
    o_ref[...] = out


def _post(x2, ya, yb, yc, u, wa, wb, wc, wo, fw, layer, *, final, tm=512):
    n = x2.shape[0]
    row = lambda i: (i, 0)
    full = lambda i: (0, 0)
    of_layer = lambda i: (layer, 0, 0)
    y_spec = pl.BlockSpec((tm, WIDTH), row)
    in_specs = [pl.BlockSpec((tm, D_MODEL), row), y_spec, y_spec, y_spec,
                pl.BlockSpec((tm, WIDTH), lambda i: (i, COL_GA // WIDTH)),
                pl.BlockSpec((tm, WIDTH), lambda i: (i, COL_GB // WIDTH)),
                pl.BlockSpec((tm, WIDTH), lambda i: (i, COL_GC // WIDTH)),
                pl.BlockSpec((tm, D_MODEL), lambda i: (i, COL_MA // D_MODEL)),
                pl.BlockSpec((tm, D_MODEL), lambda i: (i, COL_MB // D_MODEL)),
                pl.BlockSpec((tm, D_MODEL), lambda i: (i, COL_MC // D_MODEL)),
                pl.BlockSpec((None, WIDTH, D_MODEL), of_layer), pl.BlockSpec((None, WIDTH, D_MODEL), of_layer),
                pl.BlockSpec((None, WIDTH, D_MODEL), of_layer), pl.BlockSpec((None, D_MODEL, D_MODEL), of_layer),
                pl.BlockSpec((1, D_MODEL), full)]
    return pl.pallas_call(
        functools.partial(_post_kernel, final=final),
        grid=(n // tm,),
        in_specs=in_specs,
        out_specs=pl.BlockSpec((tm, D_MODEL), row),
        out_shape=jax.ShapeDtypeStruct((n, D_MODEL), jnp.float32),
        compiler_params=pltpu.CompilerParams(dimension_semantics=("parallel",),
                                             vmem_limit_bytes=VMEM_LIMIT),
        name="post_final" if final else "post",
    )(x2, ya, yb, yc, u, u, u, u, u, u, wa, wb, wc, wo, fw)


def _bucket_lookup(table, dist):
    onehot = (_rel_bucket(dist)[..., None] == jnp.arange(REL_BUCKETS)).astype(jnp.float32)
    return jnp.einsum("...b,bh->h...", onehot, table.astype(jnp.float32), precision=lax.Precision.HIGHEST)


def _bias_tables(rel_bias):
    moba_tab = rel_bias[:, :N_HEADS]
    swa_tab = _swa_head_order(rel_bias[:, N_HEADS:], 1, per_head=1)
    t = MOBA_BLOCK
    d_own = jnp.arange(t)[:, None] - jnp.arange(t)[None, :]
    own = jnp.where(d_own[None] >= 0, _bucket_lookup(moba_tab, d_own), _NEG_INF)
    prev = _bucket_lookup(moba_tab, d_own + t)
    btab = jnp.swapaxes(jnp.stack([own, prev], axis=1), 2, 3) * LOG2_E
    far = _bucket_lookup(moba_tab, jnp.full((1,), t + 1, jnp.int32))[:, 0] * LOG2_E

    w = SWA_WINDOW
    dist = jnp.arange(w)[:, None] + w - jnp.arange(2 * w)[None, :]
    in_band = (dist >= 0) & (dist < w)
    swa = jnp.where(in_band[None], _bucket_lookup(swa_tab, dist) * LOG2_E, _NEG_INF)
    return btab, far, swa


def kernel(x, norm_w, w_in, w_proj_a, w_proj_b, w_proj_c, w_out, sinks, rel_bias, final_norm_w):
    batch, seq, _ = x.shape
    depth = w_in.shape[0]
    assert seq % MOBA_BLOCK == 0 and MOBA_GROUP <= seq // MOBA_BLOCK <= MOBA_NEVER
    btab, far, swa_bias = _bias_tables(rel_bias)
    t = SB_TILE
    tri = (jnp.arange(t)[:, None] > jnp.arange(t)[None, :]).astype(jnp.bfloat16)
    tri = jnp.concatenate([tri, tri], axis=0)
    lane = jnp.arange(LANES)[None, :]
    blk = jnp.arange(MOBA_ZERO_ROW + 1)[:, None]
    block_lanes = ((lane == blk) | (lane == blk + MOBA_LO_LANE)) & (blk < MOBA_ZERO_ROW)
    block_lanes = jnp.broadcast_to(block_lanes[:, None, :],
                                   (MOBA_ZERO_ROW + 1, BF16_SUBLANES, LANES)).astype(jnp.bfloat16)
    sinks_perm = _swa_head_order(sinks.astype(jnp.float32), 1, per_head=1) * LOG2_E

    w_u = _permute_input_columns(w_in)
    norm_w3 = norm_w.reshape(depth, 1, D_MODEL)
    wa = w_proj_a.astype(jnp.bfloat16)
    wb = w_proj_b.astype(jnp.bfloat16)
    wc = _swa_head_order(w_proj_c, 1).astype(jnp.bfloat16)
    wo = w_out.astype(jnp.bfloat16)
    fw = final_norm_w.reshape(1, D_MODEL)

    x2 = x.reshape(batch * seq, D_MODEL)
    for layer in range(depth):
        u = _inproj(x2, norm_w3, w_u, layer)
        ya = _sb_attention(u, tri, batch, seq)
        yb = _moba_attention(u, far, btab, block_lanes, batch, seq)
        yc = _swa_attention(u, sinks_perm[layer], swa_bias, batch, seq)
        x2 = _post(x2, ya, yb, yc, u, wa, wb, wc, wo, fw, layer, final=(layer == depth - 1))
    return x2.reshape(batch, seq, D_MODEL)
```

```python
import functools
import math

import jax
import jax.numpy as jnp
from jax import lax
from jax.experimental import pallas as pl
from jax.experimental.pallas import tpu as pltpu

D_MODEL = 1024
HEAD_DIM = 64
N_HEADS = 8
WIDTH = N_HEADS * HEAD_DIM
MOBA_BLOCK = 256
MOBA_TOPK = 3
SWA_KV_HEADS = 2
SWA_GROUP = N_HEADS // SWA_KV_HEADS
SWA_WINDOW = 128
REL_BUCKETS = 32
REL_MAX_DIST = 128
RMS_EPS = 1e-6

LANES = 128
BF16_SUBLANES = 16
F32_SUBLANES = 8
HEADS_PER_TILE = LANES // HEAD_DIM
N_PAIRS = N_HEADS // HEADS_PER_TILE

COL_QA, COL_KA, COL_VA = 0, 512, 1024
COL_QB, COL_KB, COL_VB = 1536, 2048, 2560
COL_QC = 3072
COL_GA, COL_GB, COL_GC = 3584, 4096, 4608
COL_MA, COL_MB, COL_MC = 5120, 6144, 7168
COL_KC, COL_VC = 8192, 8320
D_IN = 8448

V7X_VMEM_BYTES = 64 * 1024 * 1024
VMEM_LIMIT = V7X_VMEM_BYTES * 3 // 4

LOG2_E = math.log2(math.e)

SB_DEAD_LOG_WEIGHT = -120.0
SB_DEAD_LOG2_WEIGHT = SB_DEAD_LOG_WEIGHT * LOG2_E

_NEG_INF = float("-inf")


def _rel_bucket(dist):
    max_exact = REL_BUCKETS // 2
    n = jnp.maximum(dist, 0)
    nf = jnp.maximum(n, 1).astype(jnp.float32)
    large = max_exact + (jnp.log(nf / max_exact) / math.log(REL_MAX_DIST / max_exact)
                         * (REL_BUCKETS - max_exact)).astype(jnp.int32)
    large = jnp.minimum(large, REL_BUCKETS - 1)
    return jnp.where(n < max_exact, n, large)


def _swa_head_order(t, axis, per_head=HEAD_DIM):
    axis = axis % t.ndim
    shape = t.shape
    t = t.reshape(shape[:axis] + (SWA_KV_HEADS, SWA_GROUP, per_head) + shape[axis + 1:])
    return jnp.swapaxes(t, axis, axis + 1).reshape(shape)


def _permute_input_columns(w):
    old = {"qa": 0, "ka": 512, "va": 1024, "ga": 1536, "qb": 2048, "kb": 2560, "vb": 3072, "gb": 3584,
           "qc": 4096, "kc": 4608, "vc": 4736, "gc": 4864, "ma": 5376, "mb": 6400, "mc": 7424}
    q_scale = HEAD_DIM ** -0.5 * LOG2_E

    def cols(name, width, scale=None):
        part = w[..., old[name]:old[name] + width]
        return part if scale is None else part * scale

    parts = [cols("qa", 512, q_scale), cols("ka", 512), cols("va", 512),
             cols("qb", 512, q_scale), cols("kb", 512), cols("vb", 512),
             _swa_head_order(cols("qc", WIDTH, q_scale), -1),
             cols("ga", 512), cols("gb", 512), _swa_head_order(cols("gc", WIDTH), -1),
             cols("ma", 1024), cols("mb", 1024), cols("mc", 1024),
             cols("kc", 128), cols("vc", 128)]
    return jnp.concatenate(parts, axis=-1).astype(jnp.bfloat16)


def _inproj_kernel(x_ref, nw_ref, w_ref, u_ref, h_ref):
    @pl.when(pl.program_id(1) == 0)
    def _():
        x = x_ref[...]
        y = x * lax.rsqrt(jnp.mean(x * x, axis=-1, keepdims=True) + RMS_EPS)
        h_ref[...] = (y * nw_ref[...]).astype(h_ref.dtype)

    u_ref[...] = jnp.dot(h_ref[...], w_ref[...], preferred_element_type=jnp.float32).astype(u_ref.dtype)


def _inproj(x2, norm_w, w, layer, *, tm=1024, tn=2816):
    n = x2.shape[0]
    return pl.pallas_call(
        _inproj_kernel,
        grid=(n // tm, D_IN // tn),
        in_specs=[pl.BlockSpec((tm, D_MODEL), lambda i, j: (i, 0)),
                  pl.BlockSpec((None, 1, D_MODEL), lambda i, j: (layer, 0, 0)),
                  pl.BlockSpec((None, D_MODEL, tn), lambda i, j: (layer, 0, j))],
        out_specs=pl.BlockSpec((tm, tn), lambda i, j: (i, j)),
        out_shape=jax.ShapeDtypeStruct((n, D_IN), jnp.bfloat16),
        scratch_shapes=[pltpu.VMEM((tm, D_MODEL), jnp.bfloat16)],
        compiler_params=pltpu.CompilerParams(dimension_semantics=("parallel", "arbitrary"),
                                             vmem_limit_bytes=VMEM_LIMIT),
        name="inproj",
    )(x2, norm_w, w)


def _head_lane_mask(hh, dtype):
    lane = lax.broadcasted_iota(jnp.int32, (1, LANES), 1)
    return ((lane // HEAD_DIM) == hh).astype(dtype)


def _qk(qm, k):
    return lax.dot_general(qm, k, (((1,), (1,)), ((), ())), preferred_element_type=jnp.float32)


def _merge_heads(parts):
    lane = lax.broadcasted_iota(jnp.int32, parts[0].shape, 1)
    return jnp.where(lane < HEAD_DIM, parts[0], parts[1])


SB_TILE = 256
SB_TILES_PER_STEP = 4
SB_ROW_CHUNK = 256


def _sb_kernel(q_ref, k_ref, v_ref, tri_ref, o_ref, acc_ref, carry_ref):
    t = SB_TILE

    def step(row0, newest, n_blocks, diagonal):
        tri = tri_ref[...]
        rc = SB_ROW_CHUNK
        chains = [(hh, r) for hh in range(HEADS_PER_TILE) for r in range(t // rc)]
        starts = [pl.multiple_of((newest - j) * t, t) for j in range(n_blocks)]

        terms = {}
        for hh, r in chains:
            qm = q_ref[pl.ds(row0 + r * rc, rc), :] * _head_lane_mask(hh, q_ref.dtype)
            for j in range(n_blocks):
                z = _qk(qm, k_ref[pl.ds(starts[j], t), :])
                neg_abs = lax.bitcast_convert_type(
                    lax.bitcast_convert_type(z, jnp.uint32) | jnp.uint32(0x80000000), jnp.float32)
                soft = jnp.log(1.0 + jnp.exp2(neg_abs)) * LOG2_E
                log_beta = jnp.minimum(z, 0.0) - soft
                log_1m = log_beta - z
                mask = None
                if diagonal and j == 0:
                    row = lax.broadcasted_iota(jnp.int32, (rc, t), 0) + r * rc
                    mask = lax.broadcasted_iota(jnp.int32, (rc, t), 1) < row
                    log_1m = jnp.where(mask, log_1m, 0.0)
                hi = lax.bitcast_convert_type(
                    lax.bitcast_convert_type(log_1m, jnp.uint32) & jnp.uint32(0xFFFF0000), jnp.float32)
                split = jnp.concatenate([hi.astype(jnp.bfloat16), (log_1m - hi).astype(jnp.bfloat16)], axis=1)
                terms[hh, r, j] = (log_beta, log_1m[:, 0:1], split, mask)

        cums = {key: jnp.dot(split, tri, preferred_element_type=jnp.float32)
                for key, (_, _, split, _) in terms.items()}

        alive = None
        for hh, r in chains:
            rows = slice(r * rc, (r + 1) * rc)
            carry = carry_ref[hh, rows, :]
            acc = acc_ref[hh, rows, :]
            for j in range(n_blocks):
                log_beta, first_term, _, mask = terms[hh, r, j]
                cum = cums[hh, r, j]
                w = jnp.exp2(log_beta + cum + jnp.concatenate([carry] * (t // LANES), axis=1))
                if mask is not None:
                    w = jnp.where(mask, w, 0.0)
                acc = acc + jnp.dot(w.astype(jnp.bfloat16), v_ref[pl.ds(starts[j], t), :],
                                    preferred_element_type=jnp.float32)
                carry = carry + jnp.broadcast_to(cum[:, 0:1] + first_term, (rc, LANES))
            acc_ref[hh, rows, :] = acc
            carry_ref[hh, rows, :] = carry
            alive = carry if alive is None else jnp.maximum(alive, carry)
        return (jnp.max(alive) < SB_DEAD_LOG2_WEIGHT).astype(jnp.int32)

    def query_tile(sub, c):
        qi = pl.program_id(2) * SB_TILES_PER_STEP + sub
        row0 = pl.multiple_of(sub * t, t)
        acc_ref[...] = jnp.zeros_like(acc_ref)
        carry_ref[...] = jnp.zeros_like(carry_ref)
        dead = lax.cond(qi >= 1, lambda: step(row0, qi, 2, True), lambda: step(row0, qi, 1, True))

        def cond(c):
            newest, dead = c
            return jnp.logical_and(newest >= 1, dead == 0)

        def body(c):
            newest, _ = c
            return newest - 2, step(row0, newest, 2, False)

        newest, dead = lax.while_loop(cond, body, (qi - 2, dead))

        @pl.when(jnp.logical_and(newest == 0, dead == 0))
        def _():
            step(row0, 0, 1, False)

        o_ref[pl.ds(row0, t), :] = _merge_heads([acc_ref[0], acc_ref[1]])
        return c

    lax.fori_loop(0, SB_TILES_PER_STEP, query_tile, 0)


def _sb_attention(u, tri, batch, seq):
    t = SB_TILE
    rows = SB_TILES_PER_STEP * t
    nq = seq // rows
    return pl.pallas_call(
        _sb_kernel,
        grid=(batch, N_PAIRS, nq),
        in_specs=[pl.BlockSpec((rows, LANES), lambda b, p, i: (b * nq + i, COL_QA // LANES + p)),
                  pl.BlockSpec((seq, LANES), lambda b, p, i: (b, COL_KA // LANES + p)),
                  pl.BlockSpec((seq, LANES), lambda b, p, i: (b, COL_VA // LANES + p)),
                  pl.BlockSpec((2 * t, t), lambda b, p, i: (0, 0))],
        out_specs=pl.BlockSpec((rows, LANES), lambda b, p, i: (b * nq + i, p)),
        out_shape=jax.ShapeDtypeStruct((batch * seq, WIDTH), jnp.float32),
        scratch_shapes=[pltpu.VMEM((HEADS_PER_TILE, t, LANES), jnp.float32),
                        pltpu.VMEM((HEADS_PER_TILE, t, LANES), jnp.float32)],
        compiler_params=pltpu.CompilerParams(dimension_semantics=("parallel", "parallel", "arbitrary"),
                                             vmem_limit_bytes=VMEM_LIMIT),
        name="sb_attention",
    )(u, u, u, tri)


MOBA_LO_LANE = 64
MOBA_NEVER = 63
MOBA_ZERO_ROW = 64
MOBA_MASKED = -1e30
MOBA_GROUP = 8
MOBA_BLOCKS_PER_STEP = 4


def _moba_kernel(*refs, nb):
    def query_block(sub, c):
        _moba_query_block(sub, *refs, nb=nb)
        return c

    lax.fori_loop(0, MOBA_BLOCKS_PER_STEP, query_block, 0)


def _moba_query_block(sub, far_ref, q_ref, k_ref, v_ref, btab_ref, oh_ref, o_ref,
                      kmean_ref, vt_ref, qa_ref, m_ref, acc_ref, *, nb):
    t = MOBA_BLOCK
    g = MOBA_GROUP
    pair = pl.program_id(1)
    own = pl.program_id(2) * MOBA_BLOCKS_PER_STEP + sub
    q_rows = pl.ds(pl.multiple_of(sub * t, t), t)
    heads = range(HEADS_PER_TILE)

    @pl.when(own == 0)
    def _():
        kmean_ref[...] = jnp.zeros_like(kmean_ref)
        kf = k_ref[...].astype(jnp.float32).reshape(nb, t, LANES)
        kmean_ref[0:nb, :] = jnp.mean(kf, axis=1)

        head_of_row = lax.broadcasted_iota(jnp.int32, (LANES, t), 0) // HEAD_DIM

        def transpose_block(b, c):
            vb = v_ref[pl.ds(pl.multiple_of(b * t, t), t), :].astype(jnp.float32).T
            for hh in heads:
                vt_ref[hh, b] = jnp.where(head_of_row == hh, vb, 1.0).astype(jnp.bfloat16)
            return c

        lax.fori_loop(0, nb, transpose_block, 0)

    km = kmean_ref[...]
    km0 = km.astype(jnp.bfloat16)
    r1 = km - km0.astype(jnp.float32)
    km1 = r1.astype(jnp.bfloat16)
    km2 = (r1 - km1.astype(jnp.float32)).astype(jnp.bfloat16)

    def update(scores, blocks, first):
        col_max = [jnp.max(s, axis=0, keepdims=True) for s in scores]
        if first:
            m_new = col_max
        else:
            m_old = [m_ref[hh][0:1, :] for hh in heads]
            m_new = [jnp.maximum(m_old[hh], col_max[hh]) for hh in heads]
            alpha = [jnp.exp2(m_old[hh] - m_new[hh]) for hh in heads]
        p = [jnp.exp2(scores[hh] - m_new[hh]).astype(jnp.bfloat16) for hh in heads]
        pv = [jnp.dot(jnp.concatenate([vt_ref[hh, b] for b in blocks], axis=1), p[hh],
                      preferred_element_type=jnp.float32) for hh in heads]
        for hh in heads:
            acc_ref[hh] = pv[hh] if first else alpha[hh] * acc_ref[hh] + pv[hh]
            m_ref[hh] = jnp.broadcast_to(m_new[hh], m_ref.shape[1:])

    def block_lanes(row):
        return jnp.concatenate([oh_ref[row]] * (t // BF16_SUBLANES), axis=0)

    nh = HEADS_PER_TILE
    blk = lax.broadcasted_iota(jnp.int32, (MOBA_LO_LANE, nh * t), 0)
    blk_f = blk.astype(jnp.float32)
    gate_rows = sum(_qk(jnp.concatenate([part * _head_lane_mask(hh, part.dtype) for hh in heads], axis=0),
                        q_ref[q_rows, :]) for part in (km0, km1, km2))
    gate = jnp.concatenate([gate_rows[hh * MOBA_LO_LANE:(hh + 1) * MOBA_LO_LANE] for hh in heads], axis=1)
    gate = jnp.where(blk < own, gate, _NEG_INF)
    sel = jnp.zeros_like(gate)
    for _ in range(MOBA_TOPK):
        best = jnp.max(gate, axis=0, keepdims=True)
        idx = jnp.min(jnp.where(gate == best, blk_f, float(LANES)), axis=0, keepdims=True)
        hit = blk_f == idx
        sel = jnp.where(jnp.logical_and(hit, best > _NEG_INF), 1.0, sel)
        gate = jnp.where(hit, _NEG_INF, gate)

    col_head = lax.broadcasted_iota(jnp.int32, (1, nh * t), 1) // t
    far_bias = jnp.where(col_head == 0, far_ref[pair * nh], far_ref[pair * nh + 1])
    term = jnp.where(sel > 0.0, jnp.where(blk < own - 1, far_bias, 0.0), MOBA_MASKED)
    hi = term.astype(jnp.bfloat16).astype(jnp.float32)
    lo = term - hi
    for hh in heads:
        cols = slice(hh * t, (hh + 1) * t)
        inj = jnp.concatenate([hi[:, cols], lo[:, cols]], axis=0).T
        qa_ref[hh] = jnp.concatenate([q_ref[q_rows, :] * _head_lane_mask(hh, q_ref.dtype), inj.astype(jnp.bfloat16)],
                                     axis=1)

    prev = jnp.maximum(own - 1, 0)
    n_far = jnp.maximum(own - 1, 0)
    n_full = n_far // g
    left = n_far - n_full * g

    def far_keys(first_blk, group):
        g0 = jnp.minimum(first_blk, nb - group)
        rows = []
        for j in range(group):
            b = g0 + j
            active = jnp.logical_and(b >= first_blk, b < n_far)
            rows.append(block_lanes(jnp.where(active, b, MOBA_NEVER)))
        keys = k_ref[pl.ds(pl.multiple_of(g0 * t, t), group * t), :]
        return g0, keys, rows

    def first_step(group):
        keys = [k_ref[pl.ds(pl.multiple_of(own * t, t), t), :], k_ref[pl.ds(pl.multiple_of(prev * t, t), t), :]]
        rows = [block_lanes(MOBA_ZERO_ROW), block_lanes(jnp.where(own >= 1, prev, MOBA_NEVER))]
        if group:
            g0, far, far_rows = far_keys(n_full * g, group)
            keys.append(far)
            rows.extend(far_rows)
        k_aug = jnp.concatenate([jnp.concatenate(keys, axis=0), jnp.concatenate(rows, axis=0)], axis=1)
        scores = [_qk(k_aug, qa_ref[hh]) for hh in heads]
        update([scores[hh][:t] + btab_ref[hh, 0] for hh in heads], [own], first=True)
        update([scores[hh][t:2 * t] + btab_ref[hh, 1] for hh in heads], [prev], first=False)
        for j in range(group):
            update([s[(2 + j) * t:(3 + j) * t] for s in scores], [g0 + j], first=False)

    for count in range(g):
        @pl.when(left == count)
        def _(count=count):
            first_step(count)

    def far_body(i, c):
        g0, keys, rows = far_keys(i * g, g)
        k_aug = jnp.concatenate([keys, jnp.concatenate(rows, axis=0)], axis=1)
        scores = [_qk(k_aug, qa_ref[hh]) for hh in heads]
        for j in range(g):
            update([s[j * t:(j + 1) * t] for s in scores], [g0 + j], first=False)
        return c

    lax.fori_loop(0, n_full, far_body, 0)

    out_t = jnp.concatenate([acc_ref[0][:HEAD_DIM] / acc_ref[0][HEAD_DIM:],
                             acc_ref[1][HEAD_DIM:] / acc_ref[1][:HEAD_DIM]], axis=0)
    o_ref[q_rows, :] = out_t.T


def _moba_attention(u, far_bias, btab, block_lanes, batch, seq):
    t = MOBA_BLOCK
    nb = seq // t
    rows = MOBA_BLOCKS_PER_STEP * t
    nstep = seq // rows
    kernel = functools.partial(_moba_kernel, nb=nb)
    grid_spec = pltpu.PrefetchScalarGridSpec(
        num_scalar_prefetch=1,
        grid=(batch, N_PAIRS, nstep),
        in_specs=[pl.BlockSpec((rows, LANES), lambda b, p, i, far: (b * nstep + i, COL_QB // LANES + p)),
                  pl.BlockSpec((seq, LANES), lambda b, p, i, far: (b, COL_KB // LANES + p)),
                  pl.BlockSpec((seq, LANES), lambda b, p, i, far: (b, COL_VB // LANES + p)),
                  pl.BlockSpec((HEADS_PER_TILE, 2, t, t), lambda b, p, i, far: (p, 0, 0, 0)),
                  pl.BlockSpec(block_lanes.shape, lambda b, p, i, far: (0, 0, 0))],
        out_specs=pl.BlockSpec((rows, LANES), lambda b, p, i, far: (b * nstep + i, p)),
        scratch_shapes=[pltpu.VMEM((MOBA_LO_LANE, LANES), jnp.float32),
                        pltpu.VMEM((HEADS_PER_TILE, nb, LANES, t), jnp.bfloat16),
                        pltpu.VMEM((HEADS_PER_TILE, t, 2 * LANES), jnp.bfloat16),
                        pltpu.VMEM((HEADS_PER_TILE, F32_SUBLANES, t), jnp.float32),
                        pltpu.VMEM((HEADS_PER_TILE, LANES, t), jnp.float32)])
    return pl.pallas_call(
        kernel,
        grid_spec=grid_spec,
        out_shape=jax.ShapeDtypeStruct((batch * seq, WIDTH), jnp.float32),
        compiler_params=pltpu.CompilerParams(dimension_semantics=("parallel", "parallel", "arbitrary"),
                                             vmem_limit_bytes=VMEM_LIMIT),
        name="moba_attention",
    )(far_bias, u, u, u, btab, block_lanes)


SWA_STEP_WINDOWS = 2


def _swa_kernel(sink_ref, q_ref, kp_ref, kc_ref, vp_ref, vc_ref, bias_ref, o_ref):
    w = SWA_WINDOW
    keys = jnp.concatenate([kp_ref[...], kc_ref[...]], axis=0)
    values = jnp.concatenate([vp_ref[...], vc_ref[...]], axis=0)
    slots = [(j, half) for j in range(SWA_GROUP) for half in range(SWA_KV_HEADS)]
    chains = [(win, slot) for win in range(SWA_STEP_WINDOWS) for slot in range(len(slots))]

    def attend(first_step):
        scores = []
        for win, slot in chains:
            j, half = slots[slot]
            q = q_ref[win * w:(win + 1) * w, j * LANES:(j + 1) * LANES]
            s = _qk(q * _head_lane_mask(half, q.dtype), keys[win * w:(win + 2) * w]) + bias_ref[slot]
            if first_step and win == 0:
                s = jnp.where(lax.broadcasted_iota(jnp.int32, (w, 2 * w), 1) >= w, s, _NEG_INF)
            scores.append(s)
        m = [jnp.maximum(jnp.broadcast_to(jnp.max(s, axis=1, keepdims=True), (w, LANES)), sink_ref[slot])
             for (win, slot), s in zip(chains, scores)]
        p = [jnp.exp2(s - jnp.concatenate([mi] * (2 * w // LANES), axis=1)).astype(jnp.bfloat16)
             for mi, s in zip(m, scores)]
        pv = []
        for (win, slot), pi in zip(chains, p):
            vw = values[win * w:(win + 2) * w]
            vw = jnp.where(_head_lane_mask(slots[slot][1], jnp.int32) > 0, vw, jnp.ones_like(vw))
            pv.append(jnp.dot(pi, vw, preferred_element_type=jnp.float32))
        out = [pvi / (pltpu.roll(pvi, HEAD_DIM, axis=1) + jnp.exp2(sink_ref[slot] - mi))
               for (win, slot), pvi, mi in zip(chains, pv, m)]
        for win in range(SWA_STEP_WINDOWS):
            for j in range(SWA_GROUP):
                first = win * len(slots) + j * SWA_KV_HEADS
                o_ref[win * w:(win + 1) * w, j * LANES:(j + 1) * LANES] = _merge_heads(
                    out[first:first + SWA_KV_HEADS])

    @pl.when(pl.program_id(1) == 0)
    def _():
        attend(True)

    @pl.when(pl.program_id(1) > 0)
    def _():
        attend(False)


def _swa_attention(u, sinks_perm, bias, batch, seq):
    w = SWA_WINDOW
    sw = SWA_STEP_WINDOWS
    nstep = seq // (sw * w)
    kcol, vcol = COL_KC // LANES, COL_VC // LANES
    prev_window = lambda b, n, s: b * (seq // w) + jnp.maximum(sw * n - 1, 0)
    grid_spec = pltpu.PrefetchScalarGridSpec(
        num_scalar_prefetch=1,
        grid=(batch, nstep),
        in_specs=[pl.BlockSpec((sw * w, WIDTH), lambda b, n, s: (b * nstep + n, COL_QC // WIDTH)),
                  pl.BlockSpec((w, LANES), lambda b, n, s: (prev_window(b, n, s), kcol)),
                  pl.BlockSpec((sw * w, LANES), lambda b, n, s: (b * nstep + n, kcol)),
                  pl.BlockSpec((w, LANES), lambda b, n, s: (prev_window(b, n, s), vcol)),
                  pl.BlockSpec((sw * w, LANES), lambda b, n, s: (b * nstep + n, vcol)),
                  pl.BlockSpec((N_HEADS, w, 2 * w), lambda b, n, s: (0, 0, 0))],
        out_specs=pl.BlockSpec((sw * w, WIDTH), lambda b, n, s: (b * nstep + n, 0)))
    return pl.pallas_call(
        _swa_kernel,
        grid_spec=grid_spec,
        out_shape=jax.ShapeDtypeStruct((batch * seq, WIDTH), jnp.float32),
        compiler_params=pltpu.CompilerParams(dimension_semantics=("parallel", "arbitrary"),
                                             vmem_limit_bytes=VMEM_LIMIT),
        name="swa_attention",
    )(sinks_perm, u, u, u, u, u, bias)


def _post_kernel(x_ref, ya_ref, yb_ref, yc_ref, ga_ref, gb_ref, gc_ref, ma_ref, mb_ref, mc_ref,
                 wa_ref, wb_ref, wc_ref, wo_ref, fw_ref, o_ref, *, final):
    def branch(y_ref, g_ref, w_ref, m_ref):
        g = g_ref[...].astype(jnp.float32)
        y = (y_ref[...] * (g * jax.nn.sigmoid(g))).astype(jnp.bfloat16)
        proj = jnp.dot(y, w_ref[...], preferred_element_type=jnp.float32)
        return jax.nn.sigmoid(m_ref[...].astype(jnp.float32)) * proj

    merged = (branch(ya_ref, ga_ref, wa_ref, ma_ref) + branch(yb_ref, gb_ref, wb_ref, mb_ref)
              + branch(yc_ref, gc_ref, wc_ref, mc_ref))
    out = x_ref[...] + jnp.dot(merged.astype(jnp.bfloat16), wo_ref[...], preferred_element_type=jnp.float32)
    if final:
        out = out * lax.rsqrt(jnp.mean(out * out, axis=-1, keepdims=True) + RMS_EPS) * fw_ref[...]
    o_ref[...] = out


def _post(x2, ya, yb, yc, u, wa, wb, wc, wo, fw, layer, *, final, tm=512):
    n = x2.shape[0]
    row = lambda i: (i, 0)
    full = lambda i: (0, 0)
    of_layer = lambda i: (layer, 0, 0)
    y_spec = pl.BlockSpec((tm, WIDTH), row)
    in_specs = [pl.BlockSpec((tm, D_MODEL), row), y_spec, y_spec, y_spec,
                pl.BlockSpec((tm, WIDTH), lambda i: (i, COL_GA // WIDTH)),
                pl.BlockSpec((tm, WIDTH), lambda i: (i, COL_GB // WIDTH)),
                pl.BlockSpec((tm, WIDTH), lambda i: (i, COL_GC // WIDTH)),
                pl.BlockSpec((tm, D_MODEL), lambda i: (i, COL_MA // D_MODEL)),
                pl.BlockSpec((tm, D_MODEL), lambda i: (i, COL_MB // D_MODEL)),
                pl.BlockSpec((tm, D_MODEL), lambda i: (i, COL_MC // D_MODEL)),
                pl.BlockSpec((None, WIDTH, D_MODEL), of_layer), pl.BlockSpec((None, WIDTH, D_MODEL), of_layer),
                pl.BlockSpec((None, WIDTH, D_MODEL), of_layer), pl.BlockSpec((None, D_MODEL, D_MODEL), of_layer),
                pl.BlockSpec((1, D_MODEL), full)]
    return pl.pallas_call(
        functools.partial(_post_kernel, final=final),
        grid=(n // tm,),
        in_specs=in_specs,
        out_specs=pl.BlockSpec((tm, D_MODEL), row),
        out_shape=jax.ShapeDtypeStruct((n, D_MODEL), jnp.float32),
        compiler_params=pltpu.CompilerParams(dimension_semantics=("parallel",),
                                             vmem_limit_bytes=VMEM_LIMIT),
        name="post_final" if final else "post",
    )(x2, ya, yb, yc, u, u, u, u, u, u, wa, wb, wc, wo, fw)


def _bucket_lookup(table, dist):
    onehot = (_rel_bucket(dist)[..., None] == jnp.arange(REL_BUCKETS)).astype(jnp.float32)
    return jnp.einsum("...b,bh->h...", onehot, table.astype(jnp.float32), precision=lax.Precision.HIGHEST)


def _bias_tables(rel_bias):
    moba_tab = rel_bias[:, :N_HEADS]
    swa_tab = _swa_head_order(rel_bias[:, N_HEADS:], 1, per_head=1)
    t = MOBA_BLOCK
    d_own = jnp.arange(t)[:, None] - jnp.arange(t)[None, :]
    own = jnp.where(d_own[None] >= 0, _bucket_lookup(moba_tab, d_own), _NEG_INF)
    prev = _bucket_lookup(moba_tab, d_own + t)
    btab = jnp.swapaxes(jnp.stack([own, prev], axis=1), 2, 3) * LOG2_E
    far = _bucket_lookup(moba_tab, jnp.full((1,), t + 1, jnp.int32))[:, 0] * LOG2_E

    w = SWA_WINDOW
    dist = jnp.arange(w)[:, None] + w - jnp.arange(2 * w)[None, :]
    in_band = (dist >= 0) & (dist < w)
    swa = jnp.where(in_band[None], _bucket_lookup(swa_tab, dist) * LOG2_E, _NEG_INF)
    return btab, far, swa


def kernel(x, norm_w, w_in, w_proj_a, w_proj_b, w_proj_c, w_out, sinks, rel_bias, final_norm_w):
    batch, seq, _ = x.shape
    depth = w_in.shape[0]
    assert seq % MOBA_BLOCK == 0 and MOBA_GROUP <= seq // MOBA_BLOCK <= MOBA_NEVER
    btab, far, swa_bias = _bias_tables(rel_bias)
    t = SB_TILE
    tri = (jnp.arange(t)[:, None] > jnp.arange(t)[None, :]).astype(jnp.bfloat16)
    tri = jnp.concatenate([tri, tri], axis=0)
    lane = jnp.arange(LANES)[None, :]
    blk = jnp.arange(MOBA_ZERO_ROW + 1)[:, None]
    block_lanes = ((lane == blk) | (lane == blk + MOBA_LO_LANE)) & (blk < MOBA_ZERO_ROW)
    block_lanes = jnp.broadcast_to(block_lanes[:, None, :],
                                   (MOBA_ZERO_ROW + 1, BF16_SUBLANES, LANES)).astype(jnp.bfloat16)
    sinks_perm = _swa_head_order(sinks.astype(jnp.float32), 1, per_head=1) * LOG2_E

    w_u = _permute_input_columns(w_in)
    norm_w3 = norm_w.reshape(depth, 1, D_MODEL)
    wa = w_proj_a.astype(jnp.bfloat16)
    wb = w_proj_b.astype(jnp.bfloat16)
    wc = _swa_head_order(w_proj_c, 1).astype(jnp.bfloat16)
    wo = w_out.astype(jnp.bfloat16)
    fw = final_norm_w.reshape(1, D_MODEL)

    x2 = x.reshape(batch * seq, D_MODEL)
    for layer in range(depth):
        u = _inproj(x2, norm_w3, w_u, layer)
        ya = _sb_attention(u, tri, batch, seq)
        yb = _moba_attention(u, far, btab, block_lanes, batch, seq)
        yc = _swa_attention(u, sinks_perm[layer], swa_bias, batch, seq)
        x2 = _post(x2, ya, yb, yc, u, wa, wb, wc, wo, fw, layer, final=(layer == depth - 1))
    return x2.reshape(batch, seq, D_MODEL)
```

```python
import functools
import math

import jax
import jax.numpy as jnp
from jax import lax
from jax.experimental import pallas as pl
from jax.experimental.pallas import tpu as pltpu

D_MODEL = 1024
HEAD_DIM = 64
N_HEADS = 8
WIDTH = N_HEADS * HEAD_DIM
MOBA_BLOCK = 256
MOBA_TOPK = 3
SWA_KV_HEADS = 2
SWA_GROUP = N_HEADS // SWA_KV_HEADS
SWA_WINDOW = 128
REL_BUCKETS = 32
REL_MAX_DIST = 128
RMS_EPS = 1e-6

LANES = 128
BF16_SUBLANES = 16
F32_SUBLANES = 8
HEADS_PER_TILE = LANES // HEAD_DIM
N_PAIRS = N_HEADS // HEADS_PER_TILE

COL_QA, COL_KA, COL_VA = 0, 512, 1024
COL_QB, COL_KB, COL_VB = 1536, 2048, 2560
COL_QC = 3072
COL_GA, COL_GB, COL_GC = 3584, 4096, 4608
COL_MA, COL_MB, COL_MC = 5120, 6144, 7168
COL_KC, COL_VC = 8192, 8320
D_IN = 8448

V7X_VMEM_BYTES = 64 * 1024 * 1024
VMEM_LIMIT = V7X_VMEM_BYTES * 3 // 4

LOG2_E = math.log2(math.e)

SB_DEAD_LOG_WEIGHT = -120.0
SB_DEAD_LOG2_WEIGHT = SB_DEAD_LOG_WEIGHT * LOG2_E

_NEG_INF = float("-inf")


def _rel_bucket(dist):
    max_exact = REL_BUCKETS // 2
    n = jnp.maximum(dist, 0)
    nf = jnp.maximum(n, 1).astype(jnp.float32)
    large = max_exact + (jnp.log(nf / max_exact) / math.log(REL_MAX_DIST / max_exact)
                         * (REL_BUCKETS - max_exact)).astype(jnp.int32)
    large = jnp.minimum(large, REL_BUCKETS - 1)
    return jnp.where(n < max_exact, n, large)


def _swa_head_order(t, axis, per_head=HEAD_DIM):
    axis = axis % t.ndim
    shape = t.shape
    t = t.reshape(shape[:axis] + (SWA_KV_HEADS, SWA_GROUP, per_head) + shape[axis + 1:])
    return jnp.swapaxes(t, axis, axis + 1).reshape(shape)


def _permute_input_columns(w):
    old = {"qa": 0, "ka": 512, "va": 1024, "ga": 1536, "qb": 2048, "kb": 2560, "vb": 3072, "gb": 3584,
           "qc": 4096, "kc": 4608, "vc": 4736, "gc": 4864, "ma": 5376, "mb": 6400, "mc": 7424}
    q_scale = HEAD_DIM ** -0.5 * LOG2_E

    def cols(name, width, scale=None):
        part = w[..., old[name]:old[name] + width]
        return part if scale is None else part * scale

    parts = [cols("qa", 512, q_scale), cols("ka", 512), cols("va", 512),
             cols("qb", 512, q_scale), cols("kb", 512), cols("vb", 512),
             _swa_head_order(cols("qc", WIDTH, q_scale), -1),
             cols("ga", 512), cols("gb", 512), _swa_head_order(cols("gc", WIDTH), -1),
             cols("ma", 1024), cols("mb", 1024), cols("mc", 1024),
             cols("kc", 128), cols("vc", 128)]
    return jnp.concatenate(parts, axis=-1).astype(jnp.bfloat16)


def _inproj_kernel(x_ref, nw_ref, w_ref, u_ref, h_ref):
    @pl.when(pl.program_id(1) == 0)
    def _():
        x = x_ref[...]
        y = x * lax.rsqrt(jnp.mean(x * x, axis=-1, keepdims=True) + RMS_EPS)
        h_ref[...] = (y * nw_ref[...]).astype(h_ref.dtype)

    u_ref[...] = jnp.dot(h_ref[...], w_ref[...], preferred_element_type=jnp.float32).astype(u_ref.dtype)


def _inproj(x2, norm_w, w, layer, *, tm=1024, tn=2816):
    n = x2.shape[0]
    return pl.pallas_call(
        _inproj_kernel,
        grid=(n // tm, D_IN // tn),
        in_specs=[pl.BlockSpec((tm, D_MODEL), lambda i, j: (i, 0)),
                  pl.BlockSpec((None, 1, D_MODEL), lambda i, j: (layer, 0, 0)),
                  pl.BlockSpec((None, D_MODEL, tn), lambda i, j: (layer, 0, j))],
        out_specs=pl.BlockSpec((tm, tn), lambda i, j: (i, j)),
        out_shape=jax.ShapeDtypeStruct((n, D_IN), jnp.bfloat16),
        scratch_shapes=[pltpu.VMEM((tm, D_MODEL), jnp.bfloat16)],
        compiler_params=pltpu.CompilerParams(dimension_semantics=("parallel", "arbitrary"),
                                             vmem_limit_bytes=VMEM_LIMIT),
        name="inproj",
    )(x2, norm_w, w)


def _head_lane_mask(hh, dtype):
    lane = lax.broadcasted_iota(jnp.int32, (1, LANES), 1)
    return ((lane // HEAD_DIM) == hh).astype(dtype)


def _qk(qm, k):
    return lax.dot_general(qm, k, (((1,), (1,)), ((), ())), preferred_element_type=jnp.float32)


def _merge_heads(parts):
    lane = lax.broadcasted_iota(jnp.int32, parts[0].shape, 1)
    return jnp.where(lane < HEAD_DIM, parts[0], parts[1])


SB_TILE = 256
SB_TILES_PER_STEP = 4
SB_ROW_CHUNK = 256


def _sb_kernel(q_ref, k_ref, v_ref, tri_ref, o_ref, acc_ref, carry_ref):
    t = SB_TILE

    def step(row0, newest, n_blocks, diagonal):
        tri = tri_ref[...]
        rc = SB_ROW_CHUNK
        chains = [(hh, r) for hh in range(HEADS_PER_TILE) for r in range(t // rc)]
        starts = [pl.multiple_of((newest - j) * t, t) for j in range(n_blocks)]

        terms = {}
        for hh, r in chains:
            qm = q_ref[pl.ds(row0 + r * rc, rc), :] * _head_lane_mask(hh, q_ref.dtype)
            for j in range(n_blocks):
                z = _qk(qm, k_ref[pl.ds(starts[j], t), :])
                neg_abs = lax.bitcast_convert_type(
                    lax.bitcast_convert_type(z, jnp.uint32) | jnp.uint32(0x80000000), jnp.float32)
                soft = jnp.log(1.0 + jnp.exp2(neg_abs)) * LOG2_E
                log_beta = jnp.minimum(z, 0.0) - soft
                log_1m = log_beta - z
                mask = None
                if diagonal and j == 0:
                    row = lax.broadcasted_iota(jnp.int32, (rc, t), 0) + r * rc
                    mask = lax.broadcasted_iota(jnp.int32, (rc, t), 1) < row
                    log_1m = jnp.where(mask, log_1m, 0.0)
                hi = lax.bitcast_convert_type(
                    lax.bitcast_convert_type(log_1m, jnp.uint32) & jnp.uint32(0xFFFF0000), jnp.float32)
                split = jnp.concatenate([hi.astype(jnp.bfloat16), (log_1m - hi).astype(jnp.bfloat16)], axis=1)
                terms[hh, r, j] = (log_beta, log_1m[:, 0:1], split, mask)

        cums = {key: jnp.dot(split, tri, preferred_element_type=jnp.float32)
                for key, (_, _, split, _) in terms.items()}

        alive = None
        for hh, r in chains:
            rows = slice(r * rc, (r + 1) * rc)
            carry = carry_ref[hh, rows, :]
            acc = acc_ref[hh, rows, :]
            for j in range(n_blocks):
                log_beta, first_term, _, mask = terms[hh, r, j]
                cum = cums[hh, r, j]
                w = jnp.exp2(log_beta + cum + jnp.concatenate([carry] * (t // LANES), axis=1))
                if mask is not None:
                    w = jnp.where(mask, w, 0.0)
                acc = acc + jnp.dot(w.astype(jnp.bfloat16), v_ref[pl.ds(starts[j], t), :],
                                    preferred_element_type=jnp.float32)
                carry = carry + jnp.broadcast_to(cum[:, 0:1] + first_term, (rc, LANES))
            acc_ref[hh, rows, :] = acc
            carry_ref[hh, rows, :] = carry
            alive = carry if alive is None else jnp.maximum(alive, carry)
        return (jnp.max(alive) < SB_DEAD_LOG2_WEIGHT).astype(jnp.int32)

    def query_tile(sub, c):
        qi = pl.program_id(2) * SB_TILES_PER_STEP + sub
        row0 = pl.multiple_of(sub * t, t)
        acc_ref[...] = jnp.zeros_like(acc_ref)
        carry_ref[...] = jnp.zeros_like(carry_ref)
        dead = lax.cond(qi >= 1, lambda: step(row0, qi, 2, True), lambda: step(row0, qi, 1, True))

        def cond(c):
            newest, dead = c
            return jnp.logical_and(newest >= 1, dead == 0)

        def body(c):
            newest, _ = c
            return newest - 2, step(row0, newest, 2, False)

        newest, dead = lax.while_loop(cond, body, (qi - 2, dead))

        @pl.when(jnp.logical_and(newest == 0, dead == 0))
        def _():
            step(row0, 0, 1, False)

        o_ref[pl.ds(row0, t), :] = _merge_heads([acc_ref[0], acc_ref[1]])
        return c

    lax.fori_loop(0, SB_TILES_PER_STEP, query_tile, 0)


def _sb_attention(u, tri, batch, seq):
    t = SB_TILE
    rows = SB_TILES_PER_STEP * t
    nq = seq // rows
    return pl.pallas_call(
        _sb_kernel,
        grid=(batch, N_PAIRS, nq),
        in_specs=[pl.BlockSpec((rows, LANES), lambda b, p, i: (b * nq + i, COL_QA // LANES + p)),
                  pl.BlockSpec((seq, LANES), lambda b, p, i: (b, COL_KA // LANES + p)),
                  pl.BlockSpec((seq, LANES), lambda b, p, i: (b, COL_VA // LANES + p)),
                  pl.BlockSpec((2 * t, t), lambda b, p, i: (0, 0))],
        out_specs=pl.BlockSpec((rows, LANES), lambda b, p, i: (b * nq + i, p)),
        out_shape=jax.ShapeDtypeStruct((batch * seq, WIDTH), jnp.float32),
        scratch_shapes=[pltpu.VMEM((HEADS_PER_TILE, t, LANES), jnp.float32),
                        pltpu.VMEM((HEADS_PER_TILE, t, LANES), jnp.float32)],
        compiler_params=pltpu.CompilerParams(dimension_semantics=("parallel", "parallel", "arbitrary"),
                                             vmem_limit_bytes=VMEM_LIMIT),
        name="sb_attention",
    )(u, u, u, tri)


MOBA_LO_LANE = 64
MOBA_NEVER = 63
MOBA_ZERO_ROW = 64
MOBA_MASKED = -1e30
MOBA_GROUP = 8
MOBA_BLOCKS_PER_STEP = 4


def _moba_kernel(far_ref, q_ref, k_ref, v_ref, btab_ref, oh_ref, o_ref,
                 kmean_ref, vt_ref, qa_ref, m_ref, acc_ref, *, nb):
    t = MOBA_BLOCK
    nsub = MOBA_BLOCKS_PER_STEP
    nh = HEADS_PER_TILE
    heads = range(nh)
    pair = pl.program_id(1)

    @pl.when(pl.program_id(2) == 0)
    def _():
        kmean_ref[...] = jnp.zeros_like(kmean_ref)
        kf = k_ref[...].astype(jnp.float32).reshape(nb, t, LANES)
        kmean_ref[0:nb, :] = jnp.mean(kf, axis=1)

        head_of_row = lax.broadcasted_iota(jnp.int32, (LANES, t), 0) // HEAD_DIM

        def transpose_block(b, c):
            vb = v_ref[pl.ds(pl.multiple_of(b * t, t), t), :].astype(jnp.float32).T
            for hh in heads:
                vt_ref[hh, b] = jnp.where(head_of_row == hh, vb, 1.0).astype(jnp.bfloat16)
            return c

        lax.fori_loop(0, nb, transpose_block, 0)

    km = kmean_ref[...]
    km0 = km.astype(jnp.bfloat16)
    r1 = km - km0.astype(jnp.float32)
    km1 = r1.astype(jnp.bfloat16)
    km2 = (r1 - km1.astype(jnp.float32)).astype(jnp.bfloat16)

    cols = nsub * t
    blk = lax.broadcasted_iota(jnp.int32, (MOBA_LO_LANE, nh * cols), 0)
    blk_f = blk.astype(jnp.float32)
    col = lax.broadcasted_iota(jnp.int32, (1, nh * cols), 1)
    own_col = pl.program_id(2) * nsub + (col % cols) // t
    gate_rows = sum(_qk(jnp.concatenate([part * _head_lane_mask(hh, part.dtype) for hh in heads], axis=0),
                        q_ref[...]) for part in (km0, km1, km2))
    gate = jnp.concatenate([gate_rows[hh * MOBA_LO_LANE:(hh + 1) * MOBA_LO_LANE] for hh in heads], axis=1)
    gate = jnp.where(blk < own_col, gate, _NEG_INF)
    sel = jnp.zeros_like(gate)
    for _ in range(MOBA_TOPK):
        best = jnp.max(gate, axis=0, keepdims=True)
        idx = jnp.min(jnp.where(gate == best, blk_f, float(LANES)), axis=0, keepdims=True)
        hit = blk_f == idx
        sel = jnp.where(jnp.logical_and(hit, best > _NEG_INF), 1.0, sel)
        gate = jnp.where(hit, _NEG_INF, gate)

    far_bias = jnp.where(col // cols == 0, far_ref[pair * nh], far_ref[pair * nh + 1])
    term = jnp.where(sel > 0.0, jnp.where(blk < own_col - 1, far_bias, 0.0), MOBA_MASKED)
    hi = term.astype(jnp.bfloat16).astype(jnp.float32)
    lo = term - hi
    for hh in heads:
        for sub in range(nsub):
            cs = slice(hh * cols + sub * t, hh * cols + (sub + 1) * t)
            inj = jnp.concatenate([hi[:, cs], lo[:, cs]], axis=0).T
            q_sub = q_ref[sub * t:(sub + 1) * t, :] * _head_lane_mask(hh, q_ref.dtype)
            qa_ref[hh, sub] = jnp.concatenate([q_sub, inj.astype(jnp.bfloat16)], axis=1)

    def query_block(sub, c):
        _moba_query_block(sub, k_ref, btab_ref, oh_ref, o_ref, vt_ref, qa_ref, m_ref, acc_ref, nb=nb)
        return c

    lax.fori_loop(0, nsub, query_block, 0)


def _moba_query_block(sub, k_ref, btab_ref, oh_ref, o_ref, vt_ref, qa_ref, m_ref, acc_ref, *, nb):
    t = MOBA_BLOCK
    g = MOBA_GROUP
    own = pl.program_id(2) * MOBA_BLOCKS_PER_STEP + sub
    q_rows = pl.ds(pl.multiple_of(sub * t, t), t)
    heads = range(HEADS_PER_TILE)

    def update(scores, blocks, first):
        col_max = [jnp.max(s, axis=0, keepdims=True) for s in scores]
        if first:
            m_new = col_max
        else:
            m_old = [m_ref[hh][0:1, :] for hh in heads]
            m_new = [jnp.maximum(m_old[hh], col_max[hh]) for hh in heads]
            alpha = [jnp.exp2(m_old[hh] - m_new[hh]) for hh in heads]
        p = [jnp.exp2(scores[hh] - m_new[hh]).astype(jnp.bfloat16) for hh in heads]
        pv = [jnp.dot(jnp.concatenate([vt_ref[hh, b] for b in blocks], axis=1), p[hh],
                      preferred_element_type=jnp.float32) for hh in heads]
        for hh in heads:
            acc_ref[hh] = pv[hh] if first else alpha[hh] * acc_ref[hh] + pv[hh]
            m_ref[hh] = jnp.broadcast_to(m_new[hh], m_ref.shape[1:])

    def block_lanes(row):
        return jnp.concatenate([oh_ref[row]] * (t // BF16_SUBLANES), axis=0)

    prev = jnp.maximum(own - 1, 0)
    n_far = jnp.maximum(own - 1, 0)
    n_full = n_far // g
    left = n_far - n_full * g

    def far_keys(first_blk, group):
        g0 = jnp.minimum(first_blk, nb - group)
        rows = []
        for j in range(group):
            b = g0 + j
            active = jnp.logical_and(b >= first_blk, b < n_far)
            rows.append(block_lanes(jnp.where(active, b, MOBA_NEVER)))
        keys = k_ref[pl.ds(pl.multiple_of(g0 * t, t), group * t), :]
        return g0, keys, rows

    def first_step(group):
        keys = [k_ref[pl.ds(pl.multiple_of(own * t, t), t), :], k_ref[pl.ds(pl.multiple_of(prev * t, t), t), :]]
        rows = [block_lanes(MOBA_ZERO_ROW), block_lanes(jnp.where(own >= 1, prev, MOBA_NEVER))]
        if group:
            g0, far, far_rows = far_keys(n_full * g, group)
            keys.append(far)
            rows.extend(far_rows)
        k_aug = jnp.concatenate([jnp.concatenate(keys, axis=0), jnp.concatenate(rows, axis=0)], axis=1)
        scores = [_qk(k_aug, qa_ref[hh, sub]) for hh in heads]
        update([scores[hh][:t] + btab_ref[hh, 0] for hh in heads], [own], first=True)
        update([scores[hh][t:2 * t] + btab_ref[hh, 1] for hh in heads], [prev], first=False)
        for j in range(group):
            update([s[(2 + j) * t:(3 + j) * t] for s in scores], [g0 + j], first=False)

    for count in range(g):
        @pl.when(left == count)
        def _(count=count):
            first_step(count)

    def far_body(i, c):
        g0, keys, rows = far_keys(i * g, g)
        k_aug = jnp.concatenate([keys, jnp.concatenate(rows, axis=0)], axis=1)
        scores = [_qk(k_aug, qa_ref[hh, sub]) for hh in heads]
        for j in range(g):
            update([s[j * t:(j + 1) * t] for s in scores], [g0 + j], first=False)
        return c

    lax.fori_loop(0, n_full, far_body, 0)

    out_t = jnp.concatenate([acc_ref[0][:HEAD_DIM] / acc_ref[0][HEAD_DIM:],
                             acc_ref[1][HEAD_DIM:] / acc_ref[1][:HEAD_DIM]], axis=0)
    o_ref[q_rows, :] = out_t.T


def _moba_attention(u, far_bias, btab, block_lanes, batch, seq):
    t = MOBA_BLOCK
    nb = seq // t
    rows = MOBA_BLOCKS_PER_STEP * t
    nstep = seq // rows
    kernel = functools.partial(_moba_kernel, nb=nb)
    grid_spec = pltpu.PrefetchScalarGridSpec(
        num_scalar_prefetch=1,
        grid=(batch, N_PAIRS, nstep),
        in_specs=[pl.BlockSpec((rows, LANES), lambda b, p, i, far: (b * nstep + i, COL_QB // LANES + p)),
                  pl.BlockSpec((seq, LANES), lambda b, p, i, far: (b, COL_KB // LANES + p)),
                  pl.BlockSpec((seq, LANES), lambda b, p, i, far: (b, COL_VB // LANES + p)),
                  pl.BlockSpec((HEADS_PER_TILE, 2, t, t), lambda b, p, i, far: (p, 0, 0, 0)),
                  pl.BlockSpec(block_lanes.shape, lambda b, p, i, far: (0, 0, 0))],
        out_specs=pl.BlockSpec((rows, LANES), lambda b, p, i, far: (b * nstep + i, p)),
        scratch_shapes=[pltpu.VMEM((MOBA_LO_LANE, LANES), jnp.float32),
                        pltpu.VMEM((HEADS_PER_TILE, nb, LANES, t), jnp.bfloat16),
                        pltpu.VMEM((HEADS_PER_TILE, MOBA_BLOCKS_PER_STEP, t, 2 * LANES), jnp.bfloat16),
                        pltpu.VMEM((HEADS_PER_TILE, F32_SUBLANES, t), jnp.float32),
                        pltpu.VMEM((HEADS_PER_TILE, LANES, t), jnp.float32)])
    return pl.pallas_call(
        kernel,
        grid_spec=grid_spec,
        out_shape=jax.ShapeDtypeStruct((batch * seq, WIDTH), jnp.float32),
        compiler_params=pltpu.CompilerParams(dimension_semantics=("parallel", "parallel", "arbitrary"),
                                             vmem_limit_bytes=VMEM_LIMIT),
        name="moba_attention",
    )(far_bias, u, u, u, btab, block_lanes)


SWA_STEP_WINDOWS = 2


def _swa_kernel(sink_ref, q_ref, kp_ref, kc_ref, vp_ref, vc_ref, bias_ref, o_ref):
    w = SWA_WINDOW
    keys = jnp.concatenate([kp_ref[...], kc_ref[...]], axis=0)
    values = jnp.concatenate([vp_ref[...], vc_ref[...]], axis=0)
    slots = [(j, half) for j in range(SWA_GROUP) for half in range(SWA_KV_HEADS)]
    chains = [(win, slot) for win in range(SWA_STEP_WINDOWS) for slot in range(len(slots))]

    def attend(first_step):
        scores = []
        for win, slot in chains:
            j, half = slots[slot]
            q = q_ref[win * w:(win + 1) * w, j * LANES:(j + 1) * LANES]
            s = _qk(q * _head_lane_mask(half, q.dtype), keys[win * w:(win + 2) * w]) + bias_ref[slot]
            if first_step and win == 0:
                s = jnp.where(lax.broadcasted_iota(jnp.int32, (w, 2 * w), 1) >= w, s, _NEG_INF)
            scores.append(s)
        m = [jnp.maximum(jnp.broadcast_to(jnp.max(s, axis=1, keepdims=True), (w, LANES)), sink_ref[slot])
             for (win, slot), s in zip(chains, scores)]
        p = [jnp.exp2(s - jnp.concatenate([mi] * (2 * w // LANES), axis=1)).astype(jnp.bfloat16)
             for mi, s in zip(m, scores)]
        pv = []
        for (win, slot), pi in zip(chains, p):
            vw = values[win * w:(win + 2) * w]
            vw = jnp.where(_head_lane_mask(slots[slot][1], jnp.int32) > 0, vw, jnp.ones_like(vw))
            pv.append(jnp.dot(pi, vw, preferred_element_type=jnp.float32))
        out = [pvi / (pltpu.roll(pvi, HEAD_DIM, axis=1) + jnp.exp2(sink_ref[slot] - mi))
               for (win, slot), pvi, mi in zip(chains, pv, m)]
        for win in range(SWA_STEP_WINDOWS):
            for j in range(SWA_GROUP):
                first = win * len(slots) + j * SWA_KV_HEADS
                o_ref[win * w:(win + 1) * w, j * LANES:(j + 1) * LANES] = _merge_heads(
                    out[first:first + SWA_KV_HEADS])

    @pl.when(pl.program_id(1) == 0)
    def _():
        attend(True)

    @pl.when(pl.program_id(1) > 0)
    def _():
        attend(False)


def _swa_attention(u, sinks_perm, bias, batch, seq):
    w = SWA_WINDOW
    sw = SWA_STEP_WINDOWS
    nstep = seq // (sw * w)
    kcol, vcol = COL_KC // LANES, COL_VC // LANES
    prev_window = lambda b, n, s: b * (seq // w) + jnp.maximum(sw * n - 1, 0)
    grid_spec = pltpu.PrefetchScalarGridSpec(
        num_scalar_prefetch=1,
        grid=(batch, nstep),
        in_specs=[pl.BlockSpec((sw * w, WIDTH), lambda b, n, s: (b * nstep + n, COL_QC // WIDTH)),
                  pl.BlockSpec((w, LANES), lambda b, n, s: (prev_window(b, n, s), kcol)),
                  pl.BlockSpec((sw * w, LANES), lambda b, n, s: (b * nstep + n, kcol)),
                  pl.BlockSpec((w, LANES), lambda b, n, s: (prev_window(b, n, s), vcol)),
                  pl.BlockSpec((sw * w, LANES), lambda b, n, s: (b * nstep + n, vcol)),
                  pl.BlockSpec((N_HEADS, w, 2 * w), lambda b, n, s: (0, 0, 0))],
        out_specs=pl.BlockSpec((sw * w, WIDTH), lambda b, n, s: (b * nstep + n, 0)))
    return pl.pallas_call(
        _swa_kernel,
        grid_spec=grid_spec,
        out_shape=jax.ShapeDtypeStruct((batch * seq, WIDTH), jnp.float32),
        compiler_params=pltpu.CompilerParams(dimension_semantics=("parallel", "arbitrary"),
                                             vmem_limit_bytes=VMEM_LIMIT),
        name="swa_attention",
    )(sinks_perm, u, u, u, u, u, bias)


def _post_kernel(x_ref, ya_ref, yb_ref, yc_ref, ga_ref, gb_ref, gc_ref, ma_ref, mb_ref, mc_ref,
                 wa_ref, wb_ref, wc_ref, wo_ref, fw_ref, o_ref, *, final):
    def branch(y_ref, g_ref, w_ref, m_ref):
        g = g_ref[...].astype(jnp.float32)
        y = (y_ref[...] * (g * jax.nn.sigmoid(g))).astype(jnp.bfloat16)
        proj = jnp.dot(y, w_ref[...], preferred_element_type=jnp.float32)
        return jax.nn.sigmoid(m_ref[...].astype(jnp.float32)) * proj

    merged = (branch(ya_ref, ga_ref, wa_ref, ma_ref) + branch(yb_ref, gb_ref, wb_ref, mb_ref)
              + branch(yc_ref, gc_ref, wc_ref, mc_ref))
    out = x_ref[...] + jnp.dot(merged.astype(jnp.bfloat16), wo_ref[...], preferred_element_type=jnp.float32)
    if final:
        out = out * lax.rsqrt(jnp.mean(out * out, axis=-1, keepdims=True) + RMS_EPS) * fw_ref[...]
    o_ref[...] = out


def _post(x2, ya, yb, yc, u, wa, wb, wc, wo, fw, layer, *, final, tm=512):
    n = x2.shape[0]
    row = lambda i: (i, 0)
    full = lambda i: (0, 0)
    of_layer = lambda i: (layer, 0, 0)
    y_spec = pl.BlockSpec((tm, WIDTH), row)
    in_specs = [pl.BlockSpec((tm, D_MODEL), row), y_spec, y_spec, y_spec,
                pl.BlockSpec((tm, WIDTH), lambda i: (i, COL_GA // WIDTH)),
                pl.BlockSpec((tm, WIDTH), lambda i: (i, COL_GB // WIDTH)),
                pl.BlockSpec((tm, WIDTH), lambda i: (i, COL_GC // WIDTH)),
                pl.BlockSpec((tm, D_MODEL), lambda i: (i, COL_MA // D_MODEL)),
                pl.BlockSpec((tm, D_MODEL), lambda i: (i, COL_MB // D_MODEL)),
                pl.BlockSpec((tm, D_MODEL), lambda i: (i, COL_MC // D_MODEL)),
                pl.BlockSpec((None, WIDTH, D_MODEL), of_layer), pl.BlockSpec((None, WIDTH, D_MODEL), of_layer),
                pl.BlockSpec((None, WIDTH, D_MODEL), of_layer), pl.BlockSpec((None, D_MODEL, D_MODEL), of_layer),
                pl.BlockSpec((1, D_MODEL), full)]
    return pl.pallas_call(
        functools.partial(_post_kernel, final=final),
        grid=(n // tm,),
        in_specs=in_specs,
        out_specs=pl.BlockSpec((tm, D_MODEL), row),
        out_shape=jax.ShapeDtypeStruct((n, D_MODEL), jnp.float32),
        compiler_params=pltpu.CompilerParams(dimension_semantics=("parallel",),
                                             vmem_limit_bytes=VMEM_LIMIT),
        name="post_final" if final else "post",
    )(x2, ya, yb, yc, u, u, u, u, u, u, wa, wb, wc, wo, fw)


def _bucket_lookup(table, dist):
    onehot = (_rel_bucket(dist)[..., None] == jnp.arange(REL_BUCKETS)).astype(jnp.float32)
    return jnp.einsum("...b,bh->h...", onehot, table.astype(jnp.float32), precision=lax.Precision.HIGHEST)


def _bias_tables(rel_bias):
    moba_tab = rel_bias[:, :N_HEADS]
    swa_tab = _swa_head_order(rel_bias[:, N_HEADS:], 1, per_head=1)
    t = MOBA_BLOCK
    d_own = jnp.arange(t)[:, None] - jnp.arange(t)[None, :]
    own = jnp.where(d_own[None] >= 0, _bucket_lookup(moba_tab, d_own), _NEG_INF)
    prev = _bucket_lookup(moba_tab, d_own + t)
    btab = jnp.swapaxes(jnp.stack([own, prev], axis=1), 2, 3) * LOG2_E
    far = _bucket_lookup(moba_tab, jnp.full((1,), t + 1, jnp.int32))[:, 0] * LOG2_E

    w = SWA_WINDOW
    dist = jnp.arange(w)[:, None] + w - jnp.arange(2 * w)[None, :]
    in_band = (dist >= 0) & (dist < w)
    swa = jnp.where(in_band[None], _bucket_lookup(swa_tab, dist) * LOG2_E, _NEG_INF)
    return btab, far, swa


def kernel(x, norm_w, w_in, w_proj_a, w_proj_b, w_proj_c, w_out, sinks, rel_bias, final_norm_w):
    batch, seq, _ = x.shape
    depth = w_in.shape[0]
    assert seq % MOBA_BLOCK == 0 and MOBA_GROUP <= seq // MOBA_BLOCK <= MOBA_NEVER
    btab, far, swa_bias = _bias_tables(rel_bias)
    t = SB_TILE
    tri = (jnp.arange(t)[:, None] > jnp.arange(t)[None, :]).astype(jnp.bfloat16)
    tri = jnp.concatenate([tri, tri], axis=0)
    lane = jnp.arange(LANES)[None, :]
    blk = jnp.arange(MOBA_ZERO_ROW + 1)[:, None]
    block_lanes = ((lane == blk) | (lane == blk + MOBA_LO_LANE)) & (blk < MOBA_ZERO_ROW)
    block_lanes = jnp.broadcast_to(block_lanes[:, None, :],
                                   (MOBA_ZERO_ROW + 1, BF16_SUBLANES, LANES)).astype(jnp.bfloat16)
    sinks_perm = _swa_head_order(sinks.astype(jnp.float32), 1, per_head=1) * LOG2_E

    w_u = _permute_input_columns(w_in)
    norm_w3 = norm_w.reshape(depth, 1, D_MODEL)
    wa = w_proj_a.astype(jnp.bfloat16)
    wb = w_proj_b.astype(jnp.bfloat16)
    wc = _swa_head_order(w_proj_c, 1).astype(jnp.bfloat16)
    wo = w_out.astype(jnp.bfloat16)
    fw = final_norm_w.reshape(1, D_MODEL)

    x2 = x.reshape(batch * seq, D_MODEL)
    for layer in range(depth):
        u = _inproj(x2, norm_w3, w_u, layer)
        ya = _sb_attention(u, tri, batch, seq)
        yb = _moba_attention(u, far, btab, block_lanes, batch, seq)
        yc = _swa_attention(u, sinks_perm[layer], swa_bias, batch, seq)
        x2 = _post(x2, ya, yb, yc, u, wa, wb, wc, wo, fw, layer, final=(layer == depth - 1))
    return x2.reshape(batch, seq, D_MODEL)
```

```python
import functools
import math

import jax
import jax.numpy as jnp
from jax import lax
from jax.experimental import pallas as pl
from jax.experimental.pallas import tpu as pltpu

D_MODEL = 1024
HEAD_DIM = 64
N_HEADS = 8
WIDTH = N_HEADS * HEAD_DIM
MOBA_BLOCK = 256
MOBA_TOPK = 3
SWA_KV_HEADS = 2
SWA_GROUP = N_HEADS // SWA_KV_HEADS
SWA_WINDOW = 128
REL_BUCKETS = 32
REL_MAX_DIST = 128
RMS_EPS = 1e-6

LANES = 128
BF16_SUBLANES = 16
F32_SUBLANES = 8
HEADS_PER_TILE = LANES // HEAD_DIM
N_PAIRS = N_HEADS // HEADS_PER_TILE

COL_QA, COL_KA, COL_VA = 0, 512, 1024
COL_QB, COL_KB, COL_VB = 1536, 2048, 2560
COL_QC = 3072
COL_GA, COL_GB, COL_GC = 3584, 4096, 4608
COL_MA, COL_MB, COL_MC = 5120, 6144, 7168
COL_KC, COL_VC = 8192, 8320
D_IN = 8448

V7X_VMEM_BYTES = 64 * 1024 * 1024
VMEM_LIMIT = V7X_VMEM_BYTES * 3 // 4

LOG2_E = math.log2(math.e)

SB_DEAD_LOG_WEIGHT = -120.0
SB_DEAD_LOG2_WEIGHT = SB_DEAD_LOG_WEIGHT * LOG2_E

_NEG_INF = float("-inf")


def _rel_bucket(dist):
    max_exact = REL_BUCKETS // 2
    n = jnp.maximum(dist, 0)
    nf = jnp.maximum(n, 1).astype(jnp.float32)
    large = max_exact + (jnp.log(nf / max_exact) / math.log(REL_MAX_DIST / max_exact)
                         * (REL_BUCKETS - max_exact)).astype(jnp.int32)
    large = jnp.minimum(large, REL_BUCKETS - 1)
    return jnp.where(n < max_exact, n, large)


def _swa_head_order(t, axis, per_head=HEAD_DIM):
    axis = axis % t.ndim
    shape = t.shape
    t = t.reshape(shape[:axis] + (SWA_KV_HEADS, SWA_GROUP, per_head) + shape[axis + 1:])
    return jnp.swapaxes(t, axis, axis + 1).reshape(shape)


def _permute_input_columns(w):
    old = {"qa": 0, "ka": 512, "va": 1024, "ga": 1536, "qb": 2048, "kb": 2560, "vb": 3072, "gb": 3584,
           "qc": 4096, "kc": 4608, "vc": 4736, "gc": 4864, "ma": 5376, "mb": 6400, "mc": 7424}
    q_scale = HEAD_DIM ** -0.5 * LOG2_E

    def cols(name, width, scale=None):
        part = w[..., old[name]:old[name] + width]
        return part if scale is None else part * scale

    parts = [cols("qa", 512, q_scale), cols("ka", 512), cols("va", 512),
             cols("qb", 512, q_scale), cols("kb", 512), cols("vb", 512),
             _swa_head_order(cols("qc", WIDTH, q_scale), -1),
             cols("ga", 512), cols("gb", 512), _swa_head_order(cols("gc", WIDTH), -1),
             cols("ma", 1024), cols("mb", 1024), cols("mc", 1024),
             cols("kc", 128), cols("vc", 128)]
    return jnp.concatenate(parts, axis=-1).astype(jnp.bfloat16)


def _inproj_kernel(x_ref, nw_ref, w_ref, u_ref, h_ref):
    @pl.when(pl.program_id(1) == 0)
    def _():
        x = x_ref[...]
        y = x * lax.rsqrt(jnp.mean(x * x, axis=-1, keepdims=True) + RMS_EPS)
        h_ref[...] = (y * nw_ref[...]).astype(h_ref.dtype)

    u_ref[...] = jnp.dot(h_ref[...], w_ref[...], preferred_element_type=jnp.float32).astype(u_ref.dtype)


def _inproj(x2, norm_w, w, layer, *, tm=1024, tn=2816):
    n = x2.shape[0]
    return pl.pallas_call(
        _inproj_kernel,
        grid=(n // tm, D_IN // tn),
        in_specs=[pl.BlockSpec((tm, D_MODEL), lambda i, j: (i, 0)),
                  pl.BlockSpec((None, 1, D_MODEL), lambda i, j: (layer, 0, 0)),
                  pl.BlockSpec((None, D_MODEL, tn), lambda i, j: (layer, 0, j))],
        out_specs=pl.BlockSpec((tm, tn), lambda i, j: (i, j)),
        out_shape=jax.ShapeDtypeStruct((n, D_IN), jnp.bfloat16),
        scratch_shapes=[pltpu.VMEM((tm, D_MODEL), jnp.bfloat16)],
        compiler_params=pltpu.CompilerParams(dimension_semantics=("parallel", "arbitrary"),
                                             vmem_limit_bytes=VMEM_LIMIT),
        name="inproj",
    )(x2, norm_w, w)


def _head_lane_mask(hh, dtype):
    lane = lax.broadcasted_iota(jnp.int32, (1, LANES), 1)
    return ((lane // HEAD_DIM) == hh).astype(dtype)


def _qk(qm, k):
    return lax.dot_general(qm, k, (((1,), (1,)), ((), ())), preferred_element_type=jnp.float32)


def _merge_heads(parts):
    lane = lax.broadcasted_iota(jnp.int32, parts[0].shape, 1)
    return jnp.where(lane < HEAD_DIM, parts[0], parts[1])


SB_TILE = 256
SB_TILES_PER_STEP = 4
SB_ROW_CHUNK = 256


def _sb_kernel(q_ref, k_ref, v_ref, tri_ref, o_ref, acc_ref, carry_ref):
    t = SB_TILE

    def step(row0, newest, n_blocks, diagonal):
        tri = tri_ref[...]
        rc = SB_ROW_CHUNK
        chains = [(hh, r) for hh in range(HEADS_PER_TILE) for r in range(t // rc)]
        starts = [pl.multiple_of((newest - j) * t, t) for j in range(n_blocks)]

        terms = {}
        for hh, r in chains:
            qm = q_ref[pl.ds(row0 + r * rc, rc), :] * _head_lane_mask(hh, q_ref.dtype)
            for j in range(n_blocks):
                z = _qk(qm, k_ref[pl.ds(starts[j], t), :])
                neg_abs = lax.bitcast_convert_type(
                    lax.bitcast_convert_type(z, jnp.uint32) | jnp.uint32(0x80000000), jnp.float32)
                soft = jnp.log(1.0 + jnp.exp2(neg_abs)) * LOG2_E
                log_beta = jnp.minimum(z, 0.0) - soft
                log_1m = log_beta - z
                mask = None
                if diagonal and j == 0:
                    row = lax.broadcasted_iota(jnp.int32, (rc, t), 0) + r * rc
                    mask = lax.broadcasted_iota(jnp.int32, (rc, t), 1) < row
                    log_1m = jnp.where(mask, log_1m, 0.0)
                hi = lax.bitcast_convert_type(
                    lax.bitcast_convert_type(log_1m, jnp.uint32) & jnp.uint32(0xFFFF0000), jnp.float32)
                split = jnp.concatenate([hi.astype(jnp.bfloat16), (log_1m - hi).astype(jnp.bfloat16)], axis=1)
                terms[hh, r, j] = (log_beta, log_1m[:, 0:1], split, mask)

        cums = {key: jnp.dot(split, tri, preferred_element_type=jnp.float32)
                for key, (_, _, split, _) in terms.items()}

        alive = None
        for hh, r in chains:
            rows = slice(r * rc, (r + 1) * rc)
            carry = carry_ref[hh, rows, :]
            acc = acc_ref[hh, rows, :]
            for j in range(n_blocks):
                log_beta, first_term, _, mask = terms[hh, r, j]
                cum = cums[hh, r, j]
                w = jnp.exp2(log_beta + cum + jnp.concatenate([carry] * (t // LANES), axis=1))
                if mask is not None:
                    w = jnp.where(mask, w, 0.0)
                acc = acc + jnp.dot(w.astype(jnp.bfloat16), v_ref[pl.ds(starts[j], t), :],
                                    preferred_element_type=jnp.float32)
                carry = carry + jnp.broadcast_to(cum[:, 0:1] + first_term, (rc, LANES))
            acc_ref[hh, rows, :] = acc
            carry_ref[hh, rows, :] = carry
            alive = carry if alive is None else jnp.maximum(alive, carry)
        return (jnp.max(alive) < SB_DEAD_LOG2_WEIGHT).astype(jnp.int32)

    def query_tile(sub, c):
        qi = pl.program_id(2) * SB_TILES_PER_STEP + sub
        row0 = pl.multiple_of(sub * t, t)
        acc_ref[...] = jnp.zeros_like(acc_ref)
        carry_ref[...] = jnp.zeros_like(carry_ref)
        dead = lax.cond(qi >= 1, lambda: step(row0, qi, 2, True), lambda: step(row0, qi, 1, True))

        def cond(c):
            newest, dead = c
            return jnp.logical_and(newest >= 1, dead == 0)

        def body(c):
            newest, _ = c
            return newest - 2, step(row0, newest, 2, False)

        newest, dead = lax.while_loop(cond, body, (qi - 2, dead))

        @pl.when(jnp.logical_and(newest == 0, dead == 0))
        def _():
            step(row0, 0, 1, False)

        o_ref[pl.ds(row0, t), :] = _merge_heads([acc_ref[0], acc_ref[1]])
        return c

    lax.fori_loop(0, SB_TILES_PER_STEP, query_tile, 0)


def _sb_attention(u, tri, batch, seq):
    t = SB_TILE
    rows = SB_TILES_PER_STEP * t
    nq = seq // rows
    return pl.pallas_call(
        _sb_kernel,
        grid=(batch, N_PAIRS, nq),
        in_specs=[pl.BlockSpec((rows, LANES), lambda b, p, i: (b * nq + i, COL_QA // LANES + p)),
                  pl.BlockSpec((seq, LANES), lambda b, p, i: (b, COL_KA // LANES + p)),
                  pl.BlockSpec((seq, LANES), lambda b, p, i: (b, COL_VA // LANES + p)),
                  pl.BlockSpec((2 * t, t), lambda b, p, i: (0, 0))],
        out_specs=pl.BlockSpec((rows, LANES), lambda b, p, i: (b * nq + i, p)),
        out_shape=jax.ShapeDtypeStruct((batch * seq, WIDTH), jnp.float32),
        scratch_shapes=[pltpu.VMEM((HEADS_PER_TILE, t, LANES), jnp.float32),
                        pltpu.VMEM((HEADS_PER_TILE, t, LANES), jnp.float32)],
        compiler_params=pltpu.CompilerParams(dimension_semantics=("parallel", "parallel", "arbitrary"),
                                             vmem_limit_bytes=VMEM_LIMIT),
        name="sb_attention",
    )(u, u, u, tri)


MOBA_LO_LANE = 64
MOBA_NEVER = 63
MOBA_ZERO_ROW = 64
MOBA_MASKED = -1e30
MOBA_GROUP = 8
MOBA_BLOCKS_PER_STEP = 4


def _moba_kernel(far_ref, q_ref, k_ref, v_ref, btab_ref, oh_ref, o_ref,
                 kmean_ref, vt_ref, qa_ref, m_ref, acc_ref, *, nb):
    t = MOBA_BLOCK
    nsub = MOBA_BLOCKS_PER_STEP
    nh = HEADS_PER_TILE
    heads = range(nh)
    pair = pl.program_id(1)

    @pl.when(pl.program_id(2) == 0)
    def _():
        kmean_ref[...] = jnp.zeros_like(kmean_ref)
        kf = k_ref[...].astype(jnp.float32).reshape(nb, t, LANES)
        kmean_ref[0:nb, :] = jnp.mean(kf, axis=1)

        head_of_row = lax.broadcasted_iota(jnp.int32, (LANES, t), 0) // HEAD_DIM

        def transpose_block(b, c):
            vb = v_ref[pl.ds(pl.multiple_of(b * t, t), t), :].astype(jnp.float32).T
            for hh in heads:
                vt_ref[hh, b] = jnp.where(head_of_row == hh, vb, 1.0).astype(jnp.bfloat16)
            return c

        lax.fori_loop(0, nb, transpose_block, 0)

    rows = -(-nb // F32_SUBLANES) * F32_SUBLANES
    km = kmean_ref[0:rows, :]
    km0 = km.astype(jnp.bfloat16)
    r1 = km - km0.astype(jnp.float32)
    km1 = r1.astype(jnp.bfloat16)
    km2 = (r1 - km1.astype(jnp.float32)).astype(jnp.bfloat16)

    cols = nsub * t
    blk = lax.broadcasted_iota(jnp.int32, (rows, nh * cols), 0)
    blk_f = blk.astype(jnp.float32)
    col = lax.broadcasted_iota(jnp.int32, (1, nh * cols), 1)
    own_col = pl.program_id(2) * nsub + (col % cols) // t
    gate_rows = sum(_qk(jnp.concatenate([part * _head_lane_mask(hh, part.dtype) for hh in heads], axis=0),
                        q_ref[...]) for part in (km0, km1, km2))
    gate = jnp.concatenate([gate_rows[hh * rows:(hh + 1) * rows] for hh in heads], axis=1)
    gate = jnp.where(blk < own_col, gate, _NEG_INF)
    sel = jnp.zeros_like(gate)
    for _ in range(MOBA_TOPK):
        best = jnp.max(gate, axis=0, keepdims=True)
        idx = jnp.min(jnp.where(gate == best, blk_f, float(LANES)), axis=0, keepdims=True)
        hit = blk_f == idx
        sel = jnp.where(jnp.logical_and(hit, best > _NEG_INF), 1.0, sel)
        gate = jnp.where(hit, _NEG_INF, gate)

    far_bias = jnp.where(col // cols == 0, far_ref[pair * nh], far_ref[pair * nh + 1])
    term = jnp.where(sel > 0.0, jnp.where(blk < own_col - 1, far_bias, 0.0), MOBA_MASKED)
    hi = term.astype(jnp.bfloat16).astype(jnp.float32)
    lo = term - hi
    pad = MOBA_LO_LANE - rows
    pad_hi = [jnp.full((pad, t), MOBA_MASKED, jnp.float32)] if pad else []
    pad_lo = [jnp.zeros((pad, t), jnp.float32)] if pad else []
    for hh in heads:
        for sub in range(nsub):
            cs = slice(hh * cols + sub * t, hh * cols + (sub + 1) * t)
            inj = jnp.concatenate([hi[:, cs]] + pad_hi + [lo[:, cs]] + pad_lo, axis=0).T
            q_sub = q_ref[sub * t:(sub + 1) * t, :] * _head_lane_mask(hh, q_ref.dtype)
            qa_ref[hh, sub] = jnp.concatenate([q_sub, inj.astype(jnp.bfloat16)], axis=1)

    def query_block(sub, c):
        _moba_query_block(sub, k_ref, btab_ref, oh_ref, o_ref, vt_ref, qa_ref, m_ref, acc_ref, nb=nb)
        return c

    lax.fori_loop(0, nsub, query_block, 0)


def _moba_query_block(sub, k_ref, btab_ref, oh_ref, o_ref, vt_ref, qa_ref, m_ref, acc_ref, *, nb):
    t = MOBA_BLOCK
    g = MOBA_GROUP
    own = pl.program_id(2) * MOBA_BLOCKS_PER_STEP + sub
    q_rows = pl.ds(pl.multiple_of(sub * t, t), t)
    heads = range(HEADS_PER_TILE)

    def update(scores, blocks, first):
        col_max = [jnp.max(s, axis=0, keepdims=True) for s in scores]
        if first:
            m_new = col_max
        else:
            m_old = [m_ref[hh][0:1, :] for hh in heads]
            m_new = [jnp.maximum(m_old[hh], col_max[hh]) for hh in heads]
            alpha = [jnp.exp2(m_old[hh] - m_new[hh]) for hh in heads]
        p = [jnp.exp2(scores[hh] - m_new[hh]).astype(jnp.bfloat16) for hh in heads]
        pv = [jnp.dot(jnp.concatenate([vt_ref[hh, b] for b in blocks], axis=1), p[hh],
                      preferred_element_type=jnp.float32) for hh in heads]
        for hh in heads:
            acc_ref[hh] = pv[hh] if first else alpha[hh] * acc_ref[hh] + pv[hh]
            m_ref[hh] = jnp.broadcast_to(m_new[hh], m_ref.shape[1:])

    def block_lanes(row):
        return jnp.concatenate([oh_ref[row]] * (t // BF16_SUBLANES), axis=0)

    prev = jnp.maximum(own - 1, 0)
    n_far = jnp.maximum(own - 1, 0)
    n_full = n_far // g
    left = n_far - n_full * g

    def far_keys(first_blk, group):
        g0 = jnp.minimum(first_blk, nb - group)
        rows = []
        for j in range(group):
            b = g0 + j
            active = jnp.logical_and(b >= first_blk, b < n_far)
            rows.append(block_lanes(jnp.where(active, b, MOBA_NEVER)))
        keys = k_ref[pl.ds(pl.multiple_of(g0 * t, t), group * t), :]
        return g0, keys, rows

    def first_step(group):
        keys = [k_ref[pl.ds(pl.multiple_of(own * t, t), t), :], k_ref[pl.ds(pl.multiple_of(prev * t, t), t), :]]
        rows = [block_lanes(MOBA_ZERO_ROW), block_lanes(jnp.where(own >= 1, prev, MOBA_NEVER))]
        if group:
            g0, far, far_rows = far_keys(n_full * g, group)
            keys.append(far)
            rows.extend(far_rows)
        k_aug = jnp.concatenate([jnp.concatenate(keys, axis=0), jnp.concatenate(rows, axis=0)], axis=1)
        scores = [_qk(k_aug, qa_ref[hh, sub]) for hh in heads]
        update([scores[hh][:t] + btab_ref[hh, 0] for hh in heads], [own], first=True)
        update([scores[hh][t:2 * t] + btab_ref[hh, 1] for hh in heads], [prev], first=False)
        for j in range(group):
            update([s[(2 + j) * t:(3 + j) * t] for s in scores], [g0 + j], first=False)

    for count in range(g):
        @pl.when(left == count)
        def _(count=count):
            first_step(count)

    def far_body(i, c):
        g0, keys, rows = far_keys(i * g, g)
        k_aug = jnp.concatenate([keys, jnp.concatenate(rows, axis=0)], axis=1)
        scores = [_qk(k_aug, qa_ref[hh, sub]) for hh in heads]
        for j in range(g):
            update([s[j * t:(j + 1) * t] for s in scores], [g0 + j], first=False)
        return c

    lax.fori_loop(0, n_full, far_body, 0)

    out_t = jnp.concatenate([acc_ref[0][:HEAD_DIM] / acc_ref[0][HEAD_DIM:],
                             acc_ref[1][HEAD_DIM:] / acc_ref[1][:HEAD_DIM]], axis=0)
    o_ref[q_rows, :] = out_t.T


def _moba_attention(u, far_bias, btab, block_lanes, batch, seq):
    t = MOBA_BLOCK
    nb = seq // t
    rows = MOBA_BLOCKS_PER_STEP * t
    nstep = seq // rows
    kernel = functools.partial(_moba_kernel, nb=nb)
    grid_spec = pltpu.PrefetchScalarGridSpec(
        num_scalar_prefetch=1,
        grid=(batch, N_PAIRS, nstep),
        in_specs=[pl.BlockSpec((rows, LANES), lambda b, p, i, far: (b * nstep + i, COL_QB // LANES + p)),
                  pl.BlockSpec((seq, LANES), lambda b, p, i, far: (b, COL_KB // LANES + p)),
                  pl.BlockSpec((seq, LANES), lambda b, p, i, far: (b, COL_VB // LANES + p)),
                  pl.BlockSpec((HEADS_PER_TILE, 2, t, t), lambda b, p, i, far: (p, 0, 0, 0)),
                  pl.BlockSpec(block_lanes.shape, lambda b, p, i, far: (0, 0, 0))],
        out_specs=pl.BlockSpec((rows, LANES), lambda b, p, i, far: (b * nstep + i, p)),
        scratch_shapes=[pltpu.VMEM((MOBA_LO_LANE, LANES), jnp.float32),
                        pltpu.VMEM((HEADS_PER_TILE, nb, LANES, t), jnp.bfloat16),
                        pltpu.VMEM((HEADS_PER_TILE, MOBA_BLOCKS_PER_STEP, t, 2 * LANES), jnp.bfloat16),
                        pltpu.VMEM((HEADS_PER_TILE, F32_SUBLANES, t), jnp.float32),
                        pltpu.VMEM((HEADS_PER_TILE, LANES, t), jnp.float32)])
    return pl.pallas_call(
        kernel,
        grid_spec=grid_spec,
        out_shape=jax.ShapeDtypeStruct((batch * seq, WIDTH), jnp.float32),
        compiler_params=pltpu.CompilerParams(dimension_semantics=("parallel", "parallel", "arbitrary"),
                                             vmem_limit_bytes=VMEM_LIMIT),
        name="moba_attention",
    )(far_bias, u, u, u, btab, block_lanes)


SWA_STEP_WINDOWS = 2


def _swa_kernel(sink_ref, q_ref, kp_ref, kc_ref, vp_ref, vc_ref, bias_ref, o_ref):
    w = SWA_WINDOW
    keys = jnp.concatenate([kp_ref[...], kc_ref[...]], axis=0)
    values = jnp.concatenate([vp_ref[...], vc_ref[...]], axis=0)
    slots = [(j, half) for j in range(SWA_GROUP) for half in range(SWA_KV_HEADS)]
    chains = [(win, slot) for win in range(SWA_STEP_WINDOWS) for slot in range(len(slots))]

    def attend(first_step):
        scores = []
        for win, slot in chains:
            j, half = slots[slot]
            q = q_ref[win * w:(win + 1) * w, j * LANES:(j + 1) * LANES]
            s = _qk(q * _head_lane_mask(half, q.dtype), keys[win * w:(win + 2) * w]) + bias_ref[slot]
            if first_step and win == 0:
                s = jnp.where(lax.broadcasted_iota(jnp.int32, (w, 2 * w), 1) >= w, s, _NEG_INF)
            scores.append(s)
        m = [jnp.maximum(jnp.broadcast_to(jnp.max(s, axis=1, keepdims=True), (w, LANES)), sink_ref[slot])
             for (win, slot), s in zip(chains, scores)]
        p = [jnp.exp2(s - jnp.concatenate([mi] * (2 * w // LANES), axis=1)).astype(jnp.bfloat16)
             for mi, s in zip(m, scores)]
        pv = []
        for (win, slot), pi in zip(chains, p):
            vw = values[win * w:(win + 2) * w]
            vw = jnp.where(_head_lane_mask(slots[slot][1], jnp.int32) > 0, vw, jnp.ones_like(vw))
            pv.append(jnp.dot(pi, vw, preferred_element_type=jnp.float32))
        out = [pvi / (pltpu.roll(pvi, HEAD_DIM, axis=1) + jnp.exp2(sink_ref[slot] - mi))
               for (win, slot), pvi, mi in zip(chains, pv, m)]
        for win in range(SWA_STEP_WINDOWS):
            for j in range(SWA_GROUP):
                first = win * len(slots) + j * SWA_KV_HEADS
                o_ref[win * w:(win + 1) * w, j * LANES:(j + 1) * LANES] = _merge_heads(
                    out[first:first + SWA_KV_HEADS])

    @pl.when(pl.program_id(1) == 0)
    def _():
        attend(True)

    @pl.when(pl.program_id(1) > 0)
    def _():
        attend(False)


def _swa_attention(u, sinks_perm, bias, batch, seq):
    w = SWA_WINDOW
    sw = SWA_STEP_WINDOWS
    nstep = seq // (sw * w)
    kcol, vcol = COL_KC // LANES, COL_VC // LANES
    prev_window = lambda b, n, s: b * (seq // w) + jnp.maximum(sw * n - 1, 0)
    grid_spec = pltpu.PrefetchScalarGridSpec(
        num_scalar_prefetch=1,
        grid=(batch, nstep),
        in_specs=[pl.BlockSpec((sw * w, WIDTH), lambda b, n, s: (b * nstep + n, COL_QC // WIDTH)),
                  pl.BlockSpec((w, LANES), lambda b, n, s: (prev_window(b, n, s), kcol)),
                  pl.BlockSpec((sw * w, LANES), lambda b, n, s: (b * nstep + n, kcol)),
                  pl.BlockSpec((w, LANES), lambda b, n, s: (prev_window(b, n, s), vcol)),
                  pl.BlockSpec((sw * w, LANES), lambda b, n, s: (b * nstep + n, vcol)),
                  pl.BlockSpec((N_HEADS, w, 2 * w), lambda b, n, s: (0, 0, 0))],
        out_specs=pl.BlockSpec((sw * w, WIDTH), lambda b, n, s: (b * nstep + n, 0)))
    return pl.pallas_call(
        _swa_kernel,
        grid_spec=grid_spec,
        out_shape=jax.ShapeDtypeStruct((batch * seq, WIDTH), jnp.float32),
        compiler_params=pltpu.CompilerParams(dimension_semantics=("parallel", "arbitrary"),
                                             vmem_limit_bytes=VMEM_LIMIT),
        name="swa_attention",
    )(sinks_perm, u, u, u, u, u, bias)


def _post_kernel(x_ref, ya_ref, yb_ref, yc_ref, ga_ref, gb_ref, gc_ref, ma_ref, mb_ref, mc_ref,
                 wa_ref, wb_ref, wc_ref, wo_ref, fw_ref, o_ref, *, final):
    def branch(y_ref, g_ref, w_ref, m_ref):
        g = g_ref[...].astype(jnp.float32)
        y = (y_ref[...] * (g * jax.nn.sigmoid(g))).astype(jnp.bfloat16)
        proj = jnp.dot(y, w_ref[...], preferred_element_type=jnp.float32)
        return jax.nn.sigmoid(m_ref[...].astype(jnp.float32)) * proj

    merged = (branch(ya_ref, ga_ref, wa_ref, ma_ref) + branch(yb_ref, gb_ref, wb_ref, mb_ref)
              + branch(yc_ref, gc_ref, wc_ref, mc_ref))
    out = x_ref[...] + jnp.dot(merged.astype(jnp.bfloat16), wo_ref[...], preferred_element_type=jnp.float32)
    if final:
        out = out * lax.rsqrt(jnp.mean(out * out, axis=-1, keepdims=True) + RMS_EPS) * fw_ref[...]
    o_ref[...] = out


def _post(x2, ya, yb, yc, u, wa, wb, wc, wo, fw, layer, *, final, tm=512):
    n = x2.shape[0]
    row = lambda i: (i, 0)
    full = lambda i: (0, 0)
    of_layer = lambda i: (layer, 0, 0)
    y_spec = pl.BlockSpec((tm, WIDTH), row)
    in_specs = [pl.BlockSpec((tm, D_MODEL), row), y_spec, y_spec, y_spec,
                pl.BlockSpec((tm, WIDTH), lambda i: (i, COL_GA // WIDTH)),
                pl.BlockSpec((tm, WIDTH), lambda i: (i, COL_GB // WIDTH)),
                pl.BlockSpec((tm, WIDTH), lambda i: (i, COL_GC // WIDTH)),
                pl.BlockSpec((tm, D_MODEL), lambda i: (i, COL_MA // D_MODEL)),
                pl.BlockSpec((tm, D_MODEL), lambda i: (i, COL_MB // D_MODEL)),
                pl.BlockSpec((tm, D_MODEL), lambda i: (i, COL_MC // D_MODEL)),
                pl.BlockSpec((None, WIDTH, D_MODEL), of_layer), pl.BlockSpec((None, WIDTH, D_MODEL), of_layer),
                pl.BlockSpec((None, WIDTH, D_MODEL), of_layer), pl.BlockSpec((None, D_MODEL, D_MODEL), of_layer),
                pl.BlockSpec((1, D_MODEL), full)]
    return pl.pallas_call(
        functools.partial(_post_kernel, final=final),
        grid=(n // tm,),
        in_specs=in_specs,
        out_specs=pl.BlockSpec((tm, D_MODEL), row),
        out_shape=jax.ShapeDtypeStruct((n, D_MODEL), jnp.float32),
        compiler_params=pltpu.CompilerParams(dimension_semantics=("parallel",),
                                             vmem_limit_bytes=VMEM_LIMIT),
        name="post_final" if final else "post",
    )(x2, ya, yb, yc, u, u, u, u, u, u, wa, wb, wc, wo, fw)


def _bucket_lookup(table, dist):
    onehot = (_rel_bucket(dist)[..., None] == jnp.arange(REL_BUCKETS)).astype(jnp.float32)
    return jnp.einsum("...b,bh->h...", onehot, table.astype(jnp.float32), precision=lax.Precision.HIGHEST)


def _bias_tables(rel_bias):
    moba_tab = rel_bias[:, :N_HEADS]
    swa_tab = _swa_head_order(rel_bias[:, N_HEADS:], 1, per_head=1)
    t = MOBA_BLOCK
    d_own = jnp.arange(t)[:, None] - jnp.arange(t)[None, :]
    own = jnp.where(d_own[None] >= 0, _bucket_lookup(moba_tab, d_own), _NEG_INF)
    prev = _bucket_lookup(moba_tab, d_own + t)
    btab = jnp.swapaxes(jnp.stack([own, prev], axis=1), 2, 3) * LOG2_E
    far = _bucket_lookup(moba_tab, jnp.full((1,), t + 1, jnp.int32))[:, 0] * LOG2_E

    w = SWA_WINDOW
    dist = jnp.arange(w)[:, None] + w - jnp.arange(2 * w)[None, :]
    in_band = (dist >= 0) & (dist < w)
    swa = jnp.where(in_band[None], _bucket_lookup(swa_tab, dist) * LOG2_E, _NEG_INF)
    return btab, far, swa


def kernel(x, norm_w, w_in, w_proj_a, w_proj_b, w_proj_c, w_out, sinks, rel_bias, final_norm_w):
    batch, seq, _ = x.shape
    depth = w_in.shape[0]
    assert seq % MOBA_BLOCK == 0 and MOBA_GROUP <= seq // MOBA_BLOCK <= MOBA_NEVER
    btab, far, swa_bias = _bias_tables(rel_bias)
    t = SB_TILE
    tri = (jnp.arange(t)[:, None] > jnp.arange(t)[None, :]).astype(jnp.bfloat16)
    tri = jnp.concatenate([tri, tri], axis=0)
    lane = jnp.arange(LANES)[None, :]
    blk = jnp.arange(MOBA_ZERO_ROW + 1)[:, None]
    block_lanes = ((lane == blk) | (lane == blk + MOBA_LO_LANE)) & (blk < MOBA_ZERO_ROW)
    block_lanes = jnp.broadcast_to(block_lanes[:, None, :],
                                   (MOBA_ZERO_ROW + 1, BF16_SUBLANES, LANES)).astype(jnp.bfloat16)
    sinks_perm = _swa_head_order(sinks.astype(jnp.float32), 1, per_head=1) * LOG2_E

    w_u = _permute_input_columns(w_in)
    norm_w3 = norm_w.reshape(depth, 1, D_MODEL)
    wa = w_proj_a.astype(jnp.bfloat16)
    wb = w_proj_b.astype(jnp.bfloat16)
    wc = _swa_head_order(w_proj_c, 1).astype(jnp.bfloat16)
    wo = w_out.astype(jnp.bfloat16)
    fw = final_norm_w.reshape(1, D_MODEL)

    x2 = x.reshape(batch * seq, D_MODEL)
    for layer in range(depth):
        u = _inproj(x2, norm_w3, w_u, layer)
        ya = _sb_attention(u, tri, batch, seq)
        yb = _moba_attention(u, far, btab, block_lanes, batch, seq)
        yc = _swa_attention(u, sinks_perm[layer], swa_bias, batch, seq)
        x2 = _post(x2, ya, yb, yc, u, wa, wb, wc, wo, fw, layer, final=(layer == depth - 1))
    return x2.reshape(batch, seq, D_MODEL)
```

```python
import functools
import math

import jax
import jax.numpy as jnp
from jax import lax
from jax.experimental import pallas as pl
from jax.experimental.pallas import tpu as pltpu

D_MODEL = 1024
HEAD_DIM = 64
N_HEADS = 8
WIDTH = N_HEADS * HEAD_DIM
MOBA_BLOCK = 256
MOBA_TOPK = 3
SWA_KV_HEADS = 2
SWA_GROUP = N_HEADS // SWA_KV_HEADS
SWA_WINDOW = 128
REL_BUCKETS = 32
REL_MAX_DIST = 128
RMS_EPS = 1e-6

LANES = 128
BF16_SUBLANES = 16
F32_SUBLANES = 8
HEADS_PER_TILE = LANES // HEAD_DIM
N_PAIRS = N_HEADS // HEADS_PER_TILE

COL_QA, COL_KA, COL_VA = 0, 512, 1024
COL_QB, COL_KB, COL_VB = 1536, 2048, 2560
COL_QC = 3072
COL_GA, COL_GB, COL_GC = 3584, 4096, 4608
COL_MA, COL_MB, COL_MC = 5120, 6144, 7168
COL_KC, COL_VC = 8192, 8320
D_IN = 8448

V7X_VMEM_BYTES = 64 * 1024 * 1024
VMEM_LIMIT = V7X_VMEM_BYTES * 3 // 4

LOG2_E = math.log2(math.e)

SB_DEAD_LOG_WEIGHT = -120.0
SB_DEAD_LOG2_WEIGHT = SB_DEAD_LOG_WEIGHT * LOG2_E

_NEG_INF = float("-inf")


def _rel_bucket(dist):
    max_exact = REL_BUCKETS // 2
    n = jnp.maximum(dist, 0)
    nf = jnp.maximum(n, 1).astype(jnp.float32)
    large = max_exact + (jnp.log(nf / max_exact) / math.log(REL_MAX_DIST / max_exact)
                         * (REL_BUCKETS - max_exact)).astype(jnp.int32)
    large = jnp.minimum(large, REL_BUCKETS - 1)
    return jnp.where(n < max_exact, n, large)


def _swa_head_order(t, axis, per_head=HEAD_DIM):
    axis = axis % t.ndim
    shape = t.shape
    t = t.reshape(shape[:axis] + (SWA_KV_HEADS, SWA_GROUP, per_head) + shape[axis + 1:])
    return jnp.swapaxes(t, axis, axis + 1).reshape(shape)


def _permute_input_columns(w):
    old = {"qa": 0, "ka": 512, "va": 1024, "ga": 1536, "qb": 2048, "kb": 2560, "vb": 3072, "gb": 3584,
           "qc": 4096, "kc": 4608, "vc": 4736, "gc": 4864, "ma": 5376, "mb": 6400, "mc": 7424}
    q_scale = HEAD_DIM ** -0.5 * LOG2_E

    def cols(name, width, scale=None):
        part = w[..., old[name]:old[name] + width]
        return part if scale is None else part * scale

    parts = [cols("qa", 512, q_scale), cols("ka", 512), cols("va", 512),
             cols("qb", 512, q_scale), cols("kb", 512), cols("vb", 512),
             _swa_head_order(cols("qc", WIDTH, q_scale), -1),
             cols("ga", 512), cols("gb", 512), _swa_head_order(cols("gc", WIDTH), -1),
             cols("ma", 1024), cols("mb", 1024), cols("mc", 1024),
             cols("kc", 128), cols("vc", 128)]
    return jnp.concatenate(parts, axis=-1).astype(jnp.bfloat16)


def _inproj_kernel(x_ref, nw_ref, w_ref, u_ref, h_ref):
    @pl.when(pl.program_id(1) == 0)
    def _():
        x = x_ref[...]
        y = x * lax.rsqrt(jnp.mean(x * x, axis=-1, keepdims=True) + RMS_EPS)
        h_ref[...] = (y * nw_ref[...]).astype(h_ref.dtype)

    u_ref[...] = jnp.dot(h_ref[...], w_ref[...], preferred_element_type=jnp.float32).astype(u_ref.dtype)


def _inproj(x2, norm_w, w, layer, *, tm=1024, tn=2816):
    n = x2.shape[0]
    return pl.pallas_call(
        _inproj_kernel,
        grid=(n // tm, D_IN // tn),
        in_specs=[pl.BlockSpec((tm, D_MODEL), lambda i, j: (i, 0)),
                  pl.BlockSpec((None, 1, D_MODEL), lambda i, j: (layer, 0, 0)),
                  pl.BlockSpec((None, D_MODEL, tn), lambda i, j: (layer, 0, j))],
        out_specs=pl.BlockSpec((tm, tn), lambda i, j: (i, j)),
        out_shape=jax.ShapeDtypeStruct((n, D_IN), jnp.bfloat16),
        scratch_shapes=[pltpu.VMEM((tm, D_MODEL), jnp.bfloat16)],
        compiler_params=pltpu.CompilerParams(dimension_semantics=("parallel", "arbitrary"),
                                             vmem_limit_bytes=VMEM_LIMIT),
        name="inproj",
    )(x2, norm_w, w)


def _head_lane_mask(hh, dtype):
    lane = lax.broadcasted_iota(jnp.int32, (1, LANES), 1)
    return ((lane // HEAD_DIM) == hh).astype(dtype)


def _qk(qm, k):
    return lax.dot_general(qm, k, (((1,), (1,)), ((), ())), preferred_element_type=jnp.float32)


def _merge_heads(parts):
    lane = lax.broadcasted_iota(jnp.int32, parts[0].shape, 1)
    return jnp.where(lane < HEAD_DIM, parts[0], parts[1])


SB_TILE = 256
SB_TILES_PER_STEP = 4
SB_PAIR = 2
SB_ROW_CHUNK = 256


def _sb_kernel(q_ref, k_ref, v_ref, tri_ref, o_ref, acc_ref, carry_ref):
    t = SB_TILE

    def step(tiles):
        tri = tri_ref[...]
        rc = SB_ROW_CHUNK
        chains = [(tile, hh, r) for tile in tiles for hh in range(HEADS_PER_TILE) for r in range(t // rc)]

        def block_start(tile, j):
            return pl.multiple_of((tile[2] - j) * t, t)

        terms = {}
        for tile, hh, r in chains:
            slot, row0, _, n_blocks, diagonal = tile
            qm = q_ref[pl.ds(row0 + r * rc, rc), :] * _head_lane_mask(hh, q_ref.dtype)
            for j in range(n_blocks):
                z = _qk(qm, k_ref[pl.ds(block_start(tile, j), t), :])
                neg_abs = lax.bitcast_convert_type(
                    lax.bitcast_convert_type(z, jnp.uint32) | jnp.uint32(0x80000000), jnp.float32)
                soft = jnp.log(1.0 + jnp.exp2(neg_abs)) * LOG2_E
                log_beta = jnp.minimum(z, 0.0) - soft
                log_1m = log_beta - z
                mask = None
                if diagonal and j == 0:
                    row = lax.broadcasted_iota(jnp.int32, (rc, t), 0) + r * rc
                    mask = lax.broadcasted_iota(jnp.int32, (rc, t), 1) < row
                    log_1m = jnp.where(mask, log_1m, 0.0)
                hi = lax.bitcast_convert_type(
                    lax.bitcast_convert_type(log_1m, jnp.uint32) & jnp.uint32(0xFFFF0000), jnp.float32)
                split = jnp.concatenate([hi.astype(jnp.bfloat16), (log_1m - hi).astype(jnp.bfloat16)], axis=1)
                terms[slot, hh, r, j] = (log_beta, log_1m[:, 0:1], split, mask)

        cums = {key: jnp.dot(split, tri, preferred_element_type=jnp.float32)
                for key, (_, _, split, _) in terms.items()}

        alive = {}
        for tile, hh, r in chains:
            slot, _, _, n_blocks, _ = tile
            rows = slice(r * rc, (r + 1) * rc)
            carry = carry_ref[slot, hh, rows, :]
            acc = acc_ref[slot, hh, rows, :]
            for j in range(n_blocks):
                log_beta, first_term, _, mask = terms[slot, hh, r, j]
                cum = cums[slot, hh, r, j]
                w = jnp.exp2(log_beta + cum + jnp.concatenate([carry] * (t // LANES), axis=1))
                if mask is not None:
                    w = jnp.where(mask, w, 0.0)
                acc = acc + jnp.dot(w.astype(jnp.bfloat16), v_ref[pl.ds(block_start(tile, j), t), :],
                                    preferred_element_type=jnp.float32)
                carry = carry + jnp.broadcast_to(cum[:, 0:1] + first_term, (rc, LANES))
            acc_ref[slot, hh, rows, :] = acc
            carry_ref[slot, hh, rows, :] = carry
            alive[slot] = carry if slot not in alive else jnp.maximum(alive[slot], carry)
        return tuple((jnp.max(alive[tile[0]]) < SB_DEAD_LOG2_WEIGHT).astype(jnp.int32) for tile in tiles)

    def query_tiles(pair, c):
        qi = [pl.program_id(2) * SB_TILES_PER_STEP + SB_PAIR * pair + s for s in range(SB_PAIR)]
        row0 = [pl.multiple_of((SB_PAIR * pair + s) * t, t) for s in range(SB_PAIR)]
        acc_ref[...] = jnp.zeros_like(acc_ref)
        carry_ref[...] = jnp.zeros_like(carry_ref)

        def first(n_first):
            return step([(0, row0[0], qi[0], n_first, True)]
                        + [(s, row0[s], qi[s], 2, True) for s in range(1, SB_PAIR)])

        dead = lax.cond(qi[0] >= 1, lambda: first(2), lambda: first(1))

        for s in range(SB_PAIR):
            def cond(c):
                newest, dead = c
                return jnp.logical_and(newest >= 1, dead == 0)

            def body(c, s=s):
                newest, _ = c
                return newest - 2, step([(s, row0[s], newest, 2, False)])[0]

            newest, still = lax.while_loop(cond, body, (qi[s] - 2, dead[s]))

            @pl.when(jnp.logical_and(newest == 0, still == 0))
            def _(s=s):
                step([(s, row0[s], 0, 1, False)])

            o_ref[pl.ds(row0[s], t), :] = _merge_heads([acc_ref[s, 0], acc_ref[s, 1]])
        return c

    lax.fori_loop(0, SB_TILES_PER_STEP // SB_PAIR, query_tiles, 0)


def _sb_attention(u, tri, batch, seq):
    t = SB_TILE
    rows = SB_TILES_PER_STEP * t
    nq = seq // rows
    return pl.pallas_call(
        _sb_kernel,
        grid=(batch, N_PAIRS, nq),
        in_specs=[pl.BlockSpec((rows, LANES), lambda b, p, i: (b * nq + i, COL_QA // LANES + p)),
                  pl.BlockSpec((seq, LANES), lambda b, p, i: (b, COL_KA // LANES + p)),
                  pl.BlockSpec((seq, LANES), lambda b, p, i: (b, COL_VA // LANES + p)),
                  pl.BlockSpec((2 * t, t), lambda b, p, i: (0, 0))],
        out_specs=pl.BlockSpec((rows, LANES), lambda b, p, i: (b * nq + i, p)),
        out_shape=jax.ShapeDtypeStruct((batch * seq, WIDTH), jnp.float32),
        scratch_shapes=[pltpu.VMEM((SB_PAIR, HEADS_PER_TILE, t, LANES), jnp.float32),
                        pltpu.VMEM((SB_PAIR, HEADS_PER_TILE, t, LANES), jnp.float32)],
        compiler_params=pltpu.CompilerParams(dimension_semantics=("parallel", "parallel", "arbitrary"),
                                             vmem_limit_bytes=VMEM_LIMIT),
        name="sb_attention",
    )(u, u, u, tri)


MOBA_LO_LANE = 64
MOBA_NEVER = 63
MOBA_ZERO_ROW = 64
MOBA_MASKED = -1e30
MOBA_GROUP = 8
MOBA_BLOCKS_PER_STEP = 4


def _moba_kernel(far_ref, q_ref, k_ref, v_ref, btab_ref, oh_ref, o_ref,
                 kmean_ref, vt_ref, qa_ref, m_ref, acc_ref, *, nb):
    t = MOBA_BLOCK
    nsub = MOBA_BLOCKS_PER_STEP
    nh = HEADS_PER_TILE
    heads = range(nh)
    pair = pl.program_id(1)

    @pl.when(pl.program_id(2) == 0)
    def _():
        kmean_ref[...] = jnp.zeros_like(kmean_ref)
        kf = k_ref[...].astype(jnp.float32).reshape(nb, t, LANES)
        kmean_ref[0:nb, :] = jnp.mean(kf, axis=1)

        head_of_row = lax.broadcasted_iota(jnp.int32, (LANES, t), 0) // HEAD_DIM

        def transpose_block(b, c):
            vb = v_ref[pl.ds(pl.multiple_of(b * t, t), t), :].astype(jnp.float32).T
            for hh in heads:
                vt_ref[hh, b] = jnp.where(head_of_row == hh, vb, 1.0).astype(jnp.bfloat16)
            return c

        lax.fori_loop(0, nb, transpose_block, 0)

    rows = -(-nb // F32_SUBLANES) * F32_SUBLANES
    km = kmean_ref[0:rows, :]
    km0 = km.astype(jnp.bfloat16)
    r1 = km - km0.astype(jnp.float32)
    km1 = r1.astype(jnp.bfloat16)
    km2 = (r1 - km1.astype(jnp.float32)).astype(jnp.bfloat16)

    cols = nsub * t
    blk = lax.broadcasted_iota(jnp.int32, (rows, nh * cols), 0)
    blk_f = blk.astype(jnp.float32)
    col = lax.broadcasted_iota(jnp.int32, (1, nh * cols), 1)
    own_col = pl.program_id(2) * nsub + (col % cols) // t
    gate_rows = sum(_qk(jnp.concatenate([part * _head_lane_mask(hh, part.dtype) for hh in heads], axis=0),
                        q_ref[...]) for part in (km0, km1, km2))
    gate = jnp.concatenate([gate_rows[hh * rows:(hh + 1) * rows] for hh in heads], axis=1)
    gate = jnp.where(blk < own_col, gate, _NEG_INF)
    sel = jnp.zeros_like(gate)
    for _ in range(MOBA_TOPK):
        best = jnp.max(gate, axis=0, keepdims=True)
        idx = jnp.min(jnp.where(gate == best, blk_f, float(LANES)), axis=0, keepdims=True)
        hit = blk_f == idx
        sel = jnp.where(jnp.logical_and(hit, best > _NEG_INF), 1.0, sel)
        gate = jnp.where(hit, _NEG_INF, gate)

    far_bias = jnp.where(col // cols == 0, far_ref[pair * nh], far_ref[pair * nh + 1])
    term = jnp.where(sel > 0.0, jnp.where(blk < own_col - 1, far_bias, 0.0), MOBA_MASKED)
    hi = term.astype(jnp.bfloat16).astype(jnp.float32)
    lo = term - hi
    pad = MOBA_LO_LANE - rows
    pad_hi = [jnp.full((pad, t), MOBA_MASKED, jnp.float32)] if pad else []
    pad_lo = [jnp.zeros((pad, t), jnp.float32)] if pad else []
    for hh in heads:
        for sub in range(nsub):
            cs = slice(hh * cols + sub * t, hh * cols + (sub + 1) * t)
            inj = jnp.concatenate([hi[:, cs]] + pad_hi + [lo[:, cs]] + pad_lo, axis=0).T
            q_sub = q_ref[sub * t:(sub + 1) * t, :] * _head_lane_mask(hh, q_ref.dtype)
            qa_ref[hh, sub] = jnp.concatenate([q_sub, inj.astype(jnp.bfloat16)], axis=1)

    def query_block(sub, c):
        _moba_query_block(sub, k_ref, btab_ref, oh_ref, o_ref, vt_ref, qa_ref, m_ref, acc_ref, nb=nb)
        return c

    lax.fori_loop(0, nsub, query_block, 0)


def _moba_query_block(sub, k_ref, btab_ref, oh_ref, o_ref, vt_ref, qa_ref, m_ref, acc_ref, *, nb):
    t = MOBA_BLOCK
    g = MOBA_GROUP
    own = pl.program_id(2) * MOBA_BLOCKS_PER_STEP + sub
    q_rows = pl.ds(pl.multiple_of(sub * t, t), t)
    heads = range(HEADS_PER_TILE)

    def update(scores, blocks, first):
        col_max = [jnp.max(s, axis=0, keepdims=True) for s in scores]
        if first:
            m_new = col_max
        else:
            m_old = [m_ref[hh][0:1, :] for hh in heads]
            m_new = [jnp.maximum(m_old[hh], col_max[hh]) for hh in heads]
            alpha = [jnp.exp2(m_old[hh] - m_new[hh]) for hh in heads]
        p = [jnp.exp2(scores[hh] - m_new[hh]).astype(jnp.bfloat16) for hh in heads]
        pv = [jnp.dot(jnp.concatenate([vt_ref[hh, b] for b in blocks], axis=1), p[hh],
                      preferred_element_type=jnp.float32) for hh in heads]
        for hh in heads:
            acc_ref[hh] = pv[hh] if first else alpha[hh] * acc_ref[hh] + pv[hh]
            m_ref[hh] = jnp.broadcast_to(m_new[hh], m_ref.shape[1:])

    def block_lanes(row):
        return jnp.concatenate([oh_ref[row]] * (t // BF16_SUBLANES), axis=0)

    prev = jnp.maximum(own - 1, 0)
    n_far = jnp.maximum(own - 1, 0)
    n_full = n_far // g
    left = n_far - n_full * g

    def far_keys(first_blk, group):
        g0 = jnp.minimum(first_blk, nb - group)
        rows = []
        for j in range(group):
            b = g0 + j
            active = jnp.logical_and(b >= first_blk, b < n_far)
            rows.append(block_lanes(jnp.where(active, b, MOBA_NEVER)))
        keys = k_ref[pl.ds(pl.multiple_of(g0 * t, t), group * t), :]
        return g0, keys, rows

    def first_step(group):
        keys = [k_ref[pl.ds(pl.multiple_of(own * t, t), t), :], k_ref[pl.ds(pl.multiple_of(prev * t, t), t), :]]
        rows = [block_lanes(MOBA_ZERO_ROW), block_lanes(jnp.where(own >= 1, prev, MOBA_NEVER))]
        if group:
            g0, far, far_rows = far_keys(n_full * g, group)
            keys.append(far)
            rows.extend(far_rows)
        k_aug = jnp.concatenate([jnp.concatenate(keys, axis=0), jnp.concatenate(rows, axis=0)], axis=1)
        scores = [_qk(k_aug, qa_ref[hh, sub]) for hh in heads]
        update([scores[hh][:t] + btab_ref[hh, 0] for hh in heads], [own], first=True)
        update([scores[hh][t:2 * t] + btab_ref[hh, 1] for hh in heads], [prev], first=False)
        for j in range(group):
            update([s[(2 + j) * t:(3 + j) * t] for s in scores], [g0 + j], first=False)

    for count in range(g):
        @pl.when(left == count)
        def _(count=count):
            first_step(count)

    def far_body(i, c):
        g0, keys, rows = far_keys(i * g, g)
        k_aug = jnp.concatenate([keys, jnp.concatenate(rows, axis=0)], axis=1)
        scores = [_qk(k_aug, qa_ref[hh, sub]) for hh in heads]
        for j in range(g):
            update([s[j * t:(j + 1) * t] for s in scores], [g0 + j], first=False)
        return c

    lax.fori_loop(0, n_full, far_body, 0)

    out_t = jnp.concatenate([acc_ref[0][:HEAD_DIM] / acc_ref[0][HEAD_DIM:],
                             acc_ref[1][HEAD_DIM:] / acc_ref[1][:HEAD_DIM]], axis=0)
    o_ref[q_rows, :] = out_t.T


def _moba_attention(u, far_bias, btab, block_lanes, batch, seq):
    t = MOBA_BLOCK
    nb = seq // t
    rows = MOBA_BLOCKS_PER_STEP * t
    nstep = seq // rows
    kernel = functools.partial(_moba_kernel, nb=nb)
    grid_spec = pltpu.PrefetchScalarGridSpec(
        num_scalar_prefetch=1,
        grid=(batch, N_PAIRS, nstep),
        in_specs=[pl.BlockSpec((rows, LANES), lambda b, p, i, far: (b * nstep + i, COL_QB // LANES + p)),
                  pl.BlockSpec((seq, LANES), lambda b, p, i, far: (b, COL_KB // LANES + p)),
                  pl.BlockSpec((seq, LANES), lambda b, p, i, far: (b, COL_VB // LANES + p)),
                  pl.BlockSpec((HEADS_PER_TILE, 2, t, t), lambda b, p, i, far: (p, 0, 0, 0)),
                  pl.BlockSpec(block_lanes.shape, lambda b, p, i, far: (0, 0, 0))],
        out_specs=pl.BlockSpec((rows, LANES), lambda b, p, i, far: (b * nstep + i, p)),
        scratch_shapes=[pltpu.VMEM((MOBA_LO_LANE, LANES), jnp.float32),
                        pltpu.VMEM((HEADS_PER_TILE, nb, LANES, t), jnp.bfloat16),
                        pltpu.VMEM((HEADS_PER_TILE, MOBA_BLOCKS_PER_STEP, t, 2 * LANES), jnp.bfloat16),
                        pltpu.VMEM((HEADS_PER_TILE, F32_SUBLANES, t), jnp.float32),
                        pltpu.VMEM((HEADS_PER_TILE, LANES, t), jnp.float32)])
    return pl.pallas_call(
        kernel,
        grid_spec=grid_spec,
        out_shape=jax.ShapeDtypeStruct((batch * seq, WIDTH), jnp.float32),
        compiler_params=pltpu.CompilerParams(dimension_semantics=("parallel", "parallel", "arbitrary"),
                                             vmem_limit_bytes=VMEM_LIMIT),
        name="moba_attention",
    )(far_bias, u, u, u, btab, block_lanes)


SWA_STEP_WINDOWS = 2


def _swa_kernel(sink_ref, q_ref, kp_ref, kc_ref, vp_ref, vc_ref, bias_ref, o_ref):
    w = SWA_WINDOW
    keys = jnp.concatenate([kp_ref[...], kc_ref[...]], axis=0)
    values = jnp.concatenate([vp_ref[...], vc_ref[...]], axis=0)
    slots = [(j, half) for j in range(SWA_GROUP) for half in range(SWA_KV_HEADS)]
    chains = [(win, slot) for win in range(SWA_STEP_WINDOWS) for slot in range(len(slots))]

    def attend(first_step):
        scores = []
        for win, slot in chains:
            j, half = slots[slot]
            q = q_ref[win * w:(win + 1) * w, j * LANES:(j + 1) * LANES]
            s = _qk(q * _head_lane_mask(half, q.dtype), keys[win * w:(win + 2) * w]) + bias_ref[slot]
            if first_step and win == 0:
                s = jnp.where(lax.broadcasted_iota(jnp.int32, (w, 2 * w), 1) >= w, s, _NEG_INF)
            scores.append(s)
        m = [jnp.maximum(jnp.broadcast_to(jnp.max(s, axis=1, keepdims=True), (w, LANES)), sink_ref[slot])
             for (win, slot), s in zip(chains, scores)]
        p = [jnp.exp2(s - jnp.concatenate([mi] * (2 * w // LANES), axis=1)).astype(jnp.bfloat16)
             for mi, s in zip(m, scores)]
        pv = []
        for (win, slot), pi in zip(chains, p):
            vw = values[win * w:(win + 2) * w]
            vw = jnp.where(_head_lane_mask(slots[slot][1], jnp.int32) > 0, vw, jnp.ones_like(vw))
            pv.append(jnp.dot(pi, vw, preferred_element_type=jnp.float32))
        out = [pvi / (pltpu.roll(pvi, HEAD_DIM, axis=1) + jnp.exp2(sink_ref[slot] - mi))
               for (win, slot), pvi, mi in zip(chains, pv, m)]
        for win in range(SWA_STEP_WINDOWS):
            for j in range(SWA_GROUP):
                first = win * len(slots) + j * SWA_KV_HEADS
                o_ref[win * w:(win + 1) * w, j * LANES:(j + 1) * LANES] = _merge_heads(
                    out[first:first + SWA_KV_HEADS])

    @pl.when(pl.program_id(1) == 0)
    def _():
        attend(True)

    @pl.when(pl.program_id(1) > 0)
    def _():
        attend(False)


def _swa_attention(u, sinks_perm, bias, batch, seq):
    w = SWA_WINDOW
    sw = SWA_STEP_WINDOWS
    nstep = seq // (sw * w)
    kcol, vcol = COL_KC // LANES, COL_VC // LANES
    prev_window = lambda b, n, s: b * (seq // w) + jnp.maximum(sw * n - 1, 0)
    grid_spec = pltpu.PrefetchScalarGridSpec(
        num_scalar_prefetch=1,
        grid=(batch, nstep),
        in_specs=[pl.BlockSpec((sw * w, WIDTH), lambda b, n, s: (b * nstep + n, COL_QC // WIDTH)),
                  pl.BlockSpec((w, LANES), lambda b, n, s: (prev_window(b, n, s), kcol)),
                  pl.BlockSpec((sw * w, LANES), lambda b, n, s: (b * nstep + n, kcol)),
                  pl.BlockSpec((w, LANES), lambda b, n, s: (prev_window(b, n, s), vcol)),
                  pl.BlockSpec((sw * w, LANES), lambda b, n, s: (b * nstep + n, vcol)),
                  pl.BlockSpec((N_HEADS, w, 2 * w), lambda b, n, s: (0, 0, 0))],
        out_specs=pl.BlockSpec((sw * w, WIDTH), lambda b, n, s: (b * nstep + n, 0)))
    return pl.pallas_call(
        _swa_kernel,
        grid_spec=grid_spec,
        out_shape=jax.ShapeDtypeStruct((batch * seq, WIDTH), jnp.float32),
        compiler_params=pltpu.CompilerParams(dimension_semantics=("parallel", "arbitrary"),
                                             vmem_limit_bytes=VMEM_LIMIT),
        name="swa_attention",
    )(sinks_perm, u, u, u, u, u, bias)


def _post_kernel(x_ref, ya_ref, yb_ref, yc_ref, ga_ref, gb_ref, gc_ref, ma_ref, mb_ref, mc_ref,
                 wa_ref, wb_ref, wc_ref, wo_ref, fw_ref, o_ref, *, final):
    def branch(y_ref, g_ref, w_ref, m_ref):
        g = g_ref[...].astype(jnp.float32)
        y = (y_ref[...] * (g * jax.nn.sigmoid(g))).astype(jnp.bfloat16)
        proj = jnp.dot(y, w_ref[...], preferred_element_type=jnp.float32)
        return jax.nn.sigmoid(m_ref[...].astype(jnp.float32)) * proj

    merged = (branch(ya_ref, ga_ref, wa_ref, ma_ref) + branch(yb_ref, gb_ref, wb_ref, mb_ref)
              + branch(yc_ref, gc_ref, wc_ref, mc_ref))
    out = x_ref[...] + jnp.dot(merged.astype(jnp.bfloat16), wo_ref[...], preferred_element_type=jnp.float32)
    if final:
        out = out * lax.rsqrt(jnp.mean(out * out, axis=-1, keepdims=True) + RMS_EPS) * fw_ref[...]
    o_ref[...] = out


def _post(x2, ya, yb, yc, u, wa, wb, wc, wo, fw, layer, *, final, tm=512):
    n = x2.shape[0]
    row = lambda i: (i, 0)
    full = lambda i: (0, 0)
    of_layer = lambda i: (layer, 0, 0)
    y_spec = pl.BlockSpec((tm, WIDTH), row)
    in_specs = [pl.BlockSpec((tm, D_MODEL), row), y_spec, y_spec, y_spec,
                pl.BlockSpec((tm, WIDTH), lambda i: (i, COL_GA // WIDTH)),
                pl.BlockSpec((tm, WIDTH), lambda i: (i, COL_GB // WIDTH)),
                pl.BlockSpec((tm, WIDTH), lambda i: (i, COL_GC // WIDTH)),
                pl.BlockSpec((tm, D_MODEL), lambda i: (i, COL_MA // D_MODEL)),
                pl.BlockSpec((tm, D_MODEL), lambda i: (i, COL_MB // D_MODEL)),
                pl.BlockSpec((tm, D_MODEL), lambda i: (i, COL_MC // D_MODEL)),
                pl.BlockSpec((None, WIDTH, D_MODEL), of_layer), pl.BlockSpec((None, WIDTH, D_MODEL), of_layer),
                pl.BlockSpec((None, WIDTH, D_MODEL), of_layer), pl.BlockSpec((None, D_MODEL, D_MODEL), of_layer),
                pl.BlockSpec((1, D_MODEL), full)]
    return pl.pallas_call(
        functools.partial(_post_kernel, final=final),
        grid=(n // tm,),
        in_specs=in_specs,
        out_specs=pl.BlockSpec((tm, D_MODEL), row),
        out_shape=jax.ShapeDtypeStruct((n, D_MODEL), jnp.float32),
        compiler_params=pltpu.CompilerParams(dimension_semantics=("parallel",),
                                             vmem_limit_bytes=VMEM_LIMIT),
        name="post_final" if final else "post",
    )(x2, ya, yb, yc, u, u, u, u, u, u, wa, wb, wc, wo, fw)


def _bucket_lookup(table, dist):
    onehot = (_rel_bucket(dist)[..., None] == jnp.arange(REL_BUCKETS)).astype(jnp.float32)
    return jnp.einsum("...b,bh->h...", onehot, table.astype(jnp.float32), precision=lax.Precision.HIGHEST)


def _bias_tables(rel_bias):
    moba_tab = rel_bias[:, :N_HEADS]
    swa_tab = _swa_head_order(rel_bias[:, N_HEADS:], 1, per_head=1)
    t = MOBA_BLOCK
    d_own = jnp.arange(t)[:, None] - jnp.arange(t)[None, :]
    own = jnp.where(d_own[None] >= 0, _bucket_lookup(moba_tab, d_own), _NEG_INF)
    prev = _bucket_lookup(moba_tab, d_own + t)
    btab = jnp.swapaxes(jnp.stack([own, prev], axis=1), 2, 3) * LOG2_E
    far = _bucket_lookup(moba_tab, jnp.full((1,), t + 1, jnp.int32))[:, 0] * LOG2_E

    w = SWA_WINDOW
    dist = jnp.arange(w)[:, None] + w - jnp.arange(2 * w)[None, :]
    in_band = (dist >= 0) & (dist < w)
    swa = jnp.where(in_band[None], _bucket_lookup(swa_tab, dist) * LOG2_E, _NEG_INF)
    return btab, far, swa


def kernel(x, norm_w, w_in, w_proj_a, w_proj_b, w_proj_c, w_out, sinks, rel_bias, final_norm_w):
    batch, seq, _ = x.shape
    depth = w_in.shape[0]
    assert seq % MOBA_BLOCK == 0 and MOBA_GROUP <= seq // MOBA_BLOCK <= MOBA_NEVER
    btab, far, swa_bias = _bias_tables(rel_bias)
    t = SB_TILE
    tri = (jnp.arange(t)[:, None] > jnp.arange(t)[None, :]).astype(jnp.bfloat16)
    tri = jnp.concatenate([tri, tri], axis=0)
    lane = jnp.arange(LANES)[None, :]
    blk = jnp.arange(MOBA_ZERO_ROW + 1)[:, None]
    block_lanes = ((lane == blk) | (lane == blk + MOBA_LO_LANE)) & (blk < MOBA_ZERO_ROW)
    block_lanes = jnp.broadcast_to(block_lanes[:, None, :],
                                   (MOBA_ZERO_ROW + 1, BF16_SUBLANES, LANES)).astype(jnp.bfloat16)
    sinks_perm = _swa_head_order(sinks.astype(jnp.float32), 1, per_head=1) * LOG2_E

    w_u = _permute_input_columns(w_in)
    norm_w3 = norm_w.reshape(depth, 1, D_MODEL)
    wa = w_proj_a.astype(jnp.bfloat16)
    wb = w_proj_b.astype(jnp.bfloat16)
    wc = _swa_head_order(w_proj_c, 1).astype(jnp.bfloat16)
    wo = w_out.astype(jnp.bfloat16)
    fw = final_norm_w.reshape(1, D_MODEL)

    x2 = x.reshape(batch * seq, D_MODEL)
    for layer in range(depth):
        u = _inproj(x2, norm_w3, w_u, layer)
        ya = _sb_attention(u, tri, batch, seq)
        yb = _moba_attention(u, far, btab, block_lanes, batch, seq)
        yc = _swa_attention(u, sinks_perm[layer], swa_bias, batch, seq)
        x2 = _post(x2, ya, yb, yc, u, wa, wb, wc, wo, fw, layer, final=(layer == depth - 1))
    return x2.reshape(batch, seq, D_MODEL)
```

```python
import functools
import math

import jax
import jax.numpy as jnp
from jax import lax
from jax.experimental import pallas as pl
from jax.experimental.pallas import tpu as pltpu

D_MODEL = 1024
HEAD_DIM = 64
N_HEADS = 8
WIDTH = N_HEADS * HEAD_DIM
MOBA_BLOCK = 256
MOBA_TOPK = 3
SWA_KV_HEADS = 2
SWA_GROUP = N_HEADS // SWA_KV_HEADS
SWA_WINDOW = 128
REL_BUCKETS = 32
REL_MAX_DIST = 128
RMS_EPS = 1e-6

LANES = 128
BF16_SUBLANES = 16
F32_SUBLANES = 8
HEADS_PER_TILE = LANES // HEAD_DIM
N_PAIRS = N_HEADS // HEADS_PER_TILE

COL_QA, COL_KA, COL_VA = 0, 512, 1024
COL_QB, COL_KB, COL_VB = 1536, 2048, 2560
COL_QC = 3072
COL_GA, COL_GB, COL_GC = 3584, 4096, 4608
COL_MA, COL_MB, COL_MC = 5120, 6144, 7168
COL_KC, COL_VC = 8192, 8320
D_IN = 8448

V7X_VMEM_BYTES = 64 * 1024 * 1024
VMEM_LIMIT = V7X_VMEM_BYTES * 3 // 4

LOG2_E = math.log2(math.e)

SB_DEAD_LOG_WEIGHT = -120.0
SB_DEAD_LOG2_WEIGHT = SB_DEAD_LOG_WEIGHT * LOG2_E

_NEG_INF = float("-inf")


def _rel_bucket(dist):
    max_exact = REL_BUCKETS // 2
    n = jnp.maximum(dist, 0)
    nf = jnp.maximum(n, 1).astype(jnp.float32)
    large = max_exact + (jnp.log(nf / max_exact) / math.log(REL_MAX_DIST / max_exact)
                         * (REL_BUCKETS - max_exact)).astype(jnp.int32)
    large = jnp.minimum(large, REL_BUCKETS - 1)
    return jnp.where(n < max_exact, n, large)


def _swa_head_order(t, axis, per_head=HEAD_DIM):
    axis = axis % t.ndim
    shape = t.shape
    t = t.reshape(shape[:axis] + (SWA_KV_HEADS, SWA_GROUP, per_head) + shape[axis + 1:])
    return jnp.swapaxes(t, axis, axis + 1).reshape(shape)


def _permute_input_columns(w):
    old = {"qa": 0, "ka": 512, "va": 1024, "ga": 1536, "qb": 2048, "kb": 2560, "vb": 3072, "gb": 3584,
           "qc": 4096, "kc": 4608, "vc": 4736, "gc": 4864, "ma": 5376, "mb": 6400, "mc": 7424}
    q_scale = HEAD_DIM ** -0.5 * LOG2_E

    def cols(name, width, scale=None):
        part = w[..., old[name]:old[name] + width]
        return part if scale is None else part * scale

    parts = [cols("qa", 512, q_scale), cols("ka", 512), cols("va", 512),
             cols("qb", 512, q_scale), cols("kb", 512), cols("vb", 512),
             _swa_head_order(cols("qc", WIDTH, q_scale), -1),
             cols("ga", 512), cols("gb", 512), _swa_head_order(cols("gc", WIDTH), -1),
             cols("ma", 1024), cols("mb", 1024), cols("mc", 1024),
             cols("kc", 128), cols("vc", 128)]
    return jnp.concatenate(parts, axis=-1).astype(jnp.bfloat16)


def _inproj_kernel(x_ref, nw_ref, w_ref, u_ref, h_ref):
    @pl.when(pl.program_id(1) == 0)
    def _():
        x = x_ref[...]
        y = x * lax.rsqrt(jnp.mean(x * x, axis=-1, keepdims=True) + RMS_EPS)
        h_ref[...] = (y * nw_ref[...]).astype(h_ref.dtype)

    u_ref[...] = jnp.dot(h_ref[...], w_ref[...], preferred_element_type=jnp.float32).astype(u_ref.dtype)


def _inproj(x2, norm_w, w, layer, *, tm=1024, tn=2816):
    n = x2.shape[0]
    return pl.pallas_call(
        _inproj_kernel,
        grid=(n // tm, D_IN // tn),
        in_specs=[pl.BlockSpec((tm, D_MODEL), lambda i, j: (i, 0)),
                  pl.BlockSpec((None, 1, D_MODEL), lambda i, j: (layer, 0, 0)),
                  pl.BlockSpec((None, D_MODEL, tn), lambda i, j: (layer, 0, j))],
        out_specs=pl.BlockSpec((tm, tn), lambda i, j: (i, j)),
        out_shape=jax.ShapeDtypeStruct((n, D_IN), jnp.bfloat16),
        scratch_shapes=[pltpu.VMEM((tm, D_MODEL), jnp.bfloat16)],
        compiler_params=pltpu.CompilerParams(dimension_semantics=("parallel", "arbitrary"),
                                             vmem_limit_bytes=VMEM_LIMIT),
        name="inproj",
    )(x2, norm_w, w)


def _head_lane_mask(hh, dtype):
    lane = lax.broadcasted_iota(jnp.int32, (1, LANES), 1)
    return ((lane // HEAD_DIM) == hh).astype(dtype)


def _qk(qm, k):
    return lax.dot_general(qm, k, (((1,), (1,)), ((), ())), preferred_element_type=jnp.float32)


def _merge_heads(parts):
    lane = lax.broadcasted_iota(jnp.int32, parts[0].shape, 1)
    return jnp.where(lane < HEAD_DIM, parts[0], parts[1])


SB_TILE = 256
SB_TILES_PER_STEP = 4
SB_PAIR = 4
SB_ROW_CHUNK = 256


def _sb_kernel(q_ref, k_ref, v_ref, tri_ref, o_ref, acc_ref, carry_ref):
    t = SB_TILE

    def step(tiles):
        tri = tri_ref[...]
        rc = SB_ROW_CHUNK
        chains = [(tile, hh, r) for tile in tiles for hh in range(HEADS_PER_TILE) for r in range(t // rc)]

        def block_start(tile, j):
            return pl.multiple_of((tile[2] - j) * t, t)

        terms = {}
        for tile, hh, r in chains:
            slot, row0, _, n_blocks, diagonal = tile
            qm = q_ref[pl.ds(row0 + r * rc, rc), :] * _head_lane_mask(hh, q_ref.dtype)
            for j in range(n_blocks):
                z = _qk(qm, k_ref[pl.ds(block_start(tile, j), t), :])
                neg_abs = lax.bitcast_convert_type(
                    lax.bitcast_convert_type(z, jnp.uint32) | jnp.uint32(0x80000000), jnp.float32)
                soft = jnp.log(1.0 + jnp.exp2(neg_abs)) * LOG2_E
                log_beta = jnp.minimum(z, 0.0) - soft
                log_1m = log_beta - z
                mask = None
                if diagonal and j == 0:
                    row = lax.broadcasted_iota(jnp.int32, (rc, t), 0) + r * rc
                    mask = lax.broadcasted_iota(jnp.int32, (rc, t), 1) < row
                    log_1m = jnp.where(mask, log_1m, 0.0)
                hi = lax.bitcast_convert_type(
                    lax.bitcast_convert_type(log_1m, jnp.uint32) & jnp.uint32(0xFFFF0000), jnp.float32)
                split = jnp.concatenate([hi.astype(jnp.bfloat16), (log_1m - hi).astype(jnp.bfloat16)], axis=1)
                terms[slot, hh, r, j] = (log_beta, log_1m[:, 0:1], split, mask)

        cums = {key: jnp.dot(split, tri, preferred_element_type=jnp.float32)
                for key, (_, _, split, _) in terms.items()}

        alive = {}
        for tile, hh, r in chains:
            slot, _, _, n_blocks, _ = tile
            rows = slice(r * rc, (r + 1) * rc)
            carry = carry_ref[slot, hh, rows, :]
            acc = acc_ref[slot, hh, rows, :]
            for j in range(n_blocks):
                log_beta, first_term, _, mask = terms[slot, hh, r, j]
                cum = cums[slot, hh, r, j]
                w = jnp.exp2(log_beta + cum + jnp.concatenate([carry] * (t // LANES), axis=1))
                if mask is not None:
                    w = jnp.where(mask, w, 0.0)
                acc = acc + jnp.dot(w.astype(jnp.bfloat16), v_ref[pl.ds(block_start(tile, j), t), :],
                                    preferred_element_type=jnp.float32)
                carry = carry + jnp.broadcast_to(cum[:, 0:1] + first_term, (rc, LANES))
            acc_ref[slot, hh, rows, :] = acc
            carry_ref[slot, hh, rows, :] = carry
            alive[slot] = carry if slot not in alive else jnp.maximum(alive[slot], carry)
        return tuple((jnp.max(alive[tile[0]]) < SB_DEAD_LOG2_WEIGHT).astype(jnp.int32) for tile in tiles)

    def query_tiles(pair, c):
        qi = [pl.program_id(2) * SB_TILES_PER_STEP + SB_PAIR * pair + s for s in range(SB_PAIR)]
        row0 = [pl.multiple_of((SB_PAIR * pair + s) * t, t) for s in range(SB_PAIR)]
        acc_ref[...] = jnp.zeros_like(acc_ref)
        carry_ref[...] = jnp.zeros_like(carry_ref)

        def first(n_first):
            return step([(0, row0[0], qi[0], n_first, True)]
                        + [(s, row0[s], qi[s], 2, True) for s in range(1, SB_PAIR)])

        dead = lax.cond(qi[0] >= 1, lambda: first(2), lambda: first(1))

        for s in range(SB_PAIR):
            def cond(c):
                newest, dead = c
                return jnp.logical_and(newest >= 1, dead == 0)

            def body(c, s=s):
                newest, _ = c
                return newest - 2, step([(s, row0[s], newest, 2, False)])[0]

            newest, still = lax.while_loop(cond, body, (qi[s] - 2, dead[s]))

            @pl.when(jnp.logical_and(newest == 0, still == 0))
            def _(s=s):
                step([(s, row0[s], 0, 1, False)])

            o_ref[pl.ds(row0[s], t), :] = _merge_heads([acc_ref[s, 0], acc_ref[s, 1]])
        return c

    lax.fori_loop(0, SB_TILES_PER_STEP // SB_PAIR, query_tiles, 0)


def _sb_attention(u, tri, batch, seq):
    t = SB_TILE
    rows = SB_TILES_PER_STEP * t
    nq = seq // rows
    return pl.pallas_call(
        _sb_kernel,
        grid=(batch, N_PAIRS, nq),
        in_specs=[pl.BlockSpec((rows, LANES), lambda b, p, i: (b * nq + i, COL_QA // LANES + p)),
                  pl.BlockSpec((seq, LANES), lambda b, p, i: (b, COL_KA // LANES + p)),
                  pl.BlockSpec((seq, LANES), lambda b, p, i: (b, COL_VA // LANES + p)),
                  pl.BlockSpec((2 * t, t), lambda b, p, i: (0, 0))],
        out_specs=pl.BlockSpec((rows, LANES), lambda b, p, i: (b * nq + i, p)),
        out_shape=jax.ShapeDtypeStruct((batch * seq, WIDTH), jnp.float32),
        scratch_shapes=[pltpu.VMEM((SB_PAIR, HEADS_PER_TILE, t, LANES), jnp.float32),
                        pltpu.VMEM((SB_PAIR, HEADS_PER_TILE, t, LANES), jnp.float32)],
        compiler_params=pltpu.CompilerParams(dimension_semantics=("parallel", "parallel", "arbitrary"),
                                             vmem_limit_bytes=VMEM_LIMIT),
        name="sb_attention",
    )(u, u, u, tri)


MOBA_LO_LANE = 64
MOBA_NEVER = 63
MOBA_ZERO_ROW = 64
MOBA_MASKED = -1e30
MOBA_GROUP = 8
MOBA_BLOCKS_PER_STEP = 4


def _moba_kernel(far_ref, q_ref, k_ref, v_ref, btab_ref, oh_ref, o_ref,
                 kmean_ref, vt_ref, qa_ref, m_ref, acc_ref, *, nb):
    t = MOBA_BLOCK
    nsub = MOBA_BLOCKS_PER_STEP
    nh = HEADS_PER_TILE
    heads = range(nh)
    pair = pl.program_id(1)

    @pl.when(pl.program_id(2) == 0)
    def _():
        kmean_ref[...] = jnp.zeros_like(kmean_ref)
        kf = k_ref[...].astype(jnp.float32).reshape(nb, t, LANES)
        kmean_ref[0:nb, :] = jnp.mean(kf, axis=1)

        head_of_row = lax.broadcasted_iota(jnp.int32, (LANES, t), 0) // HEAD_DIM

        def transpose_block(b, c):
            vb = v_ref[pl.ds(pl.multiple_of(b * t, t), t), :].astype(jnp.float32).T
            for hh in heads:
                vt_ref[hh, b] = jnp.where(head_of_row == hh, vb, 1.0).astype(jnp.bfloat16)
            return c

        lax.fori_loop(0, nb, transpose_block, 0)

    rows = -(-nb // F32_SUBLANES) * F32_SUBLANES
    km = kmean_ref[0:rows, :]
    km0 = km.astype(jnp.bfloat16)
    r1 = km - km0.astype(jnp.float32)
    km1 = r1.astype(jnp.bfloat16)
    km2 = (r1 - km1.astype(jnp.float32)).astype(jnp.bfloat16)

    cols = nsub * t
    blk = lax.broadcasted_iota(jnp.int32, (rows, nh * cols), 0)
    blk_f = blk.astype(jnp.float32)
    col = lax.broadcasted_iota(jnp.int32, (1, nh * cols), 1)
    own_col = pl.program_id(2) * nsub + (col % cols) // t
    gate_rows = sum(_qk(jnp.concatenate([part * _head_lane_mask(hh, part.dtype) for hh in heads], axis=0),
                        q_ref[...]) for part in (km0, km1, km2))
    gate = jnp.concatenate([gate_rows[hh * rows:(hh + 1) * rows] for hh in heads], axis=1)
    gate = jnp.where(blk < own_col, gate, _NEG_INF)
    sel = jnp.zeros_like(gate)
    for _ in range(MOBA_TOPK):
        best = jnp.max(gate, axis=0, keepdims=True)
        idx = jnp.min(jnp.where(gate == best, blk_f, float(LANES)), axis=0, keepdims=True)
        hit = blk_f == idx
        sel = jnp.where(jnp.logical_and(hit, best > _NEG_INF), 1.0, sel)
        gate = jnp.where(hit, _NEG_INF, gate)

    far_bias = jnp.where(col // cols == 0, far_ref[pair * nh], far_ref[pair * nh + 1])
    term = jnp.where(sel > 0.0, jnp.where(blk < own_col - 1, far_bias, 0.0), MOBA_MASKED)
    hi = term.astype(jnp.bfloat16).astype(jnp.float32)
    lo = term - hi
    pad = MOBA_LO_LANE - rows
    pad_hi = [jnp.full((pad, t), MOBA_MASKED, jnp.float32)] if pad else []
    pad_lo = [jnp.zeros((pad, t), jnp.float32)] if pad else []
    for hh in heads:
        for sub in range(nsub):
            cs = slice(hh * cols + sub * t, hh * cols + (sub + 1) * t)
            inj = jnp.concatenate([hi[:, cs]] + pad_hi + [lo[:, cs]] + pad_lo, axis=0).T
            q_sub = q_ref[sub * t:(sub + 1) * t, :] * _head_lane_mask(hh, q_ref.dtype)
            qa_ref[hh, sub] = jnp.concatenate([q_sub, inj.astype(jnp.bfloat16)], axis=1)

    def query_block(sub, c):
        _moba_query_block(sub, k_ref, btab_ref, oh_ref, vt_ref, qa_ref, m_ref, acc_ref, nb=nb)
        return c

    lax.fori_loop(0, nsub, query_block, 0)

    for sub in range(nsub):
        out_t = jnp.concatenate([acc_ref[sub, 0][:HEAD_DIM] / acc_ref[sub, 0][HEAD_DIM:],
                                 acc_ref[sub, 1][HEAD_DIM:] / acc_ref[sub, 1][:HEAD_DIM]], axis=0)
        o_ref[sub * t:(sub + 1) * t, :] = out_t.T


def _moba_query_block(sub, k_ref, btab_ref, oh_ref, vt_ref, qa_ref, m_ref, acc_ref, *, nb):
    t = MOBA_BLOCK
    g = MOBA_GROUP
    own = pl.program_id(2) * MOBA_BLOCKS_PER_STEP + sub
    heads = range(HEADS_PER_TILE)

    def update(scores, blocks, first):
        col_max = [jnp.max(s, axis=0, keepdims=True) for s in scores]
        if first:
            m_new = col_max
        else:
            m_old = [m_ref[hh][0:1, :] for hh in heads]
            m_new = [jnp.maximum(m_old[hh], col_max[hh]) for hh in heads]
            alpha = [jnp.exp2(m_old[hh] - m_new[hh]) for hh in heads]
        p = [jnp.exp2(scores[hh] - m_new[hh]).astype(jnp.bfloat16) for hh in heads]
        pv = [jnp.dot(jnp.concatenate([vt_ref[hh, b] for b in blocks], axis=1), p[hh],
                      preferred_element_type=jnp.float32) for hh in heads]
        for hh in heads:
            acc_ref[sub, hh] = pv[hh] if first else alpha[hh] * acc_ref[sub, hh] + pv[hh]
            m_ref[hh] = jnp.broadcast_to(m_new[hh], m_ref.shape[1:])

    def block_lanes(row):
        return jnp.concatenate([oh_ref[row]] * (t // BF16_SUBLANES), axis=0)

    prev = jnp.maximum(own - 1, 0)
    n_far = jnp.maximum(own - 1, 0)
    n_full = n_far // g
    left = n_far - n_full * g

    def far_keys(first_blk, group):
        g0 = jnp.minimum(first_blk, nb - group)
        rows = []
        for j in range(group):
            b = g0 + j
            active = jnp.logical_and(b >= first_blk, b < n_far)
            rows.append(block_lanes(jnp.where(active, b, MOBA_NEVER)))
        keys = k_ref[pl.ds(pl.multiple_of(g0 * t, t), group * t), :]
        return g0, keys, rows

    def first_step(group):
        keys = [k_ref[pl.ds(pl.multiple_of(own * t, t), t), :], k_ref[pl.ds(pl.multiple_of(prev * t, t), t), :]]
        rows = [block_lanes(MOBA_ZERO_ROW), block_lanes(jnp.where(own >= 1, prev, MOBA_NEVER))]
        if group:
            g0, far, far_rows = far_keys(n_full * g, group)
            keys.append(far)
            rows.extend(far_rows)
        k_aug = jnp.concatenate([jnp.concatenate(keys, axis=0), jnp.concatenate(rows, axis=0)], axis=1)
        scores = [_qk(k_aug, qa_ref[hh, sub]) for hh in heads]
        update([scores[hh][:t] + btab_ref[hh, 0] for hh in heads], [own], first=True)
        update([scores[hh][t:2 * t] + btab_ref[hh, 1] for hh in heads], [prev], first=False)
        for j in range(group):
            update([s[(2 + j) * t:(3 + j) * t] for s in scores], [g0 + j], first=False)

    for count in range(g):
        @pl.when(left == count)
        def _(count=count):
            first_step(count)

    def far_body(i, c):
        g0, keys, rows = far_keys(i * g, g)
        k_aug = jnp.concatenate([keys, jnp.concatenate(rows, axis=0)], axis=1)
        scores = [_qk(k_aug, qa_ref[hh, sub]) for hh in heads]
        for j in range(g):
            update([s[j * t:(j + 1) * t] for s in scores], [g0 + j], first=False)
        return c

    lax.fori_loop(0, n_full, far_body, 0)


def _moba_attention(u, far_bias, btab, block_lanes, batch, seq):
    t = MOBA_BLOCK
    nb = seq // t
    rows = MOBA_BLOCKS_PER_STEP * t
    nstep = seq // rows
    kernel = functools.partial(_moba_kernel, nb=nb)
    grid_spec = pltpu.PrefetchScalarGridSpec(
        num_scalar_prefetch=1,
        grid=(batch, N_PAIRS, nstep),
        in_specs=[pl.BlockSpec((rows, LANES), lambda b, p, i, far: (b * nstep + i, COL_QB // LANES + p)),
                  pl.BlockSpec((seq, LANES), lambda b, p, i, far: (b, COL_KB // LANES + p)),
                  pl.BlockSpec((seq, LANES), lambda b, p, i, far: (b, COL_VB // LANES + p)),
                  pl.BlockSpec((HEADS_PER_TILE, 2, t, t), lambda b, p, i, far: (p, 0, 0, 0)),
                  pl.BlockSpec(block_lanes.shape, lambda b, p, i, far: (0, 0, 0))],
        out_specs=pl.BlockSpec((rows, LANES), lambda b, p, i, far: (b * nstep + i, p)),
        scratch_shapes=[pltpu.VMEM((MOBA_LO_LANE, LANES), jnp.float32),
                        pltpu.VMEM((HEADS_PER_TILE, nb, LANES, t), jnp.bfloat16),
                        pltpu.VMEM((HEADS_PER_TILE, MOBA_BLOCKS_PER_STEP, t, 2 * LANES), jnp.bfloat16),
                        pltpu.VMEM((HEADS_PER_TILE, F32_SUBLANES, t), jnp.float32),
                        pltpu.VMEM((MOBA_BLOCKS_PER_STEP, HEADS_PER_TILE, LANES, t), jnp.float32)])
    return pl.pallas_call(
        kernel,
        grid_spec=grid_spec,
        out_shape=jax.ShapeDtypeStruct((batch * seq, WIDTH), jnp.float32),
        compiler_params=pltpu.CompilerParams(dimension_semantics=("parallel", "parallel", "arbitrary"),
                                             vmem_limit_bytes=VMEM_LIMIT),
        name="moba_attention",
    )(far_bias, u, u, u, btab, block_lanes)


SWA_STEP_WINDOWS = 2


def _swa_kernel(sink_ref, q_ref, kp_ref, kc_ref, vp_ref, vc_ref, bias_ref, o_ref):
    w = SWA_WINDOW
    keys = jnp.concatenate([kp_ref[...], kc_ref[...]], axis=0)
    values = jnp.concatenate([vp_ref[...], vc_ref[...]], axis=0)
    slots = [(j, half) for j in range(SWA_GROUP) for half in range(SWA_KV_HEADS)]
    chains = [(win, slot) for win in range(SWA_STEP_WINDOWS) for slot in range(len(slots))]

    def attend(first_step):
        scores = []
        for win, slot in chains:
            j, half = slots[slot]
            q = q_ref[win * w:(win + 1) * w, j * LANES:(j + 1) * LANES]
            s = _qk(q * _head_lane_mask(half, q.dtype), keys[win * w:(win + 2) * w]) + bias_ref[slot]
            if first_step and win == 0:
                s = jnp.where(lax.broadcasted_iota(jnp.int32, (w, 2 * w), 1) >= w, s, _NEG_INF)
            scores.append(s)
        m = [jnp.maximum(jnp.broadcast_to(jnp.max(s, axis=1, keepdims=True), (w, LANES)), sink_ref[slot])
             for (win, slot), s in zip(chains, scores)]
        p = [jnp.exp2(s - jnp.concatenate([mi] * (2 * w // LANES), axis=1)).astype(jnp.bfloat16)
             for mi, s in zip(m, scores)]
        pv = []
        for (win, slot), pi in zip(chains, p):
            vw = values[win * w:(win + 2) * w]
            vw = jnp.where(_head_lane_mask(slots[slot][1], jnp.int32) > 0, vw, jnp.ones_like(vw))
            pv.append(jnp.dot(pi, vw, preferred_element_type=jnp.float32))
        out = [pvi / (pltpu.roll(pvi, HEAD_DIM, axis=1) + jnp.exp2(sink_ref[slot] - mi))
               for (win, slot), pvi, mi in zip(chains, pv, m)]
        for win in range(SWA_STEP_WINDOWS):
            for j in range(SWA_GROUP):
                first = win * len(slots) + j * SWA_KV_HEADS
                o_ref[win * w:(win + 1) * w, j * LANES:(j + 1) * LANES] = _merge_heads(
                    out[first:first + SWA_KV_HEADS])

    @pl.when(pl.program_id(1) == 0)
    def _():
        attend(True)

    @pl.when(pl.program_id(1) > 0)
    def _():
        attend(False)


def _swa_attention(u, sinks_perm, bias, batch, seq):
    w = SWA_WINDOW
    sw = SWA_STEP_WINDOWS
    nstep = seq // (sw * w)
    kcol, vcol = COL_KC // LANES, COL_VC // LANES
    prev_window = lambda b, n, s: b * (seq // w) + jnp.maximum(sw * n - 1, 0)
    grid_spec = pltpu.PrefetchScalarGridSpec(
        num_scalar_prefetch=1,
        grid=(batch, nstep),
        in_specs=[pl.BlockSpec((sw * w, WIDTH), lambda b, n, s: (b * nstep + n, COL_QC // WIDTH)),
                  pl.BlockSpec((w, LANES), lambda b, n, s: (prev_window(b, n, s), kcol)),
                  pl.BlockSpec((sw * w, LANES), lambda b, n, s: (b * nstep + n, kcol)),
                  pl.BlockSpec((w, LANES), lambda b, n, s: (prev_window(b, n, s), vcol)),
                  pl.BlockSpec((sw * w, LANES), lambda b, n, s: (b * nstep + n, vcol)),
                  pl.BlockSpec((N_HEADS, w, 2 * w), lambda b, n, s: (0, 0, 0))],
        out_specs=pl.BlockSpec((sw * w, WIDTH), lambda b, n, s: (b * nstep + n, 0)))
    return pl.pallas_call(
        _swa_kernel,
        grid_spec=grid_spec,
        out_shape=jax.ShapeDtypeStruct((batch * seq, WIDTH), jnp.float32),
        compiler_params=pltpu.CompilerParams(dimension_semantics=("parallel", "arbitrary"),
                                             vmem_limit_bytes=VMEM_LIMIT),
        name="swa_attention",
    )(sinks_perm, u, u, u, u, u, bias)


def _post_kernel(x_ref, ya_ref, yb_ref, yc_ref, ga_ref, gb_ref, gc_ref, ma_ref, mb_ref, mc_ref,
                 wa_ref, wb_ref, wc_ref, wo_ref, fw_ref, o_ref, *, final):
    def branch(y_ref, g_ref, w_ref, m_ref):
        g = g_ref[...].astype(jnp.float32)
        y = (y_ref[...] * (g * jax.nn.sigmoid(g))).astype(jnp.bfloat16)
        proj = jnp.dot(y, w_ref[...], preferred_element_type=jnp.float32)
        return jax.nn.sigmoid(m_ref[...].astype(jnp.float32)) * proj

    merged = (branch(ya_ref, ga_ref, wa_ref, ma_ref) + branch(yb_ref, gb_ref, wb_ref, mb_ref)
              + branch(yc_ref, gc_ref, wc_ref, mc_ref))
    out = x_ref[...] + jnp.dot(merged.astype(jnp.bfloat16), wo_ref[...], preferred_element_type=jnp.float32)
    if final:
        out = out * lax.rsqrt(jnp.mean(out * out, axis=-1, keepdims=True) + RMS_EPS) * fw_ref[...]
    o_ref[...] = out


def _post(x2, ya, yb, yc, u, wa, wb, wc, wo, fw, layer, *, final, tm=512):
    n = x2.shape[0]
    row = lambda i: (i, 0)
    full = lambda i: (0, 0)
    of_layer = lambda i: (layer, 0, 0)
    y_spec = pl.BlockSpec((tm, WIDTH), row)
    in_specs = [pl.BlockSpec((tm, D_MODEL), row), y_spec, y_spec, y_spec,
                pl.BlockSpec((tm, WIDTH), lambda i: (i, COL_GA // WIDTH)),
                pl.BlockSpec((tm, WIDTH), lambda i: (i, COL_GB // WIDTH)),
                pl.BlockSpec((tm, WIDTH), lambda i: (i, COL_GC // WIDTH)),
                pl.BlockSpec((tm, D_MODEL), lambda i: (i, COL_MA // D_MODEL)),
                pl.BlockSpec((tm, D_MODEL), lambda i: (i, COL_MB // D_MODEL)),
                pl.BlockSpec((tm, D_MODEL), lambda i: (i, COL_MC // D_MODEL)),
                pl.BlockSpec((None, WIDTH, D_MODEL), of_layer), pl.BlockSpec((None, WIDTH, D_MODEL), of_layer),
                pl.BlockSpec((None, WIDTH, D_MODEL), of_layer), pl.BlockSpec((None, D_MODEL, D_MODEL), of_layer),
                pl.BlockSpec((1, D_MODEL), full)]
    return pl.pallas_call(
        functools.partial(_post_kernel, final=final),
        grid=(n // tm,),
        in_specs=in_specs,
        out_specs=pl.BlockSpec((tm, D_MODEL), row),
        out_shape=jax.ShapeDtypeStruct((n, D_MODEL), jnp.float32),
        compiler_params=pltpu.CompilerParams(dimension_semantics=("parallel",),
                                             vmem_limit_bytes=VMEM_LIMIT),
        name="post_final" if final else "post",
    )(x2, ya, yb, yc, u, u, u, u, u, u, wa, wb, wc, wo, fw)


def _bucket_lookup(table, dist):
    onehot = (_rel_bucket(dist)[..., None] == jnp.arange(REL_BUCKETS)).astype(jnp.float32)
    return jnp.einsum("...b,bh->h...", onehot, table.astype(jnp.float32), precision=lax.Precision.HIGHEST)


def _bias_tables(rel_bias):
    moba_tab = rel_bias[:, :N_HEADS]
    swa_tab = _swa_head_order(rel_bias[:, N_HEADS:], 1, per_head=1)
    t = MOBA_BLOCK
    d_own = jnp.arange(t)[:, None] - jnp.arange(t)[None, :]
    own = jnp.where(d_own[None] >= 0, _bucket_lookup(moba_tab, d_own), _NEG_INF)
    prev = _bucket_lookup(moba_tab, d_own + t)
    btab = jnp.swapaxes(jnp.stack([own, prev], axis=1), 2, 3) * LOG2_E
    far = _bucket_lookup(moba_tab, jnp.full((1,), t + 1, jnp.int32))[:, 0] * LOG2_E

    w = SWA_WINDOW
    dist = jnp.arange(w)[:, None] + w - jnp.arange(2 * w)[None, :]
    in_band = (dist >= 0) & (dist < w)
    swa = jnp.where(in_band[None], _bucket_lookup(swa_tab, dist) * LOG2_E, _NEG_INF)
    return btab, far, swa


def kernel(x, norm_w, w_in, w_proj_a, w_proj_b, w_proj_c, w_out, sinks, rel_bias, final_norm_w):
    batch, seq, _ = x.shape
    depth = w_in.shape[0]
    assert seq % MOBA_BLOCK == 0 and MOBA_GROUP <= seq // MOBA_BLOCK <= MOBA_NEVER
    btab, far, swa_bias = _bias_tables(rel_bias)
    t = SB_TILE
    tri = (jnp.arange(t)[:, None] > jnp.arange(t)[None, :]).astype(jnp.bfloat16)
    tri = jnp.concatenate([tri, tri], axis=0)
    lane = jnp.arange(LANES)[None, :]
    blk = jnp.arange(MOBA_ZERO_ROW + 1)[:, None]
    block_lanes = ((lane == blk) | (lane == blk + MOBA_LO_LANE)) & (blk < MOBA_ZERO_ROW)
    block_lanes = jnp.broadcast_to(block_lanes[:, None, :],
                                   (MOBA_ZERO_ROW + 1, BF16_SUBLANES, LANES)).astype(jnp.bfloat16)
    sinks_perm = _swa_head_order(sinks.astype(jnp.float32), 1, per_head=1) * LOG2_E

    w_u = _permute_input_columns(w_in)
    norm_w3 = norm_w.reshape(depth, 1, D_MODEL)
    wa = w_proj_a.astype(jnp.bfloat16)
    wb = w_proj_b.astype(jnp.bfloat16)
    wc = _swa_head_order(w_proj_c, 1).astype(jnp.bfloat16)
    wo = w_out.astype(jnp.bfloat16)
    fw = final_norm_w.reshape(1, D_MODEL)

    x2 = x.reshape(batch * seq, D_MODEL)
    for layer in range(depth):
        u = _inproj(x2, norm_w3, w_u, layer)
        ya = _sb_attention(u, tri, batch, seq)
        yb = _moba_attention(u, far, btab, block_lanes, batch, seq)
        yc = _swa_attention(u, sinks_perm[layer], swa_bias, batch, seq)
        x2 = _post(x2, ya, yb, yc, u, wa, wb, wc, wo, fw, layer, final=(layer == depth - 1))
    return x2.reshape(batch, seq, D_MODEL)
```

```python
import functools
import math

import jax
import jax.numpy as jnp
from jax import lax
from jax.experimental import pallas as pl
from jax.experimental.pallas import tpu as pltpu

D_MODEL = 1024
HEAD_DIM = 64
N_HEADS = 8
WIDTH = N_HEADS * HEAD_DIM
MOBA_BLOCK = 256
MOBA_TOPK = 3
SWA_KV_HEADS = 2
SWA_GROUP = N_HEADS // SWA_KV_HEADS
SWA_WINDOW = 128
REL_BUCKETS = 32
REL_MAX_DIST = 128
RMS_EPS = 1e-6

LANES = 128
BF16_SUBLANES = 16
F32_SUBLANES = 8
HEADS_PER_TILE = LANES // HEAD_DIM
N_PAIRS = N_HEADS // HEADS_PER_TILE

COL_QA, COL_KA, COL_VA = 0, 512, 1024
COL_QB, COL_KB, COL_VB = 1536, 2048, 2560
COL_QC = 3072
COL_GA, COL_GB, COL_GC = 3584, 4096, 4608
COL_MA, COL_MB, COL_MC = 5120, 6144, 7168
COL_KC, COL_VC = 8192, 8320
D_IN = 8448

V7X_VMEM_BYTES = 64 * 1024 * 1024
VMEM_LIMIT = V7X_VMEM_BYTES * 3 // 4

LOG2_E = math.log2(math.e)

SB_DEAD_LOG_WEIGHT = -120.0
SB_DEAD_LOG2_WEIGHT = SB_DEAD_LOG_WEIGHT * LOG2_E

_NEG_INF = float("-inf")


def _rel_bucket(dist):
    max_exact = REL_BUCKETS // 2
    n = jnp.maximum(dist, 0)
    nf = jnp.maximum(n, 1).astype(jnp.float32)
    large = max_exact + (jnp.log(nf / max_exact) / math.log(REL_MAX_DIST / max_exact)
                         * (REL_BUCKETS - max_exact)).astype(jnp.int32)
    large = jnp.minimum(large, REL_BUCKETS - 1)
    return jnp.where(n < max_exact, n, large)


def _swa_head_order(t, axis, per_head=HEAD_DIM):
    axis = axis % t.ndim
    shape = t.shape
    t = t.reshape(shape[:axis] + (SWA_KV_HEADS, SWA_GROUP, per_head) + shape[axis + 1:])
    return jnp.swapaxes(t, axis, axis + 1).reshape(shape)


def _permute_input_columns(w):
    old = {"qa": 0, "ka": 512, "va": 1024, "ga": 1536, "qb": 2048, "kb": 2560, "vb": 3072, "gb": 3584,
           "qc": 4096, "kc": 4608, "vc": 4736, "gc": 4864, "ma": 5376, "mb": 6400, "mc": 7424}
    q_scale = HEAD_DIM ** -0.5 * LOG2_E

    def cols(name, width, scale=None):
        part = w[..., old[name]:old[name] + width]
        return part if scale is None else part * scale

    parts = [cols("qa", 512, q_scale), cols("ka", 512), cols("va", 512),
             cols("qb", 512, q_scale), cols("kb", 512), cols("vb", 512),
             _swa_head_order(cols("qc", WIDTH, q_scale), -1),
             cols("ga", 512), cols("gb", 512), _swa_head_order(cols("gc", WIDTH), -1),
             cols("ma", 1024), cols("mb", 1024), cols("mc", 1024),
             cols("kc", 128), cols("vc", 128)]
    return jnp.concatenate(parts, axis=-1).astype(jnp.bfloat16)


def _inproj_kernel(x_ref, nw_ref, w_ref, u_ref, h_ref):
    @pl.when(pl.program_id(1) == 0)
    def _():
        x = x_ref[...]
        y = x * lax.rsqrt(jnp.mean(x * x, axis=-1, keepdims=True) + RMS_EPS)
        h_ref[...] = (y * nw_ref[...]).astype(h_ref.dtype)

    u_ref[...] = jnp.dot(h_ref[...], w_ref[...], preferred_element_type=jnp.float32).astype(u_ref.dtype)


def _inproj(x2, norm_w, w, layer, *, tm=1024, tn=2816):
    n = x2.shape[0]
    return pl.pallas_call(
        _inproj_kernel,
        grid=(n // tm, D_IN // tn),
        in_specs=[pl.BlockSpec((tm, D_MODEL), lambda i, j: (i, 0)),
                  pl.BlockSpec((None, 1, D_MODEL), lambda i, j: (layer, 0, 0)),
                  pl.BlockSpec((None, D_MODEL, tn), lambda i, j: (layer, 0, j))],
        out_specs=pl.BlockSpec((tm, tn), lambda i, j: (i, j)),
        out_shape=jax.ShapeDtypeStruct((n, D_IN), jnp.bfloat16),
        scratch_shapes=[pltpu.VMEM((tm, D_MODEL), jnp.bfloat16)],
        compiler_params=pltpu.CompilerParams(dimension_semantics=("parallel", "arbitrary"),
                                             vmem_limit_bytes=VMEM_LIMIT),
        name="inproj",
    )(x2, norm_w, w)


def _head_lane_mask(hh, dtype):
    lane = lax.broadcasted_iota(jnp.int32, (1, LANES), 1)
    return ((lane // HEAD_DIM) == hh).astype(dtype)


def _qk(qm, k):
    return lax.dot_general(qm, k, (((1,), (1,)), ((), ())), preferred_element_type=jnp.float32)


def _merge_heads(parts):
    lane = lax.broadcasted_iota(jnp.int32, parts[0].shape, 1)
    return jnp.where(lane < HEAD_DIM, parts[0], parts[1])


SB_TILE = 256
SB_TILES_PER_STEP = 4
SB_PAIR = 4
SB_ROW_CHUNK = 256


def _sb_kernel(q_ref, k_ref, v_ref, tri_ref, o_ref, acc_ref, carry_ref):
    t = SB_TILE

    def step(tiles):
        tri = tri_ref[...]
        rc = SB_ROW_CHUNK
        chains = [(tile, hh, r) for tile in tiles for hh in range(HEADS_PER_TILE) for r in range(t // rc)]

        def block_start(tile, j):
            return pl.multiple_of((tile[2] - j) * t, t)

        terms = {}
        for tile, hh, r in chains:
            slot, row0, _, n_blocks, diagonal = tile
            qm = q_ref[pl.ds(row0 + r * rc, rc), :] * _head_lane_mask(hh, q_ref.dtype)
            for j in range(n_blocks):
                z = _qk(qm, k_ref[pl.ds(block_start(tile, j), t), :])
                neg_abs = lax.bitcast_convert_type(
                    lax.bitcast_convert_type(z, jnp.uint32) | jnp.uint32(0x80000000), jnp.float32)
                soft = jnp.log(1.0 + jnp.exp2(neg_abs)) * LOG2_E
                log_beta = jnp.minimum(z, 0.0) - soft
                log_1m = log_beta - z
                mask = None
                if diagonal and j == 0:
                    row = lax.broadcasted_iota(jnp.int32, (rc, t), 0) + r * rc
                    mask = lax.broadcasted_iota(jnp.int32, (rc, t), 1) < row
                    log_1m = jnp.where(mask, log_1m, 0.0)
                hi = lax.bitcast_convert_type(
                    lax.bitcast_convert_type(log_1m, jnp.uint32) & jnp.uint32(0xFFFF0000), jnp.float32)
                split = jnp.concatenate([hi.astype(jnp.bfloat16), (log_1m - hi).astype(jnp.bfloat16)], axis=1)
                terms[slot, hh, r, j] = (log_beta, log_1m[:, 0:1], split, mask)

        cums = {key: jnp.dot(split, tri, preferred_element_type=jnp.float32)
                for key, (_, _, split, _) in terms.items()}

        alive = {}
        for tile, hh, r in chains:
            slot, _, _, n_blocks, _ = tile
            rows = slice(r * rc, (r + 1) * rc)
            carry = carry_ref[slot, hh, rows, :]
            acc = acc_ref[slot, hh, rows, :]
            for j in range(n_blocks):
                log_beta, first_term, _, mask = terms[slot, hh, r, j]
                cum = cums[slot, hh, r, j]
                w = jnp.exp2(log_beta + cum + jnp.concatenate([carry] * (t // LANES), axis=1))
                if mask is not None:
                    w = jnp.where(mask, w, 0.0)
                acc = acc + jnp.dot(w.astype(jnp.bfloat16), v_ref[pl.ds(block_start(tile, j), t), :],
                                    preferred_element_type=jnp.float32)
                carry = carry + jnp.broadcast_to(cum[:, 0:1] + first_term, (rc, LANES))
            acc_ref[slot, hh, rows, :] = acc
            carry_ref[slot, hh, rows, :] = carry
            alive[slot] = carry if slot not in alive else jnp.maximum(alive[slot], carry)
        return tuple((jnp.max(alive[tile[0]]) < SB_DEAD_LOG2_WEIGHT).astype(jnp.int32) for tile in tiles)

    def query_tiles(pair, c):
        qi = [pl.program_id(2) * SB_TILES_PER_STEP + SB_PAIR * pair + s for s in range(SB_PAIR)]
        row0 = [pl.multiple_of((SB_PAIR * pair + s) * t, t) for s in range(SB_PAIR)]
        acc_ref[...] = jnp.zeros_like(acc_ref)
        carry_ref[...] = jnp.zeros_like(carry_ref)

        def first(n_first):
            return step([(0, row0[0], qi[0], n_first, True)]
                        + [(s, row0[s], qi[s], 2, True) for s in range(1, SB_PAIR)])

        dead = lax.cond(qi[0] >= 1, lambda: first(2), lambda: first(1))

        for s in range(SB_PAIR):
            def cond(c):
                newest, dead = c
                return jnp.logical_and(newest >= 1, dead == 0)

            def body(c, s=s):
                newest, _ = c
                return newest - 2, step([(s, row0[s], newest, 2, False)])[0]

            newest, still = lax.while_loop(cond, body, (qi[s] - 2, dead[s]))

            @pl.when(jnp.logical_and(newest == 0, still == 0))
            def _(s=s):
                step([(s, row0[s], 0, 1, False)])

            o_ref[pl.ds(row0[s], t), :] = _merge_heads([acc_ref[s, 0], acc_ref[s, 1]])
        return c

    lax.fori_loop(0, SB_TILES_PER_STEP // SB_PAIR, query_tiles, 0)


def _sb_attention(u, tri, batch, seq):
    t = SB_TILE
    rows = SB_TILES_PER_STEP * t
    nq = seq // rows
    return pl.pallas_call(
        _sb_kernel,
        grid=(batch, N_PAIRS, nq),
        in_specs=[pl.BlockSpec((rows, LANES), lambda b, p, i: (b * nq + i, COL_QA // LANES + p)),
                  pl.BlockSpec((seq, LANES), lambda b, p, i: (b, COL_KA // LANES + p)),
                  pl.BlockSpec((seq, LANES), lambda b, p, i: (b, COL_VA // LANES + p)),
                  pl.BlockSpec((2 * t, t), lambda b, p, i: (0, 0))],
        out_specs=pl.BlockSpec((rows, LANES), lambda b, p, i: (b * nq + i, p)),
        out_shape=jax.ShapeDtypeStruct((batch * seq, WIDTH), jnp.float32),
        scratch_shapes=[pltpu.VMEM((SB_PAIR, HEADS_PER_TILE, t, LANES), jnp.float32),
                        pltpu.VMEM((SB_PAIR, HEADS_PER_TILE, t, LANES), jnp.float32)],
        compiler_params=pltpu.CompilerParams(dimension_semantics=("parallel", "parallel", "arbitrary"),
                                             vmem_limit_bytes=VMEM_LIMIT),
        name="sb_attention",
    )(u, u, u, tri)


MOBA_LO_LANE = 64
MOBA_NEVER = 63
MOBA_ZERO_ROW = 64
MOBA_MASKED = -1e30
MOBA_GROUP = 8
MOBA_BLOCKS_PER_STEP = 4


def _moba_kernel(far_ref, q_ref, k_ref, v_ref, btab_ref, oh_ref, o_ref,
                 kmean_ref, vt_ref, qa_ref, m_ref, acc_ref, *, nb):
    t = MOBA_BLOCK
    nsub = MOBA_BLOCKS_PER_STEP
    nh = HEADS_PER_TILE
    heads = range(nh)
    pair = pl.program_id(1)

    @pl.when(pl.program_id(2) == 0)
    def _():
        kmean_ref[...] = jnp.zeros_like(kmean_ref)
        kf = k_ref[...].astype(jnp.float32).reshape(nb, t, LANES)
        kmean_ref[0:nb, :] = jnp.mean(kf, axis=1)

        head_of_row = lax.broadcasted_iota(jnp.int32, (LANES, t), 0) // HEAD_DIM

        def transpose_block(b, c):
            vb = v_ref[pl.ds(pl.multiple_of(b * t, t), t), :].astype(jnp.float32).T
            for hh in heads:
                vt_ref[hh, b] = jnp.where(head_of_row == hh, vb, 1.0).astype(jnp.bfloat16)
            return c

        lax.fori_loop(0, nb, transpose_block, 0)

    rows = -(-nb // F32_SUBLANES) * F32_SUBLANES
    km = kmean_ref[0:rows, :]
    km0 = km.astype(jnp.bfloat16)
    r1 = km - km0.astype(jnp.float32)
    km1 = r1.astype(jnp.bfloat16)
    km2 = (r1 - km1.astype(jnp.float32)).astype(jnp.bfloat16)

    cols = nsub * t
    blk = lax.broadcasted_iota(jnp.int32, (rows, nh * cols), 0)
    blk_f = blk.astype(jnp.float32)
    col = lax.broadcasted_iota(jnp.int32, (1, nh * cols), 1)
    own_col = pl.program_id(2) * nsub + (col % cols) // t
    gate_rows = sum(_qk(jnp.concatenate([part * _head_lane_mask(hh, part.dtype) for hh in heads], axis=0),
                        q_ref[...]) for part in (km0, km1, km2))
    gate = jnp.concatenate([gate_rows[hh * rows:(hh + 1) * rows] for hh in heads], axis=1)
    gate = jnp.where(blk < own_col, gate, _NEG_INF)
    sel = jnp.zeros_like(gate)
    for _ in range(MOBA_TOPK):
        best = jnp.max(gate, axis=0, keepdims=True)
        idx = jnp.min(jnp.where(gate == best, blk_f, float(LANES)), axis=0, keepdims=True)
        hit = blk_f == idx
        sel = jnp.where(jnp.logical_and(hit, best > _NEG_INF), 1.0, sel)
        gate = jnp.where(hit, _NEG_INF, gate)

    far_bias = jnp.where(col // cols == 0, far_ref[pair * nh], far_ref[pair * nh + 1])
    term = jnp.where(sel > 0.0, jnp.where(blk < own_col - 1, far_bias, 0.0), MOBA_MASKED)
    hi = term.astype(jnp.bfloat16).astype(jnp.float32)
    lo = term - hi
    pad = MOBA_LO_LANE - rows
    pad_hi = [jnp.full((pad, t), MOBA_MASKED, jnp.float32)] if pad else []
    pad_lo = [jnp.zeros((pad, t), jnp.float32)] if pad else []
    for hh in heads:
        for sub in range(nsub):
            cs = slice(hh * cols + sub * t, hh * cols + (sub + 1) * t)
            inj = jnp.concatenate([hi[:, cs]] + pad_hi + [lo[:, cs]] + pad_lo, axis=0).T
            q_sub = q_ref[sub * t:(sub + 1) * t, :] * _head_lane_mask(hh, q_ref.dtype)
            qa_ref[hh, sub] = jnp.concatenate([q_sub, inj.astype(jnp.bfloat16)], axis=1)

    def query_block(sub, c):
        _moba_query_block(sub, k_ref, btab_ref, oh_ref, o_ref, vt_ref, qa_ref, m_ref, acc_ref, nb=nb)
        return c

    lax.fori_loop(0, nsub, query_block, 0)


def _moba_query_block(sub, k_ref, btab_ref, oh_ref, o_ref, vt_ref, qa_ref, m_ref, acc_ref, *, nb):
    t = MOBA_BLOCK
    g = MOBA_GROUP
    own = pl.program_id(2) * MOBA_BLOCKS_PER_STEP + sub
    q_rows = pl.ds(pl.multiple_of(sub * t, t), t)
    heads = range(HEADS_PER_TILE)

    def update(scores, blocks, first):
        col_max = [jnp.max(s, axis=0, keepdims=True) for s in scores]
        if first:
            m_new = col_max
        else:
            m_old = [m_ref[hh][0:1, :] for hh in heads]
            m_new = [jnp.maximum(m_old[hh], col_max[hh]) for hh in heads]
            alpha = [jnp.exp2(m_old[hh] - m_new[hh]) for hh in heads]
        p = [jnp.exp2(scores[hh] - m_new[hh]).astype(jnp.bfloat16) for hh in heads]
        pv = [jnp.dot(jnp.concatenate([vt_ref[hh, b] for b in blocks], axis=1), p[hh],
                      preferred_element_type=jnp.float32) for hh in heads]
        for hh in heads:
            acc_ref[hh] = pv[hh] if first else alpha[hh] * acc_ref[hh] + pv[hh]
            m_ref[hh] = jnp.broadcast_to(m_new[hh], m_ref.shape[1:])

    def block_lanes(row):
        return jnp.concatenate([oh_ref[row]] * (t // BF16_SUBLANES), axis=0)

    prev = jnp.maximum(own - 1, 0)
    n_far = jnp.maximum(own - 1, 0)
    n_full = n_far // g
    left = n_far - n_full * g

    def far_keys(first_blk, group):
        g0 = jnp.minimum(first_blk, nb - group)
        rows = []
        for j in range(group):
            b = g0 + j
            active = jnp.logical_and(b >= first_blk, b < n_far)
            rows.append(block_lanes(jnp.where(active, b, MOBA_NEVER)))
        keys = k_ref[pl.ds(pl.multiple_of(g0 * t, t), group * t), :]
        return g0, keys, rows

    def first_step(group):
        keys = [k_ref[pl.ds(pl.multiple_of(own * t, t), t), :], k_ref[pl.ds(pl.multiple_of(prev * t, t), t), :]]
        rows = [block_lanes(MOBA_ZERO_ROW), block_lanes(jnp.where(own >= 1, prev, MOBA_NEVER))]
        if group:
            g0, far, far_rows = far_keys(n_full * g, group)
            keys.append(far)
            rows.extend(far_rows)
        k_aug = jnp.concatenate([jnp.concatenate(keys, axis=0), jnp.concatenate(rows, axis=0)], axis=1)
        scores = [_qk(k_aug, qa_ref[hh, sub]) for hh in heads]
        update([scores[hh][:t] + btab_ref[hh, 0] for hh in heads], [own], first=True)
        update([scores[hh][t:2 * t] + btab_ref[hh, 1] for hh in heads], [prev], first=False)
        for j in range(group):
            update([s[(2 + j) * t:(3 + j) * t] for s in scores], [g0 + j], first=False)

    for count in range(g):
        @pl.when(left == count)
        def _(count=count):
            first_step(count)

    def far_body(i, c):
        g0, keys, rows = far_keys(i * g, g)
        k_aug = jnp.concatenate([keys, jnp.concatenate(rows, axis=0)], axis=1)
        scores = [_qk(k_aug, qa_ref[hh, sub]) for hh in heads]
        for j in range(g):
            update([s[j * t:(j + 1) * t] for s in scores], [g0 + j], first=False)
        return c

    lax.fori_loop(0, n_full, far_body, 0)

    out_t = jnp.concatenate([acc_ref[0][:HEAD_DIM] / acc_ref[0][HEAD_DIM:],
                             acc_ref[1][HEAD_DIM:] / acc_ref[1][:HEAD_DIM]], axis=0)
    o_ref[q_rows, :] = out_t.T


def _moba_attention(u, far_bias, btab, block_lanes, batch, seq):
    t = MOBA_BLOCK
    nb = seq // t
    rows = MOBA_BLOCKS_PER_STEP * t
    nstep = seq // rows
    kernel = functools.partial(_moba_kernel, nb=nb)
    grid_spec = pltpu.PrefetchScalarGridSpec(
        num_scalar_prefetch=1,
        grid=(batch, N_PAIRS, nstep),
        in_specs=[pl.BlockSpec((rows, LANES), lambda b, p, i, far: (b * nstep + i, COL_QB // LANES + p)),
                  pl.BlockSpec((seq, LANES), lambda b, p, i, far: (b, COL_KB // LANES + p)),
                  pl.BlockSpec((seq, LANES), lambda b, p, i, far: (b, COL_VB // LANES + p)),
                  pl.BlockSpec((HEADS_PER_TILE, 2, t, t), lambda b, p, i, far: (p, 0, 0, 0)),
                  pl.BlockSpec(block_lanes.shape, lambda b, p, i, far: (0, 0, 0))],
        out_specs=pl.BlockSpec((rows, LANES), lambda b, p, i, far: (b * nstep + i, p)),
        scratch_shapes=[pltpu.VMEM((MOBA_LO_LANE, LANES), jnp.float32),
                        pltpu.VMEM((HEADS_PER_TILE, nb, LANES, t), jnp.bfloat16),
                        pltpu.VMEM((HEADS_PER_TILE, MOBA_BLOCKS_PER_STEP, t, 2 * LANES), jnp.bfloat16),
                        pltpu.VMEM((HEADS_PER_TILE, F32_SUBLANES, t), jnp.float32),
                        pltpu.VMEM((HEADS_PER_TILE, LANES, t), jnp.float32)])
    return pl.pallas_call(
        kernel,
        grid_spec=grid_spec,
        out_shape=jax.ShapeDtypeStruct((batch * seq, WIDTH), jnp.float32),
        compiler_params=pltpu.CompilerParams(dimension_semantics=("parallel", "parallel", "arbitrary"),
                                             vmem_limit_bytes=VMEM_LIMIT),
        name="moba_attention",
    )(far_bias, u, u, u, btab, block_lanes)


SWA_STEP_WINDOWS = 2


def _swa_kernel(sink_ref, q_ref, kp_ref, kc_ref, vp_ref, vc_ref, bias_ref, o_ref):
    w = SWA_WINDOW
    keys = jnp.concatenate([kp_ref[...], kc_ref[...]], axis=0)
    values = jnp.concatenate([vp_ref[...], vc_ref[...]], axis=0)
    slots = [(j, half) for j in range(SWA_GROUP) for half in range(SWA_KV_HEADS)]
    chains = [(win, slot) for win in range(SWA_STEP_WINDOWS) for slot in range(len(slots))]

    def attend(first_step):
        scores = []
        for win, slot in chains:
            j, half = slots[slot]
            q = q_ref[win * w:(win + 1) * w, j * LANES:(j + 1) * LANES]
            s = _qk(q * _head_lane_mask(half, q.dtype), keys[win * w:(win + 2) * w]) + bias_ref[slot]
            if first_step and win == 0:
                s = jnp.where(lax.broadcasted_iota(jnp.int32, (w, 2 * w), 1) >= w, s, _NEG_INF)
            scores.append(s)
        m = [jnp.maximum(jnp.broadcast_to(jnp.max(s, axis=1, keepdims=True), (w, LANES)), sink_ref[slot])
             for (win, slot), s in zip(chains, scores)]
        p = [jnp.exp2(s - jnp.concatenate([mi] * (2 * w // LANES), axis=1)).astype(jnp.bfloat16)
             for mi, s in zip(m, scores)]
        pv = []
        for (win, slot), pi in zip(chains, p):
            vw = values[win * w:(win + 2) * w]
            vw = jnp.where(_head_lane_mask(slots[slot][1], jnp.int32) > 0, vw, jnp.ones_like(vw))
            pv.append(jnp.dot(pi, vw, preferred_element_type=jnp.float32))
        out = [pvi / (pltpu.roll(pvi, HEAD_DIM, axis=1) + jnp.exp2(sink_ref[slot] - mi))
               for (win, slot), pvi, mi in zip(chains, pv, m)]
        for win in range(SWA_STEP_WINDOWS):
            for j in range(SWA_GROUP):
                first = win * len(slots) + j * SWA_KV_HEADS
                o_ref[win * w:(win + 1) * w, j * LANES:(j + 1) * LANES] = _merge_heads(
                    out[first:first + SWA_KV_HEADS])

    @pl.when(pl.program_id(1) == 0)
    def _():
        attend(True)

    @pl.when(pl.program_id(1) > 0)
    def _():
        attend(False)


def _swa_attention(u, sinks_perm, bias, batch, seq):
    w = SWA_WINDOW
    sw = SWA_STEP_WINDOWS
    nstep = seq // (sw * w)
    kcol, vcol = COL_KC // LANES, COL_VC // LANES
    prev_window = lambda b, n, s: b * (seq // w) + jnp.maximum(sw * n - 1, 0)
    grid_spec = pltpu.PrefetchScalarGridSpec(
        num_scalar_prefetch=1,
        grid=(batch, nstep),
        in_specs=[pl.BlockSpec((sw * w, WIDTH), lambda b, n, s: (b * nstep + n, COL_QC // WIDTH)),
                  pl.BlockSpec((w, LANES), lambda b, n, s: (prev_window(b, n, s), kcol)),
                  pl.BlockSpec((sw * w, LANES), lambda b, n, s: (b * nstep + n, kcol)),
                  pl.BlockSpec((w, LANES), lambda b, n, s: (prev_window(b, n, s), vcol)),
                  pl.BlockSpec((sw * w, LANES), lambda b, n, s: (b * nstep + n, vcol)),
                  pl.BlockSpec((N_HEADS, w, 2 * w), lambda b, n, s: (0, 0, 0))],
        out_specs=pl.BlockSpec((sw * w, WIDTH), lambda b, n, s: (b * nstep + n, 0)))
    return pl.pallas_call(
        _swa_kernel,
        grid_spec=grid_spec,
        out_shape=jax.ShapeDtypeStruct((batch * seq, WIDTH), jnp.float32),
        compiler_params=pltpu.CompilerParams(dimension_semantics=("parallel", "arbitrary"),
                                             vmem_limit_bytes=VMEM_LIMIT),
        name="swa_attention",
    )(sinks_perm, u, u, u, u, u, bias)


def _post_kernel(x_ref, ya_ref, yb_ref, yc_ref, ga_ref, gb_ref, gc_ref, ma_ref, mb_ref, mc_ref,
                 wa_ref, wb_ref, wc_ref, wo_ref, fw_ref, o_ref, *, final):
    def branch(y_ref, g_ref, w_ref, m_ref):
        g = g_ref[...].astype(jnp.float32)
        y = (y_ref[...] * (g * jax.nn.sigmoid(g))).astype(jnp.bfloat16)
        proj = jnp.dot(y, w_ref[...], preferred_element_type=jnp.float32)
        return jax.nn.sigmoid(m_ref[...].astype(jnp.float32)) * proj

    merged = (branch(ya_ref, ga_ref, wa_ref, ma_ref) + branch(yb_ref, gb_ref, wb_ref, mb_ref)
              + branch(yc_ref, gc_ref, wc_ref, mc_ref))
    out = x_ref[...] + jnp.dot(merged.astype(jnp.bfloat16), wo_ref[...], preferred_element_type=jnp.float32)
    if final:
        out = out * lax.rsqrt(jnp.mean(out * out, axis=-1, keepdims=True) + RMS_EPS) * fw_ref[...]
    o_ref[...] = out


def _post(x2, ya, yb, yc, u, wa, wb, wc, wo, fw, layer, *, final, tm=512):
    n = x2.shape[0]
    row = lambda i: (i, 0)
    full = lambda i: (0, 0)
    of_layer = lambda i: (layer, 0, 0)
    y_spec = pl.BlockSpec((tm, WIDTH), row)
    in_specs = [pl.BlockSpec((tm, D_MODEL), row), y_spec, y_spec, y_spec,
                pl.BlockSpec((tm, WIDTH), lambda i: (i, COL_GA // WIDTH)),
                pl.BlockSpec((tm, WIDTH), lambda i: (i, COL_GB // WIDTH)),
                pl.BlockSpec((tm, WIDTH), lambda i: (i, COL_GC // WIDTH)),
                pl.BlockSpec((tm, D_MODEL), lambda i: (i, COL_MA // D_MODEL)),
                pl.BlockSpec((tm, D_MODEL), lambda i: (i, COL_MB // D_MODEL)),
                pl.BlockSpec((tm, D_MODEL), lambda i: (i, COL_MC // D_MODEL)),
                pl.BlockSpec((None, WIDTH, D_MODEL), of_layer), pl.BlockSpec((None, WIDTH, D_MODEL), of_layer),
                pl.BlockSpec((None, WIDTH, D_MODEL), of_layer), pl.BlockSpec((None, D_MODEL, D_MODEL), of_layer),
                pl.BlockSpec((1, D_MODEL), full)]
    return pl.pallas_call(
        functools.partial(_post_kernel, final=final),
        grid=(n // tm,),
        in_specs=in_specs,
        out_specs=pl.BlockSpec((tm, D_MODEL), row),
        out_shape=jax.ShapeDtypeStruct((n, D_MODEL), jnp.float32),
        compiler_params=pltpu.CompilerParams(dimension_semantics=("parallel",),
                                             vmem_limit_bytes=VMEM_LIMIT),
        name="post_final" if final else "post",
    )(x2, ya, yb, yc, u, u, u, u, u, u, wa, wb, wc, wo, fw)


def _bucket_lookup(table, dist):
    onehot = (_rel_bucket(dist)[..., None] == jnp.arange(REL_BUCKETS)).astype(jnp.float32)
    return jnp.einsum("...b,bh->h...", onehot, table.astype(jnp.float32), precision=lax.Precision.HIGHEST)


def _bias_tables(rel_bias):
    moba_tab = rel_bias[:, :N_HEADS]
    swa_tab = _swa_head_order(rel_bias[:, N_HEADS:], 1, per_head=1)
    t = MOBA_BLOCK
    d_own = jnp.arange(t)[:, None] - jnp.arange(t)[None, :]
    own = jnp.where(d_own[None] >= 0, _bucket_lookup(moba_tab, d_own), _NEG_INF)
    prev = _bucket_lookup(moba_tab, d_own + t)
    btab = jnp.swapaxes(jnp.stack([own, prev], axis=1), 2, 3) * LOG2_E
    far = _bucket_lookup(moba_tab, jnp.full((1,), t + 1, jnp.int32))[:, 0] * LOG2_E

    w = SWA_WINDOW
    dist = jnp.arange(w)[:, None] + w - jnp.arange(2 * w)[None, :]
    in_band = (dist >= 0) & (dist < w)
    swa = jnp.where(in_band[None], _bucket_lookup(swa_tab, dist) * LOG2_E, _NEG_INF)
    return btab, far, swa


def kernel(x, norm_w, w_in, w_proj_a, w_proj_b, w_proj_c, w_out, sinks, rel_bias, final_norm_w):
    batch, seq, _ = x.shape
    depth = w_in.shape[0]
    assert seq % MOBA_BLOCK == 0 and MOBA_GROUP <= seq // MOBA_BLOCK <= MOBA_NEVER
    btab, far, swa_bias = _bias_tables(rel_bias)
    t = SB_TILE
    tri = (jnp.arange(t)[:, None] > jnp.arange(t)[None, :]).astype(jnp.bfloat16)
    tri = jnp.concatenate([tri, tri], axis=0)
    lane = jnp.arange(LANES)[None, :]
    blk = jnp.arange(MOBA_ZERO_ROW + 1)[:, None]
    block_lanes = ((lane == blk) | (lane == blk + MOBA_LO_LANE)) & (blk < MOBA_ZERO_ROW)
    block_lanes = jnp.broadcast_to(block_lanes[:, None, :],
                                   (MOBA_ZERO_ROW + 1, BF16_SUBLANES, LANES)).astype(jnp.bfloat16)
    sinks_perm = _swa_head_order(sinks.astype(jnp.float32), 1, per_head=1) * LOG2_E

    w_u = _permute_input_columns(w_in)
    norm_w3 = norm_w.reshape(depth, 1, D_MODEL)
    wa = w_proj_a.astype(jnp.bfloat16)
    wb = w_proj_b.astype(jnp.bfloat16)
    wc = _swa_head_order(w_proj_c, 1).astype(jnp.bfloat16)
    wo = w_out.astype(jnp.bfloat16)
    fw = final_norm_w.reshape(1, D_MODEL)

    x2 = x.reshape(batch * seq, D_MODEL)
    for layer in range(depth):
        u = _inproj(x2, norm_w3, w_u, layer)
        ya = _sb_attention(u, tri, batch, seq)
        yb = _moba_attention(u, far, btab, block_lanes, batch, seq)
        yc = _swa_attention(u, sinks_perm[layer], swa_bias, batch, seq)
        x2 = _post(x2, ya, yb, yc, u, wa, wb, wc, wo, fw, layer, final=(layer == depth - 1))
    return x2.reshape(batch, seq, D_MODEL)
```

```python
import functools
import math

import jax
import jax.numpy as jnp
from jax import lax
from jax.experimental import pallas as pl
from jax.experimental.pallas import tpu as pltpu

D_MODEL = 1024
HEAD_DIM = 64
N_HEADS = 8
WIDTH = N_HEADS * HEAD_DIM
MOBA_BLOCK = 256
MOBA_TOPK = 3
SWA_KV_HEADS = 2
SWA_GROUP = N_HEADS // SWA_KV_HEADS
SWA_WINDOW = 128
REL_BUCKETS = 32
REL_MAX_DIST = 128
RMS_EPS = 1e-6

LANES = 128
BF16_SUBLANES = 16
F32_SUBLANES = 8
HEADS_PER_TILE = LANES // HEAD_DIM
N_PAIRS = N_HEADS // HEADS_PER_TILE

COL_QA, COL_KA, COL_VA = 0, 512, 1024
COL_QB, COL_KB, COL_VB = 1536, 2048, 2560
COL_QC = 3072
COL_GA, COL_GB, COL_GC = 3584, 4096, 4608
COL_MA, COL_MB, COL_MC = 5120, 6144, 7168
COL_KC, COL_VC = 8192, 8320
D_IN = 8448

V7X_VMEM_BYTES = 64 * 1024 * 1024
VMEM_LIMIT = V7X_VMEM_BYTES * 3 // 4

LOG2_E = math.log2(math.e)

SB_DEAD_LOG_WEIGHT = -120.0
SB_DEAD_LOG2_WEIGHT = SB_DEAD_LOG_WEIGHT * LOG2_E

_NEG_INF = float("-inf")


def _rel_bucket(dist):
    max_exact = REL_BUCKETS // 2
    n = jnp.maximum(dist, 0)
    nf = jnp.maximum(n, 1).astype(jnp.float32)
    large = max_exact + (jnp.log(nf / max_exact) / math.log(REL_MAX_DIST / max_exact)
                         * (REL_BUCKETS - max_exact)).astype(jnp.int32)
    large = jnp.minimum(large, REL_BUCKETS - 1)
    return jnp.where(n < max_exact, n, large)


def _swa_head_order(t, axis, per_head=HEAD_DIM):
    axis = axis % t.ndim
    shape = t.shape
    t = t.reshape(shape[:axis] + (SWA_KV_HEADS, SWA_GROUP, per_head) + shape[axis + 1:])
    return jnp.swapaxes(t, axis, axis + 1).reshape(shape)


def _permute_input_columns(w):
    old = {"qa": 0, "ka": 512, "va": 1024, "ga": 1536, "qb": 2048, "kb": 2560, "vb": 3072, "gb": 3584,
           "qc": 4096, "kc": 4608, "vc": 4736, "gc": 4864, "ma": 5376, "mb": 6400, "mc": 7424}
    q_scale = HEAD_DIM ** -0.5 * LOG2_E

    def cols(name, width, scale=None):
        part = w[..., old[name]:old[name] + width]
        return part if scale is None else part * scale

    parts = [cols("qa", 512, q_scale), cols("ka", 512), cols("va", 512),
             cols("qb", 512, q_scale), cols("kb", 512), cols("vb", 512),
             _swa_head_order(cols("qc", WIDTH, q_scale), -1),
             cols("ga", 512), cols("gb", 512), _swa_head_order(cols("gc", WIDTH), -1),
             cols("ma", 1024), cols("mb", 1024), cols("mc", 1024),
             cols("kc", 128), cols("vc", 128)]
    return jnp.concatenate(parts, axis=-1).astype(jnp.bfloat16)


def _inproj_kernel(x_ref, nw_ref, w_ref, u_ref, h_ref):
    @pl.when(pl.program_id(1) == 0)
    def _():
        x = x_ref[...]
        y = x * lax.rsqrt(jnp.mean(x * x, axis=-1, keepdims=True) + RMS_EPS)
        h_ref[...] = (y * nw_ref[...]).astype(h_ref.dtype)

    u_ref[...] = jnp.dot(h_ref[...], w_ref[...], preferred_element_type=jnp.float32).astype(u_ref.dtype)


def _inproj(x2, norm_w, w, layer, *, tm=1024, tn=2816):
    n = x2.shape[0]
    return pl.pallas_call(
        _inproj_kernel,
        grid=(n // tm, D_IN // tn),
        in_specs=[pl.BlockSpec((tm, D_MODEL), lambda i, j: (i, 0)),
                  pl.BlockSpec((None, 1, D_MODEL), lambda i, j: (layer, 0, 0)),
                  pl.BlockSpec((None, D_MODEL, tn), lambda i, j: (layer, 0, j))],
        out_specs=pl.BlockSpec((tm, tn), lambda i, j: (i, j)),
        out_shape=jax.ShapeDtypeStruct((n, D_IN), jnp.bfloat16),
        scratch_shapes=[pltpu.VMEM((tm, D_MODEL), jnp.bfloat16)],
        compiler_params=pltpu.CompilerParams(dimension_semantics=("parallel", "arbitrary"),
                                             vmem_limit_bytes=VMEM_LIMIT),
        name="inproj",
    )(x2, norm_w, w)


def _head_lane_mask(hh, dtype):
    lane = lax.broadcasted_iota(jnp.int32, (1, LANES), 1)
    return ((lane // HEAD_DIM) == hh).astype(dtype)


def _qk(qm, k):
    return lax.dot_general(qm, k, (((1,), (1,)), ((), ())), preferred_element_type=jnp.float32)


def _merge_heads(parts):
    lane = lax.broadcasted_iota(jnp.int32, parts[0].shape, 1)
    return jnp.where(lane < HEAD_DIM, parts[0], parts[1])


SB_TILE = 256
SB_TILES_PER_STEP = 4
SB_PAIR = 4
SB_ROW_CHUNK = 256


def _sb_kernel(q_ref, k_ref, v_ref, tri_ref, o_ref, acc_ref, carry_ref):
    t = SB_TILE

    def step(tiles):
        tri = tri_ref[...]
        rc = SB_ROW_CHUNK
        chains = [(tile, hh, r) for tile in tiles for hh in range(HEADS_PER_TILE) for r in range(t // rc)]

        def block_start(tile, j):
            return pl.multiple_of((tile[2] - j) * t, t)

        terms = {}
        for tile, hh, r in chains:
            slot, row0, _, n_blocks, diagonal = tile
            qm = q_ref[pl.ds(row0 + r * rc, rc), :] * _head_lane_mask(hh, q_ref.dtype)
            for j in range(n_blocks):
                z = _qk(qm, k_ref[pl.ds(block_start(tile, j), t), :])
                neg_abs = lax.bitcast_convert_type(
                    lax.bitcast_convert_type(z, jnp.uint32) | jnp.uint32(0x80000000), jnp.float32)
                soft = jnp.log(1.0 + jnp.exp2(neg_abs)) * LOG2_E
                log_beta = jnp.minimum(z, 0.0) - soft
                log_1m = log_beta - z
                mask = None
                if diagonal and j == 0:
                    row = lax.broadcasted_iota(jnp.int32, (rc, t), 0) + r * rc
                    mask = lax.broadcasted_iota(jnp.int32, (rc, t), 1) < row
                    log_1m = jnp.where(mask, log_1m, 0.0)
                hi = log_1m.astype(jnp.bfloat16)
                split = jnp.concatenate([hi, (log_1m - hi.astype(jnp.float32)).astype(jnp.bfloat16)], axis=1)
                terms[slot, hh, r, j] = (log_beta, log_1m[:, 0:1], split, mask)

        cums = {key: jnp.dot(split, tri, preferred_element_type=jnp.float32)
                for key, (_, _, split, _) in terms.items()}

        alive = {}
        for tile, hh, r in chains:
            slot, _, _, n_blocks, _ = tile
            rows = slice(r * rc, (r + 1) * rc)
            carry = carry_ref[slot, hh, rows, :]
            acc = acc_ref[slot, hh, rows, :]
            for j in range(n_blocks):
                log_beta, first_term, _, mask = terms[slot, hh, r, j]
                cum = cums[slot, hh, r, j]
                w = jnp.exp2(log_beta + cum + jnp.concatenate([carry] * (t // LANES), axis=1))
                if mask is not None:
                    w = jnp.where(mask, w, 0.0)
                acc = acc + jnp.dot(w.astype(jnp.bfloat16), v_ref[pl.ds(block_start(tile, j), t), :],
                                    preferred_element_type=jnp.float32)
                carry = carry + jnp.broadcast_to(cum[:, 0:1] + first_term, (rc, LANES))
            acc_ref[slot, hh, rows, :] = acc
            carry_ref[slot, hh, rows, :] = carry
            alive[slot] = carry if slot not in alive else jnp.maximum(alive[slot], carry)
        return tuple((jnp.max(alive[tile[0]]) < SB_DEAD_LOG2_WEIGHT).astype(jnp.int32) for tile in tiles)

    def query_tiles(pair, c):
        qi = [pl.program_id(2) * SB_TILES_PER_STEP + SB_PAIR * pair + s for s in range(SB_PAIR)]
        row0 = [pl.multiple_of((SB_PAIR * pair + s) * t, t) for s in range(SB_PAIR)]
        acc_ref[...] = jnp.zeros_like(acc_ref)
        carry_ref[...] = jnp.zeros_like(carry_ref)

        def first(n_first):
            return step([(0, row0[0], qi[0], n_first, True)]
                        + [(s, row0[s], qi[s], 2, True) for s in range(1, SB_PAIR)])

        dead = lax.cond(qi[0] >= 1, lambda: first(2), lambda: first(1))

        for s in range(SB_PAIR):
            def cond(c):
                newest, dead = c
                return jnp.logical_and(newest >= 1, dead == 0)

            def body(c, s=s):
                newest, _ = c
                return newest - 2, step([(s, row0[s], newest, 2, False)])[0]

            newest, still = lax.while_loop(cond, body, (qi[s] - 2, dead[s]))

            @pl.when(jnp.logical_and(newest == 0, still == 0))
            def _(s=s):
                step([(s, row0[s], 0, 1, False)])

            o_ref[pl.ds(row0[s], t), :] = _merge_heads([acc_ref[s, 0], acc_ref[s, 1]])
        return c

    lax.fori_loop(0, SB_TILES_PER_STEP // SB_PAIR, query_tiles, 0)


def _sb_attention(u, tri, batch, seq):
    t = SB_TILE
    rows = SB_TILES_PER_STEP * t
    nq = seq // rows
    return pl.pallas_call(
        _sb_kernel,
        grid=(batch, N_PAIRS, nq),
        in_specs=[pl.BlockSpec((rows, LANES), lambda b, p, i: (b * nq + i, COL_QA // LANES + p)),
                  pl.BlockSpec((seq, LANES), lambda b, p, i: (b, COL_KA // LANES + p)),
                  pl.BlockSpec((seq, LANES), lambda b, p, i: (b, COL_VA // LANES + p)),
                  pl.BlockSpec((2 * t, t), lambda b, p, i: (0, 0))],
        out_specs=pl.BlockSpec((rows, LANES), lambda b, p, i: (b * nq + i, p)),
        out_shape=jax.ShapeDtypeStruct((batch * seq, WIDTH), jnp.float32),
        scratch_shapes=[pltpu.VMEM((SB_PAIR, HEADS_PER_TILE, t, LANES), jnp.float32),
                        pltpu.VMEM((SB_PAIR, HEADS_PER_TILE, t, LANES), jnp.float32)],
        compiler_params=pltpu.CompilerParams(dimension_semantics=("parallel", "parallel", "arbitrary"),
                                             vmem_limit_bytes=VMEM_LIMIT),
        name="sb_attention",
    )(u, u, u, tri)


MOBA_LO_LANE = 64
MOBA_NEVER = 63
MOBA_ZERO_ROW = 64
MOBA_MASKED = -1e30
MOBA_GROUP = 8
MOBA_BLOCKS_PER_STEP = 4


def _moba_kernel(far_ref, q_ref, k_ref, v_ref, btab_ref, oh_ref, o_ref,
                 kmean_ref, vt_ref, qa_ref, m_ref, acc_ref, *, nb):
    t = MOBA_BLOCK
    nsub = MOBA_BLOCKS_PER_STEP
    nh = HEADS_PER_TILE
    heads = range(nh)
    pair = pl.program_id(1)

    @pl.when(pl.program_id(2) == 0)
    def _():
        kmean_ref[...] = jnp.zeros_like(kmean_ref)
        kf = k_ref[...].astype(jnp.float32).reshape(nb, t, LANES)
        kmean_ref[0:nb, :] = jnp.mean(kf, axis=1)

        head_of_row = lax.broadcasted_iota(jnp.int32, (LANES, t), 0) // HEAD_DIM

        def transpose_block(b, c):
            vb = v_ref[pl.ds(pl.multiple_of(b * t, t), t), :].astype(jnp.float32).T
            for hh in heads:
                vt_ref[hh, b] = jnp.where(head_of_row == hh, vb, 1.0).astype(jnp.bfloat16)
            return c

        lax.fori_loop(0, nb, transpose_block, 0)

    rows = -(-nb // F32_SUBLANES) * F32_SUBLANES
    km = kmean_ref[0:rows, :]
    km0 = km.astype(jnp.bfloat16)
    r1 = km - km0.astype(jnp.float32)
    km1 = r1.astype(jnp.bfloat16)
    km2 = (r1 - km1.astype(jnp.float32)).astype(jnp.bfloat16)

    cols = nsub * t
    blk = lax.broadcasted_iota(jnp.int32, (rows, nh * cols), 0)
    blk_f = blk.astype(jnp.float32)
    col = lax.broadcasted_iota(jnp.int32, (1, nh * cols), 1)
    own_col = pl.program_id(2) * nsub + (col % cols) // t
    gate_rows = sum(_qk(jnp.concatenate([part * _head_lane_mask(hh, part.dtype) for hh in heads], axis=0),
                        q_ref[...]) for part in (km0, km1, km2))
    gate = jnp.concatenate([gate_rows[hh * rows:(hh + 1) * rows] for hh in heads], axis=1)
    gate = jnp.where(blk < own_col, gate, _NEG_INF)
    sel = jnp.zeros_like(gate)
    for _ in range(MOBA_TOPK):
        best = jnp.max(gate, axis=0, keepdims=True)
        idx = jnp.min(jnp.where(gate == best, blk_f, float(LANES)), axis=0, keepdims=True)
        hit = blk_f == idx
        sel = jnp.where(jnp.logical_and(hit, best > _NEG_INF), 1.0, sel)
        gate = jnp.where(hit, _NEG_INF, gate)

    far_bias = jnp.where(col // cols == 0, far_ref[pair * nh], far_ref[pair * nh + 1])
    term = jnp.where(sel > 0.0, jnp.where(blk < own_col - 1, far_bias, 0.0), MOBA_MASKED)
    hi = term.astype(jnp.bfloat16).astype(jnp.float32)
    lo = term - hi
    pad = MOBA_LO_LANE - rows
    pad_hi = [jnp.full((pad, t), MOBA_MASKED, jnp.float32)] if pad else []
    pad_lo = [jnp.zeros((pad, t), jnp.float32)] if pad else []
    for hh in heads:
        for sub in range(nsub):
            cs = slice(hh * cols + sub * t, hh * cols + (sub + 1) * t)
            inj = jnp.concatenate([hi[:, cs]] + pad_hi + [lo[:, cs]] + pad_lo, axis=0).T
            q_sub = q_ref[sub * t:(sub + 1) * t, :] * _head_lane_mask(hh, q_ref.dtype)
            qa_ref[hh, sub] = jnp.concatenate([q_sub, inj.astype(jnp.bfloat16)], axis=1)

    def query_block(sub, c):
        _moba_query_block(sub, k_ref, btab_ref, oh_ref, o_ref, vt_ref, qa_ref, m_ref, acc_ref, nb=nb)
        return c

    lax.fori_loop(0, nsub, query_block, 0)


def _moba_query_block(sub, k_ref, btab_ref, oh_ref, o_ref, vt_ref, qa_ref, m_ref, acc_ref, *, nb):
    t = MOBA_BLOCK
    g = MOBA_GROUP
    own = pl.program_id(2) * MOBA_BLOCKS_PER_STEP + sub
    q_rows = pl.ds(pl.multiple_of(sub * t, t), t)
    heads = range(HEADS_PER_TILE)

    def update(scores, blocks, first):
        col_max = [jnp.max(s, axis=0, keepdims=True) for s in scores]
        if first:
            m_new = col_max
        else:
            m_old = [m_ref[hh][0:1, :] for hh in heads]
            m_new = [jnp.maximum(m_old[hh], col_max[hh]) for hh in heads]
            alpha = [jnp.exp2(m_old[hh] - m_new[hh]) for hh in heads]
        p = [jnp.exp2(scores[hh] - m_new[hh]).astype(jnp.bfloat16) for hh in heads]
        pv = [jnp.dot(jnp.concatenate([vt_ref[hh, b] for b in blocks], axis=1), p[hh],
                      preferred_element_type=jnp.float32) for hh in heads]
        for hh in heads:
            acc_ref[hh] = pv[hh] if first else alpha[hh] * acc_ref[hh] + pv[hh]
            m_ref[hh] = jnp.broadcast_to(m_new[hh], m_ref.shape[1:])

    def block_lanes(row):
        return jnp.concatenate([oh_ref[row]] * (t // BF16_SUBLANES), axis=0)

    prev = jnp.maximum(own - 1, 0)
    n_far = jnp.maximum(own - 1, 0)
    n_full = n_far // g
    left = n_far - n_full * g

    def far_keys(first_blk, group):
        g0 = jnp.minimum(first_blk, nb - group)
        rows = []
        for j in range(group):
            b = g0 + j
            active = jnp.logical_and(b >= first_blk, b < n_far)
            rows.append(block_lanes(jnp.where(active, b, MOBA_NEVER)))
        keys = k_ref[pl.ds(pl.multiple_of(g0 * t, t), group * t), :]
        return g0, keys, rows

    def first_step(group):
        keys = [k_ref[pl.ds(pl.multiple_of(own * t, t), t), :], k_ref[pl.ds(pl.multiple_of(prev * t, t), t), :]]
        rows = [block_lanes(MOBA_ZERO_ROW), block_lanes(jnp.where(own >= 1, prev, MOBA_NEVER))]
        if group:
            g0, far, far_rows = far_keys(n_full * g, group)
            keys.append(far)
            rows.extend(far_rows)
        k_aug = jnp.concatenate([jnp.concatenate(keys, axis=0), jnp.concatenate(rows, axis=0)], axis=1)
        scores = [_qk(k_aug, qa_ref[hh, sub]) for hh in heads]
        update([scores[hh][:t] + btab_ref[hh, 0] for hh in heads], [own], first=True)
        update([scores[hh][t:2 * t] + btab_ref[hh, 1] for hh in heads], [prev], first=False)
        for j in range(group):
            update([s[(2 + j) * t:(3 + j) * t] for s in scores], [g0 + j], first=False)

    for count in range(g):
        @pl.when(left == count)
        def _(count=count):
            first_step(count)

    def far_body(i, c):
        g0, keys, rows = far_keys(i * g, g)
        k_aug = jnp.concatenate([keys, jnp.concatenate(rows, axis=0)], axis=1)
        scores = [_qk(k_aug, qa_ref[hh, sub]) for hh in heads]
        for j in range(g):
            update([s[j * t:(j + 1) * t] for s in scores], [g0 + j], first=False)
        return c

    lax.fori_loop(0, n_full, far_body, 0)

    out_t = jnp.concatenate([acc_ref[0][:HEAD_DIM] / acc_ref[0][HEAD_DIM:],
                             acc_ref[1][HEAD_DIM:] / acc_ref[1][:HEAD_DIM]], axis=0)
    o_ref[q_rows, :] = out_t.T


def _moba_attention(u, far_bias, btab, block_lanes, batch, seq):
    t = MOBA_BLOCK
    nb = seq // t
    rows = MOBA_BLOCKS_PER_STEP * t
    nstep = seq // rows
    kernel = functools.partial(_moba_kernel, nb=nb)
    grid_spec = pltpu.PrefetchScalarGridSpec(
        num_scalar_prefetch=1,
        grid=(batch, N_PAIRS, nstep),
        in_specs=[pl.BlockSpec((rows, LANES), lambda b, p, i, far: (b * nstep + i, COL_QB // LANES + p)),
                  pl.BlockSpec((seq, LANES), lambda b, p, i, far: (b, COL_KB // LANES + p)),
                  pl.BlockSpec((seq, LANES), lambda b, p, i, far: (b, COL_VB // LANES + p)),
                  pl.BlockSpec((HEADS_PER_TILE, 2, t, t), lambda b, p, i, far: (p, 0, 0, 0)),
                  pl.BlockSpec(block_lanes.shape, lambda b, p, i, far: (0, 0, 0))],
        out_specs=pl.BlockSpec((rows, LANES), lambda b, p, i, far: (b * nstep + i, p)),
        scratch_shapes=[pltpu.VMEM((MOBA_LO_LANE, LANES), jnp.float32),
                        pltpu.VMEM((HEADS_PER_TILE, nb, LANES, t), jnp.bfloat16),
                        pltpu.VMEM((HEADS_PER_TILE, MOBA_BLOCKS_PER_STEP, t, 2 * LANES), jnp.bfloat16),
                        pltpu.VMEM((HEADS_PER_TILE, F32_SUBLANES, t), jnp.float32),
                        pltpu.VMEM((HEADS_PER_TILE, LANES, t), jnp.float32)])
    return pl.pallas_call(
        kernel,
        grid_spec=grid_spec,
        out_shape=jax.ShapeDtypeStruct((batch * seq, WIDTH), jnp.float32),
        compiler_params=pltpu.CompilerParams(dimension_semantics=("parallel", "parallel", "arbitrary"),
                                             vmem_limit_bytes=VMEM_LIMIT),
        name="moba_attention",
    )(far_bias, u, u, u, btab, block_lanes)


SWA_STEP_WINDOWS = 2


def _swa_kernel(sink_ref, q_ref, kp_ref, kc_ref, vp_ref, vc_ref, bias_ref, o_ref):
    w = SWA_WINDOW
    keys = jnp.concatenate([kp_ref[...], kc_ref[...]], axis=0)
    values = jnp.concatenate([vp_ref[...], vc_ref[...]], axis=0)
    slots = [(j, half) for j in range(SWA_GROUP) for half in range(SWA_KV_HEADS)]
    chains = [(win, slot) for win in range(SWA_STEP_WINDOWS) for slot in range(len(slots))]

    def attend(first_step):
        scores = []
        for win, slot in chains:
            j, half = slots[slot]
            q = q_ref[win * w:(win + 1) * w, j * LANES:(j + 1) * LANES]
            s = _qk(q * _head_lane_mask(half, q.dtype), keys[win * w:(win + 2) * w]) + bias_ref[slot]
            if first_step and win == 0:
                s = jnp.where(lax.broadcasted_iota(jnp.int32, (w, 2 * w), 1) >= w, s, _NEG_INF)
            scores.append(s)
        m = [jnp.maximum(jnp.broadcast_to(jnp.max(s, axis=1, keepdims=True), (w, LANES)), sink_ref[slot])
             for (win, slot), s in zip(chains, scores)]
        p = [jnp.exp2(s - jnp.concatenate([mi] * (2 * w // LANES), axis=1)).astype(jnp.bfloat16)
             for mi, s in zip(m, scores)]
        pv = []
        for (win, slot), pi in zip(chains, p):
            vw = values[win * w:(win + 2) * w]
            vw = jnp.where(_head_lane_mask(slots[slot][1], jnp.int32) > 0, vw, jnp.ones_like(vw))
            pv.append(jnp.dot(pi, vw, preferred_element_type=jnp.float32))
        out = [pvi / (pltpu.roll(pvi, HEAD_DIM, axis=1) + jnp.exp2(sink_ref[slot] - mi))
               for (win, slot), pvi, mi in zip(chains, pv, m)]
        for win in range(SWA_STEP_WINDOWS):
            for j in range(SWA_GROUP):
                first = win * len(slots) + j * SWA_KV_HEADS
                o_ref[win * w:(win + 1) * w, j * LANES:(j + 1) * LANES] = _merge_heads(
                    out[first:first + SWA_KV_HEADS])

    @pl.when(pl.program_id(1) == 0)
    def _():
        attend(True)

    @pl.when(pl.program_id(1) > 0)
    def _():
        attend(False)


def _swa_attention(u, sinks_perm, bias, batch, seq):
    w = SWA_WINDOW
    sw = SWA_STEP_WINDOWS
    nstep = seq // (sw * w)
    kcol, vcol = COL_KC // LANES, COL_VC // LANES
    prev_window = lambda b, n, s: b * (seq // w) + jnp.maximum(sw * n - 1, 0)
    grid_spec = pltpu.PrefetchScalarGridSpec(
        num_scalar_prefetch=1,
        grid=(batch, nstep),
        in_specs=[pl.BlockSpec((sw * w, WIDTH), lambda b, n, s: (b * nstep + n, COL_QC // WIDTH)),
                  pl.BlockSpec((w, LANES), lambda b, n, s: (prev_window(b, n, s), kcol)),
                  pl.BlockSpec((sw * w, LANES), lambda b, n, s: (b * nstep + n, kcol)),
                  pl.BlockSpec((w, LANES), lambda b, n, s: (prev_window(b, n, s), vcol)),
                  pl.BlockSpec((sw * w, LANES), lambda b, n, s: (b * nstep + n, vcol)),
                  pl.BlockSpec((N_HEADS, w, 2 * w), lambda b, n, s: (0, 0, 0))],
        out_specs=pl.BlockSpec((sw * w, WIDTH), lambda b, n, s: (b * nstep + n, 0)))
    return pl.pallas_call(
        _swa_kernel,
        grid_spec=grid_spec,
        out_shape=jax.ShapeDtypeStruct((batch * seq, WIDTH), jnp.float32),
        compiler_params=pltpu.CompilerParams(dimension_semantics=("parallel", "arbitrary"),
                                             vmem_limit_bytes=VMEM_LIMIT),
        name="swa_attention",
    )(sinks_perm, u, u, u, u, u, bias)


def _post_kernel(x_ref, ya_ref, yb_ref, yc_ref, ga_ref, gb_ref, gc_ref, ma_ref, mb_ref, mc_ref,
                 wa_ref, wb_ref, wc_ref, wo_ref, fw_ref, o_ref, *, final):
    def branch(y_ref, g_ref, w_ref, m_ref):
        g = g_ref[...].astype(jnp.float32)
        y = (y_ref[...] * (g * jax.nn.sigmoid(g))).astype(jnp.bfloat16)
        proj = jnp.dot(y, w_ref[...], preferred_element_type=jnp.float32)
        return jax.nn.sigmoid(m_ref[...].astype(jnp.float32)) * proj

    merged = (branch(ya_ref, ga_ref, wa_ref, ma_ref) + branch(yb_ref, gb_ref, wb_ref, mb_ref)
              + branch(yc_ref, gc_ref, wc_ref, mc_ref))
    out = x_ref[...] + jnp.dot(merged.astype(jnp.bfloat16), wo_ref[...], preferred_element_type=jnp.float32)
    if final:
        out = out * lax.rsqrt(jnp.mean(out * out, axis=-1, keepdims=True) + RMS_EPS) * fw_ref[...]
    o_ref[...] = out


def _post(x2, ya, yb, yc, u, wa, wb, wc, wo, fw, layer, *, final, tm=512):
    n = x2.shape[0]
    row = lambda i: (i, 0)
    full = lambda i: (0, 0)
    of_layer = lambda i: (layer, 0, 0)
    y_spec = pl.BlockSpec((tm, WIDTH), row)
    in_specs = [pl.BlockSpec((tm, D_MODEL), row), y_spec, y_spec, y_spec,
                pl.BlockSpec((tm, WIDTH), lambda i: (i, COL_GA // WIDTH)),
                pl.BlockSpec((tm, WIDTH), lambda i: (i, COL_GB // WIDTH)),
                pl.BlockSpec((tm, WIDTH), lambda i: (i, COL_GC // WIDTH)),
                pl.BlockSpec((tm, D_MODEL), lambda i: (i, COL_MA // D_MODEL)),
                pl.BlockSpec((tm, D_MODEL), lambda i: (i, COL_MB // D_MODEL)),
                pl.BlockSpec((tm, D_MODEL), lambda i: (i, COL_MC // D_MODEL)),
                pl.BlockSpec((None, WIDTH, D_MODEL), of_layer), pl.BlockSpec((None, WIDTH, D_MODEL), of_layer),
                pl.BlockSpec((None, WIDTH, D_MODEL), of_layer), pl.BlockSpec((None, D_MODEL, D_MODEL), of_layer),
                pl.BlockSpec((1, D_MODEL), full)]
    return pl.pallas_call(
        functools.partial(_post_kernel, final=final),
        grid=(n // tm,),
        in_specs=in_specs,
        out_specs=pl.BlockSpec((tm, D_MODEL), row),
        out_shape=jax.ShapeDtypeStruct((n, D_MODEL), jnp.float32),
        compiler_params=pltpu.CompilerParams(dimension_semantics=("parallel",),
                                             vmem_limit_bytes=VMEM_LIMIT),
        name="post_final" if final else "post",
    )(x2, ya, yb, yc, u, u, u, u, u, u, wa, wb, wc, wo, fw)


def _bucket_lookup(table, dist):
    onehot = (_rel_bucket(dist)[..., None] == jnp.arange(REL_BUCKETS)).astype(jnp.float32)
    return jnp.einsum("...b,bh->h...", onehot, table.astype(jnp.float32), precision=lax.Precision.HIGHEST)


def _bias_tables(rel_bias):
    moba_tab = rel_bias[:, :N_HEADS]
    swa_tab = _swa_head_order(rel_bias[:, N_HEADS:], 1, per_head=1)
    t = MOBA_BLOCK
    d_own = jnp.arange(t)[:, None] - jnp.arange(t)[None, :]
    own = jnp.where(d_own[None] >= 0, _bucket_lookup(moba_tab, d_own), _NEG_INF)
    prev = _bucket_lookup(moba_tab, d_own + t)
    btab = jnp.swapaxes(jnp.stack([own, prev], axis=1), 2, 3) * LOG2_E
    far = _bucket_lookup(moba_tab, jnp.full((1,), t + 1, jnp.int32))[:, 0] * LOG2_E

    w = SWA_WINDOW
    dist = jnp.arange(w)[:, None] + w - jnp.arange(2 * w)[None, :]
    in_band = (dist >= 0) & (dist < w)
    swa = jnp.where(in_band[None], _bucket_lookup(swa_tab, dist) * LOG2_E, _NEG_INF)
    return btab, far, swa


def kernel(x, norm_w, w_in, w_proj_a, w_proj_b, w_proj_c, w_out, sinks, rel_bias, final_norm_w):
    batch, seq, _ = x.shape
    depth = w_in.shape[0]
    assert seq % MOBA_BLOCK == 0 and MOBA_GROUP <= seq // MOBA_BLOCK <= MOBA_NEVER
    btab, far, swa_bias = _bias_tables(rel_bias)
    t = SB_TILE
    tri = (jnp.arange(t)[:, None] > jnp.arange(t)[None, :]).astype(jnp.bfloat16)
    tri = jnp.concatenate([tri, tri], axis=0)
    lane = jnp.arange(LANES)[None, :]
    blk = jnp.arange(MOBA_ZERO_ROW + 1)[:, None]
    block_lanes = ((lane == blk) | (lane == blk + MOBA_LO_LANE)) & (blk < MOBA_ZERO_ROW)
    block_lanes = jnp.broadcast_to(block_lanes[:, None, :],
                                   (MOBA_ZERO_ROW + 1, BF16_SUBLANES, LANES)).astype(jnp.bfloat16)
    sinks_perm = _swa_head_order(sinks.astype(jnp.float32), 1, per_head=1) * LOG2_E

    w_u = _permute_input_columns(w_in)
    norm_w3 = norm_w.reshape(depth, 1, D_MODEL)
    wa = w_proj_a.astype(jnp.bfloat16)
    wb = w_proj_b.astype(jnp.bfloat16)
    wc = _swa_head_order(w_proj_c, 1).astype(jnp.bfloat16)
    wo = w_out.astype(jnp.bfloat16)
    fw = final_norm_w.reshape(1, D_MODEL)

    x2 = x.reshape(batch * seq, D_MODEL)
    for layer in range(depth):
        u = _inproj(x2, norm_w3, w_u, layer)
        ya = _sb_attention(u, tri, batch, seq)
        yb = _moba_attention(u, far, btab, block_lanes, batch, seq)
        yc = _swa_attention(u, sinks_perm[layer], swa_bias, batch, seq)
        x2 = _post(x2, ya, yb, yc, u, wa, wb, wc, wo, fw, layer, final=(layer == depth - 1))
    return x2.reshape(batch, seq, D_MODEL)
```

```python
import functools
import math

import jax
import jax.numpy as jnp
from jax import lax
from jax.experimental import pallas as pl
from jax.experimental.pallas import tpu as pltpu

D_MODEL = 1024
HEAD_DIM = 64
N_HEADS = 8
WIDTH = N_HEADS * HEAD_DIM
MOBA_BLOCK = 256
MOBA_TOPK = 3
SWA_KV_HEADS = 2
SWA_GROUP = N_HEADS // SWA_KV_HEADS
SWA_WINDOW = 128
REL_BUCKETS = 32
REL_MAX_DIST = 128
RMS_EPS = 1e-6

LANES = 128
BF16_SUBLANES = 16
F32_SUBLANES = 8
HEADS_PER_TILE = LANES // HEAD_DIM
N_PAIRS = N_HEADS // HEADS_PER_TILE

COL_QA, COL_KA, COL_VA = 0, 512, 1024
COL_QB, COL_KB, COL_VB = 1536, 2048, 2560
COL_QC = 3072
COL_GA, COL_GB, COL_GC = 3584, 4096, 4608
COL_MA, COL_MB, COL_MC = 5120, 6144, 7168
COL_KC, COL_VC = 8192, 8320
D_IN = 8448

V7X_VMEM_BYTES = 64 * 1024 * 1024
VMEM_LIMIT = V7X_VMEM_BYTES * 3 // 4

LOG2_E = math.log2(math.e)

SB_DEAD_LOG_WEIGHT = -120.0
SB_DEAD_LOG2_WEIGHT = SB_DEAD_LOG_WEIGHT * LOG2_E

_NEG_INF = float("-inf")


def _rel_bucket(dist):
    max_exact = REL_BUCKETS // 2
    n = jnp.maximum(dist, 0)
    nf = jnp.maximum(n, 1).astype(jnp.float32)
    large = max_exact + (jnp.log(nf / max_exact) / math.log(REL_MAX_DIST / max_exact)
                         * (REL_BUCKETS - max_exact)).astype(jnp.int32)
    large = jnp.minimum(large, REL_BUCKETS - 1)
    return jnp.where(n < max_exact, n, large)


def _swa_head_order(t, axis, per_head=HEAD_DIM):
    axis = axis % t.ndim
    shape = t.shape
    t = t.reshape(shape[:axis] + (SWA_KV_HEADS, SWA_GROUP, per_head) + shape[axis + 1:])
    return jnp.swapaxes(t, axis, axis + 1).reshape(shape)


def _permute_input_columns(w):
    old = {"qa": 0, "ka": 512, "va": 1024, "ga": 1536, "qb": 2048, "kb": 2560, "vb": 3072, "gb": 3584,
           "qc": 4096, "kc": 4608, "vc": 4736, "gc": 4864, "ma": 5376, "mb": 6400, "mc": 7424}
    q_scale = HEAD_DIM ** -0.5 * LOG2_E

    def cols(name, width, scale=None):
        part = w[..., old[name]:old[name] + width]
        return part if scale is None else part * scale

    parts = [cols("qa", 512, q_scale), cols("ka", 512), cols("va", 512),
             cols("qb", 512, q_scale), cols("kb", 512), cols("vb", 512),
             _swa_head_order(cols("qc", WIDTH, q_scale), -1),
             cols("ga", 512), cols("gb", 512), _swa_head_order(cols("gc", WIDTH), -1),
             cols("ma", 1024), cols("mb", 1024), cols("mc", 1024),
             cols("kc", 128), cols("vc", 128)]
    return jnp.concatenate(parts, axis=-1).astype(jnp.bfloat16)


def _inproj_kernel(x_ref, nw_ref, w_ref, u_ref, h_ref):
    @pl.when(pl.program_id(1) == 0)
    def _():
        x = x_ref[...]
        y = x * lax.rsqrt(jnp.mean(x * x, axis=-1, keepdims=True) + RMS_EPS)
        h_ref[...] = (y * nw_ref[...]).astype(h_ref.dtype)

    u_ref[...] = jnp.dot(h_ref[...], w_ref[...], preferred_element_type=jnp.float32).astype(u_ref.dtype)


def _inproj(x2, norm_w, w, layer, *, tm=1024, tn=2816):
    n = x2.shape[0]
    return pl.pallas_call(
        _inproj_kernel,
        grid=(n // tm, D_IN // tn),
        in_specs=[pl.BlockSpec((tm, D_MODEL), lambda i, j: (i, 0)),
                  pl.BlockSpec((None, 1, D_MODEL), lambda i, j: (layer, 0, 0)),
                  pl.BlockSpec((None, D_MODEL, tn), lambda i, j: (layer, 0, j))],
        out_specs=pl.BlockSpec((tm, tn), lambda i, j: (i, j)),
        out_shape=jax.ShapeDtypeStruct((n, D_IN), jnp.bfloat16),
        scratch_shapes=[pltpu.VMEM((tm, D_MODEL), jnp.bfloat16)],
        compiler_params=pltpu.CompilerParams(dimension_semantics=("parallel", "arbitrary"),
                                             vmem_limit_bytes=VMEM_LIMIT),
        name="inproj",
    )(x2, norm_w, w)


def _head_lane_mask(hh, dtype):
    lane = lax.broadcasted_iota(jnp.int32, (1, LANES), 1)
    return ((lane // HEAD_DIM) == hh).astype(dtype)


def _qk(qm, k):
    return lax.dot_general(qm, k, (((1,), (1,)), ((), ())), preferred_element_type=jnp.float32)


def _merge_heads(parts):
    lane = lax.broadcasted_iota(jnp.int32, parts[0].shape, 1)
    return jnp.where(lane < HEAD_DIM, parts[0], parts[1])


SB_TILE = 256
SB_TILES_PER_STEP = 8
SB_PAIR = 4
SB_ROW_CHUNK = 256


def _sb_kernel(q_ref, k_ref, v_ref, tri_ref, o_ref, acc_ref, carry_ref):
    t = SB_TILE

    def step(tiles):
        tri = tri_ref[...]
        rc = SB_ROW_CHUNK
        chains = [(tile, hh, r) for tile in tiles for hh in range(HEADS_PER_TILE) for r in range(t // rc)]

        def block_start(tile, j):
            return pl.multiple_of((tile[2] - j) * t, t)

        terms = {}
        for tile, hh, r in chains:
            slot, row0, _, n_blocks, diagonal = tile
            qm = q_ref[pl.ds(row0 + r * rc, rc), :] * _head_lane_mask(hh, q_ref.dtype)
            for j in range(n_blocks):
                z = _qk(qm, k_ref[pl.ds(block_start(tile, j), t), :])
                neg_abs = lax.bitcast_convert_type(
                    lax.bitcast_convert_type(z, jnp.uint32) | jnp.uint32(0x80000000), jnp.float32)
                soft = jnp.log(1.0 + jnp.exp2(neg_abs)) * LOG2_E
                log_beta = jnp.minimum(z, 0.0) - soft
                log_1m = log_beta - z
                mask = None
                if diagonal and j == 0:
                    row = lax.broadcasted_iota(jnp.int32, (rc, t), 0) + r * rc
                    mask = lax.broadcasted_iota(jnp.int32, (rc, t), 1) < row
                    log_1m = jnp.where(mask, log_1m, 0.0)
                hi = log_1m.astype(jnp.bfloat16)
                split = jnp.concatenate([hi, (log_1m - hi.astype(jnp.float32)).astype(jnp.bfloat16)], axis=1)
                terms[slot, hh, r, j] = (log_beta, log_1m[:, 0:1], split, mask)

        cums = {key: jnp.dot(split, tri, preferred_element_type=jnp.float32)
                for key, (_, _, split, _) in terms.items()}

        alive = {}
        for tile, hh, r in chains:
            slot, _, _, n_blocks, _ = tile
            rows = slice(r * rc, (r + 1) * rc)
            carry = carry_ref[slot, hh, rows, :]
            acc = acc_ref[slot, hh, rows, :]
            for j in range(n_blocks):
                log_beta, first_term, _, mask = terms[slot, hh, r, j]
                cum = cums[slot, hh, r, j]
                w = jnp.exp2(log_beta + cum + jnp.concatenate([carry] * (t // LANES), axis=1))
                if mask is not None:
                    w = jnp.where(mask, w, 0.0)
                acc = acc + jnp.dot(w.astype(jnp.bfloat16), v_ref[pl.ds(block_start(tile, j), t), :],
                                    preferred_element_type=jnp.float32)
                carry = carry + jnp.broadcast_to(cum[:, 0:1] + first_term, (rc, LANES))
            acc_ref[slot, hh, rows, :] = acc
            carry_ref[slot, hh, rows, :] = carry
            alive[slot] = carry if slot not in alive else jnp.maximum(alive[slot], carry)
        return tuple((jnp.max(alive[tile[0]]) < SB_DEAD_LOG2_WEIGHT).astype(jnp.int32) for tile in tiles)

    def query_tiles(pair, c):
        qi = [pl.program_id(2) * SB_TILES_PER_STEP + SB_PAIR * pair + s for s in range(SB_PAIR)]
        row0 = [pl.multiple_of((SB_PAIR * pair + s) * t, t) for s in range(SB_PAIR)]
        acc_ref[...] = jnp.zeros_like(acc_ref)
        carry_ref[...] = jnp.zeros_like(carry_ref)

        def first(n_first):
            return step([(0, row0[0], qi[0], n_first, True)]
                        + [(s, row0[s], qi[s], 2, True) for s in range(1, SB_PAIR)])

        dead = lax.cond(qi[0] >= 1, lambda: first(2), lambda: first(1))

        for s in range(SB_PAIR):
            def cond(c):
                newest, dead = c
                return jnp.logical_and(newest >= 1, dead == 0)

            def body(c, s=s):
                newest, _ = c
                return newest - 2, step([(s, row0[s], newest, 2, False)])[0]

            newest, still = lax.while_loop(cond, body, (qi[s] - 2, dead[s]))

            @pl.when(jnp.logical_and(newest == 0, still == 0))
            def _(s=s):
                step([(s, row0[s], 0, 1, False)])

            o_ref[pl.ds(row0[s], t), :] = _merge_heads([acc_ref[s, 0], acc_ref[s, 1]])
        return c

    lax.fori_loop(0, SB_TILES_PER_STEP // SB_PAIR, query_tiles, 0)


def _sb_attention(u, tri, batch, seq):
    t = SB_TILE
    rows = SB_TILES_PER_STEP * t
    nq = seq // rows
    return pl.pallas_call(
        _sb_kernel,
        grid=(batch, N_PAIRS, nq),
        in_specs=[pl.BlockSpec((rows, LANES), lambda b, p, i: (b * nq + i, COL_QA // LANES + p)),
                  pl.BlockSpec((seq, LANES), lambda b, p, i: (b, COL_KA // LANES + p)),
                  pl.BlockSpec((seq, LANES), lambda b, p, i: (b, COL_VA // LANES + p)),
                  pl.BlockSpec((2 * t, t), lambda b, p, i: (0, 0))],
        out_specs=pl.BlockSpec((rows, LANES), lambda b, p, i: (b * nq + i, p)),
        out_shape=jax.ShapeDtypeStruct((batch * seq, WIDTH), jnp.float32),
        scratch_shapes=[pltpu.VMEM((SB_PAIR, HEADS_PER_TILE, t, LANES), jnp.float32),
                        pltpu.VMEM((SB_PAIR, HEADS_PER_TILE, t, LANES), jnp.float32)],
        compiler_params=pltpu.CompilerParams(dimension_semantics=("parallel", "parallel", "arbitrary"),
                                             vmem_limit_bytes=VMEM_LIMIT),
        name="sb_attention",
    )(u, u, u, tri)


MOBA_LO_LANE = 64
MOBA_NEVER = 63
MOBA_ZERO_ROW = 64
MOBA_MASKED = -1e30
MOBA_GROUP = 8
MOBA_BLOCKS_PER_STEP = 4


def _moba_kernel(far_ref, q_ref, k_ref, v_ref, btab_ref, oh_ref, o_ref,
                 kmean_ref, vt_ref, qa_ref, m_ref, acc_ref, *, nb):
    t = MOBA_BLOCK
    nsub = MOBA_BLOCKS_PER_STEP
    nh = HEADS_PER_TILE
    heads = range(nh)
    pair = pl.program_id(1)

    @pl.when(pl.program_id(2) == 0)
    def _():
        kmean_ref[...] = jnp.zeros_like(kmean_ref)
        kf = k_ref[...].astype(jnp.float32).reshape(nb, t, LANES)
        kmean_ref[0:nb, :] = jnp.mean(kf, axis=1)

        head_of_row = lax.broadcasted_iota(jnp.int32, (LANES, t), 0) // HEAD_DIM

        def transpose_block(b, c):
            vb = v_ref[pl.ds(pl.multiple_of(b * t, t), t), :].astype(jnp.float32).T
            for hh in heads:
                vt_ref[hh, b] = jnp.where(head_of_row == hh, vb, 1.0).astype(jnp.bfloat16)
            return c

        lax.fori_loop(0, nb, transpose_block, 0)

    rows = -(-nb // F32_SUBLANES) * F32_SUBLANES
    km = kmean_ref[0:rows, :]
    km0 = km.astype(jnp.bfloat16)
    r1 = km - km0.astype(jnp.float32)
    km1 = r1.astype(jnp.bfloat16)
    km2 = (r1 - km1.astype(jnp.float32)).astype(jnp.bfloat16)

    cols = nsub * t
    blk = lax.broadcasted_iota(jnp.int32, (rows, nh * cols), 0)
    blk_f = blk.astype(jnp.float32)
    col = lax.broadcasted_iota(jnp.int32, (1, nh * cols), 1)
    own_col = pl.program_id(2) * nsub + (col % cols) // t
    gate_rows = sum(_qk(jnp.concatenate([part * _head_lane_mask(hh, part.dtype) for hh in heads], axis=0),
                        q_ref[...]) for part in (km0, km1, km2))
    gate = jnp.concatenate([gate_rows[hh * rows:(hh + 1) * rows] for hh in heads], axis=1)
    gate = jnp.where(blk < own_col, gate, _NEG_INF)
    sel = jnp.zeros_like(gate)
    for _ in range(MOBA_TOPK):
        best = jnp.max(gate, axis=0, keepdims=True)
        idx = jnp.min(jnp.where(gate == best, blk_f, float(LANES)), axis=0, keepdims=True)
        hit = blk_f == idx
        sel = jnp.where(jnp.logical_and(hit, best > _NEG_INF), 1.0, sel)
        gate = jnp.where(hit, _NEG_INF, gate)

    far_bias = jnp.where(col // cols == 0, far_ref[pair * nh], far_ref[pair * nh + 1])
    term = jnp.where(sel > 0.0, jnp.where(blk < own_col - 1, far_bias, 0.0), MOBA_MASKED)
    hi = term.astype(jnp.bfloat16).astype(jnp.float32)
    lo = term - hi
    pad = MOBA_LO_LANE - rows
    pad_hi = [jnp.full((pad, t), MOBA_MASKED, jnp.float32)] if pad else []
    pad_lo = [jnp.zeros((pad, t), jnp.float32)] if pad else []
    for hh in heads:
        for sub in range(nsub):
            cs = slice(hh * cols + sub * t, hh * cols + (sub + 1) * t)
            inj = jnp.concatenate([hi[:, cs]] + pad_hi + [lo[:, cs]] + pad_lo, axis=0).T
            q_sub = q_ref[sub * t:(sub + 1) * t, :] * _head_lane_mask(hh, q_ref.dtype)
            qa_ref[hh, sub] = jnp.concatenate([q_sub, inj.astype(jnp.bfloat16)], axis=1)

    def query_block(sub, c):
        _moba_query_block(sub, k_ref, btab_ref, oh_ref, o_ref, vt_ref, qa_ref, m_ref, acc_ref, nb=nb)
        return c

    lax.fori_loop(0, nsub, query_block, 0)


def _moba_query_block(sub, k_ref, btab_ref, oh_ref, o_ref, vt_ref, qa_ref, m_ref, acc_ref, *, nb):
    t = MOBA_BLOCK
    g = MOBA_GROUP
    own = pl.program_id(2) * MOBA_BLOCKS_PER_STEP + sub
    q_rows = pl.ds(pl.multiple_of(sub * t, t), t)
    heads = range(HEADS_PER_TILE)

    def update(scores, blocks, first):
        col_max = [jnp.max(s, axis=0, keepdims=True) for s in scores]
        if first:
            m_new = col_max
        else:
            m_old = [m_ref[hh][0:1, :] for hh in heads]
            m_new = [jnp.maximum(m_old[hh], col_max[hh]) for hh in heads]
            alpha = [jnp.exp2(m_old[hh] - m_new[hh]) for hh in heads]
        p = [jnp.exp2(scores[hh] - m_new[hh]).astype(jnp.bfloat16) for hh in heads]
        pv = [jnp.dot(jnp.concatenate([vt_ref[hh, b] for b in blocks], axis=1), p[hh],
                      preferred_element_type=jnp.float32) for hh in heads]
        for hh in heads:
            acc_ref[hh] = pv[hh] if first else alpha[hh] * acc_ref[hh] + pv[hh]
            m_ref[hh] = jnp.broadcast_to(m_new[hh], m_ref.shape[1:])

    def block_lanes(row):
        return jnp.concatenate([oh_ref[row]] * (t // BF16_SUBLANES), axis=0)

    prev = jnp.maximum(own - 1, 0)
    n_far = jnp.maximum(own - 1, 0)
    n_full = n_far // g
    left = n_far - n_full * g

    def far_keys(first_blk, group):
        g0 = jnp.minimum(first_blk, nb - group)
        rows = []
        for j in range(group):
            b = g0 + j
            active = jnp.logical_and(b >= first_blk, b < n_far)
            rows.append(block_lanes(jnp.where(active, b, MOBA_NEVER)))
        keys = k_ref[pl.ds(pl.multiple_of(g0 * t, t), group * t), :]
        return g0, keys, rows

    def first_step(group):
        keys = [k_ref[pl.ds(pl.multiple_of(own * t, t), t), :], k_ref[pl.ds(pl.multiple_of(prev * t, t), t), :]]
        rows = [block_lanes(MOBA_ZERO_ROW), block_lanes(jnp.where(own >= 1, prev, MOBA_NEVER))]
        if group:
            g0, far, far_rows = far_keys(n_full * g, group)
            keys.append(far)
            rows.extend(far_rows)
        k_aug = jnp.concatenate([jnp.concatenate(keys, axis=0), jnp.concatenate(rows, axis=0)], axis=1)
        scores = [_qk(k_aug, qa_ref[hh, sub]) for hh in heads]
        update([scores[hh][:t] + btab_ref[hh, 0] for hh in heads], [own], first=True)
        update([scores[hh][t:2 * t] + btab_ref[hh, 1] for hh in heads], [prev], first=False)
        for j in range(group):
            update([s[(2 + j) * t:(3 + j) * t] for s in scores], [g0 + j], first=False)

    for count in range(g):
        @pl.when(left == count)
        def _(count=count):
            first_step(count)

    def far_body(i, c):
        g0, keys, rows = far_keys(i * g, g)
        k_aug = jnp.concatenate([keys, jnp.concatenate(rows, axis=0)], axis=1)
        scores = [_qk(k_aug, qa_ref[hh, sub]) for hh in heads]
        for j in range(g):
            update([s[j * t:(j + 1) * t] for s in scores], [g0 + j], first=False)
        return c

    lax.fori_loop(0, n_full, far_body, 0)

    out_t = jnp.concatenate([acc_ref[0][:HEAD_DIM] / acc_ref[0][HEAD_DIM:],
                             acc_ref[1][HEAD_DIM:] / acc_ref[1][:HEAD_DIM]], axis=0)
    o_ref[q_rows, :] = out_t.T


def _moba_attention(u, far_bias, btab, block_lanes, batch, seq):
    t = MOBA_BLOCK
    nb = seq // t
    rows = MOBA_BLOCKS_PER_STEP * t
    nstep = seq // rows
    kernel = functools.partial(_moba_kernel, nb=nb)
    grid_spec = pltpu.PrefetchScalarGridSpec(
        num_scalar_prefetch=1,
        grid=(batch, N_PAIRS, nstep),
        in_specs=[pl.BlockSpec((rows, LANES), lambda b, p, i, far: (b * nstep + i, COL_QB // LANES + p)),
                  pl.BlockSpec((seq, LANES), lambda b, p, i, far: (b, COL_KB // LANES + p)),
                  pl.BlockSpec((seq, LANES), lambda b, p, i, far: (b, COL_VB // LANES + p)),
                  pl.BlockSpec((HEADS_PER_TILE, 2, t, t), lambda b, p, i, far: (p, 0, 0, 0)),
                  pl.BlockSpec(block_lanes.shape, lambda b, p, i, far: (0, 0, 0))],
        out_specs=pl.BlockSpec((rows, LANES), lambda b, p, i, far: (b * nstep + i, p)),
        scratch_shapes=[pltpu.VMEM((MOBA_LO_LANE, LANES), jnp.float32),
                        pltpu.VMEM((HEADS_PER_TILE, nb, LANES, t), jnp.bfloat16),
                        pltpu.VMEM((HEADS_PER_TILE, MOBA_BLOCKS_PER_STEP, t, 2 * LANES), jnp.bfloat16),
                        pltpu.VMEM((HEADS_PER_TILE, F32_SUBLANES, t), jnp.float32),
                        pltpu.VMEM((HEADS_PER_TILE, LANES, t), jnp.float32)])
    return pl.pallas_call(
        kernel,
        grid_spec=grid_spec,
        out_shape=jax.ShapeDtypeStruct((batch * seq, WIDTH), jnp.float32),
        compiler_params=pltpu.CompilerParams(dimension_semantics=("parallel", "parallel", "arbitrary"),
                                             vmem_limit_bytes=VMEM_LIMIT),
        name="moba_attention",
    )(far_bias, u, u, u, btab, block_lanes)


SWA_STEP_WINDOWS = 4


def _swa_kernel(sink_ref, q_ref, kp_ref, kc_ref, vp_ref, vc_ref, bias_ref, o_ref):
    w = SWA_WINDOW
    keys = jnp.concatenate([kp_ref[...], kc_ref[...]], axis=0)
    values = jnp.concatenate([vp_ref[...], vc_ref[...]], axis=0)
    slots = [(j, half) for j in range(SWA_GROUP) for half in range(SWA_KV_HEADS)]
    chains = [(win, slot) for win in range(SWA_STEP_WINDOWS) for slot in range(len(slots))]

    def attend(first_step):
        scores = []
        for win, slot in chains:
            j, half = slots[slot]
            q = q_ref[win * w:(win + 1) * w, j * LANES:(j + 1) * LANES]
            s = _qk(q * _head_lane_mask(half, q.dtype), keys[win * w:(win + 2) * w]) + bias_ref[slot]
            if first_step and win == 0:
                s = jnp.where(lax.broadcasted_iota(jnp.int32, (w, 2 * w), 1) >= w, s, _NEG_INF)
            scores.append(s)
        m = [jnp.maximum(jnp.broadcast_to(jnp.max(s, axis=1, keepdims=True), (w, LANES)), sink_ref[slot])
             for (win, slot), s in zip(chains, scores)]
        p = [jnp.exp2(s - jnp.concatenate([mi] * (2 * w // LANES), axis=1)).astype(jnp.bfloat16)
             for mi, s in zip(m, scores)]
        pv = []
        for (win, slot), pi in zip(chains, p):
            vw = values[win * w:(win + 2) * w]
            vw = jnp.where(_head_lane_mask(slots[slot][1], jnp.int32) > 0, vw, jnp.ones_like(vw))
            pv.append(jnp.dot(pi, vw, preferred_element_type=jnp.float32))
        out = [pvi / (pltpu.roll(pvi, HEAD_DIM, axis=1) + jnp.exp2(sink_ref[slot] - mi))
               for (win, slot), pvi, mi in zip(chains, pv, m)]
        for win in range(SWA_STEP_WINDOWS):
            for j in range(SWA_GROUP):
                first = win * len(slots) + j * SWA_KV_HEADS
                o_ref[win * w:(win + 1) * w, j * LANES:(j + 1) * LANES] = _merge_heads(
                    out[first:first + SWA_KV_HEADS])

    @pl.when(pl.program_id(1) == 0)
    def _():
        attend(True)

    @pl.when(pl.program_id(1) > 0)
    def _():
        attend(False)


def _swa_attention(u, sinks_perm, bias, batch, seq):
    w = SWA_WINDOW
    sw = SWA_STEP_WINDOWS
    nstep = seq // (sw * w)
    kcol, vcol = COL_KC // LANES, COL_VC // LANES
    prev_window = lambda b, n, s: b * (seq // w) + jnp.maximum(sw * n - 1, 0)
    grid_spec = pltpu.PrefetchScalarGridSpec(
        num_scalar_prefetch=1,
        grid=(batch, nstep),
        in_specs=[pl.BlockSpec((sw * w, WIDTH), lambda b, n, s: (b * nstep + n, COL_QC // WIDTH)),
                  pl.BlockSpec((w, LANES), lambda b, n, s: (prev_window(b, n, s), kcol)),
                  pl.BlockSpec((sw * w, LANES), lambda b, n, s: (b * nstep + n, kcol)),
                  pl.BlockSpec((w, LANES), lambda b, n, s: (prev_window(b, n, s), vcol)),
                  pl.BlockSpec((sw * w, LANES), lambda b, n, s: (b * nstep + n, vcol)),
                  pl.BlockSpec((N_HEADS, w, 2 * w), lambda b, n, s: (0, 0, 0))],
        out_specs=pl.BlockSpec((sw * w, WIDTH), lambda b, n, s: (b * nstep + n, 0)))
    return pl.pallas_call(
        _swa_kernel,
        grid_spec=grid_spec,
        out_shape=jax.ShapeDtypeStruct((batch * seq, WIDTH), jnp.float32),
        compiler_params=pltpu.CompilerParams(dimension_semantics=("parallel", "arbitrary"),
                                             vmem_limit_bytes=VMEM_LIMIT),
        name="swa_attention",
    )(sinks_perm, u, u, u, u, u, bias)


def _post_kernel(x_ref, ya_ref, yb_ref, yc_ref, ga_ref, gb_ref, gc_ref, ma_ref, mb_ref, mc_ref,
                 wa_ref, wb_ref, wc_ref, wo_ref, fw_ref, o_ref, *, final):
    def branch(y_ref, g_ref, w_ref, m_ref):
        g = g_ref[...].astype(jnp.float32)
        y = (y_ref[...] * (g * jax.nn.sigmoid(g))).astype(jnp.bfloat16)
        proj = jnp.dot(y, w_ref[...], preferred_element_type=jnp.float32)
        return jax.nn.sigmoid(m_ref[...].astype(jnp.float32)) * proj

    merged = (branch(ya_ref, ga_ref, wa_ref, ma_ref) + branch(yb_ref, gb_ref, wb_ref, mb_ref)
              + branch(yc_ref, gc_ref, wc_ref, mc_ref))
    out = x_ref[...] + jnp.dot(merged.astype(jnp.bfloat16), wo_ref[...], preferred_element_type=jnp.float32)
    if final:
        out = out * lax.rsqrt(jnp.mean(out * out, axis=-1, keepdims=True) + RMS_EPS) * fw_ref[...]
    o_ref[...] = out


def _post(x2, ya, yb, yc, u, wa, wb, wc, wo, fw, layer, *, final, tm=512):
    n = x2.shape[0]
    row = lambda i: (i, 0)
    full = lambda i: (0, 0)
    of_layer = lambda i: (layer, 0, 0)
    y_spec = pl.BlockSpec((tm, WIDTH), row)
    in_specs = [pl.BlockSpec((tm, D_MODEL), row), y_spec, y_spec, y_spec,
                pl.BlockSpec((tm, WIDTH), lambda i: (i, COL_GA // WIDTH)),
                pl.BlockSpec((tm, WIDTH), lambda i: (i, COL_GB // WIDTH)),
                pl.BlockSpec((tm, WIDTH), lambda i: (i, COL_GC // WIDTH)),
                pl.BlockSpec((tm, D_MODEL), lambda i: (i, COL_MA // D_MODEL)),
                pl.BlockSpec((tm, D_MODEL), lambda i: (i, COL_MB // D_MODEL)),
                pl.BlockSpec((tm, D_MODEL), lambda i: (i, COL_MC // D_MODEL)),
                pl.BlockSpec((None, WIDTH, D_MODEL), of_layer), pl.BlockSpec((None, WIDTH, D_MODEL), of_layer),
                pl.BlockSpec((None, WIDTH, D_MODEL), of_layer), pl.BlockSpec((None, D_MODEL, D_MODEL), of_layer),
                pl.BlockSpec((1, D_MODEL), full)]
    return pl.pallas_call(
        functools.partial(_post_kernel, final=final),
        grid=(n // tm,),
        in_specs=in_specs,
        out_specs=pl.BlockSpec((tm, D_MODEL), row),
        out_shape=jax.ShapeDtypeStruct((n, D_MODEL), jnp.float32),
        compiler_params=pltpu.CompilerParams(dimension_semantics=("parallel",),
                                             vmem_limit_bytes=VMEM_LIMIT),
        name="post_final" if final else "post",
    )(x2, ya, yb, yc, u, u, u, u, u, u, wa, wb, wc, wo, fw)


def _bucket_lookup(table, dist):
    onehot = (_rel_bucket(dist)[..., None] == jnp.arange(REL_BUCKETS)).astype(jnp.float32)
    return jnp.einsum("...b,bh->h...", onehot, table.astype(jnp.float32), precision=lax.Precision.HIGHEST)


def _bias_tables(rel_bias):
    moba_tab = rel_bias[:, :N_HEADS]
    swa_tab = _swa_head_order(rel_bias[:, N_HEADS:], 1, per_head=1)
    t = MOBA_BLOCK
    d_own = jnp.arange(t)[:, None] - jnp.arange(t)[None, :]
    own = jnp.where(d_own[None] >= 0, _bucket_lookup(moba_tab, d_own), _NEG_INF)
    prev = _bucket_lookup(moba_tab, d_own + t)
    btab = jnp.swapaxes(jnp.stack([own, prev], axis=1), 2, 3) * LOG2_E
    far = _bucket_lookup(moba_tab, jnp.full((1,), t + 1, jnp.int32))[:, 0] * LOG2_E

    w = SWA_WINDOW
    dist = jnp.arange(w)[:, None] + w - jnp.arange(2 * w)[None, :]
    in_band = (dist >= 0) & (dist < w)
    swa = jnp.where(in_band[None], _bucket_lookup(swa_tab, dist) * LOG2_E, _NEG_INF)
    return btab, far, swa


def kernel(x, norm_w, w_in, w_proj_a, w_proj_b, w_proj_c, w_out, sinks, rel_bias, final_norm_w):
    batch, seq, _ = x.shape
    depth = w_in.shape[0]
    assert seq % MOBA_BLOCK == 0 and MOBA_GROUP <= seq // MOBA_BLOCK <= MOBA_NEVER
    btab, far, swa_bias = _bias_tables(rel_bias)
    t = SB_TILE
    tri = (jnp.arange(t)[:, None] > jnp.arange(t)[None, :]).astype(jnp.bfloat16)
    tri = jnp.concatenate([tri, tri], axis=0)
    lane = jnp.arange(LANES)[None, :]
    blk = jnp.arange(MOBA_ZERO_ROW + 1)[:, None]
    block_lanes = ((lane == blk) | (lane == blk + MOBA_LO_LANE)) & (blk < MOBA_ZERO_ROW)
    block_lanes = jnp.broadcast_to(block_lanes[:, None, :],
                                   (MOBA_ZERO_ROW + 1, BF16_SUBLANES, LANES)).astype(jnp.bfloat16)
    sinks_perm = _swa_head_order(sinks.astype(jnp.float32), 1, per_head=1) * LOG2_E

    w_u = _permute_input_columns(w_in)
    norm_w3 = norm_w.reshape(depth, 1, D_MODEL)
    wa = w_proj_a.astype(jnp.bfloat16)
    wb = w_proj_b.astype(jnp.bfloat16)
    wc = _swa_head_order(w_proj_c, 1).astype(jnp.bfloat16)
    wo = w_out.astype(jnp.bfloat16)
    fw = final_norm_w.reshape(1, D_MODEL)

    x2 = x.reshape(batch * seq, D_MODEL)
    for layer in range(depth):
        u = _inproj(x2, norm_w3, w_u, layer)
        ya = _sb_attention(u, tri, batch, seq)
        yb = _moba_attention(u, far, btab, block_lanes, batch, seq)
        yc = _swa_attention(u, sinks_perm[layer], swa_bias, batch, seq)
        x2 = _post(x2, ya, yb, yc, u, wa, wb, wc, wo, fw, layer, final=(layer == depth - 1))
    return x2.reshape(batch, seq, D_MODEL)
```

```python
import functools
import math

import jax
import jax.numpy as jnp
from jax import lax
from jax.experimental import pallas as pl
from jax.experimental.pallas import tpu as pltpu

D_MODEL = 1024
HEAD_DIM = 64
N_HEADS = 8
WIDTH = N_HEADS * HEAD_DIM
MOBA_BLOCK = 256
MOBA_TOPK = 3
SWA_KV_HEADS = 2
SWA_GROUP = N_HEADS // SWA_KV_HEADS
SWA_WINDOW = 128
REL_BUCKETS = 32
REL_MAX_DIST = 128
RMS_EPS = 1e-6

LANES = 128
BF16_SUBLANES = 16
F32_SUBLANES = 8
HEADS_PER_TILE = LANES // HEAD_DIM
N_PAIRS = N_HEADS // HEADS_PER_TILE

COL_QA, COL_KA, COL_VA = 0, 512, 1024
COL_QB, COL_KB, COL_VB = 1536, 2048, 2560
COL_QC = 3072
COL_GA, COL_GB, COL_GC = 3584, 4096, 4608
COL_MA, COL_MB, COL_MC = 5120, 6144, 7168
COL_KC, COL_VC = 8192, 8320
D_IN = 8448

V7X_VMEM_BYTES = 64 * 1024 * 1024
VMEM_LIMIT = V7X_VMEM_BYTES * 3 // 4

LOG2_E = math.log2(math.e)

SB_DEAD_LOG_WEIGHT = -120.0
SB_DEAD_LOG2_WEIGHT = SB_DEAD_LOG_WEIGHT * LOG2_E

_NEG_INF = float("-inf")


def _rel_bucket(dist):
    max_exact = REL_BUCKETS // 2
    n = jnp.maximum(dist, 0)
    nf = jnp.maximum(n, 1).astype(jnp.float32)
    large = max_exact + (jnp.log(nf / max_exact) / math.log(REL_MAX_DIST / max_exact)
                         * (REL_BUCKETS - max_exact)).astype(jnp.int32)
    large = jnp.minimum(large, REL_BUCKETS - 1)
    return jnp.where(n < max_exact, n, large)


def _swa_head_order(t, axis, per_head=HEAD_DIM):
    axis = axis % t.ndim
    shape = t.shape
    t = t.reshape(shape[:axis] + (SWA_KV_HEADS, SWA_GROUP, per_head) + shape[axis + 1:])
    return jnp.swapaxes(t, axis, axis + 1).reshape(shape)


def _permute_input_columns(w):
    old = {"qa": 0, "ka": 512, "va": 1024, "ga": 1536, "qb": 2048, "kb": 2560, "vb": 3072, "gb": 3584,
           "qc": 4096, "kc": 4608, "vc": 4736, "gc": 4864, "ma": 5376, "mb": 6400, "mc": 7424}
    q_scale = HEAD_DIM ** -0.5 * LOG2_E

    def cols(name, width, scale=None):
        part = w[..., old[name]:old[name] + width]
        return part if scale is None else part * scale

    parts = [cols("qa", 512, q_scale), cols("ka", 512), cols("va", 512),
             cols("qb", 512, q_scale), cols("kb", 512), cols("vb", 512),
             _swa_head_order(cols("qc", WIDTH, q_scale), -1),
             cols("ga", 512), cols("gb", 512), _swa_head_order(cols("gc", WIDTH), -1),
             cols("ma", 1024), cols("mb", 1024), cols("mc", 1024),
             cols("kc", 128), cols("vc", 128)]
    return jnp.concatenate(parts, axis=-1).astype(jnp.bfloat16)


def _inproj_kernel(x_ref, nw_ref, w_ref, u_ref, h_ref):
    @pl.when(pl.program_id(1) == 0)
    def _():
        x = x_ref[...]
        y = x * lax.rsqrt(jnp.mean(x * x, axis=-1, keepdims=True) + RMS_EPS)
        h_ref[...] = (y * nw_ref[...]).astype(h_ref.dtype)

    u_ref[...] = jnp.dot(h_ref[...], w_ref[...], preferred_element_type=jnp.float32).astype(u_ref.dtype)


def _inproj(x2, norm_w, w, layer, *, tm=1024, tn=2816):
    n = x2.shape[0]
    return pl.pallas_call(
        _inproj_kernel,
        grid=(n // tm, D_IN // tn),
        in_specs=[pl.BlockSpec((tm, D_MODEL), lambda i, j: (i, 0)),
                  pl.BlockSpec((None, 1, D_MODEL), lambda i, j: (layer, 0, 0)),
                  pl.BlockSpec((None, D_MODEL, tn), lambda i, j: (layer, 0, j))],
        out_specs=pl.BlockSpec((tm, tn), lambda i, j: (i, j)),
        out_shape=jax.ShapeDtypeStruct((n, D_IN), jnp.bfloat16),
        scratch_shapes=[pltpu.VMEM((tm, D_MODEL), jnp.bfloat16)],
        compiler_params=pltpu.CompilerParams(dimension_semantics=("parallel", "arbitrary"),
                                             vmem_limit_bytes=VMEM_LIMIT),
        name="inproj",
    )(x2, norm_w, w)


def _head_lane_mask(hh, dtype):
    lane = lax.broadcasted_iota(jnp.int32, (1, LANES), 1)
    return ((lane // HEAD_DIM) == hh).astype(dtype)


def _qk(qm, k):
    return lax.dot_general(qm, k, (((1,), (1,)), ((), ())), preferred_element_type=jnp.float32)


def _merge_heads(parts):
    lane = lax.broadcasted_iota(jnp.int32, parts[0].shape, 1)
    return jnp.where(lane < HEAD_DIM, parts[0], parts[1])


SB_TILE = 256
SB_TILES_PER_STEP = 8
SB_PAIR = 4
SB_ROW_CHUNK = 256


def _sb_kernel(q_ref, k_ref, v_ref, tri_ref, o_ref, acc_ref, carry_ref):
    t = SB_TILE

    def step(tiles):
        tri = tri_ref[...]
        rc = SB_ROW_CHUNK
        chains = [(tile, hh, r) for tile in tiles for hh in range(HEADS_PER_TILE) for r in range(t // rc)]

        def block_start(tile, j):
            return pl.multiple_of((tile[2] - j) * t, t)

        terms = {}
        for tile, hh, r in chains:
            slot, row0, _, n_blocks, diagonal = tile
            qm = q_ref[pl.ds(row0 + r * rc, rc), :] * _head_lane_mask(hh, q_ref.dtype)
            for j in range(n_blocks):
                z = _qk(qm, k_ref[pl.ds(block_start(tile, j), t), :])
                neg_abs = lax.bitcast_convert_type(
                    lax.bitcast_convert_type(z, jnp.uint32) | jnp.uint32(0x80000000), jnp.float32)
                soft = jnp.log(1.0 + jnp.exp2(neg_abs)) * LOG2_E
                log_beta = jnp.minimum(z, 0.0) - soft
                log_1m = log_beta - z
                mask = None
                if diagonal and j == 0:
                    row = lax.broadcasted_iota(jnp.int32, (rc, t), 0) + r * rc
                    mask = lax.broadcasted_iota(jnp.int32, (rc, t), 1) < row
                    log_1m = jnp.where(mask, log_1m, 0.0)
                hi = log_1m.astype(jnp.bfloat16)
                split = jnp.concatenate([hi, (log_1m - hi.astype(jnp.float32)).astype(jnp.bfloat16)], axis=1)
                terms[slot, hh, r, j] = (log_beta, log_1m[:, 0:1], split, mask)

        cums = {key: jnp.dot(split, tri, preferred_element_type=jnp.float32)
                for key, (_, _, split, _) in terms.items()}

        alive = {}
        for tile, hh, r in chains:
            slot, _, _, n_blocks, _ = tile
            rows = slice(r * rc, (r + 1) * rc)
            carry = carry_ref[slot, hh, rows, :]
            acc = acc_ref[slot, hh, rows, :]
            for j in range(n_blocks):
                log_beta, first_term, _, mask = terms[slot, hh, r, j]
                cum = cums[slot, hh, r, j]
                w = jnp.exp2(log_beta + cum + jnp.concatenate([carry] * (t // LANES), axis=1))
                if mask is not None:
                    w = jnp.where(mask, w, 0.0)
                acc = acc + jnp.dot(w.astype(jnp.bfloat16), v_ref[pl.ds(block_start(tile, j), t), :],
                                    preferred_element_type=jnp.float32)
                carry = carry + jnp.broadcast_to(cum[:, 0:1] + first_term, (rc, LANES))
            acc_ref[slot, hh, rows, :] = acc
            carry_ref[slot, hh, rows, :] = carry
            alive[slot] = carry if slot not in alive else jnp.maximum(alive[slot], carry)
        return tuple((jnp.max(alive[tile[0]]) < SB_DEAD_LOG2_WEIGHT).astype(jnp.int32) for tile in tiles)

    def query_tiles(pair, c):
        qi = [pl.program_id(2) * SB_TILES_PER_STEP + SB_PAIR * pair + s for s in range(SB_PAIR)]
        row0 = [pl.multiple_of((SB_PAIR * pair + s) * t, t) for s in range(SB_PAIR)]
        acc_ref[...] = jnp.zeros_like(acc_ref)
        carry_ref[...] = jnp.zeros_like(carry_ref)

        def first(n_first):
            return step([(0, row0[0], qi[0], n_first, True)]
                        + [(s, row0[s], qi[s], 2, True) for s in range(1, SB_PAIR)])

        dead = lax.cond(qi[0] >= 1, lambda: first(2), lambda: first(1))

        for s in range(SB_PAIR):
            def cond(c):
                newest, dead = c
                return jnp.logical_and(newest >= 1, dead == 0)

            def body(c, s=s):
                newest, _ = c
                return newest - 2, step([(s, row0[s], newest, 2, False)])[0]

            newest, still = lax.while_loop(cond, body, (qi[s] - 2, dead[s]))

            @pl.when(jnp.logical_and(newest == 0, still == 0))
            def _(s=s):
                step([(s, row0[s], 0, 1, False)])

            o_ref[pl.ds(row0[s], t), :] = _merge_heads([acc_ref[s, 0], acc_ref[s, 1]])
        return c

    lax.fori_loop(0, SB_TILES_PER_STEP // SB_PAIR, query_tiles, 0)


def _sb_attention(u, tri, batch, seq):
    t = SB_TILE
    rows = SB_TILES_PER_STEP * t
    nq = seq // rows
    return pl.pallas_call(
        _sb_kernel,
        grid=(batch, N_PAIRS, nq),
        in_specs=[pl.BlockSpec((rows, LANES), lambda b, p, i: (b * nq + i, COL_QA // LANES + p)),
                  pl.BlockSpec((seq, LANES), lambda b, p, i: (b, COL_KA // LANES + p)),
                  pl.BlockSpec((seq, LANES), lambda b, p, i: (b, COL_VA // LANES + p)),
                  pl.BlockSpec((2 * t, t), lambda b, p, i: (0, 0))],
        out_specs=pl.BlockSpec((rows, LANES), lambda b, p, i: (b * nq + i, p)),
        out_shape=jax.ShapeDtypeStruct((batch * seq, WIDTH), jnp.float32),
        scratch_shapes=[pltpu.VMEM((SB_PAIR, HEADS_PER_TILE, t, LANES), jnp.float32),
                        pltpu.VMEM((SB_PAIR, HEADS_PER_TILE, t, LANES), jnp.float32)],
        compiler_params=pltpu.CompilerParams(dimension_semantics=("parallel", "parallel", "arbitrary"),
                                             vmem_limit_bytes=VMEM_LIMIT),
        name="sb_attention",
    )(u, u, u, tri)


MOBA_LO_LANE = 64
MOBA_NEVER = 63
MOBA_ZERO_ROW = 64
MOBA_MASKED = -1e30
MOBA_GROUP = 8
MOBA_BLOCKS_PER_STEP = 8


def _moba_kernel(far_ref, q_ref, k_ref, v_ref, btab_ref, oh_ref, o_ref,
                 kmean_ref, vt_ref, qa_ref, m_ref, acc_ref, *, nb):
    t = MOBA_BLOCK
    nsub = MOBA_BLOCKS_PER_STEP
    nh = HEADS_PER_TILE
    heads = range(nh)
    pair = pl.program_id(1)

    @pl.when(pl.program_id(2) == 0)
    def _():
        kmean_ref[...] = jnp.zeros_like(kmean_ref)
        kf = k_ref[...].astype(jnp.float32).reshape(nb, t, LANES)
        kmean_ref[0:nb, :] = jnp.mean(kf, axis=1)

        head_of_row = lax.broadcasted_iota(jnp.int32, (LANES, t), 0) // HEAD_DIM

        def transpose_block(b, c):
            vb = v_ref[pl.ds(pl.multiple_of(b * t, t), t), :].astype(jnp.float32).T
            for hh in heads:
                vt_ref[hh, b] = jnp.where(head_of_row == hh, vb, 1.0).astype(jnp.bfloat16)
            return c

        lax.fori_loop(0, nb, transpose_block, 0)

    rows = -(-nb // F32_SUBLANES) * F32_SUBLANES
    km = kmean_ref[0:rows, :]
    km0 = km.astype(jnp.bfloat16)
    r1 = km - km0.astype(jnp.float32)
    km1 = r1.astype(jnp.bfloat16)
    km2 = (r1 - km1.astype(jnp.float32)).astype(jnp.bfloat16)

    cols = nsub * t
    blk = lax.broadcasted_iota(jnp.int32, (rows, nh * cols), 0)
    blk_f = blk.astype(jnp.float32)
    col = lax.broadcasted_iota(jnp.int32, (1, nh * cols), 1)
    own_col = pl.program_id(2) * nsub + (col % cols) // t
    gate_rows = sum(_qk(jnp.concatenate([part * _head_lane_mask(hh, part.dtype) for hh in heads], axis=0),
                        q_ref[...]) for part in (km0, km1, km2))
    gate = jnp.concatenate([gate_rows[hh * rows:(hh + 1) * rows] for hh in heads], axis=1)
    gate = jnp.where(blk < own_col, gate, _NEG_INF)
    sel = jnp.zeros_like(gate)
    for _ in range(MOBA_TOPK):
        best = jnp.max(gate, axis=0, keepdims=True)
        idx = jnp.min(jnp.where(gate == best, blk_f, float(LANES)), axis=0, keepdims=True)
        hit = blk_f == idx
        sel = jnp.where(jnp.logical_and(hit, best > _NEG_INF), 1.0, sel)
        gate = jnp.where(hit, _NEG_INF, gate)

    far_bias = jnp.where(col // cols == 0, far_ref[pair * nh], far_ref[pair * nh + 1])
    term = jnp.where(sel > 0.0, jnp.where(blk < own_col - 1, far_bias, 0.0), MOBA_MASKED)
    hi = term.astype(jnp.bfloat16).astype(jnp.float32)
    lo = term - hi
    pad = MOBA_LO_LANE - rows
    pad_hi = [jnp.full((pad, t), MOBA_MASKED, jnp.float32)] if pad else []
    pad_lo = [jnp.zeros((pad, t), jnp.float32)] if pad else []
    for hh in heads:
        for sub in range(nsub):
            cs = slice(hh * cols + sub * t, hh * cols + (sub + 1) * t)
            inj = jnp.concatenate([hi[:, cs]] + pad_hi + [lo[:, cs]] + pad_lo, axis=0).T
            q_sub = q_ref[sub * t:(sub + 1) * t, :] * _head_lane_mask(hh, q_ref.dtype)
            qa_ref[hh, sub] = jnp.concatenate([q_sub, inj.astype(jnp.bfloat16)], axis=1)

    def query_block(sub, c):
        _moba_query_block(sub, k_ref, btab_ref, oh_ref, o_ref, vt_ref, qa_ref, m_ref, acc_ref, nb=nb)
        return c

    lax.fori_loop(0, nsub, query_block, 0)


def _moba_query_block(sub, k_ref, btab_ref, oh_ref, o_ref, vt_ref, qa_ref, m_ref, acc_ref, *, nb):
    t = MOBA_BLOCK
    g = MOBA_GROUP
    own = pl.program_id(2) * MOBA_BLOCKS_PER_STEP + sub
    q_rows = pl.ds(pl.multiple_of(sub * t, t), t)
    heads = range(HEADS_PER_TILE)

    def update(scores, blocks, first):
        col_max = [jnp.max(s, axis=0, keepdims=True) for s in scores]
        if first:
            m_new = col_max
        else:
            m_old = [m_ref[hh][0:1, :] for hh in heads]
            m_new = [jnp.maximum(m_old[hh], col_max[hh]) for hh in heads]
            alpha = [jnp.exp2(m_old[hh] - m_new[hh]) for hh in heads]
        p = [jnp.exp2(scores[hh] - m_new[hh]).astype(jnp.bfloat16) for hh in heads]
        pv = [jnp.dot(jnp.concatenate([vt_ref[hh, b] for b in blocks], axis=1), p[hh],
                      preferred_element_type=jnp.float32) for hh in heads]
        for hh in heads:
            acc_ref[hh] = pv[hh] if first else alpha[hh] * acc_ref[hh] + pv[hh]
            m_ref[hh] = jnp.broadcast_to(m_new[hh], m_ref.shape[1:])

    def block_lanes(row):
        return jnp.concatenate([oh_ref[row]] * (t // BF16_SUBLANES), axis=0)

    prev = jnp.maximum(own - 1, 0)
    n_far = jnp.maximum(own - 1, 0)
    n_full = n_far // g
    left = n_far - n_full * g

    def far_keys(first_blk, group):
        g0 = jnp.minimum(first_blk, nb - group)
        rows = []
        for j in range(group):
            b = g0 + j
            active = jnp.logical_and(b >= first_blk, b < n_far)
            rows.append(block_lanes(jnp.where(active, b, MOBA_NEVER)))
        keys = k_ref[pl.ds(pl.multiple_of(g0 * t, t), group * t), :]
        return g0, keys, rows

    def first_step(group):
        keys = [k_ref[pl.ds(pl.multiple_of(own * t, t), t), :], k_ref[pl.ds(pl.multiple_of(prev * t, t), t), :]]
        rows = [block_lanes(MOBA_ZERO_ROW), block_lanes(jnp.where(own >= 1, prev, MOBA_NEVER))]
        if group:
            g0, far, far_rows = far_keys(n_full * g, group)
            keys.append(far)
            rows.extend(far_rows)
        k_aug = jnp.concatenate([jnp.concatenate(keys, axis=0), jnp.concatenate(rows, axis=0)], axis=1)
        scores = [_qk(k_aug, qa_ref[hh, sub]) for hh in heads]
        update([scores[hh][:t] + btab_ref[hh, 0] for hh in heads], [own], first=True)
        update([scores[hh][t:2 * t] + btab_ref[hh, 1] for hh in heads], [prev], first=False)
        for j in range(group):
            update([s[(2 + j) * t:(3 + j) * t] for s in scores], [g0 + j], first=False)

    for count in range(g):
        @pl.when(left == count)
        def _(count=count):
            first_step(count)

    def far_body(i, c):
        g0, keys, rows = far_keys(i * g, g)
        k_aug = jnp.concatenate([keys, jnp.concatenate(rows, axis=0)], axis=1)
        scores = [_qk(k_aug, qa_ref[hh, sub]) for hh in heads]
        for j in range(g):
            update([s[j * t:(j + 1) * t] for s in scores], [g0 + j], first=False)
        return c

    lax.fori_loop(0, n_full, far_body, 0)

    out_t = jnp.concatenate([acc_ref[0][:HEAD_DIM] / acc_ref[0][HEAD_DIM:],
                             acc_ref[1][HEAD_DIM:] / acc_ref[1][:HEAD_DIM]], axis=0)
    o_ref[q_rows, :] = out_t.T


def _moba_attention(u, far_bias, btab, block_lanes, batch, seq):
    t = MOBA_BLOCK
    nb = seq // t
    rows = MOBA_BLOCKS_PER_STEP * t
    nstep = seq // rows
    kernel = functools.partial(_moba_kernel, nb=nb)
    grid_spec = pltpu.PrefetchScalarGridSpec(
        num_scalar_prefetch=1,
        grid=(batch, N_PAIRS, nstep),
        in_specs=[pl.BlockSpec((rows, LANES), lambda b, p, i, far: (b * nstep + i, COL_QB // LANES + p)),
                  pl.BlockSpec((seq, LANES), lambda b, p, i, far: (b, COL_KB // LANES + p)),
                  pl.BlockSpec((seq, LANES), lambda b, p, i, far: (b, COL_VB // LANES + p)),
                  pl.BlockSpec((HEADS_PER_TILE, 2, t, t), lambda b, p, i, far: (p, 0, 0, 0)),
                  pl.BlockSpec(block_lanes.shape, lambda b, p, i, far: (0, 0, 0))],
        out_specs=pl.BlockSpec((rows, LANES), lambda b, p, i, far: (b * nstep + i, p)),
        scratch_shapes=[pltpu.VMEM((MOBA_LO_LANE, LANES), jnp.float32),
                        pltpu.VMEM((HEADS_PER_TILE, nb, LANES, t), jnp.bfloat16),
                        pltpu.VMEM((HEADS_PER_TILE, MOBA_BLOCKS_PER_STEP, t, 2 * LANES), jnp.bfloat16),
                        pltpu.VMEM((HEADS_PER_TILE, F32_SUBLANES, t), jnp.float32),
                        pltpu.VMEM((HEADS_PER_TILE, LANES, t), jnp.float32)])
    return pl.pallas_call(
        kernel,
        grid_spec=grid_spec,
        out_shape=jax.ShapeDtypeStruct((batch * seq, WIDTH), jnp.float32),
        compiler_params=pltpu.CompilerParams(dimension_semantics=("parallel", "parallel", "arbitrary"),
                                             vmem_limit_bytes=VMEM_LIMIT),
        name="moba_attention",
    )(far_bias, u, u, u, btab, block_lanes)


SWA_STEP_WINDOWS = 4


def _swa_kernel(sink_ref, q_ref, kp_ref, kc_ref, vp_ref, vc_ref, bias_ref, o_ref):
    w = SWA_WINDOW
    keys = jnp.concatenate([kp_ref[...], kc_ref[...]], axis=0)
    values = jnp.concatenate([vp_ref[...], vc_ref[...]], axis=0)
    slots = [(j, half) for j in range(SWA_GROUP) for half in range(SWA_KV_HEADS)]
    chains = [(win, slot) for win in range(SWA_STEP_WINDOWS) for slot in range(len(slots))]

    def attend(first_step):
        scores = []
        for win, slot in chains:
            j, half = slots[slot]
            q = q_ref[win * w:(win + 1) * w, j * LANES:(j + 1) * LANES]
            s = _qk(q * _head_lane_mask(half, q.dtype), keys[win * w:(win + 2) * w]) + bias_ref[slot]
            if first_step and win == 0:
                s = jnp.where(lax.broadcasted_iota(jnp.int32, (w, 2 * w), 1) >= w, s, _NEG_INF)
            scores.append(s)
        m = [jnp.maximum(jnp.broadcast_to(jnp.max(s, axis=1, keepdims=True), (w, LANES)), sink_ref[slot])
             for (win, slot), s in zip(chains, scores)]
        p = [jnp.exp2(s - jnp.concatenate([mi] * (2 * w // LANES), axis=1)).astype(jnp.bfloat16)
             for mi, s in zip(m, scores)]
        pv = []
        for (win, slot), pi in zip(chains, p):
            vw = values[win * w:(win + 2) * w]
            vw = jnp.where(_head_lane_mask(slots[slot][1], jnp.int32) > 0, vw, jnp.ones_like(vw))
            pv.append(jnp.dot(pi, vw, preferred_element_type=jnp.float32))
        out = [pvi / (pltpu.roll(pvi, HEAD_DIM, axis=1) + jnp.exp2(sink_ref[slot] - mi))
               for (win, slot), pvi, mi in zip(chains, pv, m)]
        for win in range(SWA_STEP_WINDOWS):
            for j in range(SWA_GROUP):
                first = win * len(slots) + j * SWA_KV_HEADS
                o_ref[win * w:(win + 1) * w, j * LANES:(j + 1) * LANES] = _merge_heads(
                    out[first:first + SWA_KV_HEADS])

    @pl.when(pl.program_id(1) == 0)
    def _():
        attend(True)

    @pl.when(pl.program_id(1) > 0)
    def _():
        attend(False)


def _swa_attention(u, sinks_perm, bias, batch, seq):
    w = SWA_WINDOW
    sw = SWA_STEP_WINDOWS
    nstep = seq // (sw * w)
    kcol, vcol = COL_KC // LANES, COL_VC // LANES
    prev_window = lambda b, n, s: b * (seq // w) + jnp.maximum(sw * n - 1, 0)
    grid_spec = pltpu.PrefetchScalarGridSpec(
        num_scalar_prefetch=1,
        grid=(batch, nstep),
        in_specs=[pl.BlockSpec((sw * w, WIDTH), lambda b, n, s: (b * nstep + n, COL_QC // WIDTH)),
                  pl.BlockSpec((w, LANES), lambda b, n, s: (prev_window(b, n, s), kcol)),
                  pl.BlockSpec((sw * w, LANES), lambda b, n, s: (b * nstep + n, kcol)),
                  pl.BlockSpec((w, LANES), lambda b, n, s: (prev_window(b, n, s), vcol)),
                  pl.BlockSpec((sw * w, LANES), lambda b, n, s: (b * nstep + n, vcol)),
                  pl.BlockSpec((N_HEADS, w, 2 * w), lambda b, n, s: (0, 0, 0))],
        out_specs=pl.BlockSpec((sw * w, WIDTH), lambda b, n, s: (b * nstep + n, 0)))
    return pl.pallas_call(
        _swa_kernel,
        grid_spec=grid_spec,
        out_shape=jax.ShapeDtypeStruct((batch * seq, WIDTH), jnp.float32),
        compiler_params=pltpu.CompilerParams(dimension_semantics=("parallel", "arbitrary"),
                                             vmem_limit_bytes=VMEM_LIMIT),
        name="swa_attention",
    )(sinks_perm, u, u, u, u, u, bias)


def _post_kernel(x_ref, ya_ref, yb_ref, yc_ref, ga_ref, gb_ref, gc_ref, ma_ref, mb_ref, mc_ref,
                 wa_ref, wb_ref, wc_ref, wo_ref, fw_ref, o_ref, *, final):
    def branch(y_ref, g_ref, w_ref, m_ref):
        g = g_ref[...].astype(jnp.float32)
        y = (y_ref[...] * (g * jax.nn.sigmoid(g))).astype(jnp.bfloat16)
        proj = jnp.dot(y, w_ref[...], preferred_element_type=jnp.float32)
        return jax.nn.sigmoid(m_ref[...].astype(jnp.float32)) * proj

    merged = (branch(ya_ref, ga_ref, wa_ref, ma_ref) + branch(yb_ref, gb_ref, wb_ref, mb_ref)
              + branch(yc_ref, gc_ref, wc_ref, mc_ref))
    out = x_ref[...] + jnp.dot(merged.astype(jnp.bfloat16), wo_ref[...], preferred_element_type=jnp.float32)
    if final:
        out = out * lax.rsqrt(jnp.mean(out * out, axis=-1, keepdims=True) + RMS_EPS) * fw_ref[...]
    o_ref[...] = out


def _post(x2, ya, yb, yc, u, wa, wb, wc, wo, fw, layer, *, final, tm=512):
    n = x2.shape[0]
    row = lambda i: (i, 0)
    full = lambda i: (0, 0)
    of_layer = lambda i: (layer, 0, 0)
    y_spec = pl.BlockSpec((tm, WIDTH), row)
    in_specs = [pl.BlockSpec((tm, D_MODEL), row), y_spec, y_spec, y_spec,
                pl.BlockSpec((tm, WIDTH), lambda i: (i, COL_GA // WIDTH)),
                pl.BlockSpec((tm, WIDTH), lambda i: (i, COL_GB // WIDTH)),
                pl.BlockSpec((tm, WIDTH), lambda i: (i, COL_GC // WIDTH)),
                pl.BlockSpec((tm, D_MODEL), lambda i: (i, COL_MA // D_MODEL)),
                pl.BlockSpec((tm, D_MODEL), lambda i: (i, COL_MB // D_MODEL)),
                pl.BlockSpec((tm, D_MODEL), lambda i: (i, COL_MC // D_MODEL)),
                pl.BlockSpec((None, WIDTH, D_MODEL), of_layer), pl.BlockSpec((None, WIDTH, D_MODEL), of_layer),
                pl.BlockSpec((None, WIDTH, D_MODEL), of_layer), pl.BlockSpec((None, D_MODEL, D_MODEL), of_layer),
                pl.BlockSpec((1, D_MODEL), full)]
    return pl.pallas_call(
        functools.partial(_post_kernel, final=final),
        grid=(n // tm,),
        in_specs=in_specs,
        out_specs=pl.BlockSpec((tm, D_MODEL), row),
        out_shape=jax.ShapeDtypeStruct((n, D_MODEL), jnp.float32),
        compiler_params=pltpu.CompilerParams(dimension_semantics=("parallel",),
                                             vmem_limit_bytes=VMEM_LIMIT),
        name="post_final" if final else "post",
    )(x2, ya, yb, yc, u, u, u, u, u, u, wa, wb, wc, wo, fw)


def _bucket_lookup(table, dist):
    onehot = (_rel_bucket(dist)[..., None] == jnp.arange(REL_BUCKETS)).astype(jnp.float32)
    return jnp.einsum("...b,bh->h...", onehot, table.astype(jnp.float32), precision=lax.Precision.HIGHEST)


def _bias_tables(rel_bias):
    moba_tab = rel_bias[:, :N_HEADS]
    swa_tab = _swa_head_order(rel_bias[:, N_HEADS:], 1, per_head=1)
    t = MOBA_BLOCK
    d_own = jnp.arange(t)[:, None] - jnp.arange(t)[None, :]
    own = jnp.where(d_own[None] >= 0, _bucket_lookup(moba_tab, d_own), _NEG_INF)
    prev = _bucket_lookup(moba_tab, d_own + t)
    btab = jnp.swapaxes(jnp.stack([own, prev], axis=1), 2, 3) * LOG2_E
    far = _bucket_lookup(moba_tab, jnp.full((1,), t + 1, jnp.int32))[:, 0] * LOG2_E

    w = SWA_WINDOW
    dist = jnp.arange(w)[:, None] + w - jnp.arange(2 * w)[None, :]
    in_band = (dist >= 0) & (dist < w)
    swa = jnp.where(in_band[None], _bucket_lookup(swa_tab, dist) * LOG2_E, _NEG_INF)
    return btab, far, swa


def kernel(x, norm_w, w_in, w_proj_a, w_proj_b, w_proj_c, w_out, sinks, rel_bias, final_norm_w):
    batch, seq, _ = x.shape
    depth = w_in.shape[0]
    assert seq % MOBA_BLOCK == 0 and MOBA_GROUP <= seq // MOBA_BLOCK <= MOBA_NEVER
    btab, far, swa_bias = _bias_tables(rel_bias)
    t = SB_TILE
    tri = (jnp.arange(t)[:, None] > jnp.arange(t)[None, :]).astype(jnp.bfloat16)
    tri = jnp.concatenate([tri, tri], axis=0)
    lane = jnp.arange(LANES)[None, :]
    blk = jnp.arange(MOBA_ZERO_ROW + 1)[:, None]
    block_lanes = ((lane == blk) | (lane == blk + MOBA_LO_LANE)) & (blk < MOBA_ZERO_ROW)
    block_lanes = jnp.broadcast_to(block_lanes[:, None, :],
                                   (MOBA_ZERO_ROW + 1, BF16_SUBLANES, LANES)).astype(jnp.bfloat16)
    sinks_perm = _swa_head_order(sinks.astype(jnp.float32), 1, per_head=1) * LOG2_E

    w_u = _permute_input_columns(w_in)
    norm_w3 = norm_w.reshape(depth, 1, D_MODEL)
    wa = w_proj_a.astype(jnp.bfloat16)
    wb = w_proj_b.astype(jnp.bfloat16)
    wc = _swa_head_order(w_proj_c, 1).astype(jnp.bfloat16)
    wo = w_out.astype(jnp.bfloat16)
    fw = final_norm_w.reshape(1, D_MODEL)

    x2 = x.reshape(batch * seq, D_MODEL)
    for layer in range(depth):
        u = _inproj(x2, norm_w3, w_u, layer)
        ya = _sb_attention(u, tri, batch, seq)
        yb = _moba_attention(u, far, btab, block_lanes, batch, seq)
        yc = _swa_attention(u, sinks_perm[layer], swa_bias, batch, seq)
        x2 = _post(x2, ya, yb, yc, u, wa, wb, wc, wo, fw, layer, final=(layer == depth - 1))
    return x2.reshape(batch, seq, D_MODEL)
```
